```python
import jax, jax.numpy as jnp
from jax import lax
import numpy as np

D_MODEL = 1024
BATCH = 8
SEQ = 8192
DEPTH = 4

GLA_HEADS = 4
GLA_DK = 64
GLA_DV = 128
GLA_GATE_RANK = 16
GLA_TAU = 16.0
GLA_CHUNK = 64
CONV_CH = D_MODEL // 2
CONV_WIDTH = 31
EVEN_SPLITS = (GLA_HEADS * GLA_DK, GLA_HEADS * GLA_DK, GLA_HEADS * GLA_DV, GLA_HEADS * GLA_DV, GLA_GATE_RANK, 2 * CONV_CH)
EVEN_IN = sum(EVEN_SPLITS)
EVEN_MIX = GLA_HEADS * GLA_DV + CONV_CH
ATTN_HEADS = 16
HEAD_DIM = D_MODEL // ATTN_HEADS
DILATED_BRANCHES = ((128, 1), (512, 4), (2048, 16))
DIL_BLOCK = 128
ROPE_DIMS = HEAD_DIM // 4
ROPE_THETA = 500000.0
N_GROUPS = 4
EXPERTS_PER_GROUP = 8
N_EXPERTS = N_GROUPS * EXPERTS_PER_GROUP
TOP_K = 2
EXPERT_HIDDEN = 512
MOE_BLOCK = 128
ADA_CHUNKS = 6
NORM_EPS = 1e-6

kernel_name = 'hybrid_gla_conv_dilated_hmoe_trunk'


def rms_norm(x, g):
    x32 = x.astype(jnp.float32)
    y = x32 * lax.rsqrt(jnp.mean(x32 * x32, axis=-1, keepdims=True) + NORM_EPS)
    return (y * g.astype(jnp.float32)).astype(x.dtype)


def layer_norm(x, g, b):
    x32 = x.astype(jnp.float32)
    mu = jnp.mean(x32, axis=-1, keepdims=True)
    var = jnp.mean(jnp.square(x32 - mu), axis=-1, keepdims=True)
    y = (x32 - mu) * lax.rsqrt(var + NORM_EPS) * g.astype(jnp.float32) + b.astype(jnp.float32)
    return y.astype(x.dtype)


def modulate(x, g, shift, scale):
    return rms_norm(x, g) * (1 + scale[:, None, :]) + shift[:, None, :]


def gla_chunked(q, k, v, log_a):
    B, S, H, DK = q.shape
    DV = v.shape[-1]
    C = GLA_CHUNK
    N = S // C

    def blocks(t):
        return t.reshape(B, N, C, H, t.shape[-1]).transpose(0, 3, 1, 2, 4)

    q, k, v, log_a = blocks(q), blocks(k), blocks(v), blocks(log_a)
    b = jnp.cumsum(log_a, axis=3)
    b_last = b[:, :, :, -1:, :]
    q_dec = q * jnp.exp(b)
    k_dec = k * jnp.exp(-b)
    causal = jnp.tril(jnp.ones((C, C), dtype=bool))
    att = jnp.where(causal, jnp.einsum('bhnik,bhnjk->bhnij', q_dec, k_dec), 0.0)
    o_intra = jnp.einsum('bhnij,bhnjv->bhniv', att, v)
    chunk_kv = jnp.einsum('bhnjk,bhnjv->bhnkv', k * jnp.exp(b_last - b), v)
    decay = jnp.exp(b_last[:, :, :, 0, :])

    def step(state, inp):
        dec, kv = inp
        return dec[..., None] * state + kv, state

    init = jnp.zeros((B, H, DK, DV), jnp.float32)
    _, states = lax.scan(step, init, (jnp.moveaxis(decay, 2, 0), jnp.moveaxis(chunk_kv, 2, 0)))
    states = jnp.moveaxis(states, 0, 2)
    o = o_intra + jnp.einsum('bhnik,bhnkv->bhniv', q_dec, states)
    return o.transpose(0, 2, 3, 1, 4).reshape(B, S, H, DV)


def even_mixer(h, w_in, w_gate2, b_gate, gla_norm_g, conv_w, conv_b, conv_ln_g, conv_ln_b, w_out):
    B, S, _ = h.shape
    f32 = jnp.float32
    z = h @ w_in
    q, k, v, g, a_lr, u = jnp.split(z, [int(i) for i in np.cumsum(EVEN_SPLITS)[:-1]], axis=-1)
    q = q.reshape(B, S, GLA_HEADS, GLA_DK).astype(f32) * (GLA_DK ** -0.5)
    k = k.reshape(B, S, GLA_HEADS, GLA_DK).astype(f32)
    v = v.reshape(B, S, GLA_HEADS, GLA_DV).astype(f32)
    log_a = jax.nn.log_sigmoid((a_lr @ w_gate2).astype(f32) + b_gate.astype(f32)) / GLA_TAU
    log_a = log_a.reshape(B, S, GLA_HEADS, GLA_DK)
    o = gla_chunked(q, k, v, log_a)
    o = rms_norm(o, gla_norm_g).reshape(B, S, GLA_HEADS * GLA_DV)
    o_gla = o * jax.nn.silu(g.astype(f32))
    u_a, u_b = jnp.split(u, 2, axis=-1)
    glu = u_a * jax.nn.sigmoid(u_b)
    y = lax.conv_general_dilated(glu, conv_w.astype(glu.dtype)[:, None, :], window_strides=(1,),
                                 padding=((CONV_WIDTH - 1, 0),),
                                 dimension_numbers=('NWC', 'WIO', 'NWC'),
                                 feature_group_count=CONV_CH)
    y = jax.nn.silu(layer_norm(y + conv_b, conv_ln_g, conv_ln_b))
    mix = jnp.concatenate([o_gla.astype(h.dtype), y.astype(h.dtype)], axis=-1)
    return mix @ w_out


def partial_rope(t, positions):
    t = t.astype(jnp.float32)
    half = ROPE_DIMS // 2
    inv_freq = ROPE_THETA ** (-jnp.arange(0, ROPE_DIMS, 2, dtype=jnp.float32) / ROPE_DIMS)
    ang = positions.astype(jnp.float32)[..., None] * inv_freq
    cos = jnp.cos(ang)[:, :, None, :]
    sin = jnp.sin(ang)[:, :, None, :]
    t1 = t[..., :half]
    t2 = t[..., half:ROPE_DIMS]
    return jnp.concatenate([t1 * cos - t2 * sin, t2 * cos + t1 * sin, t[..., ROPE_DIMS:]], axis=-1)


def dilated_branch(q, k, v, window, dilation):
    B, S, H, Dh = q.shape
    steps = window // dilation
    unit = dilation * DIL_BLOCK
    s_pad = -(-S // unit) * unit
    L = s_pad // dilation
    nb = L // DIL_BLOCK

    def to_blocks(t):
        t = jnp.pad(t, ((0, 0), (0, s_pad - S), (0, 0), (0, 0)))
        t = t.reshape(B, L, dilation, H, Dh).transpose(0, 2, 3, 1, 4)
        return t.reshape(B, dilation, H, nb, DIL_BLOCK, Dh)

    def with_prev(t):
        prev = jnp.pad(t[:, :, :, :-1], ((0, 0), (0, 0), (0, 0), (1, 0), (0, 0), (0, 0)))
        return jnp.concatenate([prev, t], axis=4)

    qb = to_blocks(q)
    kc = with_prev(to_blocks(k))
    vc = with_prev(to_blocks(v))
    s = jnp.einsum('brhnqd,brhnkd->brhnqk', qb, kc) * (Dh ** -0.5)
    qi = jnp.arange(DIL_BLOCK)[:, None]
    kj = jnp.arange(2 * DIL_BLOCK)[None, :]
    dist = qi + DIL_BLOCK - kj
    band = (dist >= 0) & (dist <= steps)
    has_prev = (jnp.arange(nb) > 0)[:, None, None] | (kj >= DIL_BLOCK)[None]
    mask = band[None] & has_prev
    s = jnp.where(mask, s, -jnp.inf)
    m = jnp.max(s, axis=-1, keepdims=True)
    p = jnp.exp(s - m)
    l = jnp.sum(p, axis=-1, keepdims=True)
    o = jnp.einsum('brhnqk,brhnkd->brhnqd', p, vc) / l
    lse = (m + jnp.log(l))[..., 0]
    o = o.reshape(B, dilation, H, L, Dh).transpose(0, 3, 1, 2, 4).reshape(B, s_pad, H, Dh)[:, :S]
    lse = lse.reshape(B, dilation, H, L).transpose(0, 3, 1, 2).reshape(B, s_pad, H)[:, :S]
    return o, lse


def odd_mixer(h, positions, w_qkv, w_out):
    B, S, _ = h.shape
    q, k, v = jnp.split(h @ w_qkv, 3, axis=-1)
    q = partial_rope(q.reshape(B, S, ATTN_HEADS, HEAD_DIM), positions)
    k = partial_rope(k.reshape(B, S, ATTN_HEADS, HEAD_DIM), positions)
    v = v.reshape(B, S, ATTN_HEADS, HEAD_DIM).astype(jnp.float32)
    outs, lses = [], []
    for window, dilation in DILATED_BRANCHES:
        o_i, lse_i = dilated_branch(q, k, v, window, dilation)
        outs.append(o_i)
        lses.append(lse_i)
    wts = jax.nn.softmax(jnp.stack(lses, axis=0), axis=0)
    o = jnp.einsum('ibsh,ibshd->bshd', wts, jnp.stack(outs, axis=0))
    return o.reshape(B, S, ATTN_HEADS * HEAD_DIM).astype(h.dtype) @ w_out


def hier_moe(h, w_grp, b_grp, w_rt, b_rt, w1, w3, w2):
    B, S, D = h.shape
    T = B * S
    ht = h.reshape(T, D)
    tok = jnp.arange(T)
    g_logits = (ht @ w_grp).astype(jnp.float32) + b_grp.astype(jnp.float32)
    g_idx = jnp.argmax(g_logits, axis=-1)
    g_w = jax.nn.softmax(g_logits, axis=-1)[tok, g_idx]
    e_logits = ((ht @ w_rt).astype(jnp.float32) + b_rt.astype(jnp.float32)).reshape(T, N_GROUPS, EXPERTS_PER_GROUP)
    e_in_grp = e_logits[tok, g_idx]
    top_v, top_i = lax.top_k(e_in_grp, TOP_K)
    top_w = jax.nn.softmax(top_v, axis=-1) * g_w[:, None]
    e_flat = (g_idx[:, None] * EXPERTS_PER_GROUP + top_i).reshape(-1).astype(jnp.int32)
    w_flat = top_w.reshape(-1)
    tok_flat = jnp.repeat(tok, TOP_K).astype(jnp.int32)
    A = T * TOP_K
    order = jnp.argsort(e_flat)
    e_sorted = e_flat[order]
    counts = jnp.bincount(e_flat, length=N_EXPERTS)
    padded = (counts + MOE_BLOCK - 1) // MOE_BLOCK * MOE_BLOCK
    starts = jnp.cumsum(counts) - counts
    pends = jnp.cumsum(padded)
    pstarts = pends - padded
    dest = pstarts[e_sorted] + (jnp.arange(A) - starts[e_sorted])
    P = A + N_EXPERTS * MOE_BLOCK
    row_tok = jnp.full((P,), T, jnp.int32).at[dest].set(tok_flat[order])
    row_w = jnp.zeros((P,), jnp.float32).at[dest].set(w_flat[order])
    n_blk = P // MOE_BLOCK
    blk_e = jnp.clip(jnp.searchsorted(pends, jnp.arange(n_blk) * MOE_BLOCK, side='right'), 0, N_EXPERTS - 1)
    h_pad = jnp.concatenate([ht, jnp.zeros((1, D), ht.dtype)], axis=0)
    xs = h_pad[row_tok].reshape(n_blk, MOE_BLOCK, D)

    def expert_block(args):
        xb, e = args
        return (jax.nn.silu(xb @ w1[e]) * (xb @ w3[e])) @ w2[e]

    ys = lax.map(expert_block, (xs, blk_e)).reshape(P, D)
    out = jnp.zeros((T + 1, D), jnp.float32).at[row_tok].add(ys.astype(jnp.float32) * row_w[:, None])
    return out[:T].reshape(B, S, D).astype(h.dtype)


def setup_inputs(seed: int = 0) -> dict:
    key = jax.random.key(seed)
    ks = jax.random.split(key, 32)
    f32 = jnp.float32
    n_even = (DEPTH + 1) // 2
    n_odd = DEPTH // 2
    D = D_MODEL

    def nrm(k, shape, scale):
        return jax.random.normal(k, shape, f32) * scale

    offsets = jax.random.randint(ks[2], (BATCH, 1), 0, 4096, dtype=jnp.int32)
    positions = (offsets + jnp.arange(SEQ, dtype=jnp.int32)[None, :]).astype(jnp.int32)
    return {
        'x': nrm(ks[0], (BATCH, SEQ, D), 1.0),
        'c': nrm(ks[1], (BATCH, D), 1.0),
        'positions': positions,
        'ada_w': nrm(ks[3], (DEPTH, D, ADA_CHUNKS * D), 0.5 * D ** -0.5),
        'ada_b': nrm(ks[4], (DEPTH, ADA_CHUNKS * D), 0.02),
        'norm1_g': 1.0 + nrm(ks[5], (DEPTH, D), 0.02),
        'norm2_g': 1.0 + nrm(ks[6], (DEPTH, D), 0.02),
        'even_w_in': nrm(ks[7], (n_even, D, EVEN_IN), D ** -0.5),
        'even_w_gate2': nrm(ks[8], (n_even, GLA_GATE_RANK, GLA_HEADS * GLA_DK), GLA_GATE_RANK ** -0.5),
        'even_b_gate': nrm(ks[9], (n_even, GLA_HEADS * GLA_DK), 0.1),
        'even_gla_norm_g': 1.0 + nrm(ks[10], (n_even, GLA_DV), 0.02),
        'even_conv_w': nrm(ks[11], (n_even, CONV_WIDTH, CONV_CH), CONV_WIDTH ** -0.5),
        'even_conv_b': nrm(ks[12], (n_even, CONV_CH), 0.02),
        'even_conv_ln_g': 1.0 + nrm(ks[13], (n_even, CONV_CH), 0.02),
        'even_conv_ln_b': nrm(ks[14], (n_even, CONV_CH), 0.02),
        'even_w_out': nrm(ks[15], (n_even, EVEN_MIX, D), EVEN_MIX ** -0.5),
        'odd_w_qkv': nrm(ks[16], (n_odd, D, 3 * ATTN_HEADS * HEAD_DIM), D ** -0.5),
        'odd_w_out': nrm(ks[17], (n_odd, ATTN_HEADS * HEAD_DIM, D), (ATTN_HEADS * HEAD_DIM) ** -0.5),
        'moe_w_grp': nrm(ks[18], (DEPTH, D, N_GROUPS), D ** -0.5),
        'moe_b_grp': nrm(ks[19], (DEPTH, N_GROUPS), 0.01),
        'moe_w_rt': nrm(ks[20], (DEPTH, D, N_EXPERTS), D ** -0.5),
        'moe_b_rt': nrm(ks[21], (DEPTH, N_EXPERTS), 0.01),
        'moe_w1': nrm(ks[22], (DEPTH, N_EXPERTS, D, EXPERT_HIDDEN), D ** -0.5),
        'moe_w3': nrm(ks[23], (DEPTH, N_EXPERTS, D, EXPERT_HIDDEN), D ** -0.5),
        'moe_w2': nrm(ks[24], (DEPTH, N_EXPERTS, EXPERT_HIDDEN, D), EXPERT_HIDDEN ** -0.5),
        'final_norm_g': 1.0 + nrm(ks[25], (D,), 0.02),
    }


def reference(x, c, positions, ada_w, ada_b, norm1_g, norm2_g, even_w_in, even_w_gate2, even_b_gate,
              even_gla_norm_g, even_conv_w, even_conv_b, even_conv_ln_g, even_conv_ln_b, even_w_out,
              odd_w_qkv, odd_w_out, moe_w_grp, moe_b_grp, moe_w_rt, moe_b_rt, moe_w1, moe_w3, moe_w2,
              final_norm_g):
    cond = jax.nn.silu(c)
    for layer in range(DEPTH):
        mod = cond.astype(ada_w.dtype) @ ada_w[layer] + ada_b[layer]
        sh1, sc1, g1, sh2, sc2, g2 = jnp.split(mod, ADA_CHUNKS, axis=-1)
        h = modulate(x, norm1_g[layer], sh1, sc1)
        i = layer // 2
        if layer % 2 == 0:
            y = even_mixer(h, even_w_in[i], even_w_gate2[i], even_b_gate[i], even_gla_norm_g[i],
                           even_conv_w[i], even_conv_b[i], even_conv_ln_g[i], even_conv_ln_b[i],
                           even_w_out[i])
        else:
            y = odd_mixer(h, positions, odd_w_qkv[i], odd_w_out[i])
        x = x + (g1[:, None, :] * y).astype(x.dtype)
        h = modulate(x, norm2_g[layer], sh2, sc2)
        y = hier_moe(h, moe_w_grp[layer], moe_b_grp[layer], moe_w_rt[layer], moe_b_rt[layer],
                     moe_w1[layer], moe_w3[layer], moe_w2[layer])
        x = x + (g2[:, None, :] * y).astype(x.dtype)
    return rms_norm(x, final_norm_g)
```

```python
import functools

import jax
import jax.numpy as jnp
from jax import lax
from jax.experimental import pallas as pl
from jax.experimental.pallas import tpu as pltpu

F32 = jnp.float32
BF16 = jnp.bfloat16
HIGHEST = lax.Precision.HIGHEST

NORM_EPS = 1e-6
GLA_HEADS = 4
GLA_DK = 64
GLA_DV = 128
GLA_GATE_RANK = 16
GLA_TAU = 16.0
GLA_CHUNK = 64
CONV_WIDTH = 31
ATTN_HEADS = 16
DILATED_BRANCHES = ((128, 1), (512, 4), (2048, 16))
DIL_BLOCK = 128
ROPE_THETA = 500000.0
N_GROUPS = 4
EXPERTS_PER_GROUP = 8
TOP_K = 2
ADA_CHUNKS = 6

LANES = 128
VMEM_LIMIT = 56 * 1024 * 1024
TOKEN_TILE = 512
ATTN_SUPER = 2048
MOE_ROWS = 512
NEG_BIG = -1e30


def _params(*sem):
    return pltpu.CompilerParams(dimension_semantics=sem, vmem_limit_bytes=VMEM_LIMIT)


def _silu(x):
    return x * jax.nn.sigmoid(x)


def _modulate(x, g, shift, scale):
    y = x * lax.rsqrt(jnp.mean(x * x, axis=-1, keepdims=True) + NORM_EPS)
    return (y * g) * (1.0 + scale) + shift


def _ada_kernel(c_ref, w_ref, b_ref, o_ref):
    cond = _silu(c_ref[...])
    o_ref[...] = jnp.dot(cond, w_ref[...], preferred_element_type=F32, precision=HIGHEST) + b_ref[...]


def _ada_mods(c, ada_w, ada_b):
    depth, d, _ = ada_w.shape
    b = c.shape[0]
    return pl.pallas_call(
        _ada_kernel,
        out_shape=jax.ShapeDtypeStruct((depth, ADA_CHUNKS, b, d), F32),
        grid=(depth, ADA_CHUNKS),
        in_specs=[
            pl.BlockSpec((b, d), lambda l, j: (0, 0)),
            pl.BlockSpec((None, d, d), lambda l, j: (l, 0, j)),
            pl.BlockSpec((None, None, 1, d), lambda l, j: (l, j, 0, 0)),
        ],
        out_specs=pl.BlockSpec((None, None, b, d), lambda l, j: (l, j, 0, 0)),
        compiler_params=_params("arbitrary", "arbitrary"),
        name="ada_mods",
    )(c, ada_w, ada_b.reshape(depth, ADA_CHUNKS, 1, d))


def _norm_matmul_kernel(x_ref, sh_ref, sc_ref, g_ref, w_ref, *o_refs):
    h = _modulate(x_ref[...], g_ref[...], sh_ref[...], sc_ref[...]).astype(BF16)
    off = 0
    for o_ref in o_refs:
        n = o_ref.shape[-1]
        o_ref[...] = jnp.dot(h, w_ref[:, off:off + n], preferred_element_type=F32).astype(o_ref.dtype)
        off += n


def _norm_matmul(x, shift, scale, g, w, widths, dtypes, tm):
    b, s, d = x.shape
    row = lambda i, j: (i, 0, 0)
    tile = lambda i, j: (i, j, 0)
    return pl.pallas_call(
        _norm_matmul_kernel,
        out_shape=[jax.ShapeDtypeStruct((b, s, n), dt) for n, dt in zip(widths, dtypes)],
        grid=(b, s // tm),
        in_specs=[
            pl.BlockSpec((None, tm, d), tile),
            pl.BlockSpec((None, 1, d), row),
            pl.BlockSpec((None, 1, d), row),
            pl.BlockSpec((1, d), lambda i, j: (0, 0)),
            pl.BlockSpec(w.shape, lambda i, j: (0, 0)),
        ],
        out_specs=[pl.BlockSpec((None, tm, n), tile) for n in widths],
        compiler_params=_params("arbitrary", "arbitrary"),
        name="norm_matmul",
    )(x, shift, scale, g, w)


def _log_sigmoid(z):
    return jnp.minimum(z, 0.0) - jnp.log1p(jnp.exp(-jnp.abs(z)))


def _gla_kernel(q_ref, k_ref, v_ref, g_ref, a_ref, wg_ref, bg_ref, ng_ref, o_ref, state_ref, la_ref, oacc_ref):
    tm = q_ref.shape[0]
    c = GLA_CHUNK

    @pl.when(pl.program_id(1) == 0)
    def _():
        state_ref[...] = jnp.zeros_like(state_ref)

    z = jnp.dot(a_ref[...].astype(BF16), wg_ref[...], preferred_element_type=F32) + bg_ref[...]
    la_ref[...] = _log_sigmoid(z) * (1.0 / GLA_TAU)

    ri = lax.broadcasted_iota(jnp.int32, (c, c), 0)
    ci = lax.broadcasted_iota(jnp.int32, (c, c), 1)
    causal = ri >= ci
    tril = causal.astype(F32)
    ones = jnp.ones((c, GLA_DV), F32)

    def chunk_body(ic, carry):
        r0 = pl.multiple_of(ic * c, c)
        rows = pl.ds(r0, c)
        la = la_ref[rows, :]
        bcum = jnp.dot(tril, la, preferred_element_type=F32, precision=HIGHEST)
        b_last = bcum[c - 1:c, :]
        q = q_ref[rows, :] * (GLA_DK ** -0.5)
        k = k_ref[rows, :]
        q_dec = (q * jnp.exp(bcum)).astype(BF16)
        k_dec = (k * jnp.exp(-bcum)).astype(BF16)
        k_rem = (k * jnp.exp(b_last - bcum)).astype(BF16)
        dec = jnp.exp(lax.dot_general(la, ones, (((0,), (0,)), ((), ())),
                                      preferred_element_type=F32, precision=HIGHEST))
        for h in range(GLA_HEADS):
            ks = slice(h * GLA_DK, (h + 1) * GLA_DK)
            vs = slice(h * GLA_DV, (h + 1) * GLA_DV)
            vh = v_ref[rows, vs]
            att = lax.dot_general(q_dec[:, ks], k_dec[:, ks], (((1,), (1,)), ((), ())),
                                  preferred_element_type=F32)
            att = jnp.where(causal, att, 0.0).astype(BF16)
            st = state_ref[h]
            o = jnp.dot(att, vh, preferred_element_type=F32)
            o = o + jnp.dot(q_dec[:, ks], st.astype(BF16), preferred_element_type=F32)
            kv = lax.dot_general(k_rem[:, ks], vh, (((0,), (0,)), ((), ())), preferred_element_type=F32)
            state_ref[h] = dec[ks, :] * st + kv
            oacc_ref[rows, vs] = o
        return carry

    lax.fori_loop(0, tm // c, chunk_body, 0)

    for h in range(GLA_HEADS):
        vs = slice(h * GLA_DV, (h + 1) * GLA_DV)
        o = oacc_ref[:, vs]
        o = o * lax.rsqrt(jnp.mean(o * o, axis=-1, keepdims=True) + NORM_EPS) * ng_ref[...]
        o_ref[:, vs] = (o * _silu(g_ref[:, vs])).astype(o_ref.dtype)


def _gla(q, k, v, g, a, w_gate2, b_gate, norm_g, tm):
    b, s, _ = q.shape
    hk = GLA_HEADS * GLA_DK
    hv = GLA_HEADS * GLA_DV
    tile = lambda i, j: (i, j, 0)
    const = lambda i, j: (0, 0)
    return pl.pallas_call(
        _gla_kernel,
        out_shape=jax.ShapeDtypeStruct((b, s, hv), BF16),
        grid=(b, s // tm),
        in_specs=[
            pl.BlockSpec((None, tm, hk), tile),
            pl.BlockSpec((None, tm, hk), tile),
            pl.BlockSpec((None, tm, hv), tile),
            pl.BlockSpec((None, tm, hv), tile),
            pl.BlockSpec((None, tm, a.shape[-1]), tile),
            pl.BlockSpec(w_gate2.shape, const),
            pl.BlockSpec((1, hk), const),
            pl.BlockSpec((1, GLA_DV), const),
        ],
        out_specs=pl.BlockSpec((None, tm, hv), tile),
        scratch_shapes=[
            pltpu.VMEM((GLA_HEADS, GLA_DK, GLA_DV), F32),
            pltpu.VMEM((tm, hk), F32),
            pltpu.VMEM((tm, hv), F32),
        ],
        compiler_params=_params("arbitrary", "arbitrary"),
        name="gla",
    )(q, k, v, g, a, w_gate2, b_gate, norm_g)


CONV_HALO = 32


def _conv_kernel(u_ref, w_ref, cb_ref, lg_ref, lb_ref, o_ref, buf_ref):
    tm = u_ref.shape[0]
    ch = o_ref.shape[-1]

    @pl.when(pl.program_id(1) == 0)
    def _():
        buf_ref[0:CONV_HALO, :] = jnp.zeros((CONV_HALO, ch), F32)

    buf_ref[CONV_HALO:, :] = u_ref[:, :ch] * jax.nn.sigmoid(u_ref[:, ch:])
    acc = jnp.zeros((tm, ch), F32)
    base = CONV_HALO - (CONV_WIDTH - 1)
    for j in range(CONV_WIDTH):
        acc = acc + buf_ref[base + j:base + j + tm, :] * w_ref[j:j + 1, :]
    buf_ref[0:CONV_HALO, :] = buf_ref[tm:tm + CONV_HALO, :]
    y = acc + cb_ref[...]
    mu = jnp.mean(y, axis=-1, keepdims=True)
    var = jnp.mean(jnp.square(y - mu), axis=-1, keepdims=True)
    y = (y - mu) * lax.rsqrt(var + NORM_EPS) * lg_ref[...] + lb_ref[...]
    o_ref[...] = _silu(y).astype(o_ref.dtype)


def _conv_module(u, conv_w, conv_b, ln_g, ln_b, tm):
    b, s, two_ch = u.shape
    ch = two_ch // 2
    tile = lambda i, j: (i, j, 0)
    const = lambda i, j: (0, 0)
    return pl.pallas_call(
        _conv_kernel,
        out_shape=jax.ShapeDtypeStruct((b, s, ch), BF16),
        grid=(b, s // tm),
        in_specs=[
            pl.BlockSpec((None, tm, two_ch), tile),
            pl.BlockSpec(conv_w.shape, const),
            pl.BlockSpec((1, ch), const),
            pl.BlockSpec((1, ch), const),
            pl.BlockSpec((1, ch), const),
        ],
        out_specs=pl.BlockSpec((None, tm, ch), tile),
        scratch_shapes=[pltpu.VMEM((tm + CONV_HALO, ch), F32)],
        compiler_params=_params("arbitrary", "arbitrary"),
        name="conv_module",
    )(u, conv_w, conv_b, ln_g, ln_b)


def _out_proj_kernel(*refs):
    *a_refs, w_ref, x_ref, gate_ref, o_ref = refs
    acc = None
    off = 0
    for a_ref in a_refs:
        kk = a_ref.shape[-1]
        part = jnp.dot(a_ref[...].astype(BF16), w_ref[off:off + kk, :], preferred_element_type=F32)
        acc = part if acc is None else acc + part
        off += kk
    o_ref[...] = x_ref[...] + gate_ref[...] * acc


def _out_proj(acts, w, x, gate, tm):
    b, s, d = x.shape
    tile = lambda i, j: (i, j, 0)
    return pl.pallas_call(
        _out_proj_kernel,
        out_shape=jax.ShapeDtypeStruct((b, s, d), F32),
        grid=(b, s // tm),
        in_specs=[pl.BlockSpec((None, tm, a.shape[-1]), tile) for a in acts] + [
            pl.BlockSpec(w.shape, lambda i, j: (0, 0)),
            pl.BlockSpec((None, tm, d), tile),
            pl.BlockSpec((None, 1, d), lambda i, j: (i, 0, 0)),
        ],
        out_specs=pl.BlockSpec((None, tm, d), tile),
        compiler_params=_params("arbitrary", "arbitrary"),
        name="out_proj",
    )(*acts, w, x, gate)


def _rope_table_kernel(pos_ref, freq_ref, sign_ref, cos_ref, sin_ref):
    ang = pos_ref[...] * freq_ref[...]
    cos_ref[...] = jnp.cos(ang)
    sin_ref[...] = jnp.sin(ang) * sign_ref[...]


def _rope_tables(positions, head_dim, tm):
    b, s = positions.shape
    rope_dims = head_dim // 4
    half = rope_dims // 2
    inv_freq = ROPE_THETA ** (-jnp.arange(0, rope_dims, 2, dtype=F32) / rope_dims)
    jj = jnp.arange(LANES) % head_dim
    freq = jnp.where(jj < rope_dims, inv_freq[jj % half], 0.0).astype(F32)[None, :]
    sign = jnp.where(jj < half, -1.0, jnp.where(jj < rope_dims, 1.0, 0.0)).astype(F32)[None, :]
    pos = positions.astype(F32)[..., None]
    tile = lambda i, j: (i, j, 0)
    const = lambda i, j: (0, 0)
    return pl.pallas_call(
        _rope_table_kernel,
        out_shape=[jax.ShapeDtypeStruct((b, s, LANES), F32)] * 2,
        grid=(b, s // tm),
        in_specs=[pl.BlockSpec((None, tm, 1), tile), pl.BlockSpec((1, LANES), const),
                  pl.BlockSpec((1, LANES), const)],
        out_specs=[pl.BlockSpec((None, tm, LANES), tile)] * 2,
        compiler_params=_params("arbitrary", "arbitrary"),
        name="rope_tables",
    )(pos, freq, sign)


def _qkv_kernel(x_ref, sh_ref, sc_ref, g_ref, w_ref, cos_ref, sin_ref, q_ref, k_ref, v_ref, *, head_dim):
    h = _modulate(x_ref[...], g_ref[...], sh_ref[...], sc_ref[...]).astype(BF16)
    d = q_ref.shape[-1]
    half = head_dim // 8
    cosf = jnp.tile(cos_ref[...], (1, d // LANES))
    sinf = jnp.tile(sin_ref[...], (1, d // LANES))
    lane = lax.broadcasted_iota(jnp.int32, (1, d), 1)
    first = (lane % head_dim) < half
    for idx, (o_ref, mult) in enumerate(((q_ref, head_dim ** -0.5), (k_ref, 1.0))):
        t = jnp.dot(h, w_ref[:, idx * d:(idx + 1) * d], preferred_element_type=F32)
        partner = jnp.where(first, pltpu.roll(t, d - half, 1), pltpu.roll(t, half, 1))
        o_ref[...] = ((t * cosf + partner * sinf) * mult).astype(o_ref.dtype)
    v_ref[...] = jnp.dot(h, w_ref[:, 2 * d:], preferred_element_type=F32).astype(v_ref.dtype)


def _qkv_rope(x, shift, scale, g, w, cos_t, sin_t, head_dim, tm):
    b, s, d = x.shape
    row = lambda i, j: (i, 0, 0)
    tile = lambda i, j: (i, j, 0)
    return pl.pallas_call(
        functools.partial(_qkv_kernel, head_dim=head_dim),
        out_shape=[jax.ShapeDtypeStruct((b, s, d), BF16)] * 3,
        grid=(b, s // tm),
        in_specs=[
            pl.BlockSpec((None, tm, d), tile),
            pl.BlockSpec((None, 1, d), row),
            pl.BlockSpec((None, 1, d), row),
            pl.BlockSpec((1, d), lambda i, j: (0, 0)),
            pl.BlockSpec(w.shape, lambda i, j: (0, 0)),
            pl.BlockSpec((None, tm, LANES), tile),
            pl.BlockSpec((None, tm, LANES), tile),
        ],
        out_specs=[pl.BlockSpec((None, tm, d), tile)] * 3,
        compiler_params=_params("arbitrary", "arbitrary"),
        name="qkv_rope",
    )(x, shift, scale, g, w, cos_t, sin_t)


def _attn_kernel(q_ref, kc_ref, kp_ref, vc_ref, vp_ref, o_ref, qf, kf, vf, ob, lb, *, head_dim, branches):
    sb = q_ref.shape[0]
    blk = DIL_BLOCK
    heads = q_ref.shape[1] // head_dim
    first_super = pl.program_id(2) == 0

    qf[...] = q_ref[...].astype(F32)
    kf[0:sb, :] = kp_ref[...].astype(F32)
    kf[sb:, :] = kc_ref[...].astype(F32)
    vf[0:sb, :] = vp_ref[...].astype(F32)
    vf[sb:, :] = vc_ref[...].astype(F32)

    qi = lax.broadcasted_iota(jnp.int32, (blk, 2 * blk), 0)
    kj = lax.broadcasted_iota(jnp.int32, (blk, 2 * blk), 1)
    dist = qi + blk - kj

    for bi, (window, dil) in enumerate(branches):
        steps = window // dil
        unit = dil * blk
        band = (dist >= 0) & (dist <= steps)
        bias_band = jnp.where(band, 0.0, NEG_BIG)
        bias_noprev = jnp.where(band & (kj >= blk), 0.0, NEG_BIG)

        def block_body(idx, carry, dil=dil, unit=unit, bi=bi, bias_band=bias_band, bias_noprev=bias_noprev):
            u = idx // dil
            r = idx % dil
            q0 = u * unit + r
            k0 = sb + q0 - unit
            no_prev = first_super & (u == 0)
            bias = jnp.where(no_prev, bias_noprev, bias_band)
            qb = qf[pl.ds(q0, blk, stride=dil), :].astype(BF16)
            kb = kf[pl.ds(k0, 2 * blk, stride=dil), :].astype(BF16)
            vb = vf[pl.ds(k0, 2 * blk, stride=dil), :].astype(BF16)
            outs, lses = [], []
            for h in range(heads):
                hs = slice(h * head_dim, (h + 1) * head_dim)
                s = lax.dot_general(qb[:, hs], kb[:, hs], (((1,), (1,)), ((), ())),
                                    preferred_element_type=F32) + bias
                m = jnp.max(s, axis=-1, keepdims=True)
                p = jnp.exp(s - m)
                l = jnp.sum(p, axis=-1, keepdims=True)
                o = jnp.dot(p.astype(BF16), vb[:, hs], preferred_element_type=F32) / l
                outs.append(o)
                lses.append(jnp.broadcast_to(m + jnp.log(l), (blk, head_dim)))
            ob[bi, pl.ds(q0, blk, stride=dil), :] = jnp.concatenate(outs, axis=-1)
            lb[bi, pl.ds(q0, blk, stride=dil), :] = jnp.concatenate(lses, axis=-1)
            return carry

        lax.fori_loop(0, sb // blk, block_body, 0)

    nb = len(branches)
    m = lb[0]
    for bi in range(1, nb):
        m = jnp.maximum(m, lb[bi])
    num = jnp.zeros_like(m)
    den = jnp.zeros_like(m)
    for bi in range(nb):
        e = jnp.exp(lb[bi] - m)
        num = num + e * ob[bi]
        den = den + e
    o_ref[...] = (num / den).astype(o_ref.dtype)


def _dilated_attention(q, k, v, head_dim, branches, sb):
    b, s, d = q.shape
    groups = d // LANES
    cur = lambda i, g, n: (i, n, g)
    prev = lambda i, g, n: (i, jnp.maximum(n - 1, 0), g)
    nb = len(branches)
    return pl.pallas_call(
        functools.partial(_attn_kernel, head_dim=head_dim, branches=branches),
        out_shape=jax.ShapeDtypeStruct((b, s, d), BF16),
        grid=(b, groups, s // sb),
        in_specs=[
            pl.BlockSpec((None, sb, LANES), cur),
            pl.BlockSpec((None, sb, LANES), cur),
            pl.BlockSpec((None, sb, LANES), prev),
            pl.BlockSpec((None, sb, LANES), cur),
            pl.BlockSpec((None, sb, LANES), prev),
        ],
        out_specs=pl.BlockSpec((None, sb, LANES), cur),
        scratch_shapes=[
            pltpu.VMEM((sb, LANES), F32),
            pltpu.VMEM((2 * sb, LANES), F32),
            pltpu.VMEM((2 * sb, LANES), F32),
            pltpu.VMEM((nb, sb, LANES), F32),
            pltpu.VMEM((nb, sb, LANES), F32),
        ],
        compiler_params=_params("arbitrary", "arbitrary", "arbitrary"),
        name="dilated_attention",
    )(q, k, k, v, v)


def _router_kernel(x_ref, sh_ref, sc_ref, g_ref, w_ref, b_ref, h_ref, ids_ref, wts_ref, *, n_experts, n_groups):
    h = _modulate(x_ref[...], g_ref[...], sh_ref[...], sc_ref[...])
    h_ref[...] = h.astype(h_ref.dtype)
    logits = jnp.dot(h, w_ref[...], preferred_element_type=F32, precision=HIGHEST) + b_ref[...]
    tm = logits.shape[0]
    epg = n_experts // n_groups
    lane = lax.broadcasted_iota(jnp.int32, logits.shape, 1)
    neg = -jnp.inf
    big = jnp.int32(LANES)

    def first_max(mask):
        val = jnp.max(jnp.where(mask, logits, neg), axis=-1, keepdims=True)
        idx = jnp.min(jnp.where(mask & (logits == val), lane, big), axis=-1, keepdims=True)
        return val, idx

    gmask = (lane >= n_experts) & (lane < n_experts + n_groups)
    gmax, gidx = first_max(gmask)
    gsum = jnp.sum(jnp.where(gmask, jnp.exp(logits - gmax), 0.0), axis=-1, keepdims=True)
    g_w = 1.0 / gsum
    grp = gidx - n_experts
    assert epg & (epg - 1) == 0
    emask = (lane < n_experts) & (lax.shift_right_logical(lane, epg.bit_length() - 1) == grp)
    v1, i1 = first_max(emask)
    v2, i2 = first_max(emask & (lane != i1))
    e2 = jnp.exp(v2 - v1)
    den = 1.0 + e2
    col = lax.broadcasted_iota(jnp.int32, (tm, TOP_K), 1)
    ids_ref[...] = jnp.where(col == 0, i1, i2)
    wts_ref[...] = jnp.where(col == 0, 1.0 / den, e2 / den) * g_w


def _router(x, shift, scale, g, w_rt, b_rt, n_experts, n_groups, tm):
    b, s, d = x.shape
    row = lambda i, j: (i, 0, 0)
    tile = lambda i, j: (i, j, 0)
    const = lambda i, j: (0, 0)
    return pl.pallas_call(
        functools.partial(_router_kernel, n_experts=n_experts, n_groups=n_groups),
        out_shape=[jax.ShapeDtypeStruct((b, s, d), BF16),
                   jax.ShapeDtypeStruct((b, s, TOP_K), jnp.int32),
                   jax.ShapeDtypeStruct((b, s, TOP_K), F32)],
        grid=(b, s // tm),
        in_specs=[
            pl.BlockSpec((None, tm, d), tile),
            pl.BlockSpec((None, 1, d), row),
            pl.BlockSpec((None, 1, d), row),
            pl.BlockSpec((1, d), const),
            pl.BlockSpec((d, LANES), const),
            pl.BlockSpec((1, LANES), const),
        ],
        out_specs=[pl.BlockSpec((None, tm, d), tile),
                   pl.BlockSpec((None, tm, TOP_K), tile),
                   pl.BlockSpec((None, tm, TOP_K), tile)],
        compiler_params=_params("arbitrary", "arbitrary"),
        name="moe_router",
    )(x, shift, scale, g, w_rt, b_rt)


def _expert_kernel(be_ref, nb_ref, xs_ref, rw_ref, w1_ref, w3_ref, w2_ref, ys_ref, w1b, w3b, w2b):
    i = pl.program_id(0)
    e = be_ref[i]
    changed = (i == 0) | (e != be_ref[jnp.maximum(i - 1, 0)])

    @pl.when(changed)
    def _():
        w1b[...] = w1_ref[...].astype(BF16)
        w3b[...] = w3_ref[...].astype(BF16)
        w2b[...] = w2_ref[...].astype(BF16)

    @pl.when(i < nb_ref[0])
    def _():
        x = xs_ref[...]
        a = jnp.dot(x, w1b[...], preferred_element_type=F32)
        g = jnp.dot(x, w3b[...], preferred_element_type=F32)
        y = jnp.dot((_silu(a) * g).astype(BF16), w2b[...], preferred_element_type=F32)
        ys_ref[...] = y * rw_ref[...]

    @pl.when(i >= nb_ref[0])
    def _():
        ys_ref[...] = jnp.zeros_like(ys_ref)


def _experts(layer, blk_e, n_used, xs, row_w, w1, w3, w2, rows):
    p, d = xs.shape
    hid = w1.shape[-1]
    blk = lambda i, be, nb: (i, 0)
    wsel = lambda i, be, nb: (layer, be[i], 0, 0)
    grid_spec = pltpu.PrefetchScalarGridSpec(
        num_scalar_prefetch=2,
        grid=(p // rows,),
        in_specs=[
            pl.BlockSpec((rows, d), blk),
            pl.BlockSpec((rows, 1), blk),
            pl.BlockSpec((None, None, d, hid), wsel),
            pl.BlockSpec((None, None, d, hid), wsel),
            pl.BlockSpec((None, None, hid, d), wsel),
        ],
        out_specs=pl.BlockSpec((rows, d), blk),
        scratch_shapes=[pltpu.VMEM((d, hid), BF16), pltpu.VMEM((d, hid), BF16), pltpu.VMEM((hid, d), BF16)],
    )
    return pl.pallas_call(
        _expert_kernel,
        out_shape=jax.ShapeDtypeStruct((p, d), F32),
        grid_spec=grid_spec,
        compiler_params=_params("arbitrary"),
        name="moe_experts",
    )(blk_e, n_used, xs, row_w, w1, w3, w2)


def _dispatch(ids, n_experts, rows):
    e_flat = ids.reshape(-1)
    a = e_flat.shape[0]
    order = jnp.argsort(e_flat, stable=True).astype(jnp.int32)
    e_sorted = e_flat[order]
    counts = jnp.bincount(e_flat, length=n_experts).astype(jnp.int32)
    padded = (counts + rows - 1) // rows * rows
    starts = jnp.cumsum(counts) - counts
    pends = jnp.cumsum(padded)
    pstarts = pends - padded
    dest_sorted = pstarts[e_sorted] + (jnp.arange(a, dtype=jnp.int32) - starts[e_sorted])
    p = a + n_experts * rows
    row_src = jnp.full((p,), a, jnp.int32).at[dest_sorted].set(order)
    dest = jnp.zeros((a,), jnp.int32).at[order].set(dest_sorted)
    n_blk = p // rows
    blk_e = jnp.clip(jnp.searchsorted(pends, jnp.arange(n_blk, dtype=jnp.int32) * rows, side='right'),
                     0, n_experts - 1).astype(jnp.int32)
    n_used = (pends[-1] // rows).astype(jnp.int32).reshape(1)
    return row_src, dest, blk_e, n_used


def _combine_kernel(x_ref, y_ref, gate_ref, o_ref):
    d = x_ref.shape[-1]
    o_ref[...] = x_ref[...] + gate_ref[...] * (y_ref[:, :d] + y_ref[:, d:])


def _combine_norm_kernel(x_ref, y_ref, gate_ref, ng_ref, o_ref):
    d = x_ref.shape[-1]
    x = x_ref[...] + gate_ref[...] * (y_ref[:, :d] + y_ref[:, d:])
    o_ref[...] = x * lax.rsqrt(jnp.mean(x * x, axis=-1, keepdims=True) + NORM_EPS) * ng_ref[...]


def _combine(x, yg, gate, final_g, tm):
    b, s, d = x.shape
    tile = lambda i, j: (i, j, 0)
    in_specs = [
        pl.BlockSpec((None, tm, d), tile),
        pl.BlockSpec((None, tm, TOP_K * d), tile),
        pl.BlockSpec((None, 1, d), lambda i, j: (i, 0, 0)),
    ]
    args = [x, yg, gate]
    kern = _combine_kernel
    if final_g is not None:
        in_specs.append(pl.BlockSpec((1, d), lambda i, j: (0, 0)))
        args.append(final_g)
        kern = _combine_norm_kernel
    return pl.pallas_call(
        kern,
        out_shape=jax.ShapeDtypeStruct((b, s, d), F32),
        grid=(b, s // tm),
        in_specs=in_specs,
        out_specs=pl.BlockSpec((None, tm, d), tile),
        compiler_params=_params("arbitrary", "arbitrary"),
        name="moe_combine",
    )(*args)


def _hier_moe(layer, x, shift, scale, norm_g, gate, w_rt_full, b_rt_full, w1, w3, w2, final_g, tm, rows):
    b, s, d = x.shape
    t = b * s
    n_experts = w1.shape[1]
    h, ids, wts = _router(x, shift, scale, norm_g, w_rt_full, b_rt_full, n_experts, N_GROUPS, tm)
    row_src, dest, blk_e, n_used = _dispatch(ids, n_experts, rows)
    a = t * TOP_K
    row_tok = jnp.minimum(row_src // TOP_K, t - 1)
    row_w = jnp.concatenate([wts.reshape(-1), jnp.zeros((1,), F32)])[row_src][:, None]
    xs = h.reshape(t, d)[row_tok]
    ys = _experts(layer, blk_e, n_used, xs, row_w, w1, w3, w2, rows)
    yg = ys[dest].reshape(b, s, TOP_K * d)
    return _combine(x, yg, gate, final_g, tm)


def _pick_tile(s, pref):
    tm = min(pref, s)
    assert s % tm == 0
    return tm


def kernel(x, c, positions, ada_w, ada_b, norm1_g, norm2_g, even_w_in, even_w_gate2, even_b_gate, even_gla_norm_g, even_conv_w, even_conv_b, even_conv_ln_g, even_conv_ln_b, even_w_out, odd_w_qkv, odd_w_out, moe_w_grp, moe_b_grp, moe_w_rt, moe_b_rt, moe_w1, moe_w3, moe_w2, final_norm_g):
    b, s, d = x.shape
    depth = ada_w.shape[0]
    n_experts = moe_w_rt.shape[-1]
    tm = _pick_tile(s, TOKEN_TILE)
    sb = _pick_tile(s, ATTN_SUPER)
    head_dim = d // ATTN_HEADS
    hk = GLA_HEADS * GLA_DK
    hv = GLA_HEADS * GLA_DV
    conv_ch = d // 2

    mods = _ada_mods(c, ada_w, ada_b)
    mod = lambda l, j: mods[l, j][:, None, :]
    cos_t = sin_t = None

    for layer in range(depth):
        i = layer // 2
        g1 = norm1_g[layer][None, :]
        if layer % 2 == 0:
            w_in = even_w_in[i]
            main = hk + hk + hv + hv
            w_cat = jnp.concatenate([
                w_in[:, :main], w_in[:, main + GLA_GATE_RANK:], w_in[:, main:main + GLA_GATE_RANK],
                jnp.zeros((d, LANES - GLA_GATE_RANK), w_in.dtype)], axis=1).astype(BF16)
            widths = (hk, hk, hv, hv, 2 * conv_ch, LANES)
            dtypes = (F32, F32, BF16, F32, F32, F32)
            q, k, v, g, u, a_lr = _norm_matmul(x, mod(layer, 0), mod(layer, 1), g1, w_cat, widths, dtypes, tm)
            wg = jnp.concatenate([even_w_gate2[i], jnp.zeros((LANES - GLA_GATE_RANK, hk), F32)], axis=0).astype(BF16)
            o_gla = _gla(q, k, v, g, a_lr, wg, even_b_gate[i][None, :], even_gla_norm_g[i][None, :], tm)
            y_conv = _conv_module(u, even_conv_w[i], even_conv_b[i][None, :], even_conv_ln_g[i][None, :],
                                  even_conv_ln_b[i][None, :], tm)
            x = _out_proj([o_gla, y_conv], even_w_out[i].astype(BF16), x, mod(layer, 2), tm)
        else:
            if cos_t is None:
                cos_t, sin_t = _rope_tables(positions, head_dim, tm)
            q, k, v = _qkv_rope(x, mod(layer, 0), mod(layer, 1), g1, odd_w_qkv[i].astype(BF16),
                                cos_t, sin_t, head_dim, tm)
            o = _dilated_attention(q, k, v, head_dim, DILATED_BRANCHES, sb)
            x = _out_proj([o], odd_w_out[i].astype(BF16), x, mod(layer, 2), tm)

        w_rt_full = jnp.concatenate([moe_w_rt[layer], moe_w_grp[layer],
                                     jnp.zeros((d, LANES - n_experts - N_GROUPS), F32)], axis=1)
        b_rt_full = jnp.concatenate([moe_b_rt[layer], moe_b_grp[layer],
                                     jnp.zeros((LANES - n_experts - N_GROUPS,), F32)])[None, :]
        final_g = final_norm_g[None, :] if layer == depth - 1 else None
        x = _hier_moe(layer, x, mod(layer, 3), mod(layer, 4), norm2_g[layer][None, :], mod(layer, 5),
                      w_rt_full, b_rt_full, moe_w1, moe_w3, moe_w2, final_g, tm, MOE_ROWS)
    return x
```

```python
import functools

import jax
import jax.numpy as jnp
import numpy as np
from jax import lax
from jax.experimental import pallas as pl
from jax.experimental.pallas import tpu as pltpu

F32 = jnp.float32
BF16 = jnp.bfloat16
HIGHEST = lax.Precision.HIGHEST

NORM_EPS = 1e-6
GLA_HEADS = 4
GLA_DK = 64
GLA_DV = 128
GLA_GATE_RANK = 16
GLA_TAU = 16.0
GLA_CHUNK = 64
CONV_WIDTH = 31
ATTN_HEADS = 16
DILATED_BRANCHES = ((128, 1), (512, 4), (2048, 16))
DIL_BLOCK = 128
ROPE_THETA = 500000.0
N_GROUPS = 4
EXPERTS_PER_GROUP = 8
TOP_K = 2
ADA_CHUNKS = 6

LANES = 128
VMEM_LIMIT = 56 * 1024 * 1024
TOKEN_TILE = 512
ATTN_SUPER = 2048
MOE_ROWS = 512
NEG_BIG = -1e30


def _params(*sem):
    return pltpu.CompilerParams(dimension_semantics=sem, vmem_limit_bytes=VMEM_LIMIT)


def _silu(x):
    return x * jax.nn.sigmoid(x)


def _modulate(x, g, shift, scale):
    y = x * lax.rsqrt(jnp.mean(x * x, axis=-1, keepdims=True) + NORM_EPS)
    return (y * g) * (1.0 + scale) + shift


def _ada_kernel(c_ref, w_ref, b_ref, o_ref):
    cond = _silu(c_ref[...])
    o_ref[...] = jnp.dot(cond, w_ref[...], preferred_element_type=F32, precision=HIGHEST) + b_ref[...]


def _ada_mods(c, ada_w, ada_b):
    depth, d, _ = ada_w.shape
    b = c.shape[0]
    return pl.pallas_call(
        _ada_kernel,
        out_shape=jax.ShapeDtypeStruct((depth, ADA_CHUNKS, b, d), F32),
        grid=(depth, ADA_CHUNKS),
        in_specs=[
            pl.BlockSpec((b, d), lambda l, j: (0, 0)),
            pl.BlockSpec((None, d, d), lambda l, j: (l, 0, j)),
            pl.BlockSpec((None, None, 1, d), lambda l, j: (l, j, 0, 0)),
        ],
        out_specs=pl.BlockSpec((None, None, b, d), lambda l, j: (l, j, 0, 0)),
        compiler_params=_params("arbitrary", "arbitrary"),
        name="ada_mods",
    )(c, ada_w, ada_b.reshape(depth, ADA_CHUNKS, 1, d))


def _norm_matmul_kernel(x_ref, sh_ref, sc_ref, g_ref, w_ref, *o_refs):
    h = _modulate(x_ref[...], g_ref[...], sh_ref[...], sc_ref[...]).astype(BF16)
    off = 0
    for o_ref in o_refs:
        n = o_ref.shape[-1]
        o_ref[...] = jnp.dot(h, w_ref[:, off:off + n], preferred_element_type=F32).astype(o_ref.dtype)
        off += n


def _norm_matmul(x, shift, scale, g, w, widths, dtypes, tm):
    b, s, d = x.shape
    row = lambda i, j: (i, 0, 0)
    tile = lambda i, j: (i, j, 0)
    return pl.pallas_call(
        _norm_matmul_kernel,
        out_shape=[jax.ShapeDtypeStruct((b, s, n), dt) for n, dt in zip(widths, dtypes)],
        grid=(b, s // tm),
        in_specs=[
            pl.BlockSpec((None, tm, d), tile),
            pl.BlockSpec((None, 1, d), row),
            pl.BlockSpec((None, 1, d), row),
            pl.BlockSpec((1, d), lambda i, j: (0, 0)),
            pl.BlockSpec(w.shape, lambda i, j: (0, 0)),
        ],
        out_specs=[pl.BlockSpec((None, tm, n), tile) for n in widths],
        compiler_params=_params("arbitrary", "arbitrary"),
        name="norm_matmul",
    )(x, shift, scale, g, w)


def _log_sigmoid(z):
    return jnp.minimum(z, 0.0) - jnp.log1p(jnp.exp(-jnp.abs(z)))


def _gla_kernel(q_ref, k_ref, v_ref, g_ref, a_ref, wg_ref, bg_ref, ng_ref, o_ref, state_ref, la_ref, oacc_ref):
    tm = q_ref.shape[0]
    c = GLA_CHUNK

    @pl.when(pl.program_id(1) == 0)
    def _():
        state_ref[...] = jnp.zeros_like(state_ref)

    z = jnp.dot(a_ref[...].astype(BF16), wg_ref[...], preferred_element_type=F32) + bg_ref[...]
    la_ref[...] = _log_sigmoid(z) * (1.0 / GLA_TAU)

    ri = lax.broadcasted_iota(jnp.int32, (c, c), 0)
    ci = lax.broadcasted_iota(jnp.int32, (c, c), 1)
    causal = ri >= ci
    tril = causal.astype(F32)
    ones = jnp.ones((c, GLA_DV), F32)

    def chunk_body(ic, carry):
        r0 = pl.multiple_of(ic * c, c)
        rows = pl.ds(r0, c)
        la = la_ref[rows, :]
        bcum = jnp.dot(tril, la, preferred_element_type=F32, precision=HIGHEST)
        b_last = bcum[c - 1:c, :]
        q = q_ref[rows, :] * (GLA_DK ** -0.5)
        k = k_ref[rows, :]
        q_dec = (q * jnp.exp(bcum)).astype(BF16)
        k_dec = (k * jnp.exp(-bcum)).astype(BF16)
        k_rem = (k * jnp.exp(b_last - bcum)).astype(BF16)
        dec = jnp.exp(lax.dot_general(la, ones, (((0,), (0,)), ((), ())),
                                      preferred_element_type=F32, precision=HIGHEST))
        for h in range(GLA_HEADS):
            ks = slice(h * GLA_DK, (h + 1) * GLA_DK)
            vs = slice(h * GLA_DV, (h + 1) * GLA_DV)
            vh = v_ref[rows, vs]
            att = lax.dot_general(q_dec[:, ks], k_dec[:, ks], (((1,), (1,)), ((), ())),
                                  preferred_element_type=F32)
            att = jnp.where(causal, att, 0.0).astype(BF16)
            st = state_ref[h]
            o = jnp.dot(att, vh, preferred_element_type=F32)
            o = o + jnp.dot(q_dec[:, ks], st.astype(BF16), preferred_element_type=F32)
            kv = lax.dot_general(k_rem[:, ks], vh, (((0,), (0,)), ((), ())), preferred_element_type=F32)
            state_ref[h] = dec[ks, :] * st + kv
            oacc_ref[rows, vs] = o
        return carry

    lax.fori_loop(0, tm // c, chunk_body, 0)

    for h in range(GLA_HEADS):
        vs = slice(h * GLA_DV, (h + 1) * GLA_DV)
        o = oacc_ref[:, vs]
        o = o * lax.rsqrt(jnp.mean(o * o, axis=-1, keepdims=True) + NORM_EPS) * ng_ref[...]
        o_ref[:, vs] = (o * _silu(g_ref[:, vs])).astype(o_ref.dtype)


def _gla(q, k, v, g, a, w_gate2, b_gate, norm_g, tm):
    b, s, _ = q.shape
    hk = GLA_HEADS * GLA_DK
    hv = GLA_HEADS * GLA_DV
    tile = lambda i, j: (i, j, 0)
    const = lambda i, j: (0, 0)
    return pl.pallas_call(
        _gla_kernel,
        out_shape=jax.ShapeDtypeStruct((b, s, hv), BF16),
        grid=(b, s // tm),
        in_specs=[
            pl.BlockSpec((None, tm, hk), tile),
            pl.BlockSpec((None, tm, hk), tile),
            pl.BlockSpec((None, tm, hv), tile),
            pl.BlockSpec((None, tm, hv), tile),
            pl.BlockSpec((None, tm, a.shape[-1]), tile),
            pl.BlockSpec(w_gate2.shape, const),
            pl.BlockSpec((1, hk), const),
            pl.BlockSpec((1, GLA_DV), const),
        ],
        out_specs=pl.BlockSpec((None, tm, hv), tile),
        scratch_shapes=[
            pltpu.VMEM((GLA_HEADS, GLA_DK, GLA_DV), F32),
            pltpu.VMEM((tm, hk), F32),
            pltpu.VMEM((tm, hv), F32),
        ],
        compiler_params=_params("arbitrary", "arbitrary"),
        name="gla",
    )(q, k, v, g, a, w_gate2, b_gate, norm_g)


CONV_HALO = 32


def _conv_kernel(u_ref, w_ref, cb_ref, lg_ref, lb_ref, o_ref, buf_ref):
    tm = u_ref.shape[0]
    ch = o_ref.shape[-1]

    @pl.when(pl.program_id(1) == 0)
    def _():
        buf_ref[0:CONV_HALO, :] = jnp.zeros((CONV_HALO, ch), F32)

    buf_ref[CONV_HALO:, :] = u_ref[:, :ch] * jax.nn.sigmoid(u_ref[:, ch:])
    acc = jnp.zeros((tm, ch), F32)
    base = CONV_HALO - (CONV_WIDTH - 1)
    for j in range(CONV_WIDTH):
        acc = acc + buf_ref[base + j:base + j + tm, :] * w_ref[j:j + 1, :]
    buf_ref[0:CONV_HALO, :] = buf_ref[tm:tm + CONV_HALO, :]
    y = acc + cb_ref[...]
    mu = jnp.mean(y, axis=-1, keepdims=True)
    var = jnp.mean(jnp.square(y - mu), axis=-1, keepdims=True)
    y = (y - mu) * lax.rsqrt(var + NORM_EPS) * lg_ref[...] + lb_ref[...]
    o_ref[...] = _silu(y).astype(o_ref.dtype)


def _conv_module(u, conv_w, conv_b, ln_g, ln_b, tm):
    b, s, two_ch = u.shape
    ch = two_ch // 2
    tile = lambda i, j: (i, j, 0)
    const = lambda i, j: (0, 0)
    return pl.pallas_call(
        _conv_kernel,
        out_shape=jax.ShapeDtypeStruct((b, s, ch), BF16),
        grid=(b, s // tm),
        in_specs=[
            pl.BlockSpec((None, tm, two_ch), tile),
            pl.BlockSpec(conv_w.shape, const),
            pl.BlockSpec((1, ch), const),
            pl.BlockSpec((1, ch), const),
            pl.BlockSpec((1, ch), const),
        ],
        out_specs=pl.BlockSpec((None, tm, ch), tile),
        scratch_shapes=[pltpu.VMEM((tm + CONV_HALO, ch), F32)],
        compiler_params=_params("arbitrary", "arbitrary"),
        name="conv_module",
    )(u, conv_w, conv_b, ln_g, ln_b)


def _out_proj_kernel(*refs):
    *a_refs, w_ref, x_ref, gate_ref, o_ref = refs
    acc = None
    off = 0
    for a_ref in a_refs:
        kk = a_ref.shape[-1]
        part = jnp.dot(a_ref[...].astype(BF16), w_ref[off:off + kk, :], preferred_element_type=F32)
        acc = part if acc is None else acc + part
        off += kk
    o_ref[...] = x_ref[...] + gate_ref[...] * acc


def _out_proj(acts, w, x, gate, tm):
    b, s, d = x.shape
    tile = lambda i, j: (i, j, 0)
    return pl.pallas_call(
        _out_proj_kernel,
        out_shape=jax.ShapeDtypeStruct((b, s, d), F32),
        grid=(b, s // tm),
        in_specs=[pl.BlockSpec((None, tm, a.shape[-1]), tile) for a in acts] + [
            pl.BlockSpec(w.shape, lambda i, j: (0, 0)),
            pl.BlockSpec((None, tm, d), tile),
            pl.BlockSpec((None, 1, d), lambda i, j: (i, 0, 0)),
        ],
        out_specs=pl.BlockSpec((None, tm, d), tile),
        compiler_params=_params("arbitrary", "arbitrary"),
        name="out_proj",
    )(*acts, w, x, gate)


def _rope_table_kernel(pos_ref, freq_ref, sign_ref, cos_ref, sin_ref):
    ang = pos_ref[...] * freq_ref[...]
    cos_ref[...] = jnp.cos(ang)
    sin_ref[...] = jnp.sin(ang) * sign_ref[...]


def _rope_tables(positions, head_dim, tm):
    b, s = positions.shape
    rope_dims = head_dim // 4
    half = rope_dims // 2
    inv_freq = ROPE_THETA ** (-jnp.arange(0, rope_dims, 2, dtype=F32) / rope_dims)
    jj = jnp.arange(LANES) % head_dim
    freq = jnp.where(jj < rope_dims, inv_freq[jj % half], 0.0).astype(F32)[None, :]
    sign = jnp.where(jj < half, -1.0, jnp.where(jj < rope_dims, 1.0, 0.0)).astype(F32)[None, :]
    pos = positions.astype(F32)[..., None]
    tile = lambda i, j: (i, j, 0)
    const = lambda i, j: (0, 0)
    return pl.pallas_call(
        _rope_table_kernel,
        out_shape=[jax.ShapeDtypeStruct((b, s, LANES), F32)] * 2,
        grid=(b, s // tm),
        in_specs=[pl.BlockSpec((None, tm, 1), tile), pl.BlockSpec((1, LANES), const),
                  pl.BlockSpec((1, LANES), const)],
        out_specs=[pl.BlockSpec((None, tm, LANES), tile)] * 2,
        compiler_params=_params("arbitrary", "arbitrary"),
        name="rope_tables",
    )(pos, freq, sign)


def _qkv_kernel(x_ref, sh_ref, sc_ref, g_ref, w_ref, cos_ref, sin_ref, q_ref, k_ref, v_ref, *, head_dim):
    h = _modulate(x_ref[...], g_ref[...], sh_ref[...], sc_ref[...]).astype(BF16)
    d = q_ref.shape[-1]
    half = head_dim // 8
    cosf = jnp.tile(cos_ref[...], (1, d // LANES))
    sinf = jnp.tile(sin_ref[...], (1, d // LANES))
    lane = lax.broadcasted_iota(jnp.int32, (1, d), 1)
    first = (lane % head_dim) < half
    for idx, (o_ref, mult) in enumerate(((q_ref, head_dim ** -0.5), (k_ref, 1.0))):
        t = jnp.dot(h, w_ref[:, idx * d:(idx + 1) * d], preferred_element_type=F32)
        partner = jnp.where(first, pltpu.roll(t, d - half, 1), pltpu.roll(t, half, 1))
        o_ref[...] = ((t * cosf + partner * sinf) * mult).astype(o_ref.dtype)
    v_ref[...] = jnp.dot(h, w_ref[:, 2 * d:], preferred_element_type=F32).astype(v_ref.dtype)


def _qkv_rope(x, shift, scale, g, w, cos_t, sin_t, head_dim, tm):
    b, s, d = x.shape
    row = lambda i, j: (i, 0, 0)
    tile = lambda i, j: (i, j, 0)
    return pl.pallas_call(
        functools.partial(_qkv_kernel, head_dim=head_dim),
        out_shape=[jax.ShapeDtypeStruct((b, s, d), BF16)] * 3,
        grid=(b, s // tm),
        in_specs=[
            pl.BlockSpec((None, tm, d), tile),
            pl.BlockSpec((None, 1, d), row),
            pl.BlockSpec((None, 1, d), row),
            pl.BlockSpec((1, d), lambda i, j: (0, 0)),
            pl.BlockSpec(w.shape, lambda i, j: (0, 0)),
            pl.BlockSpec((None, tm, LANES), tile),
            pl.BlockSpec((None, tm, LANES), tile),
        ],
        out_specs=[pl.BlockSpec((None, tm, d), tile)] * 3,
        compiler_params=_params("arbitrary", "arbitrary"),
        name="qkv_rope",
    )(x, shift, scale, g, w, cos_t, sin_t)


ATTN_UNROLL = 8


def _attn_bias(branches, heads):
    blk = DIL_BLOCK
    kj = np.arange(2 * blk)[:, None]
    qi = np.arange(blk)[None, :]
    dist = qi + blk - kj
    out = []
    for window, dil in branches:
        band = (dist >= 0) & (dist <= window // dil)
        both = np.stack([band, band & (kj >= blk)])
        out.append(np.tile(np.where(both, 0.0, NEG_BIG), (1, 1, heads)))
    return jnp.asarray(np.stack(out), F32)


def _attn_kernel(bias_ref, q_ref, kc_ref, kp_ref, vc_ref, vp_ref, o_ref, qf, kf, vf, ob, lb, st_s, vt_s,
                 *, head_dim, branches):
    sb = q_ref.shape[0]
    blk = DIL_BLOCK
    heads = q_ref.shape[1] // head_dim
    first_super = pl.program_id(2) == 0

    qf[...] = q_ref[...].astype(F32)
    kf[0:sb, :] = kp_ref[...].astype(F32)
    kf[sb:, :] = kc_ref[...].astype(F32)
    vf[0:sb, :] = vp_ref[...].astype(F32)
    vf[sb:, :] = vc_ref[...].astype(F32)

    lane = lax.broadcasted_iota(jnp.int32, (blk, LANES), 1)
    head_masks = [(lane >= h * head_dim) & (lane < (h + 1) * head_dim) for h in range(heads)]

    for bi, (window, dil) in enumerate(branches):
        assert dil & (dil - 1) == 0 and window // dil <= blk
        unit = dil * blk
        shift = dil.bit_length() - 1

        def scores(j, idx, dil=dil, unit=unit, bi=bi, shift=shift):
            u = lax.shift_right_logical(idx, shift)
            r = idx & (dil - 1)
            q0 = u * unit + r
            k0 = sb + q0 - unit
            no_prev = jnp.where(first_super & (u == 0), 1, 0)
            qb = qf[pl.ds(q0, blk, stride=dil), :]
            q2 = jnp.concatenate([jnp.where(mk, qb, 0.0) for mk in head_masks], axis=0).astype(BF16)
            kb = kf[pl.ds(k0, 2 * blk, stride=dil), :].astype(BF16)
            vt_s[j] = vf[pl.ds(k0, 2 * blk, stride=dil), :].T.astype(BF16)
            st = lax.dot_general(kb, q2, (((1,), (1,)), ((), ())), preferred_element_type=F32)
            st_s[j] = st + bias_ref[bi, no_prev]
            return q0

        def softmax_pv(j):
            m = jnp.max(st_s[j], axis=0, keepdims=True)
            p = jnp.exp(st_s[j] - m)
            l = jnp.sum(p, axis=0, keepdims=True)
            of = jnp.dot(vt_s[j], p.astype(BF16), preferred_element_type=F32)
            lse = m + jnp.log(l)
            o_rows, lse_rows = [], []
            for h in range(heads):
                cols = slice(h * blk, (h + 1) * blk)
                o_rows.append(of[h * head_dim:(h + 1) * head_dim, cols] / l[:, cols])
                lse_rows.append(jnp.broadcast_to(lse[:, cols], (head_dim, blk)))
            return jnp.concatenate(o_rows, axis=0).T, jnp.concatenate(lse_rows, axis=0).T

        def block_body(it, carry, scores=scores, softmax_pv=softmax_pv, dil=dil, bi=bi):
            starts = [scores(j, it * ATTN_UNROLL + j) for j in range(ATTN_UNROLL)]
            done = [softmax_pv(j) for j in range(ATTN_UNROLL)]
            for q0, (o_tok, lse_tok) in zip(starts, done):
                ob[bi, pl.ds(q0, blk, stride=dil), :] = o_tok
                lb[bi, pl.ds(q0, blk, stride=dil), :] = lse_tok
            return carry

        lax.fori_loop(0, sb // blk // ATTN_UNROLL, block_body, 0)

    nb = len(branches)
    m = lb[0]
    for bi in range(1, nb):
        m = jnp.maximum(m, lb[bi])
    num = jnp.zeros_like(m)
    den = jnp.zeros_like(m)
    for bi in range(nb):
        e = jnp.exp(lb[bi] - m)
        num = num + e * ob[bi]
        den = den + e
    o_ref[...] = (num / den).astype(o_ref.dtype)


def _dilated_attention(q, k, v, head_dim, branches, sb):
    b, s, d = q.shape
    groups = d // LANES
    cur = lambda i, g, n: (i, n, g)
    prev = lambda i, g, n: (i, jnp.maximum(n - 1, 0), g)
    nb = len(branches)
    bias = _attn_bias(branches, LANES // head_dim)
    return pl.pallas_call(
        functools.partial(_attn_kernel, head_dim=head_dim, branches=branches),
        out_shape=jax.ShapeDtypeStruct((b, s, d), BF16),
        grid=(b, groups, s // sb),
        in_specs=[
            pl.BlockSpec(bias.shape, lambda i, g, n: (0, 0, 0, 0)),
            pl.BlockSpec((None, sb, LANES), cur),
            pl.BlockSpec((None, sb, LANES), cur),
            pl.BlockSpec((None, sb, LANES), prev),
            pl.BlockSpec((None, sb, LANES), cur),
            pl.BlockSpec((None, sb, LANES), prev),
        ],
        out_specs=pl.BlockSpec((None, sb, LANES), cur),
        scratch_shapes=[
            pltpu.VMEM((sb, LANES), F32),
            pltpu.VMEM((2 * sb, LANES), F32),
            pltpu.VMEM((2 * sb, LANES), F32),
            pltpu.VMEM((nb, sb, LANES), F32),
            pltpu.VMEM((nb, sb, LANES), F32),
            pltpu.VMEM((ATTN_UNROLL, 2 * DIL_BLOCK, LANES // head_dim * DIL_BLOCK), F32),
            pltpu.VMEM((ATTN_UNROLL, LANES, 2 * DIL_BLOCK), BF16),
        ],
        compiler_params=_params("arbitrary", "arbitrary", "arbitrary"),
        name="dilated_attention",
    )(bias, q, k, k, v, v)


def _router_kernel(x_ref, sh_ref, sc_ref, g_ref, w_ref, b_ref, h_ref, ids_ref, wts_ref, *, n_experts, n_groups):
    h = _modulate(x_ref[...], g_ref[...], sh_ref[...], sc_ref[...])
    h_ref[...] = h.astype(h_ref.dtype)
    logits = jnp.dot(h, w_ref[...], preferred_element_type=F32, precision=HIGHEST) + b_ref[...]
    tm = logits.shape[0]
    epg = n_experts // n_groups
    lane = lax.broadcasted_iota(jnp.int32, logits.shape, 1)
    neg = -jnp.inf
    big = jnp.int32(LANES)

    def first_max(mask):
        val = jnp.max(jnp.where(mask, logits, neg), axis=-1, keepdims=True)
        idx = jnp.min(jnp.where(mask & (logits == val), lane, big), axis=-1, keepdims=True)
        return val, idx

    gmask = (lane >= n_experts) & (lane < n_experts + n_groups)
    gmax, gidx = first_max(gmask)
    gsum = jnp.sum(jnp.where(gmask, jnp.exp(logits - gmax), 0.0), axis=-1, keepdims=True)
    g_w = 1.0 / gsum
    grp = gidx - n_experts
    assert epg & (epg - 1) == 0
    emask = (lane < n_experts) & (lax.shift_right_logical(lane, epg.bit_length() - 1) == grp)
    v1, i1 = first_max(emask)
    v2, i2 = first_max(emask & (lane != i1))
    e2 = jnp.exp(v2 - v1)
    den = 1.0 + e2
    col = lax.broadcasted_iota(jnp.int32, (tm, TOP_K), 1)
    ids_ref[...] = jnp.where(col == 0, i1, i2)
    wts_ref[...] = jnp.where(col == 0, 1.0 / den, e2 / den) * g_w


def _router(x, shift, scale, g, w_rt, b_rt, n_experts, n_groups, tm):
    b, s, d = x.shape
    row = lambda i, j: (i, 0, 0)
    tile = lambda i, j: (i, j, 0)
    const = lambda i, j: (0, 0)
    return pl.pallas_call(
        functools.partial(_router_kernel, n_experts=n_experts, n_groups=n_groups),
        out_shape=[jax.ShapeDtypeStruct((b, s, d), BF16),
                   jax.ShapeDtypeStruct((b, s, TOP_K), jnp.int32),
                   jax.ShapeDtypeStruct((b, s, TOP_K), F32)],
        grid=(b, s // tm),
        in_specs=[
            pl.BlockSpec((None, tm, d), tile),
            pl.BlockSpec((None, 1, d), row),
            pl.BlockSpec((None, 1, d), row),
            pl.BlockSpec((1, d), const),
            pl.BlockSpec((d, LANES), const),
            pl.BlockSpec((1, LANES), const),
        ],
        out_specs=[pl.BlockSpec((None, tm, d), tile),
                   pl.BlockSpec((None, tm, TOP_K), tile),
                   pl.BlockSpec((None, tm, TOP_K), tile)],
        compiler_params=_params("arbitrary", "arbitrary"),
        name="moe_router",
    )(x, shift, scale, g, w_rt, b_rt)


def _expert_kernel(be_ref, nb_ref, xs_ref, rw_ref, w1_ref, w3_ref, w2_ref, ys_ref, w1b, w3b, w2b):
    i = pl.program_id(0)
    e = be_ref[i]
    changed = (i == 0) | (e != be_ref[jnp.maximum(i - 1, 0)])

    @pl.when(changed)
    def _():
        w1b[...] = w1_ref[...].astype(BF16)
        w3b[...] = w3_ref[...].astype(BF16)
        w2b[...] = w2_ref[...].astype(BF16)

    @pl.when(i < nb_ref[0])
    def _():
        x = xs_ref[...]
        a = jnp.dot(x, w1b[...], preferred_element_type=F32)
        g = jnp.dot(x, w3b[...], preferred_element_type=F32)
        y = jnp.dot((_silu(a) * g).astype(BF16), w2b[...], preferred_element_type=F32)
        ys_ref[...] = y * rw_ref[...]

    @pl.when(i >= nb_ref[0])
    def _():
        ys_ref[...] = jnp.zeros_like(ys_ref)


def _experts(layer, blk_e, n_used, xs, row_w, w1, w3, w2, rows):
    p, d = xs.shape
    hid = w1.shape[-1]
    blk = lambda i, be, nb: (i, 0)
    wsel = lambda i, be, nb: (layer, be[i], 0, 0)
    grid_spec = pltpu.PrefetchScalarGridSpec(
        num_scalar_prefetch=2,
        grid=(p // rows,),
        in_specs=[
            pl.BlockSpec((rows, d), blk),
            pl.BlockSpec((rows, 1), blk),
            pl.BlockSpec((None, None, d, hid), wsel),
            pl.BlockSpec((None, None, d, hid), wsel),
            pl.BlockSpec((None, None, hid, d), wsel),
        ],
        out_specs=pl.BlockSpec((rows, d), blk),
        scratch_shapes=[pltpu.VMEM((d, hid), BF16), pltpu.VMEM((d, hid), BF16), pltpu.VMEM((hid, d), BF16)],
    )
    return pl.pallas_call(
        _expert_kernel,
        out_shape=jax.ShapeDtypeStruct((p, d), F32),
        grid_spec=grid_spec,
        compiler_params=_params("arbitrary"),
        name="moe_experts",
    )(blk_e, n_used, xs, row_w, w1, w3, w2)


def _dispatch(ids, n_experts, rows):
    e_flat = ids.reshape(-1)
    a = e_flat.shape[0]
    order = jnp.argsort(e_flat, stable=True).astype(jnp.int32)
    e_sorted = e_flat[order]
    counts = jnp.bincount(e_flat, length=n_experts).astype(jnp.int32)
    padded = (counts + rows - 1) // rows * rows
    starts = jnp.cumsum(counts) - counts
    pends = jnp.cumsum(padded)
    pstarts = pends - padded
    dest_sorted = pstarts[e_sorted] + (jnp.arange(a, dtype=jnp.int32) - starts[e_sorted])
    p = a + n_experts * rows
    row_src = jnp.full((p,), a, jnp.int32).at[dest_sorted].set(order)
    dest = jnp.zeros((a,), jnp.int32).at[order].set(dest_sorted)
    n_blk = p // rows
    blk_e = jnp.clip(jnp.searchsorted(pends, jnp.arange(n_blk, dtype=jnp.int32) * rows, side='right'),
                     0, n_experts - 1).astype(jnp.int32)
    n_used = (pends[-1] // rows).astype(jnp.int32).reshape(1)
    return row_src, dest, blk_e, n_used


def _combine_kernel(x_ref, y_ref, gate_ref, o_ref):
    d = x_ref.shape[-1]
    o_ref[...] = x_ref[...] + gate_ref[...] * (y_ref[:, :d] + y_ref[:, d:])


def _combine_norm_kernel(x_ref, y_ref, gate_ref, ng_ref, o_ref):
    d = x_ref.shape[-1]
    x = x_ref[...] + gate_ref[...] * (y_ref[:, :d] + y_ref[:, d:])
    o_ref[...] = x * lax.rsqrt(jnp.mean(x * x, axis=-1, keepdims=True) + NORM_EPS) * ng_ref[...]


def _combine(x, yg, gate, final_g, tm):
    b, s, d = x.shape
    tile = lambda i, j: (i, j, 0)
    in_specs = [
        pl.BlockSpec((None, tm, d), tile),
        pl.BlockSpec((None, tm, TOP_K * d), tile),
        pl.BlockSpec((None, 1, d), lambda i, j: (i, 0, 0)),
    ]
    args = [x, yg, gate]
    kern = _combine_kernel
    if final_g is not None:
        in_specs.append(pl.BlockSpec((1, d), lambda i, j: (0, 0)))
        args.append(final_g)
        kern = _combine_norm_kernel
    return pl.pallas_call(
        kern,
        out_shape=jax.ShapeDtypeStruct((b, s, d), F32),
        grid=(b, s // tm),
        in_specs=in_specs,
        out_specs=pl.BlockSpec((None, tm, d), tile),
        compiler_params=_params("arbitrary", "arbitrary"),
        name="moe_combine",
    )(*args)


def _hier_moe(layer, x, shift, scale, norm_g, gate, w_rt_full, b_rt_full, w1, w3, w2, final_g, tm, rows):
    b, s, d = x.shape
    t = b * s
    n_experts = w1.shape[1]
    h, ids, wts = _router(x, shift, scale, norm_g, w_rt_full, b_rt_full, n_experts, N_GROUPS, tm)
    row_src, dest, blk_e, n_used = _dispatch(ids, n_experts, rows)
    a = t * TOP_K
    row_tok = jnp.minimum(row_src // TOP_K, t - 1)
    row_w = jnp.concatenate([wts.reshape(-1), jnp.zeros((1,), F32)])[row_src][:, None]
    xs = h.reshape(t, d)[row_tok]
    ys = _experts(layer, blk_e, n_used, xs, row_w, w1, w3, w2, rows)
    yg = ys[dest].reshape(b, s, TOP_K * d)
    return _combine(x, yg, gate, final_g, tm)


def _pick_tile(s, pref):
    tm = min(pref, s)
    assert s % tm == 0
    return tm


def kernel(x, c, positions, ada_w, ada_b, norm1_g, norm2_g, even_w_in, even_w_gate2, even_b_gate, even_gla_norm_g, even_conv_w, even_conv_b, even_conv_ln_g, even_conv_ln_b, even_w_out, odd_w_qkv, odd_w_out, moe_w_grp, moe_b_grp, moe_w_rt, moe_b_rt, moe_w1, moe_w3, moe_w2, final_norm_g):
    b, s, d = x.shape
    depth = ada_w.shape[0]
    n_experts = moe_w_rt.shape[-1]
    tm = _pick_tile(s, TOKEN_TILE)
    sb = _pick_tile(s, ATTN_SUPER)
    head_dim = d // ATTN_HEADS
    hk = GLA_HEADS * GLA_DK
    hv = GLA_HEADS * GLA_DV
    conv_ch = d // 2

    mods = _ada_mods(c, ada_w, ada_b)
    mod = lambda l, j: mods[l, j][:, None, :]
    cos_t = sin_t = None

    for layer in range(depth):
        i = layer // 2
        g1 = norm1_g[layer][None, :]
        if layer % 2 == 0:
            w_in = even_w_in[i]
            main = hk + hk + hv + hv
            w_cat = jnp.concatenate([
                w_in[:, :main], w_in[:, main + GLA_GATE_RANK:], w_in[:, main:main + GLA_GATE_RANK],
                jnp.zeros((d, LANES - GLA_GATE_RANK), w_in.dtype)], axis=1).astype(BF16)
            widths = (hk, hk, hv, hv, 2 * conv_ch, LANES)
            dtypes = (F32, F32, BF16, F32, F32, F32)
            q, k, v, g, u, a_lr = _norm_matmul(x, mod(layer, 0), mod(layer, 1), g1, w_cat, widths, dtypes, tm)
            wg = jnp.concatenate([even_w_gate2[i], jnp.zeros((LANES - GLA_GATE_RANK, hk), F32)], axis=0).astype(BF16)
            o_gla = _gla(q, k, v, g, a_lr, wg, even_b_gate[i][None, :], even_gla_norm_g[i][None, :], tm)
            y_conv = _conv_module(u, even_conv_w[i], even_conv_b[i][None, :], even_conv_ln_g[i][None, :],
                                  even_conv_ln_b[i][None, :], tm)
            x = _out_proj([o_gla, y_conv], even_w_out[i].astype(BF16), x, mod(layer, 2), tm)
        else:
            if cos_t is None:
                cos_t, sin_t = _rope_tables(positions, head_dim, tm)
            q, k, v = _qkv_rope(x, mod(layer, 0), mod(layer, 1), g1, odd_w_qkv[i].astype(BF16),
                                cos_t, sin_t, head_dim, tm)
            o = _dilated_attention(q, k, v, head_dim, DILATED_BRANCHES, sb)
            x = _out_proj([o], odd_w_out[i].astype(BF16), x, mod(layer, 2), tm)

        w_rt_full = jnp.concatenate([moe_w_rt[layer], moe_w_grp[layer],
                                     jnp.zeros((d, LANES - n_experts - N_GROUPS), F32)], axis=1)
        b_rt_full = jnp.concatenate([moe_b_rt[layer], moe_b_grp[layer],
                                     jnp.zeros((LANES - n_experts - N_GROUPS,), F32)])[None, :]
        final_g = final_norm_g[None, :] if layer == depth - 1 else None
        x = _hier_moe(layer, x, mod(layer, 3), mod(layer, 4), norm2_g[layer][None, :], mod(layer, 5),
                      w_rt_full, b_rt_full, moe_w1, moe_w3, moe_w2, final_g, tm, MOE_ROWS)
    return x
```

```python
import functools

import jax
import jax.numpy as jnp
import numpy as np
from jax import lax
from jax.experimental import pallas as pl
from jax.experimental.pallas import tpu as pltpu

F32 = jnp.float32
BF16 = jnp.bfloat16
HIGHEST = lax.Precision.HIGHEST

NORM_EPS = 1e-6
GLA_HEADS = 4
GLA_DK = 64
GLA_DV = 128
GLA_GATE_RANK = 16
GLA_TAU = 16.0
GLA_CHUNK = 64
CONV_WIDTH = 31
ATTN_HEADS = 16
DILATED_BRANCHES = ((128, 1), (512, 4), (2048, 16))
DIL_BLOCK = 128
ROPE_THETA = 500000.0
N_GROUPS = 4
EXPERTS_PER_GROUP = 8
TOP_K = 2
ADA_CHUNKS = 6

LANES = 128
SUBLANES = 8
VMEM_LIMIT = 56 * 1024 * 1024
TOKEN_TILE = 512
ATTN_SUPER = 2048
MOE_ROWS = 512
NEG_BIG = -1e30


def _params(*sem):
    return pltpu.CompilerParams(dimension_semantics=sem, vmem_limit_bytes=VMEM_LIMIT)


def _silu(x):
    return x * jax.nn.sigmoid(x)


def _modulate(x, g, shift, scale):
    y = x * lax.rsqrt(jnp.mean(x * x, axis=-1, keepdims=True) + NORM_EPS)
    return (y * g) * (1.0 + scale) + shift


def _ada_kernel(c_ref, w_ref, b_ref, o_ref):
    cond = _silu(c_ref[...])
    o_ref[...] = jnp.dot(cond, w_ref[...], preferred_element_type=F32, precision=HIGHEST) + b_ref[...]


def _ada_mods(c, ada_w, ada_b):
    depth, d, _ = ada_w.shape
    b = c.shape[0]
    return pl.pallas_call(
        _ada_kernel,
        out_shape=jax.ShapeDtypeStruct((depth, ADA_CHUNKS, b, d), F32),
        grid=(depth, ADA_CHUNKS),
        in_specs=[
            pl.BlockSpec((b, d), lambda l, j: (0, 0)),
            pl.BlockSpec((None, d, d), lambda l, j: (l, 0, j)),
            pl.BlockSpec((None, None, 1, d), lambda l, j: (l, j, 0, 0)),
        ],
        out_specs=pl.BlockSpec((None, None, b, d), lambda l, j: (l, j, 0, 0)),
        compiler_params=_params("arbitrary", "arbitrary"),
        name="ada_mods",
    )(c, ada_w, ada_b.reshape(depth, ADA_CHUNKS, 1, d))


def _norm_matmul_kernel(x_ref, sh_ref, sc_ref, g_ref, w_ref, *o_refs):
    h = _modulate(x_ref[...], g_ref[...], sh_ref[...], sc_ref[...]).astype(BF16)
    off = 0
    for o_ref in o_refs:
        n = o_ref.shape[-1]
        o_ref[...] = jnp.dot(h, w_ref[:, off:off + n], preferred_element_type=F32).astype(o_ref.dtype)
        off += n


def _norm_matmul(x, shift, scale, g, w, widths, dtypes, tm):
    b, s, d = x.shape
    row = lambda i, j: (i, 0, 0)
    tile = lambda i, j: (i, j, 0)
    return pl.pallas_call(
        _norm_matmul_kernel,
        out_shape=[jax.ShapeDtypeStruct((b, s, n), dt) for n, dt in zip(widths, dtypes)],
        grid=(b, s // tm),
        in_specs=[
            pl.BlockSpec((None, tm, d), tile),
            pl.BlockSpec((None, 1, d), row),
            pl.BlockSpec((None, 1, d), row),
            pl.BlockSpec((1, d), lambda i, j: (0, 0)),
            pl.BlockSpec(w.shape, lambda i, j: (0, 0)),
        ],
        out_specs=[pl.BlockSpec((None, tm, n), tile) for n in widths],
        compiler_params=_params("arbitrary", "arbitrary"),
        name="norm_matmul",
    )(x, shift, scale, g, w)


def _log_sigmoid(z):
    return jnp.minimum(z, 0.0) - jnp.log1p(jnp.exp(-jnp.abs(z)))


def _gla_kernel(q_ref, k_ref, v_ref, g_ref, a_ref, wg_ref, bg_ref, ng_ref, o_ref, state_ref, la_ref, oacc_ref):
    tm = q_ref.shape[0]
    c = GLA_CHUNK

    @pl.when(pl.program_id(1) == 0)
    def _():
        state_ref[...] = jnp.zeros_like(state_ref)

    z = jnp.dot(a_ref[...].astype(BF16), wg_ref[...], preferred_element_type=F32) + bg_ref[...]
    la_ref[...] = _log_sigmoid(z) * (1.0 / GLA_TAU)

    ri = lax.broadcasted_iota(jnp.int32, (c, c), 0)
    ci = lax.broadcasted_iota(jnp.int32, (c, c), 1)
    causal = ri >= ci
    tril = jnp.where(causal, 1.0, 0.0).astype(BF16)
    hk = GLA_HEADS * GLA_DK

    states = [state_ref[h] for h in range(GLA_HEADS)]
    for ic in range(tm // c):
        rows = slice(ic * c, (ic + 1) * c)
        la = la_ref[rows, :]
        p0 = la.astype(BF16)
        r1 = la - p0.astype(F32)
        p1 = r1.astype(BF16)
        p2 = (r1 - p1.astype(F32)).astype(BF16)
        parts = jnp.dot(tril, jnp.concatenate([p0, p1, p2], axis=1), preferred_element_type=F32)
        bcum = (parts[:, 2 * hk:] + parts[:, hk:2 * hk]) + parts[:, :hk]
        b_last = bcum[c - 1:c, :]
        q = q_ref[rows, :] * (GLA_DK ** -0.5)
        k = k_ref[rows, :]
        q_dec = (q * jnp.exp(bcum)).astype(BF16)
        k_dec = (k * jnp.exp(-bcum)).astype(BF16)
        k_rem = (k * jnp.exp(b_last - bcum)).astype(BF16)
        dec = jnp.exp(jnp.broadcast_to(b_last, (GLA_DV, hk)).T)
        for h in range(GLA_HEADS):
            ks = slice(h * GLA_DK, (h + 1) * GLA_DK)
            vs = slice(h * GLA_DV, (h + 1) * GLA_DV)
            vh = v_ref[rows, vs]
            att = lax.dot_general(q_dec[:, ks], k_dec[:, ks], (((1,), (1,)), ((), ())),
                                  preferred_element_type=F32)
            att = jnp.where(causal, att, 0.0).astype(BF16)
            st = states[h]
            o = jnp.dot(att, vh, preferred_element_type=F32)
            o = o + jnp.dot(q_dec[:, ks], st.astype(BF16), preferred_element_type=F32)
            kv = lax.dot_general(k_rem[:, ks], vh, (((0,), (0,)), ((), ())), preferred_element_type=F32)
            states[h] = dec[ks, :] * st + kv
            oacc_ref[rows, vs] = o
    for h in range(GLA_HEADS):
        state_ref[h] = states[h]

    for h in range(GLA_HEADS):
        vs = slice(h * GLA_DV, (h + 1) * GLA_DV)
        o = oacc_ref[:, vs]
        o = o * lax.rsqrt(jnp.mean(o * o, axis=-1, keepdims=True) + NORM_EPS) * ng_ref[...]
        o_ref[:, vs] = (o * _silu(g_ref[:, vs])).astype(o_ref.dtype)


def _gla(q, k, v, g, a, w_gate2, b_gate, norm_g, tm):
    b, s, _ = q.shape
    hk = GLA_HEADS * GLA_DK
    hv = GLA_HEADS * GLA_DV
    tile = lambda i, j: (i, j, 0)
    const = lambda i, j: (0, 0)
    return pl.pallas_call(
        _gla_kernel,
        out_shape=jax.ShapeDtypeStruct((b, s, hv), BF16),
        grid=(b, s // tm),
        in_specs=[
            pl.BlockSpec((None, tm, hk), tile),
            pl.BlockSpec((None, tm, hk), tile),
            pl.BlockSpec((None, tm, hv), tile),
            pl.BlockSpec((None, tm, hv), tile),
            pl.BlockSpec((None, tm, a.shape[-1]), tile),
            pl.BlockSpec(w_gate2.shape, const),
            pl.BlockSpec((1, hk), const),
            pl.BlockSpec((1, GLA_DV), const),
        ],
        out_specs=pl.BlockSpec((None, tm, hv), tile),
        scratch_shapes=[
            pltpu.VMEM((GLA_HEADS, GLA_DK, GLA_DV), F32),
            pltpu.VMEM((tm, hk), F32),
            pltpu.VMEM((tm, hv), F32),
        ],
        compiler_params=_params("arbitrary", "arbitrary"),
        name="gla",
    )(q, k, v, g, a, w_gate2, b_gate, norm_g)


CONV_HALO = 32


def _conv_kernel(u_ref, w_ref, cb_ref, lg_ref, lb_ref, o_ref, buf_ref):
    tm = u_ref.shape[0]
    ch = o_ref.shape[-1]

    @pl.when(pl.program_id(1) == 0)
    def _():
        buf_ref[0:CONV_HALO, :] = jnp.zeros((CONV_HALO, ch), F32)

    buf_ref[CONV_HALO:, :] = u_ref[:, :ch] * jax.nn.sigmoid(u_ref[:, ch:])
    acc = jnp.zeros((tm, ch), F32)
    base = CONV_HALO - (CONV_WIDTH - 1)
    for j in range(CONV_WIDTH):
        acc = acc + buf_ref[base + j:base + j + tm, :] * w_ref[j:j + 1, :]
    buf_ref[0:CONV_HALO, :] = buf_ref[tm:tm + CONV_HALO, :]
    y = acc + cb_ref[...]
    mu = jnp.mean(y, axis=-1, keepdims=True)
    var = jnp.mean(jnp.square(y - mu), axis=-1, keepdims=True)
    y = (y - mu) * lax.rsqrt(var + NORM_EPS) * lg_ref[...] + lb_ref[...]
    o_ref[...] = _silu(y).astype(o_ref.dtype)


def _conv_module(u, conv_w, conv_b, ln_g, ln_b, tm):
    b, s, two_ch = u.shape
    ch = two_ch // 2
    tile = lambda i, j: (i, j, 0)
    const = lambda i, j: (0, 0)
    return pl.pallas_call(
        _conv_kernel,
        out_shape=jax.ShapeDtypeStruct((b, s, ch), BF16),
        grid=(b, s // tm),
        in_specs=[
            pl.BlockSpec((None, tm, two_ch), tile),
            pl.BlockSpec(conv_w.shape, const),
            pl.BlockSpec((1, ch), const),
            pl.BlockSpec((1, ch), const),
            pl.BlockSpec((1, ch), const),
        ],
        out_specs=pl.BlockSpec((None, tm, ch), tile),
        scratch_shapes=[pltpu.VMEM((tm + CONV_HALO, ch), F32)],
        compiler_params=_params("arbitrary", "arbitrary"),
        name="conv_module",
    )(u, conv_w, conv_b, ln_g, ln_b)


def _out_proj_kernel(*refs):
    *a_refs, w_ref, x_ref, gate_ref, o_ref = refs
    acc = None
    off = 0
    for a_ref in a_refs:
        kk = a_ref.shape[-1]
        part = jnp.dot(a_ref[...].astype(BF16), w_ref[off:off + kk, :], preferred_element_type=F32)
        acc = part if acc is None else acc + part
        off += kk
    o_ref[...] = x_ref[...] + gate_ref[...] * acc


def _out_proj(acts, w, x, gate, tm):
    b, s, d = x.shape
    tile = lambda i, j: (i, j, 0)
    return pl.pallas_call(
        _out_proj_kernel,
        out_shape=jax.ShapeDtypeStruct((b, s, d), F32),
        grid=(b, s // tm),
        in_specs=[pl.BlockSpec((None, tm, a.shape[-1]), tile) for a in acts] + [
            pl.BlockSpec(w.shape, lambda i, j: (0, 0)),
            pl.BlockSpec((None, tm, d), tile),
            pl.BlockSpec((None, 1, d), lambda i, j: (i, 0, 0)),
        ],
        out_specs=pl.BlockSpec((None, tm, d), tile),
        compiler_params=_params("arbitrary", "arbitrary"),
        name="out_proj",
    )(*acts, w, x, gate)


def _rope_table_kernel(pos_ref, freq_ref, sign_ref, cos_ref, sin_ref):
    ang = pos_ref[...] * freq_ref[...]
    cos_ref[...] = jnp.cos(ang)
    sin_ref[...] = jnp.sin(ang) * sign_ref[...]


def _rope_tables(positions, head_dim, tm):
    b, s = positions.shape
    rope_dims = head_dim // 4
    half = rope_dims // 2
    inv_freq = ROPE_THETA ** (-jnp.arange(0, rope_dims, 2, dtype=F32) / rope_dims)
    jj = jnp.arange(LANES) % head_dim
    freq = jnp.where(jj < rope_dims, inv_freq[jj % half], 0.0).astype(F32)[None, :]
    sign = jnp.where(jj < half, -1.0, jnp.where(jj < rope_dims, 1.0, 0.0)).astype(F32)[None, :]
    pos = positions.astype(F32)[..., None]
    tile = lambda i, j: (i, j, 0)
    const = lambda i, j: (0, 0)
    return pl.pallas_call(
        _rope_table_kernel,
        out_shape=[jax.ShapeDtypeStruct((b, s, LANES), F32)] * 2,
        grid=(b, s // tm),
        in_specs=[pl.BlockSpec((None, tm, 1), tile), pl.BlockSpec((1, LANES), const),
                  pl.BlockSpec((1, LANES), const)],
        out_specs=[pl.BlockSpec((None, tm, LANES), tile)] * 2,
        compiler_params=_params("arbitrary", "arbitrary"),
        name="rope_tables",
    )(pos, freq, sign)


def _qkv_kernel(x_ref, sh_ref, sc_ref, g_ref, w_ref, cos_ref, sin_ref, q_ref, k_ref, v_ref, *, head_dim):
    h = _modulate(x_ref[...], g_ref[...], sh_ref[...], sc_ref[...]).astype(BF16)
    d = q_ref.shape[-1]
    half = head_dim // 8
    cosf = jnp.tile(cos_ref[...], (1, d // LANES))
    sinf = jnp.tile(sin_ref[...], (1, d // LANES))
    lane = lax.broadcasted_iota(jnp.int32, (1, d), 1)
    first = (lane % head_dim) < half
    for idx, (o_ref, mult) in enumerate(((q_ref, head_dim ** -0.5), (k_ref, 1.0))):
        t = jnp.dot(h, w_ref[:, idx * d:(idx + 1) * d], preferred_element_type=F32)
        partner = jnp.where(first, pltpu.roll(t, d - half, 1), pltpu.roll(t, half, 1))
        o_ref[...] = ((t * cosf + partner * sinf) * mult).astype(o_ref.dtype)
    v_ref[...] = jnp.dot(h, w_ref[:, 2 * d:], preferred_element_type=F32).astype(v_ref.dtype)


def _qkv_rope(x, shift, scale, g, w, cos_t, sin_t, head_dim, tm):
    b, s, d = x.shape
    row = lambda i, j: (i, 0, 0)
    tile = lambda i, j: (i, j, 0)
    return pl.pallas_call(
        functools.partial(_qkv_kernel, head_dim=head_dim),
        out_shape=[jax.ShapeDtypeStruct((b, s, d), BF16)] * 3,
        grid=(b, s // tm),
        in_specs=[
            pl.BlockSpec((None, tm, d), tile),
            pl.BlockSpec((None, 1, d), row),
            pl.BlockSpec((None, 1, d), row),
            pl.BlockSpec((1, d), lambda i, j: (0, 0)),
            pl.BlockSpec(w.shape, lambda i, j: (0, 0)),
            pl.BlockSpec((None, tm, LANES), tile),
            pl.BlockSpec((None, tm, LANES), tile),
        ],
        out_specs=[pl.BlockSpec((None, tm, d), tile)] * 3,
        compiler_params=_params("arbitrary", "arbitrary"),
        name="qkv_rope",
    )(x, shift, scale, g, w, cos_t, sin_t)


ATTN_UNROLL = 8


def _attn_bias(branches, heads):
    blk = DIL_BLOCK
    kj = np.arange(2 * blk)[:, None]
    qi = np.arange(blk)[None, :]
    dist = qi + blk - kj
    out = []
    for window, dil in branches:
        band = (dist >= 0) & (dist <= window // dil)
        both = np.stack([band, band & (kj >= blk)])
        out.append(np.tile(np.where(both, 0.0, NEG_BIG), (1, 1, heads)))
    return jnp.asarray(np.stack(out), F32)


def _attn_kernel(bias_ref, q_ref, kc_ref, kp_ref, vc_ref, vp_ref, o_ref, qf, kf, vf, ob, lb, st_s, vt_s,
                 *, head_dim, branches):
    sb = q_ref.shape[0]
    blk = DIL_BLOCK
    heads = q_ref.shape[1] // head_dim
    first_super = pl.program_id(2) == 0

    qf[...] = q_ref[...].astype(F32)
    kf[0:sb, :] = kp_ref[...].astype(F32)
    kf[sb:, :] = kc_ref[...].astype(F32)
    vf[0:sb, :] = vp_ref[...].astype(F32)
    vf[sb:, :] = vc_ref[...].astype(F32)

    lane = lax.broadcasted_iota(jnp.int32, (blk, LANES), 1)
    head_masks = [(lane >= h * head_dim) & (lane < (h + 1) * head_dim) for h in range(heads)]

    for bi, (window, dil) in enumerate(branches):
        assert dil & (dil - 1) == 0 and window // dil <= blk
        unit = dil * blk
        shift = dil.bit_length() - 1

        def scores(j, idx, dil=dil, unit=unit, bi=bi, shift=shift):
            u = lax.shift_right_logical(idx, shift)
            r = idx & (dil - 1)
            q0 = u * unit + r
            k0 = sb + q0 - unit
            no_prev = jnp.where(first_super & (u == 0), 1, 0)
            qb = qf[pl.ds(q0, blk, stride=dil), :]
            q2 = jnp.concatenate([jnp.where(mk, qb, 0.0) for mk in head_masks], axis=0).astype(BF16)
            kb = kf[pl.ds(k0, 2 * blk, stride=dil), :].astype(BF16)
            vt_s[j] = vf[pl.ds(k0, 2 * blk, stride=dil), :].T.astype(BF16)
            st = lax.dot_general(kb, q2, (((1,), (1,)), ((), ())), preferred_element_type=F32)
            st_s[j] = st + bias_ref[bi, no_prev]
            return q0

        def softmax_pv(j):
            m = jnp.max(st_s[j], axis=0, keepdims=True)
            p = jnp.exp(st_s[j] - m)
            l = jnp.sum(p, axis=0, keepdims=True)
            of = jnp.dot(vt_s[j], p.astype(BF16), preferred_element_type=F32)
            lse = m + jnp.log(l)
            o_rows, lse_rows = [], []
            for h in range(heads):
                cols = slice(h * blk, (h + 1) * blk)
                o_rows.append(of[h * head_dim:(h + 1) * head_dim, cols] / l[:, cols])
                lse_rows.append(jnp.broadcast_to(lse[:, cols], (head_dim, blk)))
            return jnp.concatenate(o_rows, axis=0).T, jnp.concatenate(lse_rows, axis=0).T

        def block_body(it, carry, scores=scores, softmax_pv=softmax_pv, dil=dil, bi=bi):
            starts = [scores(j, it * ATTN_UNROLL + j) for j in range(ATTN_UNROLL)]
            done = [softmax_pv(j) for j in range(ATTN_UNROLL)]
            for q0, (o_tok, lse_tok) in zip(starts, done):
                ob[bi, pl.ds(q0, blk, stride=dil), :] = o_tok
                lb[bi, pl.ds(q0, blk, stride=dil), :] = lse_tok
            return carry

        lax.fori_loop(0, sb // blk // ATTN_UNROLL, block_body, 0)

    nb = len(branches)
    m = lb[0]
    for bi in range(1, nb):
        m = jnp.maximum(m, lb[bi])
    num = jnp.zeros_like(m)
    den = jnp.zeros_like(m)
    for bi in range(nb):
        e = jnp.exp(lb[bi] - m)
        num = num + e * ob[bi]
        den = den + e
    o_ref[...] = (num / den).astype(o_ref.dtype)


def _dilated_attention(q, k, v, head_dim, branches, sb):
    b, s, d = q.shape
    groups = d // LANES
    cur = lambda i, g, n: (i, n, g)
    prev = lambda i, g, n: (i, jnp.maximum(n - 1, 0), g)
    nb = len(branches)
    bias = _attn_bias(branches, LANES // head_dim)
    return pl.pallas_call(
        functools.partial(_attn_kernel, head_dim=head_dim, branches=branches),
        out_shape=jax.ShapeDtypeStruct((b, s, d), BF16),
        grid=(b, groups, s // sb),
        in_specs=[
            pl.BlockSpec(bias.shape, lambda i, g, n: (0, 0, 0, 0)),
            pl.BlockSpec((None, sb, LANES), cur),
            pl.BlockSpec((None, sb, LANES), cur),
            pl.BlockSpec((None, sb, LANES), prev),
            pl.BlockSpec((None, sb, LANES), cur),
            pl.BlockSpec((None, sb, LANES), prev),
        ],
        out_specs=pl.BlockSpec((None, sb, LANES), cur),
        scratch_shapes=[
            pltpu.VMEM((sb, LANES), F32),
            pltpu.VMEM((2 * sb, LANES), F32),
            pltpu.VMEM((2 * sb, LANES), F32),
            pltpu.VMEM((nb, sb, LANES), F32),
            pltpu.VMEM((nb, sb, LANES), F32),
            pltpu.VMEM((ATTN_UNROLL, 2 * DIL_BLOCK, LANES // head_dim * DIL_BLOCK), F32),
            pltpu.VMEM((ATTN_UNROLL, LANES, 2 * DIL_BLOCK), BF16),
        ],
        compiler_params=_params("arbitrary", "arbitrary", "arbitrary"),
        name="dilated_attention",
    )(bias, q, k, k, v, v)


def _router_kernel(x_ref, sh_ref, sc_ref, g_ref, w_ref, b_ref, h_ref, ids_ref, wts_ref, rank_ref, counts_ref,
                   carry_ref, *, n_experts, n_groups):
    h = _modulate(x_ref[...], g_ref[...], sh_ref[...], sc_ref[...])
    h_ref[...] = h.astype(h_ref.dtype)
    logits = jnp.dot(h, w_ref[...], preferred_element_type=F32, precision=HIGHEST) + b_ref[...]
    tm = logits.shape[0]
    epg = n_experts // n_groups
    lane = lax.broadcasted_iota(jnp.int32, logits.shape, 1)
    neg = -jnp.inf
    big = jnp.int32(LANES)

    def first_max(mask):
        val = jnp.max(jnp.where(mask, logits, neg), axis=-1, keepdims=True)
        idx = jnp.min(jnp.where(mask & (logits == val), lane, big), axis=-1, keepdims=True)
        return val, idx

    gmask = (lane >= n_experts) & (lane < n_experts + n_groups)
    gmax, gidx = first_max(gmask)
    gsum = jnp.sum(jnp.where(gmask, jnp.exp(logits - gmax), 0.0), axis=-1, keepdims=True)
    g_w = 1.0 / gsum
    grp = gidx - n_experts
    assert epg & (epg - 1) == 0
    emask = (lane < n_experts) & (lax.shift_right_logical(lane, epg.bit_length() - 1) == grp)
    v1, i1 = first_max(emask)
    v2, i2 = first_max(emask & (lane != i1))
    e2 = jnp.exp(v2 - v1)
    den = 1.0 + e2
    col = lax.broadcasted_iota(jnp.int32, (tm, TOP_K), 1)
    ids_ref[...] = jnp.where(col == 0, i1, i2)
    wts_ref[...] = jnp.where(col == 0, 1.0 / den, e2 / den) * g_w

    @pl.when((pl.program_id(0) == 0) & (pl.program_id(1) == 0))
    def _():
        carry_ref[...] = jnp.zeros_like(carry_ref)

    hit1 = lane == i1
    hit2 = lane == i2
    onehot = jnp.where(hit1 | hit2, 1.0, 0.0)
    ri = lax.broadcasted_iota(jnp.int32, (tm, tm), 0)
    ci = lax.broadcasted_iota(jnp.int32, (tm, tm), 1)
    before = jnp.where(ci < ri, 1.0, 0.0).astype(BF16)
    prefix = jnp.dot(before, onehot.astype(BF16), preferred_element_type=F32) + carry_ref[0:1, :]
    r1 = jnp.sum(jnp.where(hit1, prefix, 0.0), axis=-1, keepdims=True)
    r2 = jnp.sum(jnp.where(hit2, prefix, 0.0), axis=-1, keepdims=True)
    rank_ref[...] = jnp.where(col == 0, r1, r2).astype(jnp.int32)
    carry_ref[...] = carry_ref[...] + jnp.sum(onehot, axis=0, keepdims=True)
    counts_ref[...] = carry_ref[...]


def _router(x, shift, scale, g, w_rt, b_rt, n_experts, n_groups, tm):
    b, s, d = x.shape
    row = lambda i, j: (i, 0, 0)
    tile = lambda i, j: (i, j, 0)
    const = lambda i, j: (0, 0)
    return pl.pallas_call(
        functools.partial(_router_kernel, n_experts=n_experts, n_groups=n_groups),
        out_shape=[jax.ShapeDtypeStruct((b, s, d), BF16),
                   jax.ShapeDtypeStruct((b, s, TOP_K), jnp.int32),
                   jax.ShapeDtypeStruct((b, s, TOP_K), F32),
                   jax.ShapeDtypeStruct((b, s, TOP_K), jnp.int32),
                   jax.ShapeDtypeStruct((SUBLANES, LANES), F32)],
        grid=(b, s // tm),
        in_specs=[
            pl.BlockSpec((None, tm, d), tile),
            pl.BlockSpec((None, 1, d), row),
            pl.BlockSpec((None, 1, d), row),
            pl.BlockSpec((1, d), const),
            pl.BlockSpec((d, LANES), const),
            pl.BlockSpec((1, LANES), const),
        ],
        out_specs=[pl.BlockSpec((None, tm, d), tile),
                   pl.BlockSpec((None, tm, TOP_K), tile),
                   pl.BlockSpec((None, tm, TOP_K), tile),
                   pl.BlockSpec((None, tm, TOP_K), tile),
                   pl.BlockSpec((SUBLANES, LANES), const)],
        scratch_shapes=[pltpu.VMEM((SUBLANES, LANES), F32)],
        compiler_params=_params("arbitrary", "arbitrary"),
        name="moe_router",
    )(x, shift, scale, g, w_rt, b_rt)


def _expert_kernel(ib_ref, ie_ref, lo_ref, hi_ref, xs_ref, w1_ref, w3_ref, w2_ref, ys_ref, w1b, w3b, w2b):
    j = pl.program_id(0)
    prev = jnp.maximum(j - 1, 0)
    e_changed = (j == 0) | (ie_ref[j] != ie_ref[prev])
    first_of_block = (j == 0) | (ib_ref[j] != ib_ref[prev])
    lo = lo_ref[j]
    hi = hi_ref[j]

    @pl.when(e_changed)
    def _():
        w1b[...] = w1_ref[...].astype(BF16)
        w3b[...] = w3_ref[...].astype(BF16)
        w2b[...] = w2_ref[...].astype(BF16)

    @pl.when(first_of_block)
    def _():
        ys_ref[...] = jnp.zeros_like(ys_ref)

    @pl.when(hi > lo)
    def _():
        x = xs_ref[...]
        a = jnp.dot(x, w1b[...], preferred_element_type=F32)
        g = jnp.dot(x, w3b[...], preferred_element_type=F32)
        y = jnp.dot((_silu(a) * g).astype(BF16), w2b[...], preferred_element_type=F32)
        row = lax.broadcasted_iota(jnp.int32, (y.shape[0], 1), 0)
        ys_ref[...] = jnp.where((row >= lo) & (row < hi), y, ys_ref[...])


def _experts(layer, items, xs, w1, w3, w2, rows):
    a, d = xs.shape
    hid = w1.shape[-1]
    blk = lambda j, ib, ie, lo, hi: (ib[j], 0)
    wsel = lambda j, ib, ie, lo, hi: (layer, ie[j], 0, 0)
    grid_spec = pltpu.PrefetchScalarGridSpec(
        num_scalar_prefetch=4,
        grid=(items[0].shape[0],),
        in_specs=[
            pl.BlockSpec((rows, d), blk),
            pl.BlockSpec((None, None, d, hid), wsel),
            pl.BlockSpec((None, None, d, hid), wsel),
            pl.BlockSpec((None, None, hid, d), wsel),
        ],
        out_specs=pl.BlockSpec((rows, d), blk),
        scratch_shapes=[pltpu.VMEM((d, hid), BF16), pltpu.VMEM((d, hid), BF16), pltpu.VMEM((hid, d), BF16)],
    )
    return pl.pallas_call(
        _expert_kernel,
        out_shape=jax.ShapeDtypeStruct((a, d), F32),
        grid_spec=grid_spec,
        compiler_params=_params("arbitrary"),
        name="moe_experts",
    )(*items, xs, w1, w3, w2)


def _dispatch(ids, rank, counts, rows):
    n_experts = counts.shape[0]
    a = ids.size
    i32 = jnp.int32
    order = jnp.argsort(ids.reshape(-1), stable=True).astype(i32)
    row_tok = order // TOP_K
    ends = jnp.cumsum(counts)
    starts = ends - counts
    dest = rank + jnp.sum(jnp.where(ids[..., None] == jnp.arange(n_experts, dtype=i32), starts, 0), axis=-1)
    n_blk = a // rows
    bstart = jnp.arange(n_blk, dtype=i32) * rows
    e_lo = jnp.clip(jnp.searchsorted(ends, bstart, side='right'), 0, n_experts - 1).astype(i32)
    e_hi = jnp.clip(jnp.searchsorted(ends, bstart + rows - 1, side='right'), 0, n_experts - 1).astype(i32)
    n_items = e_hi - e_lo + 1
    item_end = jnp.cumsum(n_items)
    item_first = item_end - n_items
    jj = jnp.arange(n_blk + n_experts - 1, dtype=i32)
    valid = jj < item_end[-1]
    ib = jnp.clip(jnp.searchsorted(item_end, jj, side='right'), 0, n_blk - 1).astype(i32)
    ie = jnp.where(valid, jnp.clip(e_lo[ib] + jj - item_first[ib], 0, n_experts - 1), e_hi[n_blk - 1]).astype(i32)
    lo = jnp.where(valid, jnp.clip(starts[ie] - ib * rows, 0, rows), 0).astype(i32)
    hi = jnp.where(valid, jnp.clip(ends[ie] - ib * rows, 0, rows), 0).astype(i32)
    return row_tok, dest, (ib, ie, lo, hi)


def _combine_kernel(x_ref, y0_ref, y1_ref, w_ref, gate_ref, o_ref):
    y = y0_ref[...] * w_ref[:, 0:1] + y1_ref[...] * w_ref[:, 1:2]
    o_ref[...] = x_ref[...] + gate_ref[...] * y


def _combine_norm_kernel(x_ref, y0_ref, y1_ref, w_ref, gate_ref, ng_ref, o_ref):
    y = y0_ref[...] * w_ref[:, 0:1] + y1_ref[...] * w_ref[:, 1:2]
    x = x_ref[...] + gate_ref[...] * y
    o_ref[...] = x * lax.rsqrt(jnp.mean(x * x, axis=-1, keepdims=True) + NORM_EPS) * ng_ref[...]


def _combine(x, y0, y1, wts, gate, final_g, tm):
    b, s, d = x.shape
    tile = lambda i, j: (i, j, 0)
    in_specs = [
        pl.BlockSpec((None, tm, d), tile),
        pl.BlockSpec((None, tm, d), tile),
        pl.BlockSpec((None, tm, d), tile),
        pl.BlockSpec((None, tm, TOP_K), tile),
        pl.BlockSpec((None, 1, d), lambda i, j: (i, 0, 0)),
    ]
    args = [x, y0, y1, wts, gate]
    kern = _combine_kernel
    if final_g is not None:
        in_specs.append(pl.BlockSpec((1, d), lambda i, j: (0, 0)))
        args.append(final_g)
        kern = _combine_norm_kernel
    return pl.pallas_call(
        kern,
        out_shape=jax.ShapeDtypeStruct((b, s, d), F32),
        grid=(b, s // tm),
        in_specs=in_specs,
        out_specs=pl.BlockSpec((None, tm, d), tile),
        compiler_params=_params("arbitrary", "arbitrary"),
        name="moe_combine",
    )(*args)


def _hier_moe(layer, x, shift, scale, norm_g, gate, w_rt_full, b_rt_full, w1, w3, w2, final_g, tm, rows):
    b, s, d = x.shape
    t = b * s
    n_experts = w1.shape[1]
    h, ids, wts, rank, counts = _router(x, shift, scale, norm_g, w_rt_full, b_rt_full, n_experts, N_GROUPS, tm)
    counts = counts[0, :n_experts].astype(jnp.int32)
    row_tok, dest, items = _dispatch(ids, rank, counts, rows)
    xs = h.reshape(t, d)[row_tok]
    ys = _experts(layer, items, xs, w1, w3, w2, rows)
    y0 = ys[dest[..., 0].reshape(-1)].reshape(b, s, d)
    y1 = ys[dest[..., 1].reshape(-1)].reshape(b, s, d)
    return _combine(x, y0, y1, wts, gate, final_g, tm)


def _pick_tile(s, pref):
    tm = min(pref, s)
    assert s % tm == 0
    return tm


def kernel(x, c, positions, ada_w, ada_b, norm1_g, norm2_g, even_w_in, even_w_gate2, even_b_gate, even_gla_norm_g, even_conv_w, even_conv_b, even_conv_ln_g, even_conv_ln_b, even_w_out, odd_w_qkv, odd_w_out, moe_w_grp, moe_b_grp, moe_w_rt, moe_b_rt, moe_w1, moe_w3, moe_w2, final_norm_g):
    b, s, d = x.shape
    depth = ada_w.shape[0]
    n_experts = moe_w_rt.shape[-1]
    tm = _pick_tile(s, TOKEN_TILE)
    sb = _pick_tile(s, ATTN_SUPER)
    head_dim = d // ATTN_HEADS
    hk = GLA_HEADS * GLA_DK
    hv = GLA_HEADS * GLA_DV
    conv_ch = d // 2

    mods = _ada_mods(c, ada_w, ada_b)
    mod = lambda l, j: mods[l, j][:, None, :]
    cos_t = sin_t = None

    for layer in range(depth):
        i = layer // 2
        g1 = norm1_g[layer][None, :]
        if layer % 2 == 0:
            w_in = even_w_in[i]
            main = hk + hk + hv + hv
            w_cat = jnp.concatenate([
                w_in[:, :main], w_in[:, main + GLA_GATE_RANK:], w_in[:, main:main + GLA_GATE_RANK],
                jnp.zeros((d, LANES - GLA_GATE_RANK), w_in.dtype)], axis=1).astype(BF16)
            widths = (hk, hk, hv, hv, 2 * conv_ch, LANES)
            dtypes = (F32, F32, BF16, F32, F32, F32)
            q, k, v, g, u, a_lr = _norm_matmul(x, mod(layer, 0), mod(layer, 1), g1, w_cat, widths, dtypes, tm)
            wg = jnp.concatenate([even_w_gate2[i], jnp.zeros((LANES - GLA_GATE_RANK, hk), F32)], axis=0).astype(BF16)
            o_gla = _gla(q, k, v, g, a_lr, wg, even_b_gate[i][None, :], even_gla_norm_g[i][None, :], tm)
            y_conv = _conv_module(u, even_conv_w[i], even_conv_b[i][None, :], even_conv_ln_g[i][None, :],
                                  even_conv_ln_b[i][None, :], tm)
            x = _out_proj([o_gla, y_conv], even_w_out[i].astype(BF16), x, mod(layer, 2), tm)
        else:
            if cos_t is None:
                cos_t, sin_t = _rope_tables(positions, head_dim, tm)
            q, k, v = _qkv_rope(x, mod(layer, 0), mod(layer, 1), g1, odd_w_qkv[i].astype(BF16),
                                cos_t, sin_t, head_dim, tm)
            o = _dilated_attention(q, k, v, head_dim, DILATED_BRANCHES, sb)
            x = _out_proj([o], odd_w_out[i].astype(BF16), x, mod(layer, 2), tm)

        w_rt_full = jnp.concatenate([moe_w_rt[layer], moe_w_grp[layer],
                                     jnp.zeros((d, LANES - n_experts - N_GROUPS), F32)], axis=1)
        b_rt_full = jnp.concatenate([moe_b_rt[layer], moe_b_grp[layer],
                                     jnp.zeros((LANES - n_experts - N_GROUPS,), F32)])[None, :]
        final_g = final_norm_g[None, :] if layer == depth - 1 else None
        x = _hier_moe(layer, x, mod(layer, 3), mod(layer, 4), norm2_g[layer][None, :], mod(layer, 5),
                      w_rt_full, b_rt_full, moe_w1, moe_w3, moe_w2, final_g, tm, MOE_ROWS)
    return x
```

```python
import functools

import jax
import jax.numpy as jnp
import numpy as np
from jax import lax
from jax.experimental import pallas as pl
from jax.experimental.pallas import tpu as pltpu

F32 = jnp.float32
BF16 = jnp.bfloat16
HIGHEST = lax.Precision.HIGHEST

NORM_EPS = 1e-6
GLA_HEADS = 4
GLA_DK = 64
GLA_DV = 128
GLA_GATE_RANK = 16
GLA_TAU = 16.0
GLA_CHUNK = 64
CONV_WIDTH = 31
ATTN_HEADS = 16
DILATED_BRANCHES = ((128, 1), (512, 4), (2048, 16))
DIL_BLOCK = 128
ROPE_THETA = 500000.0
N_GROUPS = 4
EXPERTS_PER_GROUP = 8
TOP_K = 2
ADA_CHUNKS = 6

LANES = 128
SUBLANES = 8
VMEM_LIMIT = 56 * 1024 * 1024
TOKEN_TILE = 512
ATTN_SUPER = 2048
MOE_ROWS = 512
NEG_BIG = -1e30


def _params(*sem):
    return pltpu.CompilerParams(dimension_semantics=sem, vmem_limit_bytes=VMEM_LIMIT)


def _silu(x):
    return x * jax.nn.sigmoid(x)


def _modulate(x, g, shift, scale):
    y = x * lax.rsqrt(jnp.mean(x * x, axis=-1, keepdims=True) + NORM_EPS)
    return (y * g) * (1.0 + scale) + shift


def _ada_kernel(c_ref, w_ref, b_ref, o_ref):
    cond = _silu(c_ref[...])
    o_ref[...] = jnp.dot(cond, w_ref[...], preferred_element_type=F32, precision=HIGHEST) + b_ref[...]


def _ada_mods(c, ada_w, ada_b):
    depth, d, _ = ada_w.shape
    b = c.shape[0]
    return pl.pallas_call(
        _ada_kernel,
        out_shape=jax.ShapeDtypeStruct((depth, ADA_CHUNKS, b, d), F32),
        grid=(depth, ADA_CHUNKS),
        in_specs=[
            pl.BlockSpec((b, d), lambda l, j: (0, 0)),
            pl.BlockSpec((None, d, d), lambda l, j: (l, 0, j)),
            pl.BlockSpec((None, None, 1, d), lambda l, j: (l, j, 0, 0)),
        ],
        out_specs=pl.BlockSpec((None, None, b, d), lambda l, j: (l, j, 0, 0)),
        compiler_params=_params("arbitrary", "arbitrary"),
        name="ada_mods",
    )(c, ada_w, ada_b.reshape(depth, ADA_CHUNKS, 1, d))


def _norm_matmul_kernel(x_ref, sh_ref, sc_ref, g_ref, w_ref, *o_refs):
    h = _modulate(x_ref[...], g_ref[...], sh_ref[...], sc_ref[...]).astype(BF16)
    off = 0
    for o_ref in o_refs:
        n = o_ref.shape[-1]
        o_ref[...] = jnp.dot(h, w_ref[:, off:off + n], preferred_element_type=F32).astype(o_ref.dtype)
        off += n


def _norm_matmul(x, shift, scale, g, w, widths, dtypes, tm):
    b, s, d = x.shape
    row = lambda i, j: (i, 0, 0)
    tile = lambda i, j: (i, j, 0)
    return pl.pallas_call(
        _norm_matmul_kernel,
        out_shape=[jax.ShapeDtypeStruct((b, s, n), dt) for n, dt in zip(widths, dtypes)],
        grid=(b, s // tm),
        in_specs=[
            pl.BlockSpec((None, tm, d), tile),
            pl.BlockSpec((None, 1, d), row),
            pl.BlockSpec((None, 1, d), row),
            pl.BlockSpec((1, d), lambda i, j: (0, 0)),
            pl.BlockSpec(w.shape, lambda i, j: (0, 0)),
        ],
        out_specs=[pl.BlockSpec((None, tm, n), tile) for n in widths],
        compiler_params=_params("arbitrary", "arbitrary"),
        name="norm_matmul",
    )(x, shift, scale, g, w)


def _log_sigmoid(z):
    return jnp.minimum(z, 0.0) - jnp.log1p(jnp.exp(-jnp.abs(z)))


def _gla_kernel(q_ref, k_ref, v_ref, g_ref, a_ref, wg_ref, bg_ref, ng_ref, o_ref, state_ref, la_ref, oacc_ref):
    tm = q_ref.shape[0]
    c = GLA_CHUNK

    @pl.when(pl.program_id(1) == 0)
    def _():
        state_ref[...] = jnp.zeros_like(state_ref)

    z = jnp.dot(a_ref[...].astype(BF16), wg_ref[...], preferred_element_type=F32) + bg_ref[...]
    la_ref[...] = _log_sigmoid(z) * (1.0 / GLA_TAU)

    ri = lax.broadcasted_iota(jnp.int32, (c, c), 0)
    ci = lax.broadcasted_iota(jnp.int32, (c, c), 1)
    causal = ri >= ci
    tril = jnp.where(causal, 1.0, 0.0).astype(BF16)
    hk = GLA_HEADS * GLA_DK

    states = [state_ref[h] for h in range(GLA_HEADS)]
    for ic in range(tm // c):
        rows = slice(ic * c, (ic + 1) * c)
        la = la_ref[rows, :]
        p0 = la.astype(BF16)
        r1 = la - p0.astype(F32)
        p1 = r1.astype(BF16)
        p2 = (r1 - p1.astype(F32)).astype(BF16)
        parts = jnp.dot(tril, jnp.concatenate([p0, p1, p2], axis=1), preferred_element_type=F32)
        bcum = (parts[:, 2 * hk:] + parts[:, hk:2 * hk]) + parts[:, :hk]
        b_last = bcum[c - 1:c, :]
        q = q_ref[rows, :] * (GLA_DK ** -0.5)
        k = k_ref[rows, :]
        q_dec = (q * jnp.exp(bcum)).astype(BF16)
        k_dec = (k * jnp.exp(-bcum)).astype(BF16)
        k_rem = (k * jnp.exp(b_last - bcum)).astype(BF16)
        dec = jnp.exp(jnp.broadcast_to(b_last, (GLA_DV, hk)).T)
        for h in range(GLA_HEADS):
            ks = slice(h * GLA_DK, (h + 1) * GLA_DK)
            vs = slice(h * GLA_DV, (h + 1) * GLA_DV)
            vh = v_ref[rows, vs]
            att = lax.dot_general(q_dec[:, ks], k_dec[:, ks], (((1,), (1,)), ((), ())),
                                  preferred_element_type=F32)
            att = jnp.where(causal, att, 0.0).astype(BF16)
            st = states[h]
            o = jnp.dot(att, vh, preferred_element_type=F32)
            o = o + jnp.dot(q_dec[:, ks], st.astype(BF16), preferred_element_type=F32)
            kv = lax.dot_general(k_rem[:, ks], vh, (((0,), (0,)), ((), ())), preferred_element_type=F32)
            states[h] = dec[ks, :] * st + kv
            oacc_ref[rows, vs] = o
    for h in range(GLA_HEADS):
        state_ref[h] = states[h]

    for h in range(GLA_HEADS):
        vs = slice(h * GLA_DV, (h + 1) * GLA_DV)
        o = oacc_ref[:, vs]
        o = o * lax.rsqrt(jnp.mean(o * o, axis=-1, keepdims=True) + NORM_EPS) * ng_ref[...]
        o_ref[:, vs] = (o * _silu(g_ref[:, vs])).astype(o_ref.dtype)


def _gla(q, k, v, g, a, w_gate2, b_gate, norm_g, tm):
    b, s, _ = q.shape
    hk = GLA_HEADS * GLA_DK
    hv = GLA_HEADS * GLA_DV
    tile = lambda i, j: (i, j, 0)
    const = lambda i, j: (0, 0)
    return pl.pallas_call(
        _gla_kernel,
        out_shape=jax.ShapeDtypeStruct((b, s, hv), BF16),
        grid=(b, s // tm),
        in_specs=[
            pl.BlockSpec((None, tm, hk), tile),
            pl.BlockSpec((None, tm, hk), tile),
            pl.BlockSpec((None, tm, hv), tile),
            pl.BlockSpec((None, tm, hv), tile),
            pl.BlockSpec((None, tm, a.shape[-1]), tile),
            pl.BlockSpec(w_gate2.shape, const),
            pl.BlockSpec((1, hk), const),
            pl.BlockSpec((1, GLA_DV), const),
        ],
        out_specs=pl.BlockSpec((None, tm, hv), tile),
        scratch_shapes=[
            pltpu.VMEM((GLA_HEADS, GLA_DK, GLA_DV), F32),
            pltpu.VMEM((tm, hk), F32),
            pltpu.VMEM((tm, hv), F32),
        ],
        compiler_params=_params("arbitrary", "arbitrary"),
        name="gla",
    )(q, k, v, g, a, w_gate2, b_gate, norm_g)


CONV_HALO = 32


def _conv_kernel(u_ref, w_ref, cb_ref, lg_ref, lb_ref, o_ref, buf_ref):
    tm = u_ref.shape[0]
    ch = o_ref.shape[-1]

    @pl.when(pl.program_id(1) == 0)
    def _():
        buf_ref[0:CONV_HALO, :] = jnp.zeros((CONV_HALO, ch), F32)

    buf_ref[CONV_HALO:, :] = u_ref[:, :ch] * jax.nn.sigmoid(u_ref[:, ch:])
    base = CONV_HALO - (CONV_WIDTH - 1)
    acc = None
    for b in range(SUBLANES):
        part = None
        span = tm + (SUBLANES if b else 0)
        for a in range((base + CONV_WIDTH - 1) // SUBLANES + 1):
            j = SUBLANES * a + b - base
            if 0 <= j < CONV_WIDTH:
                term = buf_ref[SUBLANES * a:SUBLANES * a + span, :] * w_ref[j:j + 1, :]
                part = term if part is None else part + term
        if part is not None:
            part = part[b:b + tm, :]
            acc = part if acc is None else acc + part
    buf_ref[0:CONV_HALO, :] = buf_ref[tm:tm + CONV_HALO, :]
    y = acc + cb_ref[...]
    mu = jnp.mean(y, axis=-1, keepdims=True)
    var = jnp.mean(jnp.square(y - mu), axis=-1, keepdims=True)
    y = (y - mu) * lax.rsqrt(var + NORM_EPS) * lg_ref[...] + lb_ref[...]
    o_ref[...] = _silu(y).astype(o_ref.dtype)


def _conv_module(u, conv_w, conv_b, ln_g, ln_b, tm):
    b, s, two_ch = u.shape
    ch = two_ch // 2
    tile = lambda i, j: (i, j, 0)
    const = lambda i, j: (0, 0)
    return pl.pallas_call(
        _conv_kernel,
        out_shape=jax.ShapeDtypeStruct((b, s, ch), BF16),
        grid=(b, s // tm),
        in_specs=[
            pl.BlockSpec((None, tm, two_ch), tile),
            pl.BlockSpec(conv_w.shape, const),
            pl.BlockSpec((1, ch), const),
            pl.BlockSpec((1, ch), const),
            pl.BlockSpec((1, ch), const),
        ],
        out_specs=pl.BlockSpec((None, tm, ch), tile),
        scratch_shapes=[pltpu.VMEM((tm + CONV_HALO, ch), F32)],
        compiler_params=_params("arbitrary", "arbitrary"),
        name="conv_module",
    )(u, conv_w, conv_b, ln_g, ln_b)


def _out_proj_kernel(*refs):
    *a_refs, w_ref, x_ref, gate_ref, o_ref = refs
    acc = None
    off = 0
    for a_ref in a_refs:
        kk = a_ref.shape[-1]
        part = jnp.dot(a_ref[...].astype(BF16), w_ref[off:off + kk, :], preferred_element_type=F32)
        acc = part if acc is None else acc + part
        off += kk
    o_ref[...] = x_ref[...] + gate_ref[...] * acc


def _out_proj(acts, w, x, gate, tm):
    b, s, d = x.shape
    tile = lambda i, j: (i, j, 0)
    return pl.pallas_call(
        _out_proj_kernel,
        out_shape=jax.ShapeDtypeStruct((b, s, d), F32),
        grid=(b, s // tm),
        in_specs=[pl.BlockSpec((None, tm, a.shape[-1]), tile) for a in acts] + [
            pl.BlockSpec(w.shape, lambda i, j: (0, 0)),
            pl.BlockSpec((None, tm, d), tile),
            pl.BlockSpec((None, 1, d), lambda i, j: (i, 0, 0)),
        ],
        out_specs=pl.BlockSpec((None, tm, d), tile),
        compiler_params=_params("arbitrary", "arbitrary"),
        name="out_proj",
    )(*acts, w, x, gate)


def _rope_table_kernel(pos_ref, freq_ref, sign_ref, cos_ref, sin_ref):
    ang = pos_ref[...] * freq_ref[...]
    cos_ref[...] = jnp.cos(ang)
    sin_ref[...] = jnp.sin(ang) * sign_ref[...]


def _rope_tables(positions, head_dim, tm):
    b, s = positions.shape
    rope_dims = head_dim // 4
    half = rope_dims // 2
    inv_freq = ROPE_THETA ** (-jnp.arange(0, rope_dims, 2, dtype=F32) / rope_dims)
    jj = jnp.arange(LANES) % head_dim
    freq = jnp.where(jj < rope_dims, inv_freq[jj % half], 0.0).astype(F32)[None, :]
    sign = jnp.where(jj < half, -1.0, jnp.where(jj < rope_dims, 1.0, 0.0)).astype(F32)[None, :]
    pos = positions.astype(F32)[..., None]
    tile = lambda i, j: (i, j, 0)
    const = lambda i, j: (0, 0)
    return pl.pallas_call(
        _rope_table_kernel,
        out_shape=[jax.ShapeDtypeStruct((b, s, LANES), F32)] * 2,
        grid=(b, s // tm),
        in_specs=[pl.BlockSpec((None, tm, 1), tile), pl.BlockSpec((1, LANES), const),
                  pl.BlockSpec((1, LANES), const)],
        out_specs=[pl.BlockSpec((None, tm, LANES), tile)] * 2,
        compiler_params=_params("arbitrary", "arbitrary"),
        name="rope_tables",
    )(pos, freq, sign)


def _qkv_kernel(x_ref, sh_ref, sc_ref, g_ref, w_ref, cos_ref, sin_ref, q_ref, k_ref, v_ref, *, head_dim):
    h = _modulate(x_ref[...], g_ref[...], sh_ref[...], sc_ref[...]).astype(BF16)
    d = q_ref.shape[-1]
    half = head_dim // 8
    cosf = jnp.tile(cos_ref[...], (1, d // LANES))
    sinf = jnp.tile(sin_ref[...], (1, d // LANES))
    lane = lax.broadcasted_iota(jnp.int32, (1, d), 1)
    first = (lane % head_dim) < half
    for idx, (o_ref, mult) in enumerate(((q_ref, head_dim ** -0.5), (k_ref, 1.0))):
        t = jnp.dot(h, w_ref[:, idx * d:(idx + 1) * d], preferred_element_type=F32)
        partner = jnp.where(first, pltpu.roll(t, d - half, 1), pltpu.roll(t, half, 1))
        o_ref[...] = ((t * cosf + partner * sinf) * mult).astype(o_ref.dtype)
    v_ref[...] = jnp.dot(h, w_ref[:, 2 * d:], preferred_element_type=F32).astype(v_ref.dtype)


def _qkv_rope(x, shift, scale, g, w, cos_t, sin_t, head_dim, tm):
    b, s, d = x.shape
    row = lambda i, j: (i, 0, 0)
    tile = lambda i, j: (i, j, 0)
    return pl.pallas_call(
        functools.partial(_qkv_kernel, head_dim=head_dim),
        out_shape=[jax.ShapeDtypeStruct((b, s, d), BF16)] * 3,
        grid=(b, s // tm),
        in_specs=[
            pl.BlockSpec((None, tm, d), tile),
            pl.BlockSpec((None, 1, d), row),
            pl.BlockSpec((None, 1, d), row),
            pl.BlockSpec((1, d), lambda i, j: (0, 0)),
            pl.BlockSpec(w.shape, lambda i, j: (0, 0)),
            pl.BlockSpec((None, tm, LANES), tile),
            pl.BlockSpec((None, tm, LANES), tile),
        ],
        out_specs=[pl.BlockSpec((None, tm, d), tile)] * 3,
        compiler_params=_params("arbitrary", "arbitrary"),
        name="qkv_rope",
    )(x, shift, scale, g, w, cos_t, sin_t)


ATTN_UNROLL = 8


def _attn_bias(branches, heads):
    blk = DIL_BLOCK
    kj = np.arange(2 * blk)[:, None]
    qi = np.arange(blk)[None, :]
    dist = qi + blk - kj
    out = []
    for window, dil in branches:
        band = (dist >= 0) & (dist <= window // dil)
        both = np.stack([band, band & (kj >= blk)])
        out.append(np.tile(np.where(both, 0.0, NEG_BIG), (1, 1, heads)))
    return jnp.asarray(np.stack(out), F32)


def _attn_kernel(bias_ref, q_ref, kc_ref, kp_ref, vc_ref, vp_ref, o_ref, qf, kf, vf, ob, lb, st_s, vt_s,
                 *, head_dim, branches):
    sb = q_ref.shape[0]
    blk = DIL_BLOCK
    heads = q_ref.shape[1] // head_dim
    first_super = pl.program_id(2) == 0

    qf[...] = q_ref[...].astype(F32)
    kf[0:sb, :] = kp_ref[...].astype(F32)
    kf[sb:, :] = kc_ref[...].astype(F32)
    vf[0:sb, :] = vp_ref[...].astype(F32)
    vf[sb:, :] = vc_ref[...].astype(F32)

    lane = lax.broadcasted_iota(jnp.int32, (blk, LANES), 1)
    head_masks = [(lane >= h * head_dim) & (lane < (h + 1) * head_dim) for h in range(heads)]

    for bi, (window, dil) in enumerate(branches):
        assert dil & (dil - 1) == 0 and window // dil <= blk
        unit = dil * blk
        shift = dil.bit_length() - 1

        def scores(j, idx, dil=dil, unit=unit, bi=bi, shift=shift):
            u = lax.shift_right_logical(idx, shift)
            r = idx & (dil - 1)
            q0 = u * unit + r
            k0 = sb + q0 - unit
            no_prev = jnp.where(first_super & (u == 0), 1, 0)
            qb = qf[pl.ds(q0, blk, stride=dil), :]
            q2 = jnp.concatenate([jnp.where(mk, qb, 0.0) for mk in head_masks], axis=0).astype(BF16)
            kb = kf[pl.ds(k0, 2 * blk, stride=dil), :].astype(BF16)
            vt_s[j] = vf[pl.ds(k0, 2 * blk, stride=dil), :].T.astype(BF16)
            st = lax.dot_general(kb, q2, (((1,), (1,)), ((), ())), preferred_element_type=F32)
            st_s[j] = st + bias_ref[bi, no_prev]
            return q0

        def softmax_pv(j):
            m = jnp.max(st_s[j], axis=0, keepdims=True)
            p = jnp.exp(st_s[j] - m)
            l = jnp.sum(p, axis=0, keepdims=True)
            of = jnp.dot(vt_s[j], p.astype(BF16), preferred_element_type=F32)
            lse = m + jnp.log(l)
            o_rows, lse_rows = [], []
            for h in range(heads):
                cols = slice(h * blk, (h + 1) * blk)
                o_rows.append(of[h * head_dim:(h + 1) * head_dim, cols] / l[:, cols])
                lse_rows.append(jnp.broadcast_to(lse[:, cols], (head_dim, blk)))
            return jnp.concatenate(o_rows, axis=0).T, jnp.concatenate(lse_rows, axis=0).T

        def block_body(it, carry, scores=scores, softmax_pv=softmax_pv, dil=dil, bi=bi):
            starts = [scores(j, it * ATTN_UNROLL + j) for j in range(ATTN_UNROLL)]
            done = [softmax_pv(j) for j in range(ATTN_UNROLL)]
            for q0, (o_tok, lse_tok) in zip(starts, done):
                ob[bi, pl.ds(q0, blk, stride=dil), :] = o_tok
                lb[bi, pl.ds(q0, blk, stride=dil), :] = lse_tok
            return carry

        lax.fori_loop(0, sb // blk // ATTN_UNROLL, block_body, 0)

    nb = len(branches)
    m = lb[0]
    for bi in range(1, nb):
        m = jnp.maximum(m, lb[bi])
    num = jnp.zeros_like(m)
    den = jnp.zeros_like(m)
    for bi in range(nb):
        e = jnp.exp(lb[bi] - m)
        num = num + e * ob[bi]
        den = den + e
    o_ref[...] = (num / den).astype(o_ref.dtype)


def _dilated_attention(q, k, v, head_dim, branches, sb):
    b, s, d = q.shape
    groups = d // LANES
    cur = lambda i, g, n: (i, n, g)
    prev = lambda i, g, n: (i, jnp.maximum(n - 1, 0), g)
    nb = len(branches)
    bias = _attn_bias(branches, LANES // head_dim)
    return pl.pallas_call(
        functools.partial(_attn_kernel, head_dim=head_dim, branches=branches),
        out_shape=jax.ShapeDtypeStruct((b, s, d), BF16),
        grid=(b, groups, s // sb),
        in_specs=[
            pl.BlockSpec(bias.shape, lambda i, g, n: (0, 0, 0, 0)),
            pl.BlockSpec((None, sb, LANES), cur),
            pl.BlockSpec((None, sb, LANES), cur),
            pl.BlockSpec((None, sb, LANES), prev),
            pl.BlockSpec((None, sb, LANES), cur),
            pl.BlockSpec((None, sb, LANES), prev),
        ],
        out_specs=pl.BlockSpec((None, sb, LANES), cur),
        scratch_shapes=[
            pltpu.VMEM((sb, LANES), F32),
            pltpu.VMEM((2 * sb, LANES), F32),
            pltpu.VMEM((2 * sb, LANES), F32),
            pltpu.VMEM((nb, sb, LANES), F32),
            pltpu.VMEM((nb, sb, LANES), F32),
            pltpu.VMEM((ATTN_UNROLL, 2 * DIL_BLOCK, LANES // head_dim * DIL_BLOCK), F32),
            pltpu.VMEM((ATTN_UNROLL, LANES, 2 * DIL_BLOCK), BF16),
        ],
        compiler_params=_params("arbitrary", "arbitrary", "arbitrary"),
        name="dilated_attention",
    )(bias, q, k, k, v, v)


def _router_kernel(x_ref, sh_ref, sc_ref, g_ref, w_ref, b_ref, h_ref, ids_ref, wts_ref, rank_ref, counts_ref,
                   carry_ref, *, n_experts, n_groups):
    h = _modulate(x_ref[...], g_ref[...], sh_ref[...], sc_ref[...]).astype(BF16)
    h_ref[...] = h
    logits = jnp.dot(h, w_ref[...], preferred_element_type=F32) + b_ref[...]
    tm = logits.shape[0]
    epg = n_experts // n_groups
    lane = lax.broadcasted_iota(jnp.int32, logits.shape, 1)
    neg = -jnp.inf
    big = jnp.int32(LANES)

    def first_max(mask):
        val = jnp.max(jnp.where(mask, logits, neg), axis=-1, keepdims=True)
        idx = jnp.min(jnp.where(mask & (logits == val), lane, big), axis=-1, keepdims=True)
        return val, idx

    gmask = (lane >= n_experts) & (lane < n_experts + n_groups)
    gmax, gidx = first_max(gmask)
    gsum = jnp.sum(jnp.where(gmask, jnp.exp(logits - gmax), 0.0), axis=-1, keepdims=True)
    g_w = 1.0 / gsum
    grp = gidx - n_experts
    assert epg & (epg - 1) == 0
    emask = (lane < n_experts) & (lax.shift_right_logical(lane, epg.bit_length() - 1) == grp)
    v1, i1 = first_max(emask)
    v2, i2 = first_max(emask & (lane != i1))
    e2 = jnp.exp(v2 - v1)
    den = 1.0 + e2
    col = lax.broadcasted_iota(jnp.int32, (tm, TOP_K), 1)
    ids_ref[...] = jnp.where(col == 0, i1, i2)
    wts_ref[...] = jnp.where(col == 0, 1.0 / den, e2 / den) * g_w

    @pl.when((pl.program_id(0) == 0) & (pl.program_id(1) == 0))
    def _():
        carry_ref[...] = jnp.zeros_like(carry_ref)

    hit1 = lane == i1
    hit2 = lane == i2
    onehot = jnp.where(hit1 | hit2, 1.0, 0.0)
    ri = lax.broadcasted_iota(jnp.int32, (tm, tm), 0)
    ci = lax.broadcasted_iota(jnp.int32, (tm, tm), 1)
    before = jnp.where(ci < ri, 1.0, 0.0).astype(BF16)
    prefix = jnp.dot(before, onehot.astype(BF16), preferred_element_type=F32) + carry_ref[0:1, :]
    r1 = jnp.sum(jnp.where(hit1, prefix, 0.0), axis=-1, keepdims=True)
    r2 = jnp.sum(jnp.where(hit2, prefix, 0.0), axis=-1, keepdims=True)
    rank_ref[...] = jnp.where(col == 0, r1, r2).astype(jnp.int32)
    carry_ref[...] = carry_ref[...] + jnp.sum(onehot, axis=0, keepdims=True)
    counts_ref[...] = carry_ref[...]


def _router(x, shift, scale, g, w_rt, b_rt, n_experts, n_groups, tm):
    b, s, d = x.shape
    row = lambda i, j: (i, 0, 0)
    tile = lambda i, j: (i, j, 0)
    const = lambda i, j: (0, 0)
    return pl.pallas_call(
        functools.partial(_router_kernel, n_experts=n_experts, n_groups=n_groups),
        out_shape=[jax.ShapeDtypeStruct((b, s, d), BF16),
                   jax.ShapeDtypeStruct((b, s, TOP_K), jnp.int32),
                   jax.ShapeDtypeStruct((b, s, TOP_K), F32),
                   jax.ShapeDtypeStruct((b, s, TOP_K), jnp.int32),
                   jax.ShapeDtypeStruct((SUBLANES, LANES), F32)],
        grid=(b, s // tm),
        in_specs=[
            pl.BlockSpec((None, tm, d), tile),
            pl.BlockSpec((None, 1, d), row),
            pl.BlockSpec((None, 1, d), row),
            pl.BlockSpec((1, d), const),
            pl.BlockSpec((d, LANES), const),
            pl.BlockSpec((1, LANES), const),
        ],
        out_specs=[pl.BlockSpec((None, tm, d), tile),
                   pl.BlockSpec((None, tm, TOP_K), tile),
                   pl.BlockSpec((None, tm, TOP_K), tile),
                   pl.BlockSpec((None, tm, TOP_K), tile),
                   pl.BlockSpec((SUBLANES, LANES), const)],
        scratch_shapes=[pltpu.VMEM((SUBLANES, LANES), F32)],
        compiler_params=_params("arbitrary", "arbitrary"),
        name="moe_router",
    )(x, shift, scale, g, w_rt, b_rt)


def _expert_kernel(ib_ref, ie_ref, lo_ref, hi_ref, xs_ref, w1_ref, w3_ref, w2_ref, ys_ref, w1b, w3b, w2b):
    j = pl.program_id(0)
    prev = jnp.maximum(j - 1, 0)
    e_changed = (j == 0) | (ie_ref[j] != ie_ref[prev])
    first_of_block = (j == 0) | (ib_ref[j] != ib_ref[prev])
    lo = lo_ref[j]
    hi = hi_ref[j]

    @pl.when(e_changed)
    def _():
        w1b[...] = w1_ref[...].astype(BF16)
        w3b[...] = w3_ref[...].astype(BF16)
        w2b[...] = w2_ref[...].astype(BF16)

    @pl.when(first_of_block)
    def _():
        ys_ref[...] = jnp.zeros_like(ys_ref)

    @pl.when(hi > lo)
    def _():
        x = xs_ref[...]
        a = jnp.dot(x, w1b[...], preferred_element_type=F32)
        g = jnp.dot(x, w3b[...], preferred_element_type=F32)
        y = jnp.dot((_silu(a) * g).astype(BF16), w2b[...], preferred_element_type=F32)
        row = lax.broadcasted_iota(jnp.int32, (y.shape[0], 1), 0)
        ys_ref[...] = jnp.where((row >= lo) & (row < hi), y, ys_ref[...])


def _experts(layer, items, xs, w1, w3, w2, rows):
    a, d = xs.shape
    hid = w1.shape[-1]
    blk = lambda j, ib, ie, lo, hi: (ib[j], 0)
    wsel = lambda j, ib, ie, lo, hi: (layer, ie[j], 0, 0)
    grid_spec = pltpu.PrefetchScalarGridSpec(
        num_scalar_prefetch=4,
        grid=(items[0].shape[0],),
        in_specs=[
            pl.BlockSpec((rows, d), blk),
            pl.BlockSpec((None, None, d, hid), wsel),
            pl.BlockSpec((None, None, d, hid), wsel),
            pl.BlockSpec((None, None, hid, d), wsel),
        ],
        out_specs=pl.BlockSpec((rows, d), blk),
        scratch_shapes=[pltpu.VMEM((d, hid), BF16), pltpu.VMEM((d, hid), BF16), pltpu.VMEM((hid, d), BF16)],
    )
    return pl.pallas_call(
        _expert_kernel,
        out_shape=jax.ShapeDtypeStruct((a, d), F32),
        grid_spec=grid_spec,
        compiler_params=_params("arbitrary"),
        name="moe_experts",
    )(*items, xs, w1, w3, w2)


def _dispatch(ids, rank, counts, rows):
    n_experts = counts.shape[0]
    a = ids.size
    i32 = jnp.int32
    order = jnp.argsort(ids.reshape(-1), stable=True).astype(i32)
    row_tok = order // TOP_K
    ends = jnp.cumsum(counts)
    starts = ends - counts
    dest = rank + jnp.sum(jnp.where(ids[..., None] == jnp.arange(n_experts, dtype=i32), starts, 0), axis=-1)
    n_blk = a // rows
    bstart = jnp.arange(n_blk, dtype=i32) * rows
    count_le = lambda bounds, x: jnp.sum((bounds[None, :] <= x[:, None]).astype(i32), axis=1)
    e_lo = jnp.minimum(count_le(ends, bstart), n_experts - 1)
    e_hi = jnp.minimum(count_le(ends, bstart + rows - 1), n_experts - 1)
    n_items = e_hi - e_lo + 1
    item_end = jnp.cumsum(n_items)
    item_first = item_end - n_items
    jj = jnp.arange(n_blk + n_experts - 1, dtype=i32)
    valid = jj < item_end[-1]
    ib = jnp.minimum(count_le(item_end, jj), n_blk - 1)
    ie = jnp.where(valid, jnp.clip(e_lo[ib] + jj - item_first[ib], 0, n_experts - 1), e_hi[n_blk - 1]).astype(i32)
    lo = jnp.where(valid, jnp.clip(starts[ie] - ib * rows, 0, rows), 0).astype(i32)
    hi = jnp.where(valid, jnp.clip(ends[ie] - ib * rows, 0, rows), 0).astype(i32)
    return row_tok, dest, (ib, ie, lo, hi)


def _combine_kernel(x_ref, y0_ref, y1_ref, w_ref, gate_ref, o_ref):
    y = y0_ref[...] * w_ref[:, 0:1] + y1_ref[...] * w_ref[:, 1:2]
    o_ref[...] = x_ref[...] + gate_ref[...] * y


def _combine_norm_kernel(x_ref, y0_ref, y1_ref, w_ref, gate_ref, ng_ref, o_ref):
    y = y0_ref[...] * w_ref[:, 0:1] + y1_ref[...] * w_ref[:, 1:2]
    x = x_ref[...] + gate_ref[...] * y
    o_ref[...] = x * lax.rsqrt(jnp.mean(x * x, axis=-1, keepdims=True) + NORM_EPS) * ng_ref[...]


def _combine(x, y0, y1, wts, gate, final_g, tm):
    b, s, d = x.shape
    tile = lambda i, j: (i, j, 0)
    in_specs = [
        pl.BlockSpec((None, tm, d), tile),
        pl.BlockSpec((None, tm, d), tile),
        pl.BlockSpec((None, tm, d), tile),
        pl.BlockSpec((None, tm, TOP_K), tile),
        pl.BlockSpec((None, 1, d), lambda i, j: (i, 0, 0)),
    ]
    args = [x, y0, y1, wts, gate]
    kern = _combine_kernel
    if final_g is not None:
        in_specs.append(pl.BlockSpec((1, d), lambda i, j: (0, 0)))
        args.append(final_g)
        kern = _combine_norm_kernel
    return pl.pallas_call(
        kern,
        out_shape=jax.ShapeDtypeStruct((b, s, d), F32),
        grid=(b, s // tm),
        in_specs=in_specs,
        out_specs=pl.BlockSpec((None, tm, d), tile),
        compiler_params=_params("arbitrary", "arbitrary"),
        name="moe_combine",
    )(*args)


def _hier_moe(layer, x, shift, scale, norm_g, gate, w_rt_full, b_rt_full, w1, w3, w2, final_g, tm, rows):
    b, s, d = x.shape
    t = b * s
    n_experts = w1.shape[1]
    h, ids, wts, rank, counts = _router(x, shift, scale, norm_g, w_rt_full, b_rt_full, n_experts, N_GROUPS, tm)
    counts = counts[0, :n_experts].astype(jnp.int32)
    row_tok, dest, items = _dispatch(ids, rank, counts, rows)
    xs = h.reshape(t, d)[row_tok]
    ys = _experts(layer, items, xs, w1, w3, w2, rows)
    y0 = ys[dest[..., 0].reshape(-1)].reshape(b, s, d)
    y1 = ys[dest[..., 1].reshape(-1)].reshape(b, s, d)
    return _combine(x, y0, y1, wts, gate, final_g, tm)


def _pick_tile(s, pref):
    tm = min(pref, s)
    assert s % tm == 0
    return tm


def kernel(x, c, positions, ada_w, ada_b, norm1_g, norm2_g, even_w_in, even_w_gate2, even_b_gate, even_gla_norm_g, even_conv_w, even_conv_b, even_conv_ln_g, even_conv_ln_b, even_w_out, odd_w_qkv, odd_w_out, moe_w_grp, moe_b_grp, moe_w_rt, moe_b_rt, moe_w1, moe_w3, moe_w2, final_norm_g):
    b, s, d = x.shape
    depth = ada_w.shape[0]
    n_experts = moe_w_rt.shape[-1]
    tm = _pick_tile(s, TOKEN_TILE)
    sb = _pick_tile(s, ATTN_SUPER)
    head_dim = d // ATTN_HEADS
    hk = GLA_HEADS * GLA_DK
    hv = GLA_HEADS * GLA_DV
    conv_ch = d // 2

    mods = _ada_mods(c, ada_w, ada_b)
    mod = lambda l, j: mods[l, j][:, None, :]
    cos_t = sin_t = None

    for layer in range(depth):
        i = layer // 2
        g1 = norm1_g[layer][None, :]
        if layer % 2 == 0:
            w_in = even_w_in[i]
            main = hk + hk + hv + hv
            w_cat = jnp.concatenate([
                w_in[:, :main], w_in[:, main + GLA_GATE_RANK:], w_in[:, main:main + GLA_GATE_RANK],
                jnp.zeros((d, LANES - GLA_GATE_RANK), w_in.dtype)], axis=1).astype(BF16)
            widths = (hk, hk, hv, hv, 2 * conv_ch, LANES)
            dtypes = (F32, F32, BF16, F32, F32, F32)
            q, k, v, g, u, a_lr = _norm_matmul(x, mod(layer, 0), mod(layer, 1), g1, w_cat, widths, dtypes, tm)
            wg = jnp.concatenate([even_w_gate2[i], jnp.zeros((LANES - GLA_GATE_RANK, hk), F32)], axis=0).astype(BF16)
            o_gla = _gla(q, k, v, g, a_lr, wg, even_b_gate[i][None, :], even_gla_norm_g[i][None, :], tm)
            y_conv = _conv_module(u, even_conv_w[i], even_conv_b[i][None, :], even_conv_ln_g[i][None, :],
                                  even_conv_ln_b[i][None, :], tm)
            x = _out_proj([o_gla, y_conv], even_w_out[i].astype(BF16), x, mod(layer, 2), tm)
        else:
            if cos_t is None:
                cos_t, sin_t = _rope_tables(positions, head_dim, tm)
            q, k, v = _qkv_rope(x, mod(layer, 0), mod(layer, 1), g1, odd_w_qkv[i].astype(BF16),
                                cos_t, sin_t, head_dim, tm)
            o = _dilated_attention(q, k, v, head_dim, DILATED_BRANCHES, sb)
            x = _out_proj([o], odd_w_out[i].astype(BF16), x, mod(layer, 2), tm)

        w_rt_full = jnp.concatenate([moe_w_rt[layer], moe_w_grp[layer],
                                     jnp.zeros((d, LANES - n_experts - N_GROUPS), F32)], axis=1).astype(BF16)
        b_rt_full = jnp.concatenate([moe_b_rt[layer], moe_b_grp[layer],
                                     jnp.zeros((LANES - n_experts - N_GROUPS,), F32)])[None, :]
        final_g = final_norm_g[None, :] if layer == depth - 1 else None
        x = _hier_moe(layer, x, mod(layer, 3), mod(layer, 4), norm2_g[layer][None, :], mod(layer, 5),
                      w_rt_full, b_rt_full, moe_w1, moe_w3, moe_w2, final_g, tm, MOE_ROWS)
    return x
```

```python
import functools

import jax
import jax.numpy as jnp
import numpy as np
from jax import lax
from jax.experimental import pallas as pl
from jax.experimental.pallas import tpu as pltpu

F32 = jnp.float32
BF16 = jnp.bfloat16
HIGHEST = lax.Precision.HIGHEST

NORM_EPS = 1e-6
GLA_HEADS = 4
GLA_DK = 64
GLA_DV = 128
GLA_GATE_RANK = 16
GLA_TAU = 16.0
GLA_CHUNK = 64
CONV_WIDTH = 31
ATTN_HEADS = 16
DILATED_BRANCHES = ((128, 1), (512, 4), (2048, 16))
DIL_BLOCK = 128
ROPE_THETA = 500000.0
N_GROUPS = 4
EXPERTS_PER_GROUP = 8
TOP_K = 2
ADA_CHUNKS = 6

LANES = 128
SUBLANES = 8
VMEM_LIMIT = 56 * 1024 * 1024
TOKEN_TILE = 512
ATTN_SUPER = 2048
MOE_ROWS = 512
NEG_BIG = -1e30


def _params(*sem):
    return pltpu.CompilerParams(dimension_semantics=sem, vmem_limit_bytes=VMEM_LIMIT)


def _silu(x):
    return x * jax.nn.sigmoid(x)


def _modulate(x, g, shift, scale):
    y = x * lax.rsqrt(jnp.mean(x * x, axis=-1, keepdims=True) + NORM_EPS)
    return (y * g) * (1.0 + scale) + shift


def _ada_kernel(c_ref, w_ref, b_ref, o_ref):
    cond = _silu(c_ref[...])
    o_ref[...] = jnp.dot(cond, w_ref[...], preferred_element_type=F32, precision=HIGHEST) + b_ref[...]


def _ada_mods(c, ada_w, ada_b):
    depth, d, _ = ada_w.shape
    b = c.shape[0]
    return pl.pallas_call(
        _ada_kernel,
        out_shape=jax.ShapeDtypeStruct((depth, ADA_CHUNKS, b, d), F32),
        grid=(depth, ADA_CHUNKS),
        in_specs=[
            pl.BlockSpec((b, d), lambda l, j: (0, 0)),
            pl.BlockSpec((None, d, d), lambda l, j: (l, 0, j)),
            pl.BlockSpec((None, None, 1, d), lambda l, j: (l, j, 0, 0)),
        ],
        out_specs=pl.BlockSpec((None, None, b, d), lambda l, j: (l, j, 0, 0)),
        compiler_params=_params("arbitrary", "arbitrary"),
        name="ada_mods",
    )(c, ada_w, ada_b.reshape(depth, ADA_CHUNKS, 1, d))


def _norm_matmul_kernel(x_ref, sh_ref, sc_ref, g_ref, w_ref, *o_refs):
    h = _modulate(x_ref[...], g_ref[...], sh_ref[...], sc_ref[...]).astype(BF16)
    off = 0
    for o_ref in o_refs:
        n = o_ref.shape[-1]
        o_ref[...] = jnp.dot(h, w_ref[:, off:off + n], preferred_element_type=F32).astype(o_ref.dtype)
        off += n


def _norm_matmul(x, shift, scale, g, w, widths, dtypes, tm):
    b, s, d = x.shape
    row = lambda i, j: (i, 0, 0)
    tile = lambda i, j: (i, j, 0)
    return pl.pallas_call(
        _norm_matmul_kernel,
        out_shape=[jax.ShapeDtypeStruct((b, s, n), dt) for n, dt in zip(widths, dtypes)],
        grid=(b, s // tm),
        in_specs=[
            pl.BlockSpec((None, tm, d), tile),
            pl.BlockSpec((None, 1, d), row),
            pl.BlockSpec((None, 1, d), row),
            pl.BlockSpec((1, d), lambda i, j: (0, 0)),
            pl.BlockSpec(w.shape, lambda i, j: (0, 0)),
        ],
        out_specs=[pl.BlockSpec((None, tm, n), tile) for n in widths],
        compiler_params=_params("arbitrary", "arbitrary"),
        name="norm_matmul",
    )(x, shift, scale, g, w)


def _log_sigmoid(z):
    return jnp.minimum(z, 0.0) - jnp.log1p(jnp.exp(-jnp.abs(z)))


def _gla_kernel(q_ref, k_ref, v_ref, g_ref, a_ref, wg_ref, bg_ref, ng_ref, o_ref, state_ref, la_ref, oacc_ref):
    tm = q_ref.shape[0]
    c = GLA_CHUNK

    @pl.when(pl.program_id(1) == 0)
    def _():
        state_ref[...] = jnp.zeros_like(state_ref)

    z = jnp.dot(a_ref[...].astype(BF16), wg_ref[...], preferred_element_type=F32) + bg_ref[...]
    la_ref[...] = _log_sigmoid(z) * (1.0 / GLA_TAU)

    ri = lax.broadcasted_iota(jnp.int32, (c, c), 0)
    ci = lax.broadcasted_iota(jnp.int32, (c, c), 1)
    causal = ri >= ci
    tril = jnp.where(causal, 1.0, 0.0).astype(BF16)
    hk = GLA_HEADS * GLA_DK

    states = [state_ref[h] for h in range(GLA_HEADS)]
    for ic in range(tm // c):
        rows = slice(ic * c, (ic + 1) * c)
        la = la_ref[rows, :]
        p0 = la.astype(BF16)
        r1 = la - p0.astype(F32)
        p1 = r1.astype(BF16)
        p2 = (r1 - p1.astype(F32)).astype(BF16)
        parts = jnp.dot(tril, jnp.concatenate([p0, p1, p2], axis=1), preferred_element_type=F32)
        bcum = (parts[:, 2 * hk:] + parts[:, hk:2 * hk]) + parts[:, :hk]
        b_last = bcum[c - 1:c, :]
        q = q_ref[rows, :] * (GLA_DK ** -0.5)
        k = k_ref[rows, :]
        q_dec = (q * jnp.exp(bcum)).astype(BF16)
        k_dec = (k * jnp.exp(-bcum)).astype(BF16)
        k_rem = (k * jnp.exp(b_last - bcum)).astype(BF16)
        dec = jnp.exp(jnp.broadcast_to(b_last, (GLA_DV, hk)).T)
        for h in range(GLA_HEADS):
            ks = slice(h * GLA_DK, (h + 1) * GLA_DK)
            vs = slice(h * GLA_DV, (h + 1) * GLA_DV)
            vh = v_ref[rows, vs]
            att = lax.dot_general(q_dec[:, ks], k_dec[:, ks], (((1,), (1,)), ((), ())),
                                  preferred_element_type=F32)
            att = jnp.where(causal, att, 0.0).astype(BF16)
            st = states[h]
            o = jnp.dot(att, vh, preferred_element_type=F32)
            o = o + jnp.dot(q_dec[:, ks], st.astype(BF16), preferred_element_type=F32)
            kv = lax.dot_general(k_rem[:, ks], vh, (((0,), (0,)), ((), ())), preferred_element_type=F32)
            states[h] = dec[ks, :] * st + kv
            oacc_ref[rows, vs] = o
    for h in range(GLA_HEADS):
        state_ref[h] = states[h]

    for h in range(GLA_HEADS):
        vs = slice(h * GLA_DV, (h + 1) * GLA_DV)
        o = oacc_ref[:, vs]
        o = o * lax.rsqrt(jnp.mean(o * o, axis=-1, keepdims=True) + NORM_EPS) * ng_ref[...]
        o_ref[:, vs] = (o * _silu(g_ref[:, vs])).astype(o_ref.dtype)


def _gla(q, k, v, g, a, w_gate2, b_gate, norm_g, tm):
    b, s, _ = q.shape
    hk = GLA_HEADS * GLA_DK
    hv = GLA_HEADS * GLA_DV
    tile = lambda i, j: (i, j, 0)
    const = lambda i, j: (0, 0)
    return pl.pallas_call(
        _gla_kernel,
        out_shape=jax.ShapeDtypeStruct((b, s, hv), BF16),
        grid=(b, s // tm),
        in_specs=[
            pl.BlockSpec((None, tm, hk), tile),
            pl.BlockSpec((None, tm, hk), tile),
            pl.BlockSpec((None, tm, hv), tile),
            pl.BlockSpec((None, tm, hv), tile),
            pl.BlockSpec((None, tm, a.shape[-1]), tile),
            pl.BlockSpec(w_gate2.shape, const),
            pl.BlockSpec((1, hk), const),
            pl.BlockSpec((1, GLA_DV), const),
        ],
        out_specs=pl.BlockSpec((None, tm, hv), tile),
        scratch_shapes=[
            pltpu.VMEM((GLA_HEADS, GLA_DK, GLA_DV), F32),
            pltpu.VMEM((tm, hk), F32),
            pltpu.VMEM((tm, hv), F32),
        ],
        compiler_params=_params("arbitrary", "arbitrary"),
        name="gla",
    )(q, k, v, g, a, w_gate2, b_gate, norm_g)


CONV_HALO = 32


def _conv_kernel(u_ref, w_ref, cb_ref, lg_ref, lb_ref, o_ref, buf_ref):
    tm = u_ref.shape[0]
    ch = o_ref.shape[-1]

    @pl.when(pl.program_id(1) == 0)
    def _():
        buf_ref[0:CONV_HALO, :] = jnp.zeros((CONV_HALO, ch), F32)

    buf_ref[CONV_HALO:, :] = u_ref[:, :ch] * jax.nn.sigmoid(u_ref[:, ch:])
    base = CONV_HALO - (CONV_WIDTH - 1)
    acc = None
    for b in range(SUBLANES):
        part = None
        span = tm + (SUBLANES if b else 0)
        for a in range((base + CONV_WIDTH - 1) // SUBLANES + 1):
            j = SUBLANES * a + b - base
            if 0 <= j < CONV_WIDTH:
                term = buf_ref[SUBLANES * a:SUBLANES * a + span, :] * w_ref[j:j + 1, :]
                part = term if part is None else part + term
        if part is not None:
            part = part[b:b + tm, :]
            acc = part if acc is None else acc + part
    buf_ref[0:CONV_HALO, :] = buf_ref[tm:tm + CONV_HALO, :]
    y = acc + cb_ref[...]
    mu = jnp.mean(y, axis=-1, keepdims=True)
    var = jnp.mean(jnp.square(y - mu), axis=-1, keepdims=True)
    y = (y - mu) * lax.rsqrt(var + NORM_EPS) * lg_ref[...] + lb_ref[...]
    o_ref[...] = _silu(y).astype(o_ref.dtype)


def _conv_module(u, conv_w, conv_b, ln_g, ln_b, tm):
    b, s, two_ch = u.shape
    ch = two_ch // 2
    tile = lambda i, j: (i, j, 0)
    const = lambda i, j: (0, 0)
    return pl.pallas_call(
        _conv_kernel,
        out_shape=jax.ShapeDtypeStruct((b, s, ch), BF16),
        grid=(b, s // tm),
        in_specs=[
            pl.BlockSpec((None, tm, two_ch), tile),
            pl.BlockSpec(conv_w.shape, const),
            pl.BlockSpec((1, ch), const),
            pl.BlockSpec((1, ch), const),
            pl.BlockSpec((1, ch), const),
        ],
        out_specs=pl.BlockSpec((None, tm, ch), tile),
        scratch_shapes=[pltpu.VMEM((tm + CONV_HALO, ch), F32)],
        compiler_params=_params("arbitrary", "arbitrary"),
        name="conv_module",
    )(u, conv_w, conv_b, ln_g, ln_b)


def _out_proj_kernel(*refs):
    *a_refs, w_ref, x_ref, gate_ref, o_ref = refs
    acc = None
    off = 0
    for a_ref in a_refs:
        kk = a_ref.shape[-1]
        part = jnp.dot(a_ref[...].astype(BF16), w_ref[off:off + kk, :], preferred_element_type=F32)
        acc = part if acc is None else acc + part
        off += kk
    o_ref[...] = x_ref[...] + gate_ref[...] * acc


def _out_proj(acts, w, x, gate, tm):
    b, s, d = x.shape
    tile = lambda i, j: (i, j, 0)
    return pl.pallas_call(
        _out_proj_kernel,
        out_shape=jax.ShapeDtypeStruct((b, s, d), F32),
        grid=(b, s // tm),
        in_specs=[pl.BlockSpec((None, tm, a.shape[-1]), tile) for a in acts] + [
            pl.BlockSpec(w.shape, lambda i, j: (0, 0)),
            pl.BlockSpec((None, tm, d), tile),
            pl.BlockSpec((None, 1, d), lambda i, j: (i, 0, 0)),
        ],
        out_specs=pl.BlockSpec((None, tm, d), tile),
        compiler_params=_params("arbitrary", "arbitrary"),
        name="out_proj",
    )(*acts, w, x, gate)


def _rope_table_kernel(pos_ref, freq_ref, sign_ref, cos_ref, sin_ref):
    ang = pos_ref[...] * freq_ref[...]
    cos_ref[...] = jnp.cos(ang)
    sin_ref[...] = jnp.sin(ang) * sign_ref[...]


def _rope_tables(positions, head_dim, tm):
    b, s = positions.shape
    rope_dims = head_dim // 4
    half = rope_dims // 2
    inv_freq = ROPE_THETA ** (-jnp.arange(0, rope_dims, 2, dtype=F32) / rope_dims)
    jj = jnp.arange(LANES) % head_dim
    freq = jnp.where(jj < rope_dims, inv_freq[jj % half], 0.0).astype(F32)[None, :]
    sign = jnp.where(jj < half, -1.0, jnp.where(jj < rope_dims, 1.0, 0.0)).astype(F32)[None, :]
    pos = positions.astype(F32)[..., None]
    tile = lambda i, j: (i, j, 0)
    const = lambda i, j: (0, 0)
    return pl.pallas_call(
        _rope_table_kernel,
        out_shape=[jax.ShapeDtypeStruct((b, s, LANES), F32)] * 2,
        grid=(b, s // tm),
        in_specs=[pl.BlockSpec((None, tm, 1), tile), pl.BlockSpec((1, LANES), const),
                  pl.BlockSpec((1, LANES), const)],
        out_specs=[pl.BlockSpec((None, tm, LANES), tile)] * 2,
        compiler_params=_params("arbitrary", "arbitrary"),
        name="rope_tables",
    )(pos, freq, sign)


def _qkv_kernel(x_ref, sh_ref, sc_ref, g_ref, w_ref, cos_ref, sin_ref, q_ref, k_ref, v_ref, *, head_dim):
    h = _modulate(x_ref[...], g_ref[...], sh_ref[...], sc_ref[...]).astype(BF16)
    d = q_ref.shape[-1]
    half = head_dim // 8
    cosf = jnp.tile(cos_ref[...], (1, d // LANES))
    sinf = jnp.tile(sin_ref[...], (1, d // LANES))
    lane = lax.broadcasted_iota(jnp.int32, (1, d), 1)
    first = (lane % head_dim) < half
    for idx, (o_ref, mult) in enumerate(((q_ref, head_dim ** -0.5), (k_ref, 1.0))):
        t = jnp.dot(h, w_ref[:, idx * d:(idx + 1) * d], preferred_element_type=F32)
        partner = jnp.where(first, pltpu.roll(t, d - half, 1), pltpu.roll(t, half, 1))
        o_ref[...] = ((t * cosf + partner * sinf) * mult).astype(o_ref.dtype)
    v_ref[...] = jnp.dot(h, w_ref[:, 2 * d:], preferred_element_type=F32).astype(v_ref.dtype)


def _qkv_rope(x, shift, scale, g, w, cos_t, sin_t, head_dim, tm):
    b, s, d = x.shape
    row = lambda i, j: (i, 0, 0)
    tile = lambda i, j: (i, j, 0)
    return pl.pallas_call(
        functools.partial(_qkv_kernel, head_dim=head_dim),
        out_shape=[jax.ShapeDtypeStruct((b, s, d), BF16)] * 3,
        grid=(b, s // tm),
        in_specs=[
            pl.BlockSpec((None, tm, d), tile),
            pl.BlockSpec((None, 1, d), row),
            pl.BlockSpec((None, 1, d), row),
            pl.BlockSpec((1, d), lambda i, j: (0, 0)),
            pl.BlockSpec(w.shape, lambda i, j: (0, 0)),
            pl.BlockSpec((None, tm, LANES), tile),
            pl.BlockSpec((None, tm, LANES), tile),
        ],
        out_specs=[pl.BlockSpec((None, tm, d), tile)] * 3,
        compiler_params=_params("arbitrary", "arbitrary"),
        name="qkv_rope",
    )(x, shift, scale, g, w, cos_t, sin_t)


ONES_ROWS = 16
ATTN_UNROLL = 16


def _attn_bias(branches, heads):
    blk = DIL_BLOCK
    kj = np.arange(2 * blk)[:, None]
    qi = np.arange(blk)[None, :]
    dist = qi + blk - kj
    out = []
    for window, dil in branches:
        band = (dist >= 0) & (dist <= window // dil)
        both = np.stack([band, band & (kj >= blk)])
        out.append(np.tile(np.where(both, 0.0, NEG_BIG), (1, 1, heads)))
    return jnp.asarray(np.stack(out), F32)


def _attn_kernel(bias_ref, q_ref, kc_ref, kp_ref, vc_ref, vp_ref, o_ref, qf, kf, vf, ob, lb, st_s, vt_s,
                 *, head_dim, branches):
    sb = q_ref.shape[0]
    blk = DIL_BLOCK
    heads = q_ref.shape[1] // head_dim
    first_super = pl.program_id(2) == 0

    qf[...] = q_ref[...].astype(F32)
    kf[0:sb, :] = kp_ref[...].astype(F32)
    kf[sb:, :] = kc_ref[...].astype(F32)
    vf[0:sb, :] = vp_ref[...].astype(F32)
    vf[sb:, :] = vc_ref[...].astype(F32)

    vt_s[:, LANES:, :] = jnp.ones((ATTN_UNROLL, ONES_ROWS, 2 * blk), BF16)

    lane = lax.broadcasted_iota(jnp.int32, (blk, LANES), 1)
    head_masks = [(lane >= h * head_dim) & (lane < (h + 1) * head_dim) for h in range(heads)]

    for bi, (window, dil) in enumerate(branches):
        assert dil & (dil - 1) == 0 and window // dil <= blk
        unit = dil * blk
        shift = dil.bit_length() - 1

        def scores(j, idx, dil=dil, unit=unit, bi=bi, shift=shift):
            u = lax.shift_right_logical(idx, shift)
            r = idx & (dil - 1)
            q0 = u * unit + r
            k0 = sb + q0 - unit
            no_prev = jnp.where(first_super & (u == 0), 1, 0)
            qb = qf[pl.ds(q0, blk, stride=dil), :]
            q2 = jnp.concatenate([jnp.where(mk, qb, 0.0) for mk in head_masks], axis=0).astype(BF16)
            kb = kf[pl.ds(k0, 2 * blk, stride=dil), :].astype(BF16)
            vt_s[j, 0:LANES, :] = vf[pl.ds(k0, 2 * blk, stride=dil), :].T.astype(BF16)
            st = lax.dot_general(kb, q2, (((1,), (1,)), ((), ())), preferred_element_type=F32)
            st_s[j] = st + bias_ref[bi, no_prev]
            return q0

        def softmax_pv(j):
            m = jnp.max(st_s[j], axis=0, keepdims=True)
            p = jnp.exp(st_s[j] - m).astype(BF16)
            of = jnp.dot(vt_s[j], p, preferred_element_type=F32)
            l = of[LANES:LANES + 1, :]
            lse = m + jnp.log(l)
            o_rows, lse_rows = [], []
            for h in range(heads):
                cols = slice(h * blk, (h + 1) * blk)
                o_rows.append(of[h * head_dim:(h + 1) * head_dim, cols] / l[:, cols])
                lse_rows.append(jnp.broadcast_to(lse[:, cols], (head_dim, blk)))
            return jnp.concatenate(o_rows, axis=0).T, jnp.concatenate(lse_rows, axis=0).T

        def block_body(it, carry, scores=scores, softmax_pv=softmax_pv, dil=dil, bi=bi):
            starts = [scores(j, it * ATTN_UNROLL + j) for j in range(ATTN_UNROLL)]
            done = [softmax_pv(j) for j in range(ATTN_UNROLL)]
            for q0, (o_tok, lse_tok) in zip(starts, done):
                ob[bi, pl.ds(q0, blk, stride=dil), :] = o_tok
                lb[bi, pl.ds(q0, blk, stride=dil), :] = lse_tok
            return carry

        lax.fori_loop(0, sb // blk // ATTN_UNROLL, block_body, 0)

    nb = len(branches)
    m = lb[0]
    for bi in range(1, nb):
        m = jnp.maximum(m, lb[bi])
    num = jnp.zeros_like(m)
    den = jnp.zeros_like(m)
    for bi in range(nb):
        e = jnp.exp(lb[bi] - m)
        num = num + e * ob[bi]
        den = den + e
    o_ref[...] = (num / den).astype(o_ref.dtype)


def _dilated_attention(q, k, v, head_dim, branches, sb):
    b, s, d = q.shape
    groups = d // LANES
    cur = lambda i, g, n: (i, n, g)
    prev = lambda i, g, n: (i, jnp.maximum(n - 1, 0), g)
    nb = len(branches)
    bias = _attn_bias(branches, LANES // head_dim)
    return pl.pallas_call(
        functools.partial(_attn_kernel, head_dim=head_dim, branches=branches),
        out_shape=jax.ShapeDtypeStruct((b, s, d), BF16),
        grid=(b, groups, s // sb),
        in_specs=[
            pl.BlockSpec(bias.shape, lambda i, g, n: (0, 0, 0, 0)),
            pl.BlockSpec((None, sb, LANES), cur),
            pl.BlockSpec((None, sb, LANES), cur),
            pl.BlockSpec((None, sb, LANES), prev),
            pl.BlockSpec((None, sb, LANES), cur),
            pl.BlockSpec((None, sb, LANES), prev),
        ],
        out_specs=pl.BlockSpec((None, sb, LANES), cur),
        scratch_shapes=[
            pltpu.VMEM((sb, LANES), F32),
            pltpu.VMEM((2 * sb, LANES), F32),
            pltpu.VMEM((2 * sb, LANES), F32),
            pltpu.VMEM((nb, sb, LANES), F32),
            pltpu.VMEM((nb, sb, LANES), F32),
            pltpu.VMEM((ATTN_UNROLL, 2 * DIL_BLOCK, LANES // head_dim * DIL_BLOCK), F32),
            pltpu.VMEM((ATTN_UNROLL, LANES + ONES_ROWS, 2 * DIL_BLOCK), BF16),
        ],
        compiler_params=_params("arbitrary", "arbitrary", "arbitrary"),
        name="dilated_attention",
    )(bias, q, k, k, v, v)


def _router_kernel(x_ref, sh_ref, sc_ref, g_ref, w_ref, b_ref, h_ref, ids_ref, wts_ref, rank_ref, counts_ref,
                   carry_ref, *, n_experts, n_groups):
    h = _modulate(x_ref[...], g_ref[...], sh_ref[...], sc_ref[...]).astype(BF16)
    h_ref[...] = h
    logits = jnp.dot(h, w_ref[...], preferred_element_type=F32) + b_ref[...]
    tm = logits.shape[0]
    epg = n_experts // n_groups
    lane = lax.broadcasted_iota(jnp.int32, logits.shape, 1)
    neg = -jnp.inf
    big = jnp.int32(LANES)

    def first_max(mask):
        val = jnp.max(jnp.where(mask, logits, neg), axis=-1, keepdims=True)
        idx = jnp.min(jnp.where(mask & (logits == val), lane, big), axis=-1, keepdims=True)
        return val, idx

    gmask = (lane >= n_experts) & (lane < n_experts + n_groups)
    gmax, gidx = first_max(gmask)
    gsum = jnp.sum(jnp.where(gmask, jnp.exp(logits - gmax), 0.0), axis=-1, keepdims=True)
    g_w = 1.0 / gsum
    grp = gidx - n_experts
    assert epg & (epg - 1) == 0
    emask = (lane < n_experts) & (lax.shift_right_logical(lane, epg.bit_length() - 1) == grp)
    v1, i1 = first_max(emask)
    v2, i2 = first_max(emask & (lane != i1))
    e2 = jnp.exp(v2 - v1)
    den = 1.0 + e2
    col = lax.broadcasted_iota(jnp.int32, (tm, TOP_K), 1)
    ids_ref[...] = jnp.where(col == 0, i1, i2)
    wts_ref[...] = jnp.where(col == 0, 1.0 / den, e2 / den) * g_w

    @pl.when((pl.program_id(0) == 0) & (pl.program_id(1) == 0))
    def _():
        carry_ref[...] = jnp.zeros_like(carry_ref)

    hit1 = lane == i1
    hit2 = lane == i2
    onehot = jnp.where(hit1 | hit2, 1.0, 0.0)
    ri = lax.broadcasted_iota(jnp.int32, (tm, tm), 0)
    ci = lax.broadcasted_iota(jnp.int32, (tm, tm), 1)
    before = jnp.where(ci < ri, 1.0, 0.0).astype(BF16)
    prefix = jnp.dot(before, onehot.astype(BF16), preferred_element_type=F32) + carry_ref[0:1, :]
    r1 = jnp.sum(jnp.where(hit1, prefix, 0.0), axis=-1, keepdims=True)
    r2 = jnp.sum(jnp.where(hit2, prefix, 0.0), axis=-1, keepdims=True)
    rank_ref[...] = jnp.where(col == 0, r1, r2).astype(jnp.int32)
    carry_ref[...] = carry_ref[...] + jnp.sum(onehot, axis=0, keepdims=True)
    counts_ref[...] = carry_ref[...]


def _router(x, shift, scale, g, w_rt, b_rt, n_experts, n_groups, tm):
    b, s, d = x.shape
    row = lambda i, j: (i, 0, 0)
    tile = lambda i, j: (i, j, 0)
    const = lambda i, j: (0, 0)
    return pl.pallas_call(
        functools.partial(_router_kernel, n_experts=n_experts, n_groups=n_groups),
        out_shape=[jax.ShapeDtypeStruct((b, s, d), BF16),
                   jax.ShapeDtypeStruct((b, s, TOP_K), jnp.int32),
                   jax.ShapeDtypeStruct((b, s, TOP_K), F32),
                   jax.ShapeDtypeStruct((b, s, TOP_K), jnp.int32),
                   jax.ShapeDtypeStruct((SUBLANES, LANES), F32)],
        grid=(b, s // tm),
        in_specs=[
            pl.BlockSpec((None, tm, d), tile),
            pl.BlockSpec((None, 1, d), row),
            pl.BlockSpec((None, 1, d), row),
            pl.BlockSpec((1, d), const),
            pl.BlockSpec((d, LANES), const),
            pl.BlockSpec((1, LANES), const),
        ],
        out_specs=[pl.BlockSpec((None, tm, d), tile),
                   pl.BlockSpec((None, tm, TOP_K), tile),
                   pl.BlockSpec((None, tm, TOP_K), tile),
                   pl.BlockSpec((None, tm, TOP_K), tile),
                   pl.BlockSpec((SUBLANES, LANES), const)],
        scratch_shapes=[pltpu.VMEM((SUBLANES, LANES), F32)],
        compiler_params=_params("arbitrary", "arbitrary"),
        name="moe_router",
    )(x, shift, scale, g, w_rt, b_rt)


def _expert_kernel(ib_ref, ie_ref, lo_ref, hi_ref, xs_ref, w1_ref, w3_ref, w2_ref, ys_ref, w1b, w3b, w2b):
    j = pl.program_id(0)
    prev = jnp.maximum(j - 1, 0)
    e_changed = (j == 0) | (ie_ref[j] != ie_ref[prev])
    first_of_block = (j == 0) | (ib_ref[j] != ib_ref[prev])
    lo = lo_ref[j]
    hi = hi_ref[j]

    @pl.when(e_changed)
    def _():
        w1b[...] = w1_ref[...].astype(BF16)
        w3b[...] = w3_ref[...].astype(BF16)
        w2b[...] = w2_ref[...].astype(BF16)

    rows = ys_ref.shape[0]
    whole = (lo == 0) & (hi == rows)

    @pl.when(first_of_block & jnp.logical_not(whole))
    def _():
        ys_ref[...] = jnp.zeros_like(ys_ref)

    @pl.when(hi > lo)
    def _():
        x = xs_ref[...]
        a = jnp.dot(x, w1b[...], preferred_element_type=F32)
        g = jnp.dot(x, w3b[...], preferred_element_type=F32)
        y = jnp.dot((_silu(a) * g).astype(BF16), w2b[...], preferred_element_type=F32)

        @pl.when(whole)
        def _():
            ys_ref[...] = y

        @pl.when(jnp.logical_not(whole))
        def _():
            row = lax.broadcasted_iota(jnp.int32, (rows, 1), 0)
            ys_ref[...] = jnp.where((row >= lo) & (row < hi), y, ys_ref[...])


def _experts(layer, items, xs, w1, w3, w2, rows):
    a, d = xs.shape
    hid = w1.shape[-1]
    blk = lambda j, ib, ie, lo, hi: (ib[j], 0)
    wsel = lambda j, ib, ie, lo, hi: (layer, ie[j], 0, 0)
    grid_spec = pltpu.PrefetchScalarGridSpec(
        num_scalar_prefetch=4,
        grid=(items[0].shape[0],),
        in_specs=[
            pl.BlockSpec((rows, d), blk),
            pl.BlockSpec((None, None, d, hid), wsel),
            pl.BlockSpec((None, None, d, hid), wsel),
            pl.BlockSpec((None, None, hid, d), wsel),
        ],
        out_specs=pl.BlockSpec((rows, d), blk),
        scratch_shapes=[pltpu.VMEM((d, hid), BF16), pltpu.VMEM((d, hid), BF16), pltpu.VMEM((hid, d), BF16)],
    )
    return pl.pallas_call(
        _expert_kernel,
        out_shape=jax.ShapeDtypeStruct((a, d), F32),
        grid_spec=grid_spec,
        compiler_params=_params("arbitrary"),
        name="moe_experts",
    )(*items, xs, w1, w3, w2)


def _dispatch(ids, rank, counts, rows):
    n_experts = counts.shape[0]
    a = ids.size
    i32 = jnp.int32
    order = jnp.argsort(ids.reshape(-1), stable=True).astype(i32)
    row_tok = order // TOP_K
    ends = jnp.cumsum(counts)
    starts = ends - counts
    dest = rank + jnp.sum(jnp.where(ids[..., None] == jnp.arange(n_experts, dtype=i32), starts, 0), axis=-1)
    n_blk = a // rows
    bstart = jnp.arange(n_blk, dtype=i32) * rows
    count_le = lambda bounds, x: jnp.sum((bounds[None, :] <= x[:, None]).astype(i32), axis=1)
    e_lo = jnp.minimum(count_le(ends, bstart), n_experts - 1)
    e_hi = jnp.minimum(count_le(ends, bstart + rows - 1), n_experts - 1)
    n_items = e_hi - e_lo + 1
    item_end = jnp.cumsum(n_items)
    item_first = item_end - n_items
    jj = jnp.arange(n_blk + n_experts - 1, dtype=i32)
    valid = jj < item_end[-1]
    ib = jnp.minimum(count_le(item_end, jj), n_blk - 1)
    ie = jnp.where(valid, jnp.clip(e_lo[ib] + jj - item_first[ib], 0, n_experts - 1), e_hi[n_blk - 1]).astype(i32)
    lo = jnp.where(valid, jnp.clip(starts[ie] - ib * rows, 0, rows), 0).astype(i32)
    hi = jnp.where(valid, jnp.clip(ends[ie] - ib * rows, 0, rows), 0).astype(i32)
    return row_tok, dest, (ib, ie, lo, hi)


def _combine_kernel(x_ref, y0_ref, y1_ref, w_ref, gate_ref, o_ref):
    y = y0_ref[...] * w_ref[:, 0:1] + y1_ref[...] * w_ref[:, 1:2]
    o_ref[...] = x_ref[...] + gate_ref[...] * y


def _combine_norm_kernel(x_ref, y0_ref, y1_ref, w_ref, gate_ref, ng_ref, o_ref):
    y = y0_ref[...] * w_ref[:, 0:1] + y1_ref[...] * w_ref[:, 1:2]
    x = x_ref[...] + gate_ref[...] * y
    o_ref[...] = x * lax.rsqrt(jnp.mean(x * x, axis=-1, keepdims=True) + NORM_EPS) * ng_ref[...]


def _combine(x, y0, y1, wts, gate, final_g, tm):
    b, s, d = x.shape
    tile = lambda i, j: (i, j, 0)
    in_specs = [
        pl.BlockSpec((None, tm, d), tile),
        pl.BlockSpec((None, tm, d), tile),
        pl.BlockSpec((None, tm, d), tile),
        pl.BlockSpec((None, tm, TOP_K), tile),
        pl.BlockSpec((None, 1, d), lambda i, j: (i, 0, 0)),
    ]
    args = [x, y0, y1, wts, gate]
    kern = _combine_kernel
    if final_g is not None:
        in_specs.append(pl.BlockSpec((1, d), lambda i, j: (0, 0)))
        args.append(final_g)
        kern = _combine_norm_kernel
    return pl.pallas_call(
        kern,
        out_shape=jax.ShapeDtypeStruct((b, s, d), F32),
        grid=(b, s // tm),
        in_specs=in_specs,
        out_specs=pl.BlockSpec((None, tm, d), tile),
        compiler_params=_params("arbitrary", "arbitrary"),
        name="moe_combine",
    )(*args)


def _hier_moe(layer, x, shift, scale, norm_g, gate, w_rt_full, b_rt_full, w1, w3, w2, final_g, tm, rows):
    b, s, d = x.shape
    t = b * s
    n_experts = w1.shape[1]
    h, ids, wts, rank, counts = _router(x, shift, scale, norm_g, w_rt_full, b_rt_full, n_experts, N_GROUPS, tm)
    counts = counts[0, :n_experts].astype(jnp.int32)
    row_tok, dest, items = _dispatch(ids, rank, counts, rows)
    xs = h.reshape(t, d)[row_tok]
    ys = _experts(layer, items, xs, w1, w3, w2, rows)
    y0 = ys[dest[..., 0].reshape(-1)].reshape(b, s, d)
    y1 = ys[dest[..., 1].reshape(-1)].reshape(b, s, d)
    return _combine(x, y0, y1, wts, gate, final_g, tm)


def _pick_tile(s, pref):
    tm = min(pref, s)
    assert s % tm == 0
    return tm


def kernel(x, c, positions, ada_w, ada_b, norm1_g, norm2_g, even_w_in, even_w_gate2, even_b_gate, even_gla_norm_g, even_conv_w, even_conv_b, even_conv_ln_g, even_conv_ln_b, even_w_out, odd_w_qkv, odd_w_out, moe_w_grp, moe_b_grp, moe_w_rt, moe_b_rt, moe_w1, moe_w3, moe_w2, final_norm_g):
    b, s, d = x.shape
    depth = ada_w.shape[0]
    n_experts = moe_w_rt.shape[-1]
    tm = _pick_tile(s, TOKEN_TILE)
    sb = _pick_tile(s, ATTN_SUPER)
    head_dim = d // ATTN_HEADS
    hk = GLA_HEADS * GLA_DK
    hv = GLA_HEADS * GLA_DV
    conv_ch = d // 2

    mods = _ada_mods(c, ada_w, ada_b)
    mod = lambda l, j: mods[l, j][:, None, :]
    cos_t = sin_t = None

    for layer in range(depth):
        i = layer // 2
        g1 = norm1_g[layer][None, :]
        if layer % 2 == 0:
            w_in = even_w_in[i]
            main = hk + hk + hv + hv
            w_cat = jnp.concatenate([
                w_in[:, :main], w_in[:, main + GLA_GATE_RANK:], w_in[:, main:main + GLA_GATE_RANK],
                jnp.zeros((d, LANES - GLA_GATE_RANK), w_in.dtype)], axis=1).astype(BF16)
            widths = (hk, hk, hv, hv, 2 * conv_ch, LANES)
            dtypes = (F32, F32, BF16, F32, F32, F32)
            q, k, v, g, u, a_lr = _norm_matmul(x, mod(layer, 0), mod(layer, 1), g1, w_cat, widths, dtypes, tm)
            wg = jnp.concatenate([even_w_gate2[i], jnp.zeros((LANES - GLA_GATE_RANK, hk), F32)], axis=0).astype(BF16)
            o_gla = _gla(q, k, v, g, a_lr, wg, even_b_gate[i][None, :], even_gla_norm_g[i][None, :], tm)
            y_conv = _conv_module(u, even_conv_w[i], even_conv_b[i][None, :], even_conv_ln_g[i][None, :],
                                  even_conv_ln_b[i][None, :], tm)
            x = _out_proj([o_gla, y_conv], even_w_out[i].astype(BF16), x, mod(layer, 2), tm)
        else:
            if cos_t is None:
                cos_t, sin_t = _rope_tables(positions, head_dim, tm)
            q, k, v = _qkv_rope(x, mod(layer, 0), mod(layer, 1), g1, odd_w_qkv[i].astype(BF16),
                                cos_t, sin_t, head_dim, tm)
            o = _dilated_attention(q, k, v, head_dim, DILATED_BRANCHES, sb)
            x = _out_proj([o], odd_w_out[i].astype(BF16), x, mod(layer, 2), tm)

        w_rt_full = jnp.concatenate([moe_w_rt[layer], moe_w_grp[layer],
                                     jnp.zeros((d, LANES - n_experts - N_GROUPS), F32)], axis=1).astype(BF16)
        b_rt_full = jnp.concatenate([moe_b_rt[layer], moe_b_grp[layer],
                                     jnp.zeros((LANES - n_experts - N_GROUPS,), F32)])[None, :]
        final_g = final_norm_g[None, :] if layer == depth - 1 else None
        x = _hier_moe(layer, x, mod(layer, 3), mod(layer, 4), norm2_g[layer][None, :], mod(layer, 5),
                      w_rt_full, b_rt_full, moe_w1, moe_w3, moe_w2, final_g, tm, MOE_ROWS)
    return x
```

```python
import functools

import jax
import jax.numpy as jnp
import numpy as np
from jax import lax
from jax.experimental import pallas as pl
from jax.experimental.pallas import tpu as pltpu

F32 = jnp.float32
BF16 = jnp.bfloat16
HIGHEST = lax.Precision.HIGHEST

NORM_EPS = 1e-6
GLA_HEADS = 4
GLA_DK = 64
GLA_DV = 128
GLA_GATE_RANK = 16
GLA_TAU = 16.0
GLA_CHUNK = 64
CONV_WIDTH = 31
ATTN_HEADS = 16
DILATED_BRANCHES = ((128, 1), (512, 4), (2048, 16))
DIL_BLOCK = 128
ROPE_THETA = 500000.0
N_GROUPS = 4
EXPERTS_PER_GROUP = 8
TOP_K = 2
ADA_CHUNKS = 6

LANES = 128
SUBLANES = 8
VMEM_LIMIT = 56 * 1024 * 1024
TOKEN_TILE = 512
ATTN_SUPER = 2048
MOE_ROWS = 512
NEG_BIG = -1e30


def _params(*sem):
    return pltpu.CompilerParams(dimension_semantics=sem, vmem_limit_bytes=VMEM_LIMIT)


def _silu(x):
    return x * jax.nn.sigmoid(x)


def _modulate(x, g, shift, scale):
    y = x * lax.rsqrt(jnp.mean(x * x, axis=-1, keepdims=True) + NORM_EPS)
    return (y * g) * (1.0 + scale) + shift


def _ada_kernel(c_ref, w_ref, b_ref, o_ref):
    cond = _silu(c_ref[...])
    o_ref[...] = jnp.dot(cond, w_ref[...], preferred_element_type=F32, precision=HIGHEST) + b_ref[...]


def _ada_mods(c, ada_w, ada_b):
    depth, d, _ = ada_w.shape
    b = c.shape[0]
    return pl.pallas_call(
        _ada_kernel,
        out_shape=jax.ShapeDtypeStruct((depth, ADA_CHUNKS, b, d), F32),
        grid=(depth, ADA_CHUNKS),
        in_specs=[
            pl.BlockSpec((b, d), lambda l, j: (0, 0)),
            pl.BlockSpec((None, d, d), lambda l, j: (l, 0, j)),
            pl.BlockSpec((None, None, 1, d), lambda l, j: (l, j, 0, 0)),
        ],
        out_specs=pl.BlockSpec((None, None, b, d), lambda l, j: (l, j, 0, 0)),
        compiler_params=_params("arbitrary", "arbitrary"),
        name="ada_mods",
    )(c, ada_w, ada_b.reshape(depth, ADA_CHUNKS, 1, d))


def _norm_matmul_kernel(x_ref, sh_ref, sc_ref, g_ref, w_ref, *o_refs):
    h = _modulate(x_ref[...], g_ref[...], sh_ref[...], sc_ref[...]).astype(BF16)
    off = 0
    for o_ref in o_refs:
        n = o_ref.shape[-1]
        o_ref[...] = jnp.dot(h, w_ref[:, off:off + n], preferred_element_type=F32).astype(o_ref.dtype)
        off += n


def _norm_matmul(x, shift, scale, g, w, widths, dtypes, tm):
    b, s, d = x.shape
    row = lambda i, j: (i, 0, 0)
    tile = lambda i, j: (i, j, 0)
    return pl.pallas_call(
        _norm_matmul_kernel,
        out_shape=[jax.ShapeDtypeStruct((b, s, n), dt) for n, dt in zip(widths, dtypes)],
        grid=(b, s // tm),
        in_specs=[
            pl.BlockSpec((None, tm, d), tile),
            pl.BlockSpec((None, 1, d), row),
            pl.BlockSpec((None, 1, d), row),
            pl.BlockSpec((1, d), lambda i, j: (0, 0)),
            pl.BlockSpec(w.shape, lambda i, j: (0, 0)),
        ],
        out_specs=[pl.BlockSpec((None, tm, n), tile) for n in widths],
        compiler_params=_params("arbitrary", "arbitrary"),
        name="norm_matmul",
    )(x, shift, scale, g, w)


def _log_sigmoid(z):
    return jnp.minimum(z, 0.0) - jnp.log1p(jnp.exp(-jnp.abs(z)))


def _gla_kernel(q_ref, k_ref, v_ref, g_ref, a_ref, wg_ref, bg_ref, ng_ref, o_ref, state_ref, la_ref, oacc_ref):
    tm = q_ref.shape[0]
    c = GLA_CHUNK

    @pl.when(pl.program_id(1) == 0)
    def _():
        state_ref[...] = jnp.zeros_like(state_ref)

    z = jnp.dot(a_ref[...].astype(BF16), wg_ref[...], preferred_element_type=F32) + bg_ref[...]
    la_ref[...] = _log_sigmoid(z) * (1.0 / GLA_TAU)

    ri = lax.broadcasted_iota(jnp.int32, (c, c), 0)
    ci = lax.broadcasted_iota(jnp.int32, (c, c), 1)
    causal = ri >= ci
    tril = jnp.where(causal, 1.0, 0.0).astype(BF16)
    hk = GLA_HEADS * GLA_DK

    nh = GLA_HEADS
    lane_head = lax.shift_right_logical(lax.broadcasted_iota(jnp.int32, (c, hk), 1), GLA_DK.bit_length() - 1)
    r4 = lax.broadcasted_iota(jnp.int32, (nh * c, nh * c), 0)
    c4 = lax.broadcasted_iota(jnp.int32, (nh * c, nh * c), 1)
    shift_c = c.bit_length() - 1
    causal4 = (lax.shift_right_logical(r4, shift_c) == lax.shift_right_logical(c4, shift_c)) & (r4 >= c4)

    def stack_heads(t):
        return jnp.concatenate([jnp.where(lane_head == h, t, 0.0) for h in range(nh)], axis=0).astype(BF16)

    state = state_ref[...]
    for ic in range(tm // c):
        rows = slice(ic * c, (ic + 1) * c)
        la = la_ref[rows, :]
        p0 = la.astype(BF16)
        r1 = la - p0.astype(F32)
        p1 = r1.astype(BF16)
        p2 = (r1 - p1.astype(F32)).astype(BF16)
        parts = jnp.dot(tril, jnp.concatenate([p0, p1, p2], axis=1), preferred_element_type=F32)
        bcum = (parts[:, 2 * hk:] + parts[:, hk:2 * hk]) + parts[:, :hk]
        b_last = bcum[c - 1:c, :]
        q = q_ref[rows, :] * (GLA_DK ** -0.5)
        k = k_ref[rows, :]
        q4 = stack_heads(q * jnp.exp(bcum))
        k4 = stack_heads(k * jnp.exp(-bcum))
        kr4 = stack_heads(k * jnp.exp(b_last - bcum))
        dec = jnp.exp(jnp.broadcast_to(b_last, (GLA_DV, hk)).T)
        v4 = jnp.concatenate([v_ref[rows, h * GLA_DV:(h + 1) * GLA_DV] for h in range(nh)], axis=0)
        att = lax.dot_general(q4, k4, (((1,), (1,)), ((), ())), preferred_element_type=F32)
        att = jnp.where(causal4, att, 0.0).astype(BF16)
        o4 = jnp.dot(att, v4, preferred_element_type=F32)
        o4 = o4 + jnp.dot(q4, state.astype(BF16), preferred_element_type=F32)
        kv = lax.dot_general(kr4, v4, (((0,), (0,)), ((), ())), preferred_element_type=F32)
        state = dec * state + kv
        oacc_ref[rows, :] = jnp.concatenate([o4[h * c:(h + 1) * c, :] for h in range(nh)], axis=1)
    state_ref[...] = state

    for h in range(GLA_HEADS):
        vs = slice(h * GLA_DV, (h + 1) * GLA_DV)
        o = oacc_ref[:, vs]
        o = o * lax.rsqrt(jnp.mean(o * o, axis=-1, keepdims=True) + NORM_EPS) * ng_ref[...]
        o_ref[:, vs] = (o * _silu(g_ref[:, vs])).astype(o_ref.dtype)


def _gla(q, k, v, g, a, w_gate2, b_gate, norm_g, tm):
    b, s, _ = q.shape
    hk = GLA_HEADS * GLA_DK
    hv = GLA_HEADS * GLA_DV
    tile = lambda i, j: (i, j, 0)
    const = lambda i, j: (0, 0)
    return pl.pallas_call(
        _gla_kernel,
        out_shape=jax.ShapeDtypeStruct((b, s, hv), BF16),
        grid=(b, s // tm),
        in_specs=[
            pl.BlockSpec((None, tm, hk), tile),
            pl.BlockSpec((None, tm, hk), tile),
            pl.BlockSpec((None, tm, hv), tile),
            pl.BlockSpec((None, tm, hv), tile),
            pl.BlockSpec((None, tm, a.shape[-1]), tile),
            pl.BlockSpec(w_gate2.shape, const),
            pl.BlockSpec((1, hk), const),
            pl.BlockSpec((1, GLA_DV), const),
        ],
        out_specs=pl.BlockSpec((None, tm, hv), tile),
        scratch_shapes=[
            pltpu.VMEM((GLA_HEADS * GLA_DK, GLA_DV), F32),
            pltpu.VMEM((tm, hk), F32),
            pltpu.VMEM((tm, hv), F32),
        ],
        compiler_params=_params("arbitrary", "arbitrary"),
        name="gla",
    )(q, k, v, g, a, w_gate2, b_gate, norm_g)


CONV_HALO = 32


def _conv_kernel(u_ref, w_ref, cb_ref, lg_ref, lb_ref, o_ref, buf_ref):
    tm = u_ref.shape[0]
    ch = o_ref.shape[-1]

    @pl.when(pl.program_id(1) == 0)
    def _():
        buf_ref[0:CONV_HALO, :] = jnp.zeros((CONV_HALO, ch), F32)

    buf_ref[CONV_HALO:, :] = u_ref[:, :ch] * jax.nn.sigmoid(u_ref[:, ch:])
    base = CONV_HALO - (CONV_WIDTH - 1)
    acc = None
    for b in range(SUBLANES):
        part = None
        span = tm + (SUBLANES if b else 0)
        for a in range((base + CONV_WIDTH - 1) // SUBLANES + 1):
            j = SUBLANES * a + b - base
            if 0 <= j < CONV_WIDTH:
                term = buf_ref[SUBLANES * a:SUBLANES * a + span, :] * w_ref[j:j + 1, :]
                part = term if part is None else part + term
        if part is not None:
            part = part[b:b + tm, :]
            acc = part if acc is None else acc + part
    buf_ref[0:CONV_HALO, :] = buf_ref[tm:tm + CONV_HALO, :]
    y = acc + cb_ref[...]
    mu = jnp.mean(y, axis=-1, keepdims=True)
    var = jnp.mean(jnp.square(y - mu), axis=-1, keepdims=True)
    y = (y - mu) * lax.rsqrt(var + NORM_EPS) * lg_ref[...] + lb_ref[...]
    o_ref[...] = _silu(y).astype(o_ref.dtype)


def _conv_module(u, conv_w, conv_b, ln_g, ln_b, tm):
    b, s, two_ch = u.shape
    ch = two_ch // 2
    tile = lambda i, j: (i, j, 0)
    const = lambda i, j: (0, 0)
    return pl.pallas_call(
        _conv_kernel,
        out_shape=jax.ShapeDtypeStruct((b, s, ch), BF16),
        grid=(b, s // tm),
        in_specs=[
            pl.BlockSpec((None, tm, two_ch), tile),
            pl.BlockSpec(conv_w.shape, const),
            pl.BlockSpec((1, ch), const),
            pl.BlockSpec((1, ch), const),
            pl.BlockSpec((1, ch), const),
        ],
        out_specs=pl.BlockSpec((None, tm, ch), tile),
        scratch_shapes=[pltpu.VMEM((tm + CONV_HALO, ch), F32)],
        compiler_params=_params("arbitrary", "arbitrary"),
        name="conv_module",
    )(u, conv_w, conv_b, ln_g, ln_b)


def _out_proj_kernel(*refs):
    *a_refs, w_ref, x_ref, gate_ref, o_ref = refs
    acc = None
    off = 0
    for a_ref in a_refs:
        kk = a_ref.shape[-1]
        part = jnp.dot(a_ref[...].astype(BF16), w_ref[off:off + kk, :], preferred_element_type=F32)
        acc = part if acc is None else acc + part
        off += kk
    o_ref[...] = x_ref[...] + gate_ref[...] * acc


def _out_proj(acts, w, x, gate, tm):
    b, s, d = x.shape
    tile = lambda i, j: (i, j, 0)
    return pl.pallas_call(
        _out_proj_kernel,
        out_shape=jax.ShapeDtypeStruct((b, s, d), F32),
        grid=(b, s // tm),
        in_specs=[pl.BlockSpec((None, tm, a.shape[-1]), tile) for a in acts] + [
            pl.BlockSpec(w.shape, lambda i, j: (0, 0)),
            pl.BlockSpec((None, tm, d), tile),
            pl.BlockSpec((None, 1, d), lambda i, j: (i, 0, 0)),
        ],
        out_specs=pl.BlockSpec((None, tm, d), tile),
        compiler_params=_params("arbitrary", "arbitrary"),
        name="out_proj",
    )(*acts, w, x, gate)


def _rope_table_kernel(pos_ref, freq_ref, sign_ref, cos_ref, sin_ref):
    ang = pos_ref[...] * freq_ref[...]
    cos_ref[...] = jnp.cos(ang)
    sin_ref[...] = jnp.sin(ang) * sign_ref[...]


def _rope_tables(positions, head_dim, tm):
    b, s = positions.shape
    rope_dims = head_dim // 4
    half = rope_dims // 2
    inv_freq = ROPE_THETA ** (-jnp.arange(0, rope_dims, 2, dtype=F32) / rope_dims)
    jj = jnp.arange(LANES) % head_dim
    freq = jnp.where(jj < rope_dims, inv_freq[jj % half], 0.0).astype(F32)[None, :]
    sign = jnp.where(jj < half, -1.0, jnp.where(jj < rope_dims, 1.0, 0.0)).astype(F32)[None, :]
    pos = positions.astype(F32)[..., None]
    tile = lambda i, j: (i, j, 0)
    const = lambda i, j: (0, 0)
    return pl.pallas_call(
        _rope_table_kernel,
        out_shape=[jax.ShapeDtypeStruct((b, s, LANES), F32)] * 2,
        grid=(b, s // tm),
        in_specs=[pl.BlockSpec((None, tm, 1), tile), pl.BlockSpec((1, LANES), const),
                  pl.BlockSpec((1, LANES), const)],
        out_specs=[pl.BlockSpec((None, tm, LANES), tile)] * 2,
        compiler_params=_params("arbitrary", "arbitrary"),
        name="rope_tables",
    )(pos, freq, sign)


def _qkv_kernel(x_ref, sh_ref, sc_ref, g_ref, w_ref, cos_ref, sin_ref, q_ref, k_ref, v_ref, *, head_dim):
    h = _modulate(x_ref[...], g_ref[...], sh_ref[...], sc_ref[...]).astype(BF16)
    d = q_ref.shape[-1]
    half = head_dim // 8
    cosf = jnp.tile(cos_ref[...], (1, d // LANES))
    sinf = jnp.tile(sin_ref[...], (1, d // LANES))
    lane = lax.broadcasted_iota(jnp.int32, (1, d), 1)
    first = (lane % head_dim) < half
    for idx, (o_ref, mult) in enumerate(((q_ref, head_dim ** -0.5), (k_ref, 1.0))):
        t = jnp.dot(h, w_ref[:, idx * d:(idx + 1) * d], preferred_element_type=F32)
        partner = jnp.where(first, pltpu.roll(t, d - half, 1), pltpu.roll(t, half, 1))
        o_ref[...] = ((t * cosf + partner * sinf) * mult).astype(o_ref.dtype)
    v_ref[...] = jnp.dot(h, w_ref[:, 2 * d:], preferred_element_type=F32).astype(v_ref.dtype)


def _qkv_rope(x, shift, scale, g, w, cos_t, sin_t, head_dim, tm):
    b, s, d = x.shape
    row = lambda i, j: (i, 0, 0)
    tile = lambda i, j: (i, j, 0)
    return pl.pallas_call(
        functools.partial(_qkv_kernel, head_dim=head_dim),
        out_shape=[jax.ShapeDtypeStruct((b, s, d), BF16)] * 3,
        grid=(b, s // tm),
        in_specs=[
            pl.BlockSpec((None, tm, d), tile),
            pl.BlockSpec((None, 1, d), row),
            pl.BlockSpec((None, 1, d), row),
            pl.BlockSpec((1, d), lambda i, j: (0, 0)),
            pl.BlockSpec(w.shape, lambda i, j: (0, 0)),
            pl.BlockSpec((None, tm, LANES), tile),
            pl.BlockSpec((None, tm, LANES), tile),
        ],
        out_specs=[pl.BlockSpec((None, tm, d), tile)] * 3,
        compiler_params=_params("arbitrary", "arbitrary"),
        name="qkv_rope",
    )(x, shift, scale, g, w, cos_t, sin_t)


ONES_ROWS = 16
ATTN_UNROLL = 16


def _attn_bias(branches, heads):
    blk = DIL_BLOCK
    kj = np.arange(2 * blk)[:, None]
    qi = np.arange(blk)[None, :]
    dist = qi + blk - kj
    out = []
    for window, dil in branches:
        band = (dist >= 0) & (dist <= window // dil)
        both = np.stack([band, band & (kj >= blk)])
        out.append(np.tile(np.where(both, 0.0, NEG_BIG), (1, 1, heads)))
    return jnp.asarray(np.stack(out), F32)


def _attn_kernel(bias_ref, q_ref, kc_ref, kp_ref, vc_ref, vp_ref, o_ref, qf, kf, vf, ob, lb, st_s, vt_s,
                 *, head_dim, branches):
    sb = q_ref.shape[0]
    blk = DIL_BLOCK
    heads = q_ref.shape[1] // head_dim
    first_super = pl.program_id(2) == 0

    qf[...] = q_ref[...].astype(F32)
    kf[0:sb, :] = kp_ref[...].astype(F32)
    kf[sb:, :] = kc_ref[...].astype(F32)
    vf[0:sb, :] = vp_ref[...].astype(F32)
    vf[sb:, :] = vc_ref[...].astype(F32)

    vt_s[:, LANES:, :] = jnp.ones((ATTN_UNROLL, ONES_ROWS, 2 * blk), BF16)

    lane = lax.broadcasted_iota(jnp.int32, (blk, LANES), 1)
    head_masks = [(lane >= h * head_dim) & (lane < (h + 1) * head_dim) for h in range(heads)]

    for bi, (window, dil) in enumerate(branches):
        assert dil & (dil - 1) == 0 and window // dil <= blk
        unit = dil * blk
        shift = dil.bit_length() - 1

        def scores(j, idx, dil=dil, unit=unit, bi=bi, shift=shift):
            u = lax.shift_right_logical(idx, shift)
            r = idx & (dil - 1)
            q0 = u * unit + r
            k0 = sb + q0 - unit
            no_prev = jnp.where(first_super & (u == 0), 1, 0)
            qb = qf[pl.ds(q0, blk, stride=dil), :]
            q2 = jnp.concatenate([jnp.where(mk, qb, 0.0) for mk in head_masks], axis=0).astype(BF16)
            kb = kf[pl.ds(k0, 2 * blk, stride=dil), :].astype(BF16)
            vt_s[j, 0:LANES, :] = vf[pl.ds(k0, 2 * blk, stride=dil), :].T.astype(BF16)
            st = lax.dot_general(kb, q2, (((1,), (1,)), ((), ())), preferred_element_type=F32)
            st_s[j] = st + bias_ref[bi, no_prev]
            return q0

        def softmax_pv(j):
            m = jnp.max(st_s[j], axis=0, keepdims=True)
            p = jnp.exp(st_s[j] - m).astype(BF16)
            of = jnp.dot(vt_s[j], p, preferred_element_type=F32)
            l = of[LANES:LANES + 1, :]
            lse = m + jnp.log(l)
            o_rows, lse_rows = [], []
            for h in range(heads):
                cols = slice(h * blk, (h + 1) * blk)
                o_rows.append(of[h * head_dim:(h + 1) * head_dim, cols] / l[:, cols])
                lse_rows.append(jnp.broadcast_to(lse[:, cols], (head_dim, blk)))
            return jnp.concatenate(o_rows, axis=0).T, jnp.concatenate(lse_rows, axis=0).T

        def block_body(it, carry, scores=scores, softmax_pv=softmax_pv, dil=dil, bi=bi):
            starts = [scores(j, it * ATTN_UNROLL + j) for j in range(ATTN_UNROLL)]
            done = [softmax_pv(j) for j in range(ATTN_UNROLL)]
            for q0, (o_tok, lse_tok) in zip(starts, done):
                ob[bi, pl.ds(q0, blk, stride=dil), :] = o_tok
                lb[bi, pl.ds(q0, blk, stride=dil), :] = lse_tok
            return carry

        lax.fori_loop(0, sb // blk // ATTN_UNROLL, block_body, 0)

    nb = len(branches)
    m = lb[0]
    for bi in range(1, nb):
        m = jnp.maximum(m, lb[bi])
    num = jnp.zeros_like(m)
    den = jnp.zeros_like(m)
    for bi in range(nb):
        e = jnp.exp(lb[bi] - m)
        num = num + e * ob[bi]
        den = den + e
    o_ref[...] = (num / den).astype(o_ref.dtype)


def _dilated_attention(q, k, v, head_dim, branches, sb):
    b, s, d = q.shape
    groups = d // LANES
    cur = lambda i, g, n: (i, n, g)
    prev = lambda i, g, n: (i, jnp.maximum(n - 1, 0), g)
    nb = len(branches)
    bias = _attn_bias(branches, LANES // head_dim)
    return pl.pallas_call(
        functools.partial(_attn_kernel, head_dim=head_dim, branches=branches),
        out_shape=jax.ShapeDtypeStruct((b, s, d), BF16),
        grid=(b, groups, s // sb),
        in_specs=[
            pl.BlockSpec(bias.shape, lambda i, g, n: (0, 0, 0, 0)),
            pl.BlockSpec((None, sb, LANES), cur),
            pl.BlockSpec((None, sb, LANES), cur),
            pl.BlockSpec((None, sb, LANES), prev),
            pl.BlockSpec((None, sb, LANES), cur),
            pl.BlockSpec((None, sb, LANES), prev),
        ],
        out_specs=pl.BlockSpec((None, sb, LANES), cur),
        scratch_shapes=[
            pltpu.VMEM((sb, LANES), F32),
            pltpu.VMEM((2 * sb, LANES), F32),
            pltpu.VMEM((2 * sb, LANES), F32),
            pltpu.VMEM((nb, sb, LANES), F32),
            pltpu.VMEM((nb, sb, LANES), F32),
            pltpu.VMEM((ATTN_UNROLL, 2 * DIL_BLOCK, LANES // head_dim * DIL_BLOCK), F32),
            pltpu.VMEM((ATTN_UNROLL, LANES + ONES_ROWS, 2 * DIL_BLOCK), BF16),
        ],
        compiler_params=_params("arbitrary", "arbitrary", "arbitrary"),
        name="dilated_attention",
    )(bias, q, k, k, v, v)


def _router_kernel(x_ref, sh_ref, sc_ref, g_ref, w_ref, b_ref, h_ref, wts_ref, route_ref, counts_ref,
                   carry_ref, *, n_experts, n_groups):
    h = _modulate(x_ref[...], g_ref[...], sh_ref[...], sc_ref[...]).astype(BF16)
    h_ref[...] = h
    logits = jnp.dot(h, w_ref[...], preferred_element_type=F32) + b_ref[...]
    tm = logits.shape[0]
    epg = n_experts // n_groups
    lane = lax.broadcasted_iota(jnp.int32, logits.shape, 1)
    neg = -jnp.inf
    big = jnp.int32(LANES)

    def first_max(mask):
        val = jnp.max(jnp.where(mask, logits, neg), axis=-1, keepdims=True)
        idx = jnp.min(jnp.where(mask & (logits == val), lane, big), axis=-1, keepdims=True)
        return val, idx

    gmask = (lane >= n_experts) & (lane < n_experts + n_groups)
    gmax, gidx = first_max(gmask)
    gsum = jnp.sum(jnp.where(gmask, jnp.exp(logits - gmax), 0.0), axis=-1, keepdims=True)
    g_w = 1.0 / gsum
    grp = gidx - n_experts
    assert epg & (epg - 1) == 0
    emask = (lane < n_experts) & (lax.shift_right_logical(lane, epg.bit_length() - 1) == grp)
    v1, i1 = first_max(emask)
    v2, i2 = first_max(emask & (lane != i1))
    e2 = jnp.exp(v2 - v1)
    den = 1.0 + e2
    col = lax.broadcasted_iota(jnp.int32, (tm, TOP_K), 1)
    wts_ref[...] = jnp.where(col == 0, 1.0 / den, e2 / den) * g_w

    @pl.when((pl.program_id(0) == 0) & (pl.program_id(1) == 0))
    def _():
        carry_ref[...] = jnp.zeros_like(carry_ref)

    hit1 = lane == i1
    hit2 = lane == i2
    onehot = jnp.where(hit1 | hit2, 1.0, 0.0)
    ri = lax.broadcasted_iota(jnp.int32, (tm, tm), 0)
    ci = lax.broadcasted_iota(jnp.int32, (tm, tm), 1)
    before = jnp.where(ci < ri, 1.0, 0.0).astype(BF16)
    prefix = jnp.dot(before, onehot.astype(BF16), preferred_element_type=F32) + carry_ref[0:1, :]
    r1 = jnp.sum(jnp.where(hit1, prefix, 0.0), axis=-1, keepdims=True)
    r2 = jnp.sum(jnp.where(hit2, prefix, 0.0), axis=-1, keepdims=True)
    packed = jnp.where(lane == 0, i1.astype(F32), jnp.where(lane == 1, i2.astype(F32),
                       jnp.where(lane == 2, r1, jnp.where(lane == 3, r2, 0.0))))
    route_ref[...] = packed.T[0:SUBLANES, :]
    carry_ref[...] = carry_ref[...] + jnp.sum(onehot, axis=0, keepdims=True)
    counts_ref[...] = carry_ref[...]


def _router(x, shift, scale, g, w_rt, b_rt, n_experts, n_groups, tm):
    b, s, d = x.shape
    row = lambda i, j: (i, 0, 0)
    tile = lambda i, j: (i, j, 0)
    const = lambda i, j: (0, 0)
    return pl.pallas_call(
        functools.partial(_router_kernel, n_experts=n_experts, n_groups=n_groups),
        out_shape=[jax.ShapeDtypeStruct((b, s, d), BF16),
                   jax.ShapeDtypeStruct((b, s, TOP_K), F32),
                   jax.ShapeDtypeStruct((SUBLANES, b * s), F32),
                   jax.ShapeDtypeStruct((SUBLANES, LANES), F32)],
        grid=(b, s // tm),
        in_specs=[
            pl.BlockSpec((None, tm, d), tile),
            pl.BlockSpec((None, 1, d), row),
            pl.BlockSpec((None, 1, d), row),
            pl.BlockSpec((1, d), const),
            pl.BlockSpec((d, LANES), const),
            pl.BlockSpec((1, LANES), const),
        ],
        out_specs=[pl.BlockSpec((None, tm, d), tile),
                   pl.BlockSpec((None, tm, TOP_K), tile),
                   pl.BlockSpec((SUBLANES, tm), lambda i, j: (0, i * (s // tm) + j)),
                   pl.BlockSpec((SUBLANES, LANES), const)],
        scratch_shapes=[pltpu.VMEM((SUBLANES, LANES), F32)],
        compiler_params=_params("arbitrary", "arbitrary"),
        name="moe_router",
    )(x, shift, scale, g, w_rt, b_rt)


def _expert_kernel(ib_ref, ie_ref, lo_ref, hi_ref, xs_ref, w1_ref, w3_ref, w2_ref, ys_ref, w1b, w3b, w2b):
    j = pl.program_id(0)
    prev = jnp.maximum(j - 1, 0)
    e_changed = (j == 0) | (ie_ref[j] != ie_ref[prev])
    first_of_block = (j == 0) | (ib_ref[j] != ib_ref[prev])
    lo = lo_ref[j]
    hi = hi_ref[j]

    @pl.when(e_changed)
    def _():
        w1b[...] = w1_ref[...].astype(BF16)
        w3b[...] = w3_ref[...].astype(BF16)
        w2b[...] = w2_ref[...].astype(BF16)

    rows = ys_ref.shape[0]
    whole = (lo == 0) & (hi == rows)

    @pl.when(first_of_block & jnp.logical_not(whole))
    def _():
        ys_ref[...] = jnp.zeros_like(ys_ref)

    @pl.when(hi > lo)
    def _():
        x = xs_ref[...]
        a = jnp.dot(x, w1b[...], preferred_element_type=F32)
        g = jnp.dot(x, w3b[...], preferred_element_type=F32)
        y = jnp.dot((_silu(a) * g).astype(BF16), w2b[...], preferred_element_type=F32)

        @pl.when(whole)
        def _():
            ys_ref[...] = y

        @pl.when(jnp.logical_not(whole))
        def _():
            row = lax.broadcasted_iota(jnp.int32, (rows, 1), 0)
            ys_ref[...] = jnp.where((row >= lo) & (row < hi), y, ys_ref[...])


def _experts(layer, items, xs, w1, w3, w2, rows):
    a, d = xs.shape
    hid = w1.shape[-1]
    blk = lambda j, ib, ie, lo, hi: (ib[j], 0)
    wsel = lambda j, ib, ie, lo, hi: (layer, ie[j], 0, 0)
    grid_spec = pltpu.PrefetchScalarGridSpec(
        num_scalar_prefetch=4,
        grid=(items[0].shape[0],),
        in_specs=[
            pl.BlockSpec((rows, d), blk),
            pl.BlockSpec((None, None, d, hid), wsel),
            pl.BlockSpec((None, None, d, hid), wsel),
            pl.BlockSpec((None, None, hid, d), wsel),
        ],
        out_specs=pl.BlockSpec((rows, d), blk),
        scratch_shapes=[pltpu.VMEM((d, hid), BF16), pltpu.VMEM((d, hid), BF16), pltpu.VMEM((hid, d), BF16)],
    )
    return pl.pallas_call(
        _expert_kernel,
        out_shape=jax.ShapeDtypeStruct((a, d), F32),
        grid_spec=grid_spec,
        compiler_params=_params("arbitrary"),
        name="moe_experts",
    )(*items, xs, w1, w3, w2)


def _dispatch(ids, rank, counts, rows):
    n_experts = counts.shape[0]
    a = ids.size
    i32 = jnp.int32
    ends = jnp.cumsum(counts)
    starts = ends - counts
    dest = rank
    for e in range(n_experts):
        dest = dest + jnp.where(ids == e, starts[e], 0)
    t = ids.shape[1]
    tok = jnp.tile(jnp.arange(t, dtype=i32), TOP_K)
    row_tok = lax.sort_key_val(dest.reshape(-1), tok)[1]
    n_blk = a // rows
    bstart = jnp.arange(n_blk, dtype=i32) * rows
    count_le = lambda bounds, x: jnp.sum((bounds[None, :] <= x[:, None]).astype(i32), axis=1)
    e_lo = jnp.minimum(count_le(ends, bstart), n_experts - 1)
    e_hi = jnp.minimum(count_le(ends, bstart + rows - 1), n_experts - 1)
    n_items = e_hi - e_lo + 1
    item_end = jnp.cumsum(n_items)
    item_first = item_end - n_items
    jj = jnp.arange(n_blk + n_experts - 1, dtype=i32)
    valid = jj < item_end[-1]
    ib = jnp.minimum(count_le(item_end, jj), n_blk - 1)
    ie = jnp.where(valid, jnp.clip(e_lo[ib] + jj - item_first[ib], 0, n_experts - 1), e_hi[n_blk - 1]).astype(i32)
    lo = jnp.where(valid, jnp.clip(starts[ie] - ib * rows, 0, rows), 0).astype(i32)
    hi = jnp.where(valid, jnp.clip(ends[ie] - ib * rows, 0, rows), 0).astype(i32)
    return row_tok, dest, (ib, ie, lo, hi)


def _combine_kernel(x_ref, y0_ref, y1_ref, w_ref, gate_ref, o_ref):
    y = y0_ref[...] * w_ref[:, 0:1] + y1_ref[...] * w_ref[:, 1:2]
    o_ref[...] = x_ref[...] + gate_ref[...] * y


def _combine_norm_kernel(x_ref, y0_ref, y1_ref, w_ref, gate_ref, ng_ref, o_ref):
    y = y0_ref[...] * w_ref[:, 0:1] + y1_ref[...] * w_ref[:, 1:2]
    x = x_ref[...] + gate_ref[...] * y
    o_ref[...] = x * lax.rsqrt(jnp.mean(x * x, axis=-1, keepdims=True) + NORM_EPS) * ng_ref[...]


def _combine(x, y0, y1, wts, gate, final_g, tm):
    b, s, d = x.shape
    tile = lambda i, j: (i, j, 0)
    in_specs = [
        pl.BlockSpec((None, tm, d), tile),
        pl.BlockSpec((None, tm, d), tile),
        pl.BlockSpec((None, tm, d), tile),
        pl.BlockSpec((None, tm, TOP_K), tile),
        pl.BlockSpec((None, 1, d), lambda i, j: (i, 0, 0)),
    ]
    args = [x, y0, y1, wts, gate]
    kern = _combine_kernel
    if final_g is not None:
        in_specs.append(pl.BlockSpec((1, d), lambda i, j: (0, 0)))
        args.append(final_g)
        kern = _combine_norm_kernel
    return pl.pallas_call(
        kern,
        out_shape=jax.ShapeDtypeStruct((b, s, d), F32),
        grid=(b, s // tm),
        in_specs=in_specs,
        out_specs=pl.BlockSpec((None, tm, d), tile),
        compiler_params=_params("arbitrary", "arbitrary"),
        name="moe_combine",
    )(*args)


def _hier_moe(layer, x, shift, scale, norm_g, gate, w_rt_full, b_rt_full, w1, w3, w2, final_g, tm, rows):
    b, s, d = x.shape
    t = b * s
    n_experts = w1.shape[1]
    h, wts, route, counts = _router(x, shift, scale, norm_g, w_rt_full, b_rt_full, n_experts, N_GROUPS, tm)
    counts = counts[0, :n_experts].astype(jnp.int32)
    route = route.astype(jnp.int32)
    row_tok, dest, items = _dispatch(route[0:TOP_K], route[TOP_K:2 * TOP_K], counts, rows)
    xs = h.reshape(t, d)[row_tok]
    ys = _experts(layer, items, xs, w1, w3, w2, rows)
    y0 = ys[dest[0]].reshape(b, s, d)
    y1 = ys[dest[1]].reshape(b, s, d)
    return _combine(x, y0, y1, wts, gate, final_g, tm)


def _pick_tile(s, pref):
    tm = min(pref, s)
    assert s % tm == 0
    return tm


def kernel(x, c, positions, ada_w, ada_b, norm1_g, norm2_g, even_w_in, even_w_gate2, even_b_gate, even_gla_norm_g, even_conv_w, even_conv_b, even_conv_ln_g, even_conv_ln_b, even_w_out, odd_w_qkv, odd_w_out, moe_w_grp, moe_b_grp, moe_w_rt, moe_b_rt, moe_w1, moe_w3, moe_w2, final_norm_g):
    b, s, d = x.shape
    depth = ada_w.shape[0]
    n_experts = moe_w_rt.shape[-1]
    tm = _pick_tile(s, TOKEN_TILE)
    sb = _pick_tile(s, ATTN_SUPER)
    head_dim = d // ATTN_HEADS
    hk = GLA_HEADS * GLA_DK
    hv = GLA_HEADS * GLA_DV
    conv_ch = d // 2

    mods = _ada_mods(c, ada_w, ada_b)
    mod = lambda l, j: mods[l, j][:, None, :]
    cos_t = sin_t = None

    for layer in range(depth):
        i = layer // 2
        g1 = norm1_g[layer][None, :]
        if layer % 2 == 0:
            w_in = even_w_in[i]
            main = hk + hk + hv + hv
            w_cat = jnp.concatenate([
                w_in[:, :main], w_in[:, main + GLA_GATE_RANK:], w_in[:, main:main + GLA_GATE_RANK],
                jnp.zeros((d, LANES - GLA_GATE_RANK), w_in.dtype)], axis=1).astype(BF16)
            widths = (hk, hk, hv, hv, 2 * conv_ch, LANES)
            dtypes = (F32, F32, BF16, F32, F32, F32)
            q, k, v, g, u, a_lr = _norm_matmul(x, mod(layer, 0), mod(layer, 1), g1, w_cat, widths, dtypes, tm)
            wg = jnp.concatenate([even_w_gate2[i], jnp.zeros((LANES - GLA_GATE_RANK, hk), F32)], axis=0).astype(BF16)
            o_gla = _gla(q, k, v, g, a_lr, wg, even_b_gate[i][None, :], even_gla_norm_g[i][None, :], tm)
            y_conv = _conv_module(u, even_conv_w[i], even_conv_b[i][None, :], even_conv_ln_g[i][None, :],
                                  even_conv_ln_b[i][None, :], tm)
            x = _out_proj([o_gla, y_conv], even_w_out[i].astype(BF16), x, mod(layer, 2), tm)
        else:
            if cos_t is None:
                cos_t, sin_t = _rope_tables(positions, head_dim, tm)
            q, k, v = _qkv_rope(x, mod(layer, 0), mod(layer, 1), g1, odd_w_qkv[i].astype(BF16),
                                cos_t, sin_t, head_dim, tm)
            o = _dilated_attention(q, k, v, head_dim, DILATED_BRANCHES, sb)
            x = _out_proj([o], odd_w_out[i].astype(BF16), x, mod(layer, 2), tm)

        w_rt_full = jnp.concatenate([moe_w_rt[layer], moe_w_grp[layer],
                                     jnp.zeros((d, LANES - n_experts - N_GROUPS), F32)], axis=1).astype(BF16)
        b_rt_full = jnp.concatenate([moe_b_rt[layer], moe_b_grp[layer],
                                     jnp.zeros((LANES - n_experts - N_GROUPS,), F32)])[None, :]
        final_g = final_norm_g[None, :] if layer == depth - 1 else None
        x = _hier_moe(layer, x, mod(layer, 3), mod(layer, 4), norm2_g[layer][None, :], mod(layer, 5),
                      w_rt_full, b_rt_full, moe_w1, moe_w3, moe_w2, final_g, tm, MOE_ROWS)
    return x
```

```python
import functools

import jax
import jax.numpy as jnp
import numpy as np
from jax import lax
from jax.experimental import pallas as pl
from jax.experimental.pallas import tpu as pltpu

F32 = jnp.float32
BF16 = jnp.bfloat16
HIGHEST = lax.Precision.HIGHEST

NORM_EPS = 1e-6
GLA_HEADS = 4
GLA_DK = 64
GLA_DV = 128
GLA_GATE_RANK = 16
GLA_TAU = 16.0
GLA_CHUNK = 64
CONV_WIDTH = 31
ATTN_HEADS = 16
DILATED_BRANCHES = ((128, 1), (512, 4), (2048, 16))
DIL_BLOCK = 128
ROPE_THETA = 500000.0
N_GROUPS = 4
EXPERTS_PER_GROUP = 8
TOP_K = 2
ADA_CHUNKS = 6

LANES = 128
SUBLANES = 8
VMEM_LIMIT = 56 * 1024 * 1024
TOKEN_TILE = 512
ATTN_SUPER = 2048
MOE_ROWS = 512
NEG_BIG = -1e30


def _params(*sem):
    return pltpu.CompilerParams(dimension_semantics=sem, vmem_limit_bytes=VMEM_LIMIT)


def _silu(x):
    return x * jax.nn.sigmoid(x)


def _modulate(x, g, shift, scale):
    y = x * lax.rsqrt(jnp.mean(x * x, axis=-1, keepdims=True) + NORM_EPS)
    return (y * g) * (1.0 + scale) + shift


def _ada_kernel(c_ref, w_ref, b_ref, o_ref):
    cond = _silu(c_ref[...])
    o_ref[...] = jnp.dot(cond, w_ref[...], preferred_element_type=F32, precision=HIGHEST) + b_ref[...]


def _ada_mods(c, ada_w, ada_b):
    depth, d, _ = ada_w.shape
    b = c.shape[0]
    return pl.pallas_call(
        _ada_kernel,
        out_shape=jax.ShapeDtypeStruct((depth, ADA_CHUNKS, b, d), F32),
        grid=(depth, ADA_CHUNKS),
        in_specs=[
            pl.BlockSpec((b, d), lambda l, j: (0, 0)),
            pl.BlockSpec((None, d, d), lambda l, j: (l, 0, j)),
            pl.BlockSpec((None, None, 1, d), lambda l, j: (l, j, 0, 0)),
        ],
        out_specs=pl.BlockSpec((None, None, b, d), lambda l, j: (l, j, 0, 0)),
        compiler_params=_params("arbitrary", "arbitrary"),
        name="ada_mods",
    )(c, ada_w, ada_b.reshape(depth, ADA_CHUNKS, 1, d))


def _norm_matmul_kernel(x_ref, sh_ref, sc_ref, g_ref, w_ref, *o_refs):
    h = _modulate(x_ref[...], g_ref[...], sh_ref[...], sc_ref[...]).astype(BF16)
    off = 0
    for o_ref in o_refs:
        n = o_ref.shape[-1]
        o_ref[...] = jnp.dot(h, w_ref[:, off:off + n], preferred_element_type=F32).astype(o_ref.dtype)
        off += n


def _norm_matmul(x, shift, scale, g, w, widths, dtypes, tm):
    b, s, d = x.shape
    row = lambda i, j: (i, 0, 0)
    tile = lambda i, j: (i, j, 0)
    return pl.pallas_call(
        _norm_matmul_kernel,
        out_shape=[jax.ShapeDtypeStruct((b, s, n), dt) for n, dt in zip(widths, dtypes)],
        grid=(b, s // tm),
        in_specs=[
            pl.BlockSpec((None, tm, d), tile),
            pl.BlockSpec((None, 1, d), row),
            pl.BlockSpec((None, 1, d), row),
            pl.BlockSpec((1, d), lambda i, j: (0, 0)),
            pl.BlockSpec(w.shape, lambda i, j: (0, 0)),
        ],
        out_specs=[pl.BlockSpec((None, tm, n), tile) for n in widths],
        compiler_params=_params("arbitrary", "arbitrary"),
        name="norm_matmul",
    )(x, shift, scale, g, w)


def _log_sigmoid(z):
    return jnp.minimum(z, 0.0) - jnp.log1p(jnp.exp(-jnp.abs(z)))


def _gla_kernel(q_ref, k_ref, v_ref, g_ref, a_ref, wg_ref, bg_ref, ng_ref, o_ref, state_ref, la_ref, oacc_ref):
    tm = q_ref.shape[0]
    c = GLA_CHUNK

    @pl.when(pl.program_id(1) == 0)
    def _():
        state_ref[...] = jnp.zeros_like(state_ref)

    z = jnp.dot(a_ref[...].astype(BF16), wg_ref[...], preferred_element_type=F32) + bg_ref[...]
    la_ref[...] = _log_sigmoid(z) * (1.0 / GLA_TAU)

    ri = lax.broadcasted_iota(jnp.int32, (c, c), 0)
    ci = lax.broadcasted_iota(jnp.int32, (c, c), 1)
    causal = ri >= ci
    tril = jnp.where(causal, 1.0, 0.0).astype(BF16)
    hk = GLA_HEADS * GLA_DK

    nh = GLA_HEADS
    lane_head = lax.shift_right_logical(lax.broadcasted_iota(jnp.int32, (c, hk), 1), GLA_DK.bit_length() - 1)
    r4 = lax.broadcasted_iota(jnp.int32, (nh * c, nh * c), 0)
    c4 = lax.broadcasted_iota(jnp.int32, (nh * c, nh * c), 1)
    shift_c = c.bit_length() - 1
    causal4 = (lax.shift_right_logical(r4, shift_c) == lax.shift_right_logical(c4, shift_c)) & (r4 >= c4)

    def stack_heads(t):
        return jnp.concatenate([jnp.where(lane_head == h, t, 0.0) for h in range(nh)], axis=0).astype(BF16)

    state = state_ref[...]
    for ic in range(tm // c):
        rows = slice(ic * c, (ic + 1) * c)
        la = la_ref[rows, :]
        p0 = la.astype(BF16)
        r1 = la - p0.astype(F32)
        p1 = r1.astype(BF16)
        p2 = (r1 - p1.astype(F32)).astype(BF16)
        parts = jnp.dot(tril, jnp.concatenate([p0, p1, p2], axis=1), preferred_element_type=F32)
        bcum = (parts[:, 2 * hk:] + parts[:, hk:2 * hk]) + parts[:, :hk]
        b_last = bcum[c - 1:c, :]
        q = q_ref[rows, :] * (GLA_DK ** -0.5)
        k = k_ref[rows, :]
        q4 = stack_heads(q * jnp.exp(bcum))
        k4 = stack_heads(k * jnp.exp(-bcum))
        kr4 = stack_heads(k * jnp.exp(b_last - bcum))
        dec = jnp.exp(jnp.broadcast_to(b_last, (GLA_DV, hk)).T)
        v4 = jnp.concatenate([v_ref[rows, h * GLA_DV:(h + 1) * GLA_DV] for h in range(nh)], axis=0)
        att = lax.dot_general(q4, k4, (((1,), (1,)), ((), ())), preferred_element_type=F32)
        att = jnp.where(causal4, att, 0.0).astype(BF16)
        o4 = jnp.dot(att, v4, preferred_element_type=F32)
        o4 = o4 + jnp.dot(q4, state.astype(BF16), preferred_element_type=F32)
        kv = lax.dot_general(kr4, v4, (((0,), (0,)), ((), ())), preferred_element_type=F32)
        state = dec * state + kv
        oacc_ref[rows, :] = jnp.concatenate([o4[h * c:(h + 1) * c, :] for h in range(nh)], axis=1)
    state_ref[...] = state

    for h in range(GLA_HEADS):
        vs = slice(h * GLA_DV, (h + 1) * GLA_DV)
        o = oacc_ref[:, vs]
        o = o * lax.rsqrt(jnp.mean(o * o, axis=-1, keepdims=True) + NORM_EPS) * ng_ref[...]
        o_ref[:, vs] = (o * _silu(g_ref[:, vs])).astype(o_ref.dtype)


def _gla(q, k, v, g, a, w_gate2, b_gate, norm_g, tm):
    b, s, _ = q.shape
    hk = GLA_HEADS * GLA_DK
    hv = GLA_HEADS * GLA_DV
    tile = lambda i, j: (i, j, 0)
    const = lambda i, j: (0, 0)
    return pl.pallas_call(
        _gla_kernel,
        out_shape=jax.ShapeDtypeStruct((b, s, hv), BF16),
        grid=(b, s // tm),
        in_specs=[
            pl.BlockSpec((None, tm, hk), tile),
            pl.BlockSpec((None, tm, hk), tile),
            pl.BlockSpec((None, tm, hv), tile),
            pl.BlockSpec((None, tm, hv), tile),
            pl.BlockSpec((None, tm, a.shape[-1]), tile),
            pl.BlockSpec(w_gate2.shape, const),
            pl.BlockSpec((1, hk), const),
            pl.BlockSpec((1, GLA_DV), const),
        ],
        out_specs=pl.BlockSpec((None, tm, hv), tile),
        scratch_shapes=[
            pltpu.VMEM((GLA_HEADS * GLA_DK, GLA_DV), F32),
            pltpu.VMEM((tm, hk), F32),
            pltpu.VMEM((tm, hv), F32),
        ],
        compiler_params=_params("arbitrary", "arbitrary"),
        name="gla",
    )(q, k, v, g, a, w_gate2, b_gate, norm_g)


CONV_HALO = 32


def _conv_kernel(u_ref, w_ref, cb_ref, lg_ref, lb_ref, o_ref, buf_ref):
    tm = u_ref.shape[0]
    ch = o_ref.shape[-1]

    @pl.when(pl.program_id(1) == 0)
    def _():
        buf_ref[0:CONV_HALO, :] = jnp.zeros((CONV_HALO, ch), F32)

    buf_ref[CONV_HALO:, :] = u_ref[:, :ch] * jax.nn.sigmoid(u_ref[:, ch:])
    base = CONV_HALO - (CONV_WIDTH - 1)
    acc = None
    for b in range(SUBLANES):
        part = None
        span = tm + (SUBLANES if b else 0)
        for a in range((base + CONV_WIDTH - 1) // SUBLANES + 1):
            j = SUBLANES * a + b - base
            if 0 <= j < CONV_WIDTH:
                term = buf_ref[SUBLANES * a:SUBLANES * a + span, :] * w_ref[j:j + 1, :]
                part = term if part is None else part + term
        if part is not None:
            part = part[b:b + tm, :]
            acc = part if acc is None else acc + part
    buf_ref[0:CONV_HALO, :] = buf_ref[tm:tm + CONV_HALO, :]
    y = acc + cb_ref[...]
    mu = jnp.mean(y, axis=-1, keepdims=True)
    var = jnp.mean(jnp.square(y - mu), axis=-1, keepdims=True)
    y = (y - mu) * lax.rsqrt(var + NORM_EPS) * lg_ref[...] + lb_ref[...]
    o_ref[...] = _silu(y).astype(o_ref.dtype)


def _conv_module(u, conv_w, conv_b, ln_g, ln_b, tm):
    b, s, two_ch = u.shape
    ch = two_ch // 2
    tile = lambda i, j: (i, j, 0)
    const = lambda i, j: (0, 0)
    return pl.pallas_call(
        _conv_kernel,
        out_shape=jax.ShapeDtypeStruct((b, s, ch), BF16),
        grid=(b, s // tm),
        in_specs=[
            pl.BlockSpec((None, tm, two_ch), tile),
            pl.BlockSpec(conv_w.shape, const),
            pl.BlockSpec((1, ch), const),
            pl.BlockSpec((1, ch), const),
            pl.BlockSpec((1, ch), const),
        ],
        out_specs=pl.BlockSpec((None, tm, ch), tile),
        scratch_shapes=[pltpu.VMEM((tm + CONV_HALO, ch), F32)],
        compiler_params=_params("arbitrary", "arbitrary"),
        name="conv_module",
    )(u, conv_w, conv_b, ln_g, ln_b)


def _rope_table_kernel(pos_ref, freq_ref, sign_ref, cos_ref, sin_ref):
    ang = pos_ref[...] * freq_ref[...]
    cos_ref[...] = jnp.cos(ang)
    sin_ref[...] = jnp.sin(ang) * sign_ref[...]


def _rope_tables(positions, head_dim, tm):
    b, s = positions.shape
    rope_dims = head_dim // 4
    half = rope_dims // 2
    inv_freq = ROPE_THETA ** (-jnp.arange(0, rope_dims, 2, dtype=F32) / rope_dims)
    jj = jnp.arange(LANES) % head_dim
    freq = jnp.where(jj < rope_dims, inv_freq[jj % half], 0.0).astype(F32)[None, :]
    sign = jnp.where(jj < half, -1.0, jnp.where(jj < rope_dims, 1.0, 0.0)).astype(F32)[None, :]
    pos = positions.astype(F32)[..., None]
    tile = lambda i, j: (i, j, 0)
    const = lambda i, j: (0, 0)
    return pl.pallas_call(
        _rope_table_kernel,
        out_shape=[jax.ShapeDtypeStruct((b, s, LANES), F32)] * 2,
        grid=(b, s // tm),
        in_specs=[pl.BlockSpec((None, tm, 1), tile), pl.BlockSpec((1, LANES), const),
                  pl.BlockSpec((1, LANES), const)],
        out_specs=[pl.BlockSpec((None, tm, LANES), tile)] * 2,
        compiler_params=_params("arbitrary", "arbitrary"),
        name="rope_tables",
    )(pos, freq, sign)


def _qkv_kernel(x_ref, sh_ref, sc_ref, g_ref, w_ref, cos_ref, sin_ref, q_ref, k_ref, v_ref, *, head_dim):
    h = _modulate(x_ref[...], g_ref[...], sh_ref[...], sc_ref[...]).astype(BF16)
    d = q_ref.shape[-1]
    half = head_dim // 8
    cosf = jnp.tile(cos_ref[...], (1, d // LANES))
    sinf = jnp.tile(sin_ref[...], (1, d // LANES))
    lane = lax.broadcasted_iota(jnp.int32, (1, d), 1)
    first = (lane % head_dim) < half
    for idx, (o_ref, mult) in enumerate(((q_ref, head_dim ** -0.5), (k_ref, 1.0))):
        t = jnp.dot(h, w_ref[:, idx * d:(idx + 1) * d], preferred_element_type=F32)
        partner = jnp.where(first, pltpu.roll(t, d - half, 1), pltpu.roll(t, half, 1))
        o_ref[...] = ((t * cosf + partner * sinf) * mult).astype(o_ref.dtype)
    v_ref[...] = jnp.dot(h, w_ref[:, 2 * d:], preferred_element_type=F32).astype(v_ref.dtype)


def _qkv_rope(x, shift, scale, g, w, cos_t, sin_t, head_dim, tm):
    b, s, d = x.shape
    row = lambda i, j: (i, 0, 0)
    tile = lambda i, j: (i, j, 0)
    return pl.pallas_call(
        functools.partial(_qkv_kernel, head_dim=head_dim),
        out_shape=[jax.ShapeDtypeStruct((b, s, d), BF16)] * 3,
        grid=(b, s // tm),
        in_specs=[
            pl.BlockSpec((None, tm, d), tile),
            pl.BlockSpec((None, 1, d), row),
            pl.BlockSpec((None, 1, d), row),
            pl.BlockSpec((1, d), lambda i, j: (0, 0)),
            pl.BlockSpec(w.shape, lambda i, j: (0, 0)),
            pl.BlockSpec((None, tm, LANES), tile),
            pl.BlockSpec((None, tm, LANES), tile),
        ],
        out_specs=[pl.BlockSpec((None, tm, d), tile)] * 3,
        compiler_params=_params("arbitrary", "arbitrary"),
        name="qkv_rope",
    )(x, shift, scale, g, w, cos_t, sin_t)


ONES_ROWS = 16
ATTN_BLOCKS = ATTN_SUPER // DIL_BLOCK


def _attn_bias(branches, heads):
    blk = DIL_BLOCK
    kj = np.arange(2 * blk)[:, None]
    qi = np.arange(blk)[None, :]
    dist = qi + blk - kj
    out = []
    for window, dil in branches:
        band = (dist >= 0) & (dist <= window // dil)
        both = np.stack([band, band & (kj >= blk)])
        out.append(np.tile(np.where(both, 0.0, NEG_BIG), (1, 1, heads)))
    return jnp.asarray(np.stack(out), F32)


def _attn_kernel(bias_ref, q_ref, kc_ref, kp_ref, vc_ref, vp_ref, o_ref, qf, kf, vf, ob, lb, st_s, vt_s,
                 *, head_dim, branches):
    sb = q_ref.shape[0]
    blk = DIL_BLOCK
    heads = q_ref.shape[1] // head_dim
    first_super = pl.program_id(2) == 0
    slot0 = jnp.maximum(pl.program_id(2) - pl.num_programs(2), 0)

    qf[...] = q_ref[...].astype(F32)
    kf[0:sb, :] = kp_ref[...].astype(F32)
    kf[sb:, :] = kc_ref[...].astype(F32)
    vf[0:sb, :] = vp_ref[...].astype(F32)
    vf[sb:, :] = vc_ref[...].astype(F32)

    vt_s[:, LANES:, :] = jnp.ones((ATTN_BLOCKS, ONES_ROWS, 2 * blk), BF16)

    lane = lax.broadcasted_iota(jnp.int32, (blk, LANES), 1)
    head_masks = [(lane >= h * head_dim) & (lane < (h + 1) * head_dim) for h in range(heads)]

    for bi, (window, dil) in enumerate(branches):
        assert window // dil <= blk and sb == ATTN_SUPER and sb % (dil * blk) == 0
        unit = dil * blk

        def scores(j, q0, first_unit, dil=dil, unit=unit, bi=bi):
            k0 = sb + q0 - unit
            no_prev = jnp.where(first_super, 1, 0) if first_unit else 0
            qb = qf[pl.ds(q0, blk, stride=dil), :]
            q2 = jnp.concatenate([jnp.where(mk, qb, 0.0) for mk in head_masks], axis=0).astype(BF16)
            kb = kf[pl.ds(k0, 2 * blk, stride=dil), :].astype(BF16)
            vt_s[j, 0:LANES, :] = vf[pl.ds(k0, 2 * blk, stride=dil), :].T.astype(BF16)
            st = lax.dot_general(kb, q2, (((1,), (1,)), ((), ())), preferred_element_type=F32)
            st_s[slot0 + j] = st + bias_ref[bi, no_prev]

        def softmax_pv(j):
            m = jnp.max(st_s[slot0 + j], axis=0, keepdims=True)
            p = jnp.exp(st_s[slot0 + j] - m).astype(BF16)
            of = jnp.dot(vt_s[j], p, preferred_element_type=F32)
            l = of[LANES:LANES + 1, :]
            lse = m + jnp.log(l)
            o_rows, lse_rows = [], []
            for h in range(heads):
                cols = slice(h * blk, (h + 1) * blk)
                o_rows.append(of[h * head_dim:(h + 1) * head_dim, cols] / l[:, cols])
                lse_rows.append(jnp.broadcast_to(lse[:, cols], (head_dim, blk)))
            return jnp.concatenate(o_rows, axis=0).T, jnp.concatenate(lse_rows, axis=0).T

        starts = [(idx // dil) * unit + idx % dil for idx in range(ATTN_BLOCKS)]
        for j, q0 in enumerate(starts):
            scores(j, q0, q0 < unit)
        for j, q0 in enumerate(starts):
            o_tok, lse_tok = softmax_pv(j)
            ob[bi, pl.ds(q0, blk, stride=dil), :] = o_tok
            lb[bi, pl.ds(q0, blk, stride=dil), :] = lse_tok

    nb = len(branches)
    m = lb[0]
    for bi in range(1, nb):
        m = jnp.maximum(m, lb[bi])
    num = jnp.zeros_like(m)
    den = jnp.zeros_like(m)
    for bi in range(nb):
        e = jnp.exp(lb[bi] - m)
        num = num + e * ob[bi]
        den = den + e
    o_ref[...] = (num / den).astype(o_ref.dtype)


def _dilated_attention(q, k, v, head_dim, branches, sb):
    b, s, d = q.shape
    groups = d // LANES
    cur = lambda i, g, n: (i, n, g)
    prev = lambda i, g, n: (i, jnp.maximum(n - 1, 0), g)
    nb = len(branches)
    bias = _attn_bias(branches, LANES // head_dim)
    return pl.pallas_call(
        functools.partial(_attn_kernel, head_dim=head_dim, branches=branches),
        out_shape=jax.ShapeDtypeStruct((b, s, d), BF16),
        grid=(b, groups, s // sb),
        in_specs=[
            pl.BlockSpec(bias.shape, lambda i, g, n: (0, 0, 0, 0)),
            pl.BlockSpec((None, sb, LANES), cur),
            pl.BlockSpec((None, sb, LANES), cur),
            pl.BlockSpec((None, sb, LANES), prev),
            pl.BlockSpec((None, sb, LANES), cur),
            pl.BlockSpec((None, sb, LANES), prev),
        ],
        out_specs=pl.BlockSpec((None, sb, LANES), cur),
        scratch_shapes=[
            pltpu.VMEM((sb, LANES), F32),
            pltpu.VMEM((2 * sb, LANES), F32),
            pltpu.VMEM((2 * sb, LANES), F32),
            pltpu.VMEM((nb, sb, LANES), F32),
            pltpu.VMEM((nb, sb, LANES), F32),
            pltpu.VMEM((ATTN_BLOCKS, 2 * DIL_BLOCK, LANES // head_dim * DIL_BLOCK), F32),
            pltpu.VMEM((ATTN_BLOCKS, LANES + ONES_ROWS, 2 * DIL_BLOCK), BF16),
        ],
        compiler_params=_params("arbitrary", "arbitrary", "arbitrary"),
        name="dilated_attention",
    )(bias, q, k, k, v, v)


def _route_tile(x, sh_ref, sc_ref, g_ref, w_ref, b_ref, h_ref, wts_ref, route_ref, counts_ref, carry_ref,
                n_experts, n_groups):
    h = _modulate(x, g_ref[...], sh_ref[...], sc_ref[...]).astype(BF16)
    h_ref[...] = h
    logits = jnp.dot(h, w_ref[...], preferred_element_type=F32) + b_ref[...]
    tm = logits.shape[0]
    epg = n_experts // n_groups
    lane = lax.broadcasted_iota(jnp.int32, logits.shape, 1)
    neg = -jnp.inf
    big = jnp.int32(LANES)

    def first_max(mask):
        val = jnp.max(jnp.where(mask, logits, neg), axis=-1, keepdims=True)
        idx = jnp.min(jnp.where(mask & (logits == val), lane, big), axis=-1, keepdims=True)
        return val, idx

    gmask = (lane >= n_experts) & (lane < n_experts + n_groups)
    gmax, gidx = first_max(gmask)
    gsum = jnp.sum(jnp.where(gmask, jnp.exp(logits - gmax), 0.0), axis=-1, keepdims=True)
    g_w = 1.0 / gsum
    grp = gidx - n_experts
    assert epg & (epg - 1) == 0
    emask = (lane < n_experts) & (lax.shift_right_logical(lane, epg.bit_length() - 1) == grp)
    v1, i1 = first_max(emask)
    v2, i2 = first_max(emask & (lane != i1))
    e2 = jnp.exp(v2 - v1)
    den = 1.0 + e2
    col = lax.broadcasted_iota(jnp.int32, (tm, TOP_K), 1)
    wts_ref[...] = jnp.where(col == 0, 1.0 / den, e2 / den) * g_w

    @pl.when((pl.program_id(0) == 0) & (pl.program_id(1) == 0))
    def _():
        carry_ref[...] = jnp.zeros_like(carry_ref)

    hit1 = lane == i1
    hit2 = lane == i2
    onehot = jnp.where(hit1 | hit2, 1.0, 0.0)
    ri = lax.broadcasted_iota(jnp.int32, (tm, tm), 0)
    ci = lax.broadcasted_iota(jnp.int32, (tm, tm), 1)
    before = jnp.where(ci < ri, 1.0, 0.0).astype(BF16)
    prefix = jnp.dot(before, onehot.astype(BF16), preferred_element_type=F32) + carry_ref[0:1, :]
    r1 = jnp.sum(jnp.where(hit1, prefix, 0.0), axis=-1, keepdims=True)
    r2 = jnp.sum(jnp.where(hit2, prefix, 0.0), axis=-1, keepdims=True)
    packed = jnp.where(lane == 0, i1.astype(F32), jnp.where(lane == 1, i2.astype(F32),
                       jnp.where(lane == 2, r1, jnp.where(lane == 3, r2, 0.0))))
    route_ref[...] = packed.T[0:SUBLANES, :]
    carry_ref[...] = carry_ref[...] + jnp.sum(onehot, axis=0, keepdims=True)
    counts_ref[...] = carry_ref[...]


def _out_proj_router_kernel(*refs, n_acts, n_experts, n_groups):
    a_refs = refs[:n_acts]
    w_ref, x_ref, gate_ref, sh_ref, sc_ref, g_ref, wr_ref, br_ref = refs[n_acts:n_acts + 8]
    xo_ref, h_ref, wts_ref, route_ref, counts_ref, carry_ref = refs[n_acts + 8:]
    acc = None
    off = 0
    for a_ref in a_refs:
        kk = a_ref.shape[-1]
        part = jnp.dot(a_ref[...].astype(BF16), w_ref[off:off + kk, :], preferred_element_type=F32)
        acc = part if acc is None else acc + part
        off += kk
    x = x_ref[...] + gate_ref[...] * acc
    xo_ref[...] = x
    _route_tile(x, sh_ref, sc_ref, g_ref, wr_ref, br_ref, h_ref, wts_ref, route_ref, counts_ref, carry_ref,
                n_experts, n_groups)


def _out_proj_router(acts, w, x, gate, shift, scale, g, w_rt, b_rt, n_experts, n_groups, tm):
    b, s, d = x.shape
    row = lambda i, j: (i, 0, 0)
    tile = lambda i, j: (i, j, 0)
    const = lambda i, j: (0, 0)
    return pl.pallas_call(
        functools.partial(_out_proj_router_kernel, n_acts=len(acts), n_experts=n_experts, n_groups=n_groups),
        out_shape=[jax.ShapeDtypeStruct((b, s, d), F32),
                   jax.ShapeDtypeStruct((b, s, d), BF16),
                   jax.ShapeDtypeStruct((b, s, TOP_K), F32),
                   jax.ShapeDtypeStruct((SUBLANES, b * s), F32),
                   jax.ShapeDtypeStruct((SUBLANES, LANES), F32)],
        grid=(b, s // tm),
        in_specs=[pl.BlockSpec((None, tm, a.shape[-1]), tile) for a in acts] + [
            pl.BlockSpec(w.shape, const),
            pl.BlockSpec((None, tm, d), tile),
            pl.BlockSpec((None, 1, d), row),
            pl.BlockSpec((None, 1, d), row),
            pl.BlockSpec((None, 1, d), row),
            pl.BlockSpec((1, d), const),
            pl.BlockSpec((d, LANES), const),
            pl.BlockSpec((1, LANES), const),
        ],
        out_specs=[pl.BlockSpec((None, tm, d), tile),
                   pl.BlockSpec((None, tm, d), tile),
                   pl.BlockSpec((None, tm, TOP_K), tile),
                   pl.BlockSpec((SUBLANES, tm), lambda i, j: (0, i * (s // tm) + j)),
                   pl.BlockSpec((SUBLANES, LANES), const)],
        scratch_shapes=[pltpu.VMEM((SUBLANES, LANES), F32)],
        compiler_params=_params("arbitrary", "arbitrary"),
        name="out_proj_router",
    )(*acts, w, x, gate, shift, scale, g, w_rt, b_rt)


def _expert_kernel(ib_ref, ie_ref, lo_ref, hi_ref, xs_ref, w1_ref, w3_ref, w2_ref, ys_ref, w1b, w3b, w2b):
    j = pl.program_id(0)
    prev = jnp.maximum(j - 1, 0)
    e_changed = (j == 0) | (ie_ref[j] != ie_ref[prev])
    first_of_block = (j == 0) | (ib_ref[j] != ib_ref[prev])
    lo = lo_ref[j]
    hi = hi_ref[j]

    @pl.when(e_changed)
    def _():
        w1b[...] = w1_ref[...].astype(BF16)
        w3b[...] = w3_ref[...].astype(BF16)
        w2b[...] = w2_ref[...].astype(BF16)

    rows = ys_ref.shape[0]
    whole = (lo == 0) & (hi == rows)

    @pl.when(first_of_block & jnp.logical_not(whole))
    def _():
        ys_ref[...] = jnp.zeros_like(ys_ref)

    @pl.when(hi > lo)
    def _():
        x = xs_ref[...]
        a = jnp.dot(x, w1b[...], preferred_element_type=F32)
        g = jnp.dot(x, w3b[...], preferred_element_type=F32)
        y = jnp.dot((_silu(a) * g).astype(BF16), w2b[...], preferred_element_type=F32)

        @pl.when(whole)
        def _():
            ys_ref[...] = y

        @pl.when(jnp.logical_not(whole))
        def _():
            row = lax.broadcasted_iota(jnp.int32, (rows, 1), 0)
            ys_ref[...] = jnp.where((row >= lo) & (row < hi), y, ys_ref[...])


def _experts(layer, items, xs, w1, w3, w2, rows):
    a, d = xs.shape
    hid = w1.shape[-1]
    blk = lambda j, ib, ie, lo, hi: (ib[j], 0)
    wsel = lambda j, ib, ie, lo, hi: (layer, ie[j], 0, 0)
    grid_spec = pltpu.PrefetchScalarGridSpec(
        num_scalar_prefetch=4,
        grid=(items[0].shape[0],),
        in_specs=[
            pl.BlockSpec((rows, d), blk),
            pl.BlockSpec((None, None, d, hid), wsel),
            pl.BlockSpec((None, None, d, hid), wsel),
            pl.BlockSpec((None, None, hid, d), wsel),
        ],
        out_specs=pl.BlockSpec((rows, d), blk),
        scratch_shapes=[pltpu.VMEM((d, hid), BF16), pltpu.VMEM((d, hid), BF16), pltpu.VMEM((hid, d), BF16)],
    )
    return pl.pallas_call(
        _expert_kernel,
        out_shape=jax.ShapeDtypeStruct((a, d), F32),
        grid_spec=grid_spec,
        compiler_params=_params("arbitrary"),
        name="moe_experts",
    )(*items, xs, w1, w3, w2)


def _dispatch(ids, rank, counts, rows):
    n_experts = counts.shape[0]
    a = ids.size
    i32 = jnp.int32
    ends = jnp.cumsum(counts)
    starts = ends - counts
    dest = rank
    for e in range(n_experts):
        dest = dest + jnp.where(ids == e, starts[e], 0)
    t = ids.shape[1]
    tok = jnp.tile(jnp.arange(t, dtype=i32), TOP_K)
    row_tok = lax.sort_key_val(dest.reshape(-1), tok)[1]
    n_blk = a // rows
    bstart = jnp.arange(n_blk, dtype=i32) * rows
    count_le = lambda bounds, x: jnp.sum((bounds[None, :] <= x[:, None]).astype(i32), axis=1)
    e_lo = jnp.minimum(count_le(ends, bstart), n_experts - 1)
    e_hi = jnp.minimum(count_le(ends, bstart + rows - 1), n_experts - 1)
    n_items = e_hi - e_lo + 1
    item_end = jnp.cumsum(n_items)
    item_first = item_end - n_items
    jj = jnp.arange(n_blk + n_experts - 1, dtype=i32)
    valid = jj < item_end[-1]
    ib = jnp.minimum(count_le(item_end, jj), n_blk - 1)
    ie = jnp.where(valid, jnp.clip(e_lo[ib] + jj - item_first[ib], 0, n_experts - 1), e_hi[n_blk - 1]).astype(i32)
    lo = jnp.where(valid, jnp.clip(starts[ie] - ib * rows, 0, rows), 0).astype(i32)
    hi = jnp.where(valid, jnp.clip(ends[ie] - ib * rows, 0, rows), 0).astype(i32)
    return row_tok, dest, (ib, ie, lo, hi)


def _combine_kernel(x_ref, y0_ref, y1_ref, w_ref, gate_ref, o_ref):
    y = y0_ref[...] * w_ref[:, 0:1] + y1_ref[...] * w_ref[:, 1:2]
    o_ref[...] = x_ref[...] + gate_ref[...] * y


def _combine_norm_kernel(x_ref, y0_ref, y1_ref, w_ref, gate_ref, ng_ref, o_ref):
    y = y0_ref[...] * w_ref[:, 0:1] + y1_ref[...] * w_ref[:, 1:2]
    x = x_ref[...] + gate_ref[...] * y
    o_ref[...] = x * lax.rsqrt(jnp.mean(x * x, axis=-1, keepdims=True) + NORM_EPS) * ng_ref[...]


def _combine(x, y0, y1, wts, gate, final_g, tm):
    b, s, d = x.shape
    tile = lambda i, j: (i, j, 0)
    in_specs = [
        pl.BlockSpec((None, tm, d), tile),
        pl.BlockSpec((None, tm, d), tile),
        pl.BlockSpec((None, tm, d), tile),
        pl.BlockSpec((None, tm, TOP_K), tile),
        pl.BlockSpec((None, 1, d), lambda i, j: (i, 0, 0)),
    ]
    args = [x, y0, y1, wts, gate]
    kern = _combine_kernel
    if final_g is not None:
        in_specs.append(pl.BlockSpec((1, d), lambda i, j: (0, 0)))
        args.append(final_g)
        kern = _combine_norm_kernel
    return pl.pallas_call(
        kern,
        out_shape=jax.ShapeDtypeStruct((b, s, d), F32),
        grid=(b, s // tm),
        in_specs=in_specs,
        out_specs=pl.BlockSpec((None, tm, d), tile),
        compiler_params=_params("arbitrary", "arbitrary"),
        name="moe_combine",
    )(*args)


def _hier_moe(layer, x, routed, gate, w1, w3, w2, final_g, tm, rows):
    b, s, d = x.shape
    t = b * s
    n_experts = w1.shape[1]
    h, wts, route, counts = routed
    counts = counts[0, :n_experts].astype(jnp.int32)
    route = route.astype(jnp.int32)
    row_tok, dest, items = _dispatch(route[0:TOP_K], route[TOP_K:2 * TOP_K], counts, rows)
    xs = h.reshape(t, d)[row_tok]
    ys = _experts(layer, items, xs, w1, w3, w2, rows)
    y0 = ys[dest[0]].reshape(b, s, d)
    y1 = ys[dest[1]].reshape(b, s, d)
    return _combine(x, y0, y1, wts, gate, final_g, tm)


def _pick_tile(s, pref):
    tm = min(pref, s)
    assert s % tm == 0
    return tm


def kernel(x, c, positions, ada_w, ada_b, norm1_g, norm2_g, even_w_in, even_w_gate2, even_b_gate, even_gla_norm_g, even_conv_w, even_conv_b, even_conv_ln_g, even_conv_ln_b, even_w_out, odd_w_qkv, odd_w_out, moe_w_grp, moe_b_grp, moe_w_rt, moe_b_rt, moe_w1, moe_w3, moe_w2, final_norm_g):
    b, s, d = x.shape
    depth = ada_w.shape[0]
    n_experts = moe_w_rt.shape[-1]
    tm = _pick_tile(s, TOKEN_TILE)
    sb = _pick_tile(s, ATTN_SUPER)
    head_dim = d // ATTN_HEADS
    hk = GLA_HEADS * GLA_DK
    hv = GLA_HEADS * GLA_DV
    conv_ch = d // 2

    mods = _ada_mods(c, ada_w, ada_b)
    mod = lambda l, j: mods[l, j][:, None, :]
    cos_t = sin_t = None

    for layer in range(depth):
        i = layer // 2
        g1 = norm1_g[layer][None, :]
        if layer % 2 == 0:
            w_in = even_w_in[i]
            main = hk + hk + hv + hv
            w_cat = jnp.concatenate([
                w_in[:, :main], w_in[:, main + GLA_GATE_RANK:], w_in[:, main:main + GLA_GATE_RANK],
                jnp.zeros((d, LANES - GLA_GATE_RANK), w_in.dtype)], axis=1).astype(BF16)
            widths = (hk, hk, hv, hv, 2 * conv_ch, LANES)
            dtypes = (F32, F32, BF16, F32, F32, F32)
            q, k, v, g, u, a_lr = _norm_matmul(x, mod(layer, 0), mod(layer, 1), g1, w_cat, widths, dtypes, tm)
            wg = jnp.concatenate([even_w_gate2[i], jnp.zeros((LANES - GLA_GATE_RANK, hk), F32)], axis=0).astype(BF16)
            o_gla = _gla(q, k, v, g, a_lr, wg, even_b_gate[i][None, :], even_gla_norm_g[i][None, :], tm)
            y_conv = _conv_module(u, even_conv_w[i], even_conv_b[i][None, :], even_conv_ln_g[i][None, :],
                                  even_conv_ln_b[i][None, :], tm)
            acts, w_out = [o_gla, y_conv], even_w_out[i]
        else:
            if cos_t is None:
                cos_t, sin_t = _rope_tables(positions, head_dim, tm)
            q, k, v = _qkv_rope(x, mod(layer, 0), mod(layer, 1), g1, odd_w_qkv[i].astype(BF16),
                                cos_t, sin_t, head_dim, tm)
            o = _dilated_attention(q, k, v, head_dim, DILATED_BRANCHES, sb)
            acts, w_out = [o], odd_w_out[i]

        w_rt_full = jnp.concatenate([moe_w_rt[layer], moe_w_grp[layer],
                                     jnp.zeros((d, LANES - n_experts - N_GROUPS), F32)], axis=1).astype(BF16)
        b_rt_full = jnp.concatenate([moe_b_rt[layer], moe_b_grp[layer],
                                     jnp.zeros((LANES - n_experts - N_GROUPS,), F32)])[None, :]
        final_g = final_norm_g[None, :] if layer == depth - 1 else None
        x, *routed = _out_proj_router(acts, w_out.astype(BF16), x, mod(layer, 2), mod(layer, 3), mod(layer, 4),
                                      norm2_g[layer][None, :], w_rt_full, b_rt_full, n_experts, N_GROUPS, tm)
        x = _hier_moe(layer, x, routed, mod(layer, 5), moe_w1, moe_w3, moe_w2, final_g, tm, MOE_ROWS)
    return x
```

```python
import functools

import jax
import jax.numpy as jnp
import numpy as np
from jax import lax
from jax.experimental import pallas as pl
from jax.experimental.pallas import tpu as pltpu

F32 = jnp.float32
BF16 = jnp.bfloat16
HIGHEST = lax.Precision.HIGHEST

NORM_EPS = 1e-6
GLA_HEADS = 4
GLA_DK = 64
GLA_DV = 128
GLA_GATE_RANK = 16
GLA_TAU = 16.0
GLA_CHUNK = 64
CONV_WIDTH = 31
ATTN_HEADS = 16
DILATED_BRANCHES = ((128, 1), (512, 4), (2048, 16))
DIL_BLOCK = 128
ROPE_THETA = 500000.0
N_GROUPS = 4
EXPERTS_PER_GROUP = 8
TOP_K = 2
ADA_CHUNKS = 6

LANES = 128
SUBLANES = 8
VMEM_LIMIT = 56 * 1024 * 1024
TOKEN_TILE = 512
ATTN_SUPER = 2048
MOE_ROWS = 512
NEG_BIG = -1e30


def _params(*sem):
    return pltpu.CompilerParams(dimension_semantics=sem, vmem_limit_bytes=VMEM_LIMIT)


def _silu(x):
    return x * jax.nn.sigmoid(x)


def _modulate(x, g, shift, scale):
    y = x * lax.rsqrt(jnp.mean(x * x, axis=-1, keepdims=True) + NORM_EPS)
    return (y * g) * (1.0 + scale) + shift


def _ada_kernel(c_ref, w_ref, b_ref, o_ref):
    cond = _silu(c_ref[...])
    o_ref[...] = jnp.dot(cond, w_ref[...], preferred_element_type=F32, precision=HIGHEST) + b_ref[...]


def _ada_mods(c, ada_w, ada_b):
    depth, d, _ = ada_w.shape
    b = c.shape[0]
    return pl.pallas_call(
        _ada_kernel,
        out_shape=jax.ShapeDtypeStruct((depth, ADA_CHUNKS, b, d), F32),
        grid=(depth, ADA_CHUNKS),
        in_specs=[
            pl.BlockSpec((b, d), lambda l, j: (0, 0)),
            pl.BlockSpec((None, d, d), lambda l, j: (l, 0, j)),
            pl.BlockSpec((None, None, 1, d), lambda l, j: (l, j, 0, 0)),
        ],
        out_specs=pl.BlockSpec((None, None, b, d), lambda l, j: (l, j, 0, 0)),
        compiler_params=_params("arbitrary", "arbitrary"),
        name="ada_mods",
    )(c, ada_w, ada_b.reshape(depth, ADA_CHUNKS, 1, d))


def _norm_matmul_kernel(x_ref, sh_ref, sc_ref, g_ref, w_ref, *o_refs):
    h = _modulate(x_ref[...], g_ref[...], sh_ref[...], sc_ref[...]).astype(BF16)
    off = 0
    for o_ref in o_refs:
        n = o_ref.shape[-1]
        o_ref[...] = jnp.dot(h, w_ref[:, off:off + n], preferred_element_type=F32).astype(o_ref.dtype)
        off += n


def _log_sigmoid(z):
    return jnp.minimum(z, 0.0) - jnp.log1p(jnp.exp(-jnp.abs(z)))


def _gla_kernel(q_ref, k_ref, v_ref, g_ref, a_ref, wg_ref, bg_ref, ng_ref, o_ref, state_ref, la_ref, oacc_ref):
    tm = q_ref.shape[0]
    c = GLA_CHUNK

    z = jnp.dot(a_ref[...].astype(BF16), wg_ref[...], preferred_element_type=F32) + bg_ref[...]
    la_ref[...] = _log_sigmoid(z) * (1.0 / GLA_TAU)

    ri = lax.broadcasted_iota(jnp.int32, (c, c), 0)
    ci = lax.broadcasted_iota(jnp.int32, (c, c), 1)
    causal = ri >= ci
    tril = jnp.where(causal, 1.0, 0.0).astype(BF16)
    hk = GLA_HEADS * GLA_DK

    nh = GLA_HEADS
    lane_head = lax.shift_right_logical(lax.broadcasted_iota(jnp.int32, (c, hk), 1), GLA_DK.bit_length() - 1)
    r4 = lax.broadcasted_iota(jnp.int32, (nh * c, nh * c), 0)
    c4 = lax.broadcasted_iota(jnp.int32, (nh * c, nh * c), 1)
    shift_c = c.bit_length() - 1
    causal4 = (lax.shift_right_logical(r4, shift_c) == lax.shift_right_logical(c4, shift_c)) & (r4 >= c4)

    def stack_heads(t):
        return jnp.concatenate([jnp.where(lane_head == h, t, 0.0) for h in range(nh)], axis=0).astype(BF16)

    state = state_ref[...]
    for ic in range(tm // c):
        rows = slice(ic * c, (ic + 1) * c)
        la = la_ref[rows, :]
        p0 = la.astype(BF16)
        r1 = la - p0.astype(F32)
        p1 = r1.astype(BF16)
        p2 = (r1 - p1.astype(F32)).astype(BF16)
        parts = jnp.dot(tril, jnp.concatenate([p0, p1, p2], axis=1), preferred_element_type=F32)
        bcum = (parts[:, 2 * hk:] + parts[:, hk:2 * hk]) + parts[:, :hk]
        b_last = bcum[c - 1:c, :]
        q = q_ref[rows, :] * (GLA_DK ** -0.5)
        k = k_ref[rows, :]
        q4 = stack_heads(q * jnp.exp(bcum))
        k4 = stack_heads(k * jnp.exp(-bcum))
        kr4 = stack_heads(k * jnp.exp(b_last - bcum))
        dec = jnp.exp(jnp.broadcast_to(b_last, (GLA_DV, hk)).T)
        v4 = jnp.concatenate([v_ref[rows, h * GLA_DV:(h + 1) * GLA_DV] for h in range(nh)], axis=0)
        att = lax.dot_general(q4, k4, (((1,), (1,)), ((), ())), preferred_element_type=F32)
        att = jnp.where(causal4, att, 0.0).astype(BF16)
        o4 = jnp.dot(att, v4, preferred_element_type=F32)
        o4 = o4 + jnp.dot(q4, state.astype(BF16), preferred_element_type=F32)
        kv = lax.dot_general(kr4, v4, (((0,), (0,)), ((), ())), preferred_element_type=F32)
        state = dec * state + kv
        oacc_ref[rows, :] = jnp.concatenate([o4[h * c:(h + 1) * c, :] for h in range(nh)], axis=1)
    state_ref[...] = state

    for h in range(GLA_HEADS):
        vs = slice(h * GLA_DV, (h + 1) * GLA_DV)
        o = oacc_ref[:, vs]
        o = o * lax.rsqrt(jnp.mean(o * o, axis=-1, keepdims=True) + NORM_EPS) * ng_ref[...]
        o_ref[:, vs] = (o * _silu(g_ref[:, vs])).astype(o_ref.dtype)


CONV_HALO = 32


def _conv_kernel(u_ref, w_ref, cb_ref, lg_ref, lb_ref, o_ref, buf_ref):
    tm = u_ref.shape[0]
    ch = o_ref.shape[-1]

    buf_ref[CONV_HALO:, :] = u_ref[:, :ch] * jax.nn.sigmoid(u_ref[:, ch:])
    base = CONV_HALO - (CONV_WIDTH - 1)
    acc = None
    for b in range(SUBLANES):
        part = None
        span = tm + (SUBLANES if b else 0)
        for a in range((base + CONV_WIDTH - 1) // SUBLANES + 1):
            j = SUBLANES * a + b - base
            if 0 <= j < CONV_WIDTH:
                term = buf_ref[SUBLANES * a:SUBLANES * a + span, :] * w_ref[j:j + 1, :]
                part = term if part is None else part + term
        if part is not None:
            part = part[b:b + tm, :]
            acc = part if acc is None else acc + part
    buf_ref[0:CONV_HALO, :] = buf_ref[tm:tm + CONV_HALO, :]
    y = acc + cb_ref[...]
    mu = jnp.mean(y, axis=-1, keepdims=True)
    var = jnp.mean(jnp.square(y - mu), axis=-1, keepdims=True)
    y = (y - mu) * lax.rsqrt(var + NORM_EPS) * lg_ref[...] + lb_ref[...]
    o_ref[...] = _silu(y).astype(o_ref.dtype)


def _even_mixer_kernel(x_ref, sh_ref, sc_ref, g1_ref, w_ref, wg_ref, bg_ref, ng_ref, cw_ref, cb_ref, lg_ref, lb_ref,
                       o_gla_ref, y_conv_ref, q_s, k_s, v_s, g_s, u_s, a_s, state_ref, la_ref, oacc_ref, buf_ref):
    @pl.when(pl.program_id(1) == 0)
    def _():
        state_ref[...] = jnp.zeros_like(state_ref)
        buf_ref[0:CONV_HALO, :] = jnp.zeros((CONV_HALO, buf_ref.shape[1]), F32)

    _norm_matmul_kernel(x_ref, sh_ref, sc_ref, g1_ref, w_ref, q_s, k_s, v_s, g_s, u_s, a_s)
    _gla_kernel(q_s, k_s, v_s, g_s, a_s, wg_ref, bg_ref, ng_ref, o_gla_ref, state_ref, la_ref, oacc_ref)
    _conv_kernel(u_s, cw_ref, cb_ref, lg_ref, lb_ref, y_conv_ref, buf_ref)


def _even_mixer(x, shift, scale, g1, w_cat, w_gate2, b_gate, norm_g, conv_w, conv_b, ln_g, ln_b, tm):
    b, s, d = x.shape
    hk = GLA_HEADS * GLA_DK
    hv = GLA_HEADS * GLA_DV
    ch = conv_w.shape[-1]
    row = lambda i, j: (i, 0, 0)
    tile = lambda i, j: (i, j, 0)
    const = lambda i, j: (0, 0)
    full = lambda arr: pl.BlockSpec(arr.shape, const)
    return pl.pallas_call(
        _even_mixer_kernel,
        out_shape=[jax.ShapeDtypeStruct((b, s, hv), BF16), jax.ShapeDtypeStruct((b, s, ch), BF16)],
        grid=(b, s // tm),
        in_specs=[
            pl.BlockSpec((None, tm, d), tile),
            pl.BlockSpec((None, 1, d), row),
            pl.BlockSpec((None, 1, d), row),
            full(g1), full(w_cat), full(w_gate2), full(b_gate), full(norm_g),
            full(conv_w), full(conv_b), full(ln_g), full(ln_b),
        ],
        out_specs=[pl.BlockSpec((None, tm, hv), tile), pl.BlockSpec((None, tm, ch), tile)],
        scratch_shapes=[
            pltpu.VMEM((tm, hk), F32),
            pltpu.VMEM((tm, hk), F32),
            pltpu.VMEM((tm, hv), BF16),
            pltpu.VMEM((tm, hv), F32),
            pltpu.VMEM((tm, 2 * ch), F32),
            pltpu.VMEM((tm, w_gate2.shape[0]), F32),
            pltpu.VMEM((GLA_HEADS * GLA_DK, GLA_DV), F32),
            pltpu.VMEM((tm, hk), F32),
            pltpu.VMEM((tm, hv), F32),
            pltpu.VMEM((tm + CONV_HALO, ch), F32),
        ],
        compiler_params=_params("arbitrary", "arbitrary"),
        name="even_mixer",
    )(x, shift, scale, g1, w_cat, w_gate2, b_gate, norm_g, conv_w, conv_b, ln_g, ln_b)


def _rope_table_kernel(pos_ref, freq_ref, sign_ref, cos_ref, sin_ref):
    ang = pos_ref[...] * freq_ref[...]
    cos_ref[...] = jnp.cos(ang)
    sin_ref[...] = jnp.sin(ang) * sign_ref[...]


def _rope_tables(positions, head_dim, tm):
    b, s = positions.shape
    rope_dims = head_dim // 4
    half = rope_dims // 2
    inv_freq = ROPE_THETA ** (-jnp.arange(0, rope_dims, 2, dtype=F32) / rope_dims)
    jj = jnp.arange(LANES) % head_dim
    freq = jnp.where(jj < rope_dims, inv_freq[jj % half], 0.0).astype(F32)[None, :]
    sign = jnp.where(jj < half, -1.0, jnp.where(jj < rope_dims, 1.0, 0.0)).astype(F32)[None, :]
    pos = positions.astype(F32)[..., None]
    tile = lambda i, j: (i, j, 0)
    const = lambda i, j: (0, 0)
    return pl.pallas_call(
        _rope_table_kernel,
        out_shape=[jax.ShapeDtypeStruct((b, s, LANES), F32)] * 2,
        grid=(b, s // tm),
        in_specs=[pl.BlockSpec((None, tm, 1), tile), pl.BlockSpec((1, LANES), const),
                  pl.BlockSpec((1, LANES), const)],
        out_specs=[pl.BlockSpec((None, tm, LANES), tile)] * 2,
        compiler_params=_params("arbitrary", "arbitrary"),
        name="rope_tables",
    )(pos, freq, sign)


def _qkv_kernel(x_ref, sh_ref, sc_ref, g_ref, w_ref, cos_ref, sin_ref, q_ref, k_ref, v_ref, *, head_dim):
    h = _modulate(x_ref[...], g_ref[...], sh_ref[...], sc_ref[...]).astype(BF16)
    d = q_ref.shape[-1]
    half = head_dim // 8
    cosf = jnp.tile(cos_ref[...], (1, d // LANES))
    sinf = jnp.tile(sin_ref[...], (1, d // LANES))
    lane = lax.broadcasted_iota(jnp.int32, (1, d), 1)
    first = (lane % head_dim) < half
    for idx, (o_ref, mult) in enumerate(((q_ref, head_dim ** -0.5), (k_ref, 1.0))):
        t = jnp.dot(h, w_ref[:, idx * d:(idx + 1) * d], preferred_element_type=F32)
        partner = jnp.where(first, pltpu.roll(t, d - half, 1), pltpu.roll(t, half, 1))
        o_ref[...] = ((t * cosf + partner * sinf) * mult).astype(o_ref.dtype)
    v_ref[...] = jnp.dot(h, w_ref[:, 2 * d:], preferred_element_type=F32).astype(v_ref.dtype)


def _qkv_rope(x, shift, scale, g, w, cos_t, sin_t, head_dim, tm):
    b, s, d = x.shape
    row = lambda i, j: (i, 0, 0)
    tile = lambda i, j: (i, j, 0)
    return pl.pallas_call(
        functools.partial(_qkv_kernel, head_dim=head_dim),
        out_shape=[jax.ShapeDtypeStruct((b, s, d), BF16)] * 3,
        grid=(b, s // tm),
        in_specs=[
            pl.BlockSpec((None, tm, d), tile),
            pl.BlockSpec((None, 1, d), row),
            pl.BlockSpec((None, 1, d), row),
            pl.BlockSpec((1, d), lambda i, j: (0, 0)),
            pl.BlockSpec(w.shape, lambda i, j: (0, 0)),
            pl.BlockSpec((None, tm, LANES), tile),
            pl.BlockSpec((None, tm, LANES), tile),
        ],
        out_specs=[pl.BlockSpec((None, tm, d), tile)] * 3,
        compiler_params=_params("arbitrary", "arbitrary"),
        name="qkv_rope",
    )(x, shift, scale, g, w, cos_t, sin_t)


ONES_ROWS = 16
ATTN_BLOCKS = ATTN_SUPER // DIL_BLOCK


def _attn_bias(branches, heads):
    blk = DIL_BLOCK
    kj = np.arange(2 * blk)[:, None]
    qi = np.arange(blk)[None, :]
    dist = qi + blk - kj
    out = []
    for window, dil in branches:
        band = (dist >= 0) & (dist <= window // dil)
        both = np.stack([band, band & (kj >= blk)])
        out.append(np.tile(np.where(both, 0.0, NEG_BIG), (1, 1, heads)))
    return jnp.asarray(np.stack(out), F32)


def _attn_kernel(bias_ref, q_ref, kc_ref, kp_ref, vc_ref, vp_ref, o_ref, qf, kf, vf, ob, lb, st_s, vt_s,
                 *, head_dim, branches):
    sb = q_ref.shape[0]
    blk = DIL_BLOCK
    heads = q_ref.shape[1] // head_dim
    first_super = pl.program_id(2) == 0
    slot0 = jnp.maximum(pl.program_id(2) - pl.num_programs(2), 0)

    qf[...] = q_ref[...].astype(F32)
    kf[0:sb, :] = kp_ref[...].astype(F32)
    kf[sb:, :] = kc_ref[...].astype(F32)
    vf[0:sb, :] = vp_ref[...].astype(F32)
    vf[sb:, :] = vc_ref[...].astype(F32)

    vt_s[:, LANES:, :] = jnp.ones((ATTN_BLOCKS, ONES_ROWS, 2 * blk), BF16)

    lane = lax.broadcasted_iota(jnp.int32, (blk, LANES), 1)
    head_masks = [(lane >= h * head_dim) & (lane < (h + 1) * head_dim) for h in range(heads)]

    for bi, (window, dil) in enumerate(branches):
        assert window // dil <= blk and sb == ATTN_SUPER and sb % (dil * blk) == 0
        unit = dil * blk

        def scores(j, q0, first_unit, dil=dil, unit=unit, bi=bi):
            k0 = sb + q0 - unit
            no_prev = jnp.where(first_super, 1, 0) if first_unit else 0
            qb = qf[pl.ds(q0, blk, stride=dil), :]
            q2 = jnp.concatenate([jnp.where(mk, qb, 0.0) for mk in head_masks], axis=0).astype(BF16)
            kb = kf[pl.ds(k0, 2 * blk, stride=dil), :].astype(BF16)
            vt_s[j, 0:LANES, :] = vf[pl.ds(k0, 2 * blk, stride=dil), :].T.astype(BF16)
            st = lax.dot_general(kb, q2, (((1,), (1,)), ((), ())), preferred_element_type=F32)
            st_s[slot0 + j] = st + bias_ref[bi, no_prev]

        def softmax_pv(j):
            m = jnp.max(st_s[slot0 + j], axis=0, keepdims=True)
            p = jnp.exp(st_s[slot0 + j] - m).astype(BF16)
            of = jnp.dot(vt_s[j], p, preferred_element_type=F32)
            l = of[LANES:LANES + 1, :]
            lse = m + jnp.log(l)
            o_rows, lse_rows = [], []
            for h in range(heads):
                cols = slice(h * blk, (h + 1) * blk)
                o_rows.append(of[h * head_dim:(h + 1) * head_dim, cols] / l[:, cols])
                lse_rows.append(jnp.broadcast_to(lse[:, cols], (head_dim, blk)))
            return jnp.concatenate(o_rows, axis=0).T, jnp.concatenate(lse_rows, axis=0).T

        starts = [(idx // dil) * unit + idx % dil for idx in range(ATTN_BLOCKS)]
        for j, q0 in enumerate(starts):
            scores(j, q0, q0 < unit)
        for j, q0 in enumerate(starts):
            o_tok, lse_tok = softmax_pv(j)
            ob[bi, pl.ds(q0, blk, stride=dil), :] = o_tok
            lb[bi, pl.ds(q0, blk, stride=dil), :] = lse_tok

    nb = len(branches)
    m = lb[0]
    for bi in range(1, nb):
        m = jnp.maximum(m, lb[bi])
    num = jnp.zeros_like(m)
    den = jnp.zeros_like(m)
    for bi in range(nb):
        e = jnp.exp(lb[bi] - m)
        num = num + e * ob[bi]
        den = den + e
    o_ref[...] = (num / den).astype(o_ref.dtype)


def _dilated_attention(q, k, v, head_dim, branches, sb):
    b, s, d = q.shape
    groups = d // LANES
    cur = lambda i, g, n: (i, n, g)
    prev = lambda i, g, n: (i, jnp.maximum(n - 1, 0), g)
    nb = len(branches)
    bias = _attn_bias(branches, LANES // head_dim)
    return pl.pallas_call(
        functools.partial(_attn_kernel, head_dim=head_dim, branches=branches),
        out_shape=jax.ShapeDtypeStruct((b, s, d), BF16),
        grid=(b, groups, s // sb),
        in_specs=[
            pl.BlockSpec(bias.shape, lambda i, g, n: (0, 0, 0, 0)),
            pl.BlockSpec((None, sb, LANES), cur),
            pl.BlockSpec((None, sb, LANES), cur),
            pl.BlockSpec((None, sb, LANES), prev),
            pl.BlockSpec((None, sb, LANES), cur),
            pl.BlockSpec((None, sb, LANES), prev),
        ],
        out_specs=pl.BlockSpec((None, sb, LANES), cur),
        scratch_shapes=[
            pltpu.VMEM((sb, LANES), F32),
            pltpu.VMEM((2 * sb, LANES), F32),
            pltpu.VMEM((2 * sb, LANES), F32),
            pltpu.VMEM((nb, sb, LANES), F32),
            pltpu.VMEM((nb, sb, LANES), F32),
            pltpu.VMEM((ATTN_BLOCKS, 2 * DIL_BLOCK, LANES // head_dim * DIL_BLOCK), F32),
            pltpu.VMEM((ATTN_BLOCKS, LANES + ONES_ROWS, 2 * DIL_BLOCK), BF16),
        ],
        compiler_params=_params("arbitrary", "arbitrary", "arbitrary"),
        name="dilated_attention",
    )(bias, q, k, k, v, v)


def _route_tile(x, sh_ref, sc_ref, g_ref, w_ref, b_ref, h_ref, wts_ref, route_ref, counts_ref, carry_ref,
                n_experts, n_groups):
    h = _modulate(x, g_ref[...], sh_ref[...], sc_ref[...]).astype(BF16)
    h_ref[...] = h
    logits = jnp.dot(h, w_ref[...], preferred_element_type=F32) + b_ref[...]
    tm = logits.shape[0]
    epg = n_experts // n_groups
    lane = lax.broadcasted_iota(jnp.int32, logits.shape, 1)
    neg = -jnp.inf
    big = jnp.int32(LANES)

    def first_max(mask):
        val = jnp.max(jnp.where(mask, logits, neg), axis=-1, keepdims=True)
        idx = jnp.min(jnp.where(mask & (logits == val), lane, big), axis=-1, keepdims=True)
        return val, idx

    gmask = (lane >= n_experts) & (lane < n_experts + n_groups)
    gmax, gidx = first_max(gmask)
    gsum = jnp.sum(jnp.where(gmask, jnp.exp(logits - gmax), 0.0), axis=-1, keepdims=True)
    g_w = 1.0 / gsum
    grp = gidx - n_experts
    assert epg & (epg - 1) == 0
    emask = (lane < n_experts) & (lax.shift_right_logical(lane, epg.bit_length() - 1) == grp)
    v1, i1 = first_max(emask)
    v2, i2 = first_max(emask & (lane != i1))
    e2 = jnp.exp(v2 - v1)
    den = 1.0 + e2
    col = lax.broadcasted_iota(jnp.int32, (tm, TOP_K), 1)
    wts_ref[...] = jnp.where(col == 0, 1.0 / den, e2 / den) * g_w

    hit1 = lane == i1
    hit2 = lane == i2
    onehot = jnp.where(hit1 | hit2, 1.0, 0.0)
    ri = lax.broadcasted_iota(jnp.int32, (tm, tm), 0)
    ci = lax.broadcasted_iota(jnp.int32, (tm, tm), 1)
    before = jnp.where(ci < ri, 1.0, 0.0).astype(BF16)
    prefix = jnp.dot(before, onehot.astype(BF16), preferred_element_type=F32) + carry_ref[0:1, :]
    r1 = jnp.sum(jnp.where(hit1, prefix, 0.0), axis=-1, keepdims=True)
    r2 = jnp.sum(jnp.where(hit2, prefix, 0.0), axis=-1, keepdims=True)
    packed = jnp.where(lane == 0, i1.astype(F32), jnp.where(lane == 1, i2.astype(F32),
                       jnp.where(lane == 2, r1, jnp.where(lane == 3, r2, 0.0))))
    route_ref[...] = packed.T[0:SUBLANES, :]
    carry_ref[...] = carry_ref[...] + jnp.sum(onehot, axis=0, keepdims=True)
    counts_ref[...] = carry_ref[...]


def _out_proj_router_kernel(*refs, n_acts, n_experts, n_groups):
    a_refs = refs[:n_acts]
    w_ref, x_ref, gate_ref, sh_ref, sc_ref, g_ref, wr_ref, br_ref = refs[n_acts:n_acts + 8]
    xo_ref, h_ref, wts_ref, route_ref, counts_ref, carry_ref = refs[n_acts + 8:]

    @pl.when((pl.program_id(0) == 0) & (pl.program_id(1) == 0))
    def _():
        carry_ref[...] = jnp.zeros_like(carry_ref)

    acc = None
    off = 0
    for a_ref in a_refs:
        kk = a_ref.shape[-1]
        part = jnp.dot(a_ref[...].astype(BF16), w_ref[off:off + kk, :], preferred_element_type=F32)
        acc = part if acc is None else acc + part
        off += kk
    x = x_ref[...] + gate_ref[...] * acc
    xo_ref[...] = x
    _route_tile(x, sh_ref, sc_ref, g_ref, wr_ref, br_ref, h_ref, wts_ref, route_ref, counts_ref, carry_ref,
                n_experts, n_groups)


def _out_proj_router(acts, w, x, gate, shift, scale, g, w_rt, b_rt, n_experts, n_groups, tm):
    b, s, d = x.shape
    row = lambda i, j: (i, 0, 0)
    tile = lambda i, j: (i, j, 0)
    const = lambda i, j: (0, 0)
    return pl.pallas_call(
        functools.partial(_out_proj_router_kernel, n_acts=len(acts), n_experts=n_experts, n_groups=n_groups),
        out_shape=[jax.ShapeDtypeStruct((b, s, d), F32),
                   jax.ShapeDtypeStruct((b, s, d), BF16),
                   jax.ShapeDtypeStruct((b, s, TOP_K), F32),
                   jax.ShapeDtypeStruct((SUBLANES, b * s), F32),
                   jax.ShapeDtypeStruct((SUBLANES, LANES), F32)],
        grid=(b, s // tm),
        in_specs=[pl.BlockSpec((None, tm, a.shape[-1]), tile) for a in acts] + [
            pl.BlockSpec(w.shape, const),
            pl.BlockSpec((None, tm, d), tile),
            pl.BlockSpec((None, 1, d), row),
            pl.BlockSpec((None, 1, d), row),
            pl.BlockSpec((None, 1, d), row),
            pl.BlockSpec((1, d), const),
            pl.BlockSpec((d, LANES), const),
            pl.BlockSpec((1, LANES), const),
        ],
        out_specs=[pl.BlockSpec((None, tm, d), tile),
                   pl.BlockSpec((None, tm, d), tile),
                   pl.BlockSpec((None, tm, TOP_K), tile),
                   pl.BlockSpec((SUBLANES, tm), lambda i, j: (0, i * (s // tm) + j)),
                   pl.BlockSpec((SUBLANES, LANES), const)],
        scratch_shapes=[pltpu.VMEM((SUBLANES, LANES), F32)],
        compiler_params=_params("arbitrary", "arbitrary"),
        name="out_proj_router",
    )(*acts, w, x, gate, shift, scale, g, w_rt, b_rt)


def _expert_kernel(ib_ref, ie_ref, lo_ref, hi_ref, xs_ref, w1_ref, w3_ref, w2_ref, ys_ref, w1b, w3b, w2b):
    j = pl.program_id(0)
    prev = jnp.maximum(j - 1, 0)
    e_changed = (j == 0) | (ie_ref[j] != ie_ref[prev])
    first_of_block = (j == 0) | (ib_ref[j] != ib_ref[prev])
    lo = lo_ref[j]
    hi = hi_ref[j]

    @pl.when(e_changed)
    def _():
        w1b[...] = w1_ref[...].astype(BF16)
        w3b[...] = w3_ref[...].astype(BF16)
        w2b[...] = w2_ref[...].astype(BF16)

    rows = ys_ref.shape[0]
    whole = (lo == 0) & (hi == rows)

    @pl.when(first_of_block & jnp.logical_not(whole))
    def _():
        ys_ref[...] = jnp.zeros_like(ys_ref)

    @pl.when(hi > lo)
    def _():
        x = xs_ref[...]
        a = jnp.dot(x, w1b[...], preferred_element_type=F32)
        g = jnp.dot(x, w3b[...], preferred_element_type=F32)
        y = jnp.dot((_silu(a) * g).astype(BF16), w2b[...], preferred_element_type=F32)

        @pl.when(whole)
        def _():
            ys_ref[...] = y

        @pl.when(jnp.logical_not(whole))
        def _():
            row = lax.broadcasted_iota(jnp.int32, (rows, 1), 0)
            ys_ref[...] = jnp.where((row >= lo) & (row < hi), y, ys_ref[...])


def _experts(layer, items, xs, w1, w3, w2, rows):
    a, d = xs.shape
    hid = w1.shape[-1]
    blk = lambda j, ib, ie, lo, hi: (ib[j], 0)
    wsel = lambda j, ib, ie, lo, hi: (layer, ie[j], 0, 0)
    grid_spec = pltpu.PrefetchScalarGridSpec(
        num_scalar_prefetch=4,
        grid=(items[0].shape[0],),
        in_specs=[
            pl.BlockSpec((rows, d), blk),
            pl.BlockSpec((None, None, d, hid), wsel),
            pl.BlockSpec((None, None, d, hid), wsel),
            pl.BlockSpec((None, None, hid, d), wsel),
        ],
        out_specs=pl.BlockSpec((rows, d), blk),
        scratch_shapes=[pltpu.VMEM((d, hid), BF16), pltpu.VMEM((d, hid), BF16), pltpu.VMEM((hid, d), BF16)],
    )
    return pl.pallas_call(
        _expert_kernel,
        out_shape=jax.ShapeDtypeStruct((a, d), F32),
        grid_spec=grid_spec,
        compiler_params=_params("arbitrary"),
        name="moe_experts",
    )(*items, xs, w1, w3, w2)


def _dispatch(ids, rank, counts, rows):
    n_experts = counts.shape[0]
    a = ids.size
    i32 = jnp.int32
    ends = jnp.cumsum(counts)
    starts = ends - counts
    dest = rank
    for e in range(n_experts):
        dest = dest + jnp.where(ids == e, starts[e], 0)
    t = ids.shape[1]
    tok = jnp.tile(jnp.arange(t, dtype=i32), TOP_K)
    row_tok = lax.sort_key_val(dest.reshape(-1), tok)[1]
    n_blk = a // rows
    bstart = jnp.arange(n_blk, dtype=i32) * rows
    count_le = lambda bounds, x: jnp.sum((bounds[None, :] <= x[:, None]).astype(i32), axis=1)
    e_lo = jnp.minimum(count_le(ends, bstart), n_experts - 1)
    e_hi = jnp.minimum(count_le(ends, bstart + rows - 1), n_experts - 1)
    n_items = e_hi - e_lo + 1
    item_end = jnp.cumsum(n_items)
    item_first = item_end - n_items
    jj = jnp.arange(n_blk + n_experts - 1, dtype=i32)
    valid = jj < item_end[-1]
    ib = jnp.minimum(count_le(item_end, jj), n_blk - 1)
    ie = jnp.where(valid, jnp.clip(e_lo[ib] + jj - item_first[ib], 0, n_experts - 1), e_hi[n_blk - 1]).astype(i32)
    lo = jnp.where(valid, jnp.clip(starts[ie] - ib * rows, 0, rows), 0).astype(i32)
    hi = jnp.where(valid, jnp.clip(ends[ie] - ib * rows, 0, rows), 0).astype(i32)
    return row_tok, dest, (ib, ie, lo, hi)


def _combine_kernel(x_ref, y0_ref, y1_ref, w_ref, gate_ref, o_ref):
    y = y0_ref[...] * w_ref[:, 0:1] + y1_ref[...] * w_ref[:, 1:2]
    o_ref[...] = x_ref[...] + gate_ref[...] * y


def _combine_norm_kernel(x_ref, y0_ref, y1_ref, w_ref, gate_ref, ng_ref, o_ref):
    y = y0_ref[...] * w_ref[:, 0:1] + y1_ref[...] * w_ref[:, 1:2]
    x = x_ref[...] + gate_ref[...] * y
    o_ref[...] = x * lax.rsqrt(jnp.mean(x * x, axis=-1, keepdims=True) + NORM_EPS) * ng_ref[...]


def _combine(x, y0, y1, wts, gate, final_g, tm):
    b, s, d = x.shape
    tile = lambda i, j: (i, j, 0)
    in_specs = [
        pl.BlockSpec((None, tm, d), tile),
        pl.BlockSpec((None, tm, d), tile),
        pl.BlockSpec((None, tm, d), tile),
        pl.BlockSpec((None, tm, TOP_K), tile),
        pl.BlockSpec((None, 1, d), lambda i, j: (i, 0, 0)),
    ]
    args = [x, y0, y1, wts, gate]
    kern = _combine_kernel
    if final_g is not None:
        in_specs.append(pl.BlockSpec((1, d), lambda i, j: (0, 0)))
        args.append(final_g)
        kern = _combine_norm_kernel
    return pl.pallas_call(
        kern,
        out_shape=jax.ShapeDtypeStruct((b, s, d), F32),
        grid=(b, s // tm),
        in_specs=in_specs,
        out_specs=pl.BlockSpec((None, tm, d), tile),
        compiler_params=_params("arbitrary", "arbitrary"),
        name="moe_combine",
    )(*args)


def _hier_moe(layer, x, routed, gate, w1, w3, w2, final_g, tm, rows):
    b, s, d = x.shape
    t = b * s
    n_experts = w1.shape[1]
    h, wts, route, counts = routed
    counts = counts[0, :n_experts].astype(jnp.int32)
    route = route.astype(jnp.int32)
    row_tok, dest, items = _dispatch(route[0:TOP_K], route[TOP_K:2 * TOP_K], counts, rows)
    xs = h.reshape(t, d)[row_tok]
    ys = _experts(layer, items, xs, w1, w3, w2, rows)
    y0 = ys[dest[0]].reshape(b, s, d)
    y1 = ys[dest[1]].reshape(b, s, d)
    return _combine(x, y0, y1, wts, gate, final_g, tm)


def _pick_tile(s, pref):
    tm = min(pref, s)
    assert s % tm == 0
    return tm


def kernel(x, c, positions, ada_w, ada_b, norm1_g, norm2_g, even_w_in, even_w_gate2, even_b_gate, even_gla_norm_g, even_conv_w, even_conv_b, even_conv_ln_g, even_conv_ln_b, even_w_out, odd_w_qkv, odd_w_out, moe_w_grp, moe_b_grp, moe_w_rt, moe_b_rt, moe_w1, moe_w3, moe_w2, final_norm_g):
    b, s, d = x.shape
    depth = ada_w.shape[0]
    n_experts = moe_w_rt.shape[-1]
    tm = _pick_tile(s, TOKEN_TILE)
    sb = _pick_tile(s, ATTN_SUPER)
    head_dim = d // ATTN_HEADS
    hk = GLA_HEADS * GLA_DK
    hv = GLA_HEADS * GLA_DV
    conv_ch = d // 2

    mods = _ada_mods(c, ada_w, ada_b)
    mod = lambda l, j: mods[l, j][:, None, :]
    cos_t = sin_t = None

    for layer in range(depth):
        i = layer // 2
        g1 = norm1_g[layer][None, :]
        if layer % 2 == 0:
            w_in = even_w_in[i]
            main = hk + hk + hv + hv
            w_cat = jnp.concatenate([
                w_in[:, :main], w_in[:, main + GLA_GATE_RANK:], w_in[:, main:main + GLA_GATE_RANK],
                jnp.zeros((d, LANES - GLA_GATE_RANK), w_in.dtype)], axis=1).astype(BF16)
            wg = jnp.concatenate([even_w_gate2[i], jnp.zeros((LANES - GLA_GATE_RANK, hk), F32)], axis=0).astype(BF16)
            o_gla, y_conv = _even_mixer(
                x, mod(layer, 0), mod(layer, 1), g1, w_cat, wg, even_b_gate[i][None, :],
                even_gla_norm_g[i][None, :], even_conv_w[i], even_conv_b[i][None, :],
                even_conv_ln_g[i][None, :], even_conv_ln_b[i][None, :], tm)
            acts, w_out = [o_gla, y_conv], even_w_out[i]
        else:
            if cos_t is None:
                cos_t, sin_t = _rope_tables(positions, head_dim, tm)
            q, k, v = _qkv_rope(x, mod(layer, 0), mod(layer, 1), g1, odd_w_qkv[i].astype(BF16),
                                cos_t, sin_t, head_dim, tm)
            o = _dilated_attention(q, k, v, head_dim, DILATED_BRANCHES, sb)
            acts, w_out = [o], odd_w_out[i]

        w_rt_full = jnp.concatenate([moe_w_rt[layer], moe_w_grp[layer],
                                     jnp.zeros((d, LANES - n_experts - N_GROUPS), F32)], axis=1).astype(BF16)
        b_rt_full = jnp.concatenate([moe_b_rt[layer], moe_b_grp[layer],
                                     jnp.zeros((LANES - n_experts - N_GROUPS,), F32)])[None, :]
        final_g = final_norm_g[None, :] if layer == depth - 1 else None
        x, *routed = _out_proj_router(acts, w_out.astype(BF16), x, mod(layer, 2), mod(layer, 3), mod(layer, 4),
                                      norm2_g[layer][None, :], w_rt_full, b_rt_full, n_experts, N_GROUPS, tm)
        x = _hier_moe(layer, x, routed, mod(layer, 5), moe_w1, moe_w3, moe_w2, final_g, tm, MOE_ROWS)
    return x
```

```python
import functools

import jax
import jax.numpy as jnp
import numpy as np
from jax import lax
from jax.experimental import pallas as pl
from jax.experimental.pallas import tpu as pltpu

F32 = jnp.float32
BF16 = jnp.bfloat16
HIGHEST = lax.Precision.HIGHEST

NORM_EPS = 1e-6
GLA_HEADS = 4
GLA_DK = 64
GLA_DV = 128
GLA_GATE_RANK = 16
GLA_TAU = 16.0
GLA_CHUNK = 64
CONV_WIDTH = 31
ATTN_HEADS = 16
DILATED_BRANCHES = ((128, 1), (512, 4), (2048, 16))
DIL_BLOCK = 128
ROPE_THETA = 500000.0
N_GROUPS = 4
EXPERTS_PER_GROUP = 8
TOP_K = 2
ADA_CHUNKS = 6

LANES = 128
SUBLANES = 8
VMEM_LIMIT = 56 * 1024 * 1024
TOKEN_TILE = 512
ATTN_SUPER = 2048
MOE_ROWS = 512
NEG_BIG = -1e30


def _params(*sem):
    return pltpu.CompilerParams(dimension_semantics=sem, vmem_limit_bytes=VMEM_LIMIT)


def _silu(x):
    return x * jax.nn.sigmoid(x)


def _modulate(x, g, shift, scale):
    y = x * lax.rsqrt(jnp.mean(x * x, axis=-1, keepdims=True) + NORM_EPS)
    return (y * g) * (1.0 + scale) + shift


def _ada_kernel(c_ref, w_ref, b_ref, o_ref):
    cond = _silu(c_ref[...])
    o_ref[...] = jnp.dot(cond, w_ref[...], preferred_element_type=F32, precision=HIGHEST) + b_ref[...]


def _ada_mods(c, ada_w, ada_b):
    depth, d, _ = ada_w.shape
    b = c.shape[0]
    return pl.pallas_call(
        _ada_kernel,
        out_shape=jax.ShapeDtypeStruct((depth, ADA_CHUNKS, b, d), F32),
        grid=(depth, ADA_CHUNKS),
        in_specs=[
            pl.BlockSpec((b, d), lambda l, j: (0, 0)),
            pl.BlockSpec((None, d, d), lambda l, j: (l, 0, j)),
            pl.BlockSpec((None, None, 1, d), lambda l, j: (l, j, 0, 0)),
        ],
        out_specs=pl.BlockSpec((None, None, b, d), lambda l, j: (l, j, 0, 0)),
        compiler_params=_params("arbitrary", "arbitrary"),
        name="ada_mods",
    )(c, ada_w, ada_b.reshape(depth, ADA_CHUNKS, 1, d))


def _norm_matmul_kernel(x_ref, sh_ref, sc_ref, g_ref, w_ref, *o_refs):
    h = _modulate(x_ref[...], g_ref[...], sh_ref[...], sc_ref[...]).astype(BF16)
    off = 0
    for o_ref in o_refs:
        n = o_ref.shape[-1]
        o_ref[...] = jnp.dot(h, w_ref[:, off:off + n], preferred_element_type=F32).astype(o_ref.dtype)
        off += n


def _log_sigmoid(z):
    return jnp.minimum(z, 0.0) - jnp.log1p(jnp.exp(-jnp.abs(z)))


def _gla_kernel(q_ref, k_ref, v_ref, g_ref, a_ref, wg_ref, bg_ref, ng_ref, o_ref, state_ref, la_ref, oacc_ref):
    tm = q_ref.shape[0]
    c = GLA_CHUNK

    z = jnp.dot(a_ref[...].astype(BF16), wg_ref[...], preferred_element_type=F32) + bg_ref[...]
    la_ref[...] = _log_sigmoid(z) * (1.0 / GLA_TAU)

    ri = lax.broadcasted_iota(jnp.int32, (c, c), 0)
    ci = lax.broadcasted_iota(jnp.int32, (c, c), 1)
    causal = ri >= ci
    tril = jnp.where(causal, 1.0, 0.0).astype(BF16)
    hk = GLA_HEADS * GLA_DK

    nh = GLA_HEADS
    lane_head = lax.shift_right_logical(lax.broadcasted_iota(jnp.int32, (c, hk), 1), GLA_DK.bit_length() - 1)
    r4 = lax.broadcasted_iota(jnp.int32, (nh * c, nh * c), 0)
    c4 = lax.broadcasted_iota(jnp.int32, (nh * c, nh * c), 1)
    shift_c = c.bit_length() - 1
    causal4 = (lax.shift_right_logical(r4, shift_c) == lax.shift_right_logical(c4, shift_c)) & (r4 >= c4)

    def stack_heads(t):
        return jnp.concatenate([jnp.where(lane_head == h, t, 0.0) for h in range(nh)], axis=0).astype(BF16)

    state = state_ref[...]
    for ic in range(tm // c):
        rows = slice(ic * c, (ic + 1) * c)
        la = la_ref[rows, :]
        p0 = la.astype(BF16)
        r1 = la - p0.astype(F32)
        p1 = r1.astype(BF16)
        p2 = (r1 - p1.astype(F32)).astype(BF16)
        parts = jnp.dot(tril, jnp.concatenate([p0, p1, p2], axis=1), preferred_element_type=F32)
        bcum = (parts[:, 2 * hk:] + parts[:, hk:2 * hk]) + parts[:, :hk]
        b_last = bcum[c - 1:c, :]
        q = q_ref[rows, :] * (GLA_DK ** -0.5)
        k = k_ref[rows, :]
        q4 = stack_heads(q * jnp.exp(bcum))
        k4 = stack_heads(k * jnp.exp(-bcum))
        kr4 = stack_heads(k * jnp.exp(b_last - bcum))
        dec = jnp.exp(jnp.broadcast_to(b_last, (GLA_DV, hk)).T)
        v4 = jnp.concatenate([v_ref[rows, h * GLA_DV:(h + 1) * GLA_DV] for h in range(nh)], axis=0)
        att = lax.dot_general(q4, k4, (((1,), (1,)), ((), ())), preferred_element_type=F32)
        att = jnp.where(causal4, att, 0.0).astype(BF16)
        o4 = jnp.dot(att, v4, preferred_element_type=F32)
        o4 = o4 + jnp.dot(q4, state.astype(BF16), preferred_element_type=F32)
        kv = lax.dot_general(kr4, v4, (((0,), (0,)), ((), ())), preferred_element_type=F32)
        state = dec * state + kv
        oacc_ref[rows, :] = jnp.concatenate([o4[h * c:(h + 1) * c, :] for h in range(nh)], axis=1)
    state_ref[...] = state

    for h in range(GLA_HEADS):
        vs = slice(h * GLA_DV, (h + 1) * GLA_DV)
        o = oacc_ref[:, vs]
        o = o * lax.rsqrt(jnp.mean(o * o, axis=-1, keepdims=True) + NORM_EPS) * ng_ref[...]
        o_ref[:, vs] = (o * _silu(g_ref[:, vs])).astype(o_ref.dtype)


CONV_HALO = 32


def _conv_kernel(u_ref, w_ref, cb_ref, lg_ref, lb_ref, o_ref, buf_ref):
    tm = u_ref.shape[0]
    ch = o_ref.shape[-1]

    buf_ref[CONV_HALO:, :] = u_ref[:, :ch] * jax.nn.sigmoid(u_ref[:, ch:])
    base = CONV_HALO - (CONV_WIDTH - 1)
    acc = None
    for b in range(SUBLANES):
        part = None
        span = tm + (SUBLANES if b else 0)
        for a in range((base + CONV_WIDTH - 1) // SUBLANES + 1):
            j = SUBLANES * a + b - base
            if 0 <= j < CONV_WIDTH:
                term = buf_ref[SUBLANES * a:SUBLANES * a + span, :] * w_ref[j:j + 1, :]
                part = term if part is None else part + term
        if part is not None:
            part = part[b:b + tm, :]
            acc = part if acc is None else acc + part
    buf_ref[0:CONV_HALO, :] = buf_ref[tm:tm + CONV_HALO, :]
    y = acc + cb_ref[...]
    mu = jnp.mean(y, axis=-1, keepdims=True)
    var = jnp.mean(jnp.square(y - mu), axis=-1, keepdims=True)
    y = (y - mu) * lax.rsqrt(var + NORM_EPS) * lg_ref[...] + lb_ref[...]
    o_ref[...] = _silu(y).astype(o_ref.dtype)


N_PENDING = 4


def _apply_pending(x_ref, pending, xo_ref):
    y0_ref, y1_ref, w_ref, gate_ref = pending
    y = y0_ref[...].astype(F32) * w_ref[:, 0:1] + y1_ref[...].astype(F32) * w_ref[:, 1:2]
    xo_ref[...] = x_ref[...] + gate_ref[...] * y
    return xo_ref


def _pending_specs(pending, tm, d):
    tile = lambda i, j: (i, j, 0)
    return [pl.BlockSpec((None, tm, d), tile), pl.BlockSpec((None, tm, d), tile),
            pl.BlockSpec((None, tm, TOP_K), tile), pl.BlockSpec((None, 1, d), lambda i, j: (i, 0, 0))]


def _even_mixer_kernel(*refs, has_pending):
    x_ref, refs = refs[0], refs[1:]
    if has_pending:
        pending, refs = refs[:N_PENDING], refs[N_PENDING:]
    sh_ref, sc_ref, g1_ref, w_ref, wg_ref, bg_ref, ng_ref, cw_ref, cb_ref, lg_ref, lb_ref = refs[:11]
    refs = refs[11:]
    if has_pending:
        xo_ref, refs = refs[0], refs[1:]
    o_gla_ref, y_conv_ref, q_s, k_s, v_s, g_s, u_s, a_s, state_ref, la_ref, oacc_ref, buf_ref = refs

    @pl.when(pl.program_id(1) == 0)
    def _():
        state_ref[...] = jnp.zeros_like(state_ref)
        buf_ref[0:CONV_HALO, :] = jnp.zeros((CONV_HALO, buf_ref.shape[1]), F32)

    if has_pending:
        x_ref = _apply_pending(x_ref, pending, xo_ref)
    _norm_matmul_kernel(x_ref, sh_ref, sc_ref, g1_ref, w_ref, q_s, k_s, v_s, g_s, u_s, a_s)
    _gla_kernel(q_s, k_s, v_s, g_s, a_s, wg_ref, bg_ref, ng_ref, o_gla_ref, state_ref, la_ref, oacc_ref)
    _conv_kernel(u_s, cw_ref, cb_ref, lg_ref, lb_ref, y_conv_ref, buf_ref)


def _even_mixer(x, pending, shift, scale, g1, w_cat, w_gate2, b_gate, norm_g, conv_w, conv_b, ln_g, ln_b, tm):
    b, s, d = x.shape
    hk = GLA_HEADS * GLA_DK
    hv = GLA_HEADS * GLA_DV
    ch = conv_w.shape[-1]
    row = lambda i, j: (i, 0, 0)
    tile = lambda i, j: (i, j, 0)
    const = lambda i, j: (0, 0)
    full = lambda arr: pl.BlockSpec(arr.shape, const)
    pending = list(pending or ())
    x_out = [jax.ShapeDtypeStruct((b, s, d), F32)] if pending else []
    outs = pl.pallas_call(
        functools.partial(_even_mixer_kernel, has_pending=bool(pending)),
        out_shape=x_out + [jax.ShapeDtypeStruct((b, s, hv), BF16), jax.ShapeDtypeStruct((b, s, ch), BF16)],
        grid=(b, s // tm),
        in_specs=[pl.BlockSpec((None, tm, d), tile)] + (_pending_specs(pending, tm, d) if pending else []) + [
            pl.BlockSpec((None, 1, d), row),
            pl.BlockSpec((None, 1, d), row),
            full(g1), full(w_cat), full(w_gate2), full(b_gate), full(norm_g),
            full(conv_w), full(conv_b), full(ln_g), full(ln_b),
        ],
        out_specs=[pl.BlockSpec((None, tm, d), tile)] * len(x_out) + [
            pl.BlockSpec((None, tm, hv), tile), pl.BlockSpec((None, tm, ch), tile)],
        scratch_shapes=[
            pltpu.VMEM((tm, hk), F32),
            pltpu.VMEM((tm, hk), F32),
            pltpu.VMEM((tm, hv), BF16),
            pltpu.VMEM((tm, hv), F32),
            pltpu.VMEM((tm, 2 * ch), F32),
            pltpu.VMEM((tm, w_gate2.shape[0]), F32),
            pltpu.VMEM((GLA_HEADS * GLA_DK, GLA_DV), F32),
            pltpu.VMEM((tm, hk), F32),
            pltpu.VMEM((tm, hv), F32),
            pltpu.VMEM((tm + CONV_HALO, ch), F32),
        ],
        compiler_params=_params("arbitrary", "arbitrary"),
        name="even_mixer",
    )(x, *pending, shift, scale, g1, w_cat, w_gate2, b_gate, norm_g, conv_w, conv_b, ln_g, ln_b)
    return outs if pending else [x] + list(outs)


def _rope_table_kernel(pos_ref, freq_ref, sign_ref, cos_ref, sin_ref):
    ang = pos_ref[...] * freq_ref[...]
    cos_ref[...] = jnp.cos(ang)
    sin_ref[...] = jnp.sin(ang) * sign_ref[...]


def _rope_tables(positions, head_dim, tm):
    b, s = positions.shape
    rope_dims = head_dim // 4
    half = rope_dims // 2
    inv_freq = ROPE_THETA ** (-jnp.arange(0, rope_dims, 2, dtype=F32) / rope_dims)
    jj = jnp.arange(LANES) % head_dim
    freq = jnp.where(jj < rope_dims, inv_freq[jj % half], 0.0).astype(F32)[None, :]
    sign = jnp.where(jj < half, -1.0, jnp.where(jj < rope_dims, 1.0, 0.0)).astype(F32)[None, :]
    pos = positions.astype(F32)[..., None]
    tile = lambda i, j: (i, j, 0)
    const = lambda i, j: (0, 0)
    return pl.pallas_call(
        _rope_table_kernel,
        out_shape=[jax.ShapeDtypeStruct((b, s, LANES), F32)] * 2,
        grid=(b, s // tm),
        in_specs=[pl.BlockSpec((None, tm, 1), tile), pl.BlockSpec((1, LANES), const),
                  pl.BlockSpec((1, LANES), const)],
        out_specs=[pl.BlockSpec((None, tm, LANES), tile)] * 2,
        compiler_params=_params("arbitrary", "arbitrary"),
        name="rope_tables",
    )(pos, freq, sign)


def _qkv_kernel(*refs, head_dim, has_pending):
    x_ref, refs = refs[0], refs[1:]
    if has_pending:
        pending, refs = refs[:N_PENDING], refs[N_PENDING:]
    sh_ref, sc_ref, g_ref, w_ref, cos_ref, sin_ref = refs[:6]
    refs = refs[6:]
    if has_pending:
        x_ref, refs = _apply_pending(x_ref, pending, refs[0]), refs[1:]
    q_ref, k_ref, v_ref = refs
    h = _modulate(x_ref[...], g_ref[...], sh_ref[...], sc_ref[...]).astype(BF16)
    d = q_ref.shape[-1]
    half = head_dim // 8
    cosf = jnp.tile(cos_ref[...], (1, d // LANES))
    sinf = jnp.tile(sin_ref[...], (1, d // LANES))
    lane = lax.broadcasted_iota(jnp.int32, (1, d), 1)
    first = (lane % head_dim) < half
    for idx, (o_ref, mult) in enumerate(((q_ref, head_dim ** -0.5), (k_ref, 1.0))):
        t = jnp.dot(h, w_ref[:, idx * d:(idx + 1) * d], preferred_element_type=F32)
        partner = jnp.where(first, pltpu.roll(t, d - half, 1), pltpu.roll(t, half, 1))
        o_ref[...] = ((t * cosf + partner * sinf) * mult).astype(o_ref.dtype)
    v_ref[...] = jnp.dot(h, w_ref[:, 2 * d:], preferred_element_type=F32).astype(v_ref.dtype)


def _qkv_rope(x, pending, shift, scale, g, w, cos_t, sin_t, head_dim, tm):
    b, s, d = x.shape
    row = lambda i, j: (i, 0, 0)
    tile = lambda i, j: (i, j, 0)
    pending = list(pending or ())
    x_out = [jax.ShapeDtypeStruct((b, s, d), F32)] if pending else []
    outs = pl.pallas_call(
        functools.partial(_qkv_kernel, head_dim=head_dim, has_pending=bool(pending)),
        out_shape=x_out + [jax.ShapeDtypeStruct((b, s, d), BF16)] * 3,
        grid=(b, s // tm),
        in_specs=[pl.BlockSpec((None, tm, d), tile)] + (_pending_specs(pending, tm, d) if pending else []) + [
            pl.BlockSpec((None, 1, d), row),
            pl.BlockSpec((None, 1, d), row),
            pl.BlockSpec((1, d), lambda i, j: (0, 0)),
            pl.BlockSpec(w.shape, lambda i, j: (0, 0)),
            pl.BlockSpec((None, tm, LANES), tile),
            pl.BlockSpec((None, tm, LANES), tile),
        ],
        out_specs=[pl.BlockSpec((None, tm, d), tile)] * (len(x_out) + 3),
        compiler_params=_params("arbitrary", "arbitrary"),
        name="qkv_rope",
    )(x, *pending, shift, scale, g, w, cos_t, sin_t)
    return outs if pending else [x] + list(outs)


ONES_ROWS = 16
ATTN_BLOCKS = ATTN_SUPER // DIL_BLOCK


def _attn_bias(branches, heads):
    blk = DIL_BLOCK
    kj = np.arange(2 * blk)[:, None]
    qi = np.arange(blk)[None, :]
    dist = qi + blk - kj
    out = []
    for window, dil in branches:
        band = (dist >= 0) & (dist <= window // dil)
        both = np.stack([band, band & (kj >= blk)])
        out.append(np.tile(np.where(both, 0.0, NEG_BIG), (1, 1, heads)))
    return jnp.asarray(np.stack(out), F32)


def _attn_kernel(bias_ref, q_ref, kc_ref, kp_ref, vc_ref, vp_ref, o_ref, qf, kf, vf, ob, lb, st_s, vt_s,
                 *, head_dim, branches):
    sb = q_ref.shape[0]
    blk = DIL_BLOCK
    heads = q_ref.shape[1] // head_dim
    first_super = pl.program_id(2) == 0
    slot0 = jnp.maximum(pl.program_id(2) - pl.num_programs(2), 0)

    qf[...] = q_ref[...].astype(F32)
    kf[0:sb, :] = kp_ref[...].astype(F32)
    kf[sb:, :] = kc_ref[...].astype(F32)
    vf[0:sb, :] = vp_ref[...].astype(F32)
    vf[sb:, :] = vc_ref[...].astype(F32)

    vt_s[:, LANES:, :] = jnp.ones((ATTN_BLOCKS, ONES_ROWS, 2 * blk), BF16)

    lane = lax.broadcasted_iota(jnp.int32, (blk, LANES), 1)
    head_masks = [(lane >= h * head_dim) & (lane < (h + 1) * head_dim) for h in range(heads)]

    for bi, (window, dil) in enumerate(branches):
        assert window // dil <= blk and sb == ATTN_SUPER and sb % (dil * blk) == 0
        unit = dil * blk

        def scores(j, q0, first_unit, dil=dil, unit=unit, bi=bi):
            k0 = sb + q0 - unit
            no_prev = jnp.where(first_super, 1, 0) if first_unit else 0
            qb = qf[pl.ds(q0, blk, stride=dil), :]
            q2 = jnp.concatenate([jnp.where(mk, qb, 0.0) for mk in head_masks], axis=0).astype(BF16)
            kb = kf[pl.ds(k0, 2 * blk, stride=dil), :].astype(BF16)
            vt_s[j, 0:LANES, :] = vf[pl.ds(k0, 2 * blk, stride=dil), :].T.astype(BF16)
            st = lax.dot_general(kb, q2, (((1,), (1,)), ((), ())), preferred_element_type=F32)
            st_s[slot0 + j] = st + bias_ref[bi, no_prev]

        def softmax_pv(j):
            m = jnp.max(st_s[slot0 + j], axis=0, keepdims=True)
            p = jnp.exp(st_s[slot0 + j] - m).astype(BF16)
            of = jnp.dot(vt_s[j], p, preferred_element_type=F32)
            l = of[LANES:LANES + 1, :]
            lse = m + jnp.log(l)
            o_rows, lse_rows = [], []
            for h in range(heads):
                cols = slice(h * blk, (h + 1) * blk)
                o_rows.append(of[h * head_dim:(h + 1) * head_dim, cols] / l[:, cols])
                lse_rows.append(jnp.broadcast_to(lse[:, cols], (head_dim, blk)))
            return jnp.concatenate(o_rows, axis=0).T, jnp.concatenate(lse_rows, axis=0).T

        starts = [(idx // dil) * unit + idx % dil for idx in range(ATTN_BLOCKS)]
        for j, q0 in enumerate(starts):
            scores(j, q0, q0 < unit)
        for j, q0 in enumerate(starts):
            o_tok, lse_tok = softmax_pv(j)
            ob[bi, pl.ds(q0, blk, stride=dil), :] = o_tok
            lb[bi, pl.ds(q0, blk, stride=dil), :] = lse_tok

    nb = len(branches)
    m = lb[0]
    for bi in range(1, nb):
        m = jnp.maximum(m, lb[bi])
    num = jnp.zeros_like(m)
    den = jnp.zeros_like(m)
    for bi in range(nb):
        e = jnp.exp(lb[bi] - m)
        num = num + e * ob[bi]
        den = den + e
    o_ref[...] = (num / den).astype(o_ref.dtype)


def _dilated_attention(q, k, v, head_dim, branches, sb):
    b, s, d = q.shape
    groups = d // LANES
    cur = lambda i, g, n: (i, n, g)
    prev = lambda i, g, n: (i, jnp.maximum(n - 1, 0), g)
    nb = len(branches)
    bias = _attn_bias(branches, LANES // head_dim)
    return pl.pallas_call(
        functools.partial(_attn_kernel, head_dim=head_dim, branches=branches),
        out_shape=jax.ShapeDtypeStruct((b, s, d), BF16),
        grid=(b, groups, s // sb),
        in_specs=[
            pl.BlockSpec(bias.shape, lambda i, g, n: (0, 0, 0, 0)),
            pl.BlockSpec((None, sb, LANES), cur),
            pl.BlockSpec((None, sb, LANES), cur),
            pl.BlockSpec((None, sb, LANES), prev),
            pl.BlockSpec((None, sb, LANES), cur),
            pl.BlockSpec((None, sb, LANES), prev),
        ],
        out_specs=pl.BlockSpec((None, sb, LANES), cur),
        scratch_shapes=[
            pltpu.VMEM((sb, LANES), F32),
            pltpu.VMEM((2 * sb, LANES), F32),
            pltpu.VMEM((2 * sb, LANES), F32),
            pltpu.VMEM((nb, sb, LANES), F32),
            pltpu.VMEM((nb, sb, LANES), F32),
            pltpu.VMEM((ATTN_BLOCKS, 2 * DIL_BLOCK, LANES // head_dim * DIL_BLOCK), F32),
            pltpu.VMEM((ATTN_BLOCKS, LANES + ONES_ROWS, 2 * DIL_BLOCK), BF16),
        ],
        compiler_params=_params("arbitrary", "arbitrary", "arbitrary"),
        name="dilated_attention",
    )(bias, q, k, k, v, v)


def _route_tile(x, sh_ref, sc_ref, g_ref, w_ref, b_ref, h_ref, wts_ref, route_ref, counts_ref, carry_ref,
                n_experts, n_groups):
    h = _modulate(x, g_ref[...], sh_ref[...], sc_ref[...]).astype(BF16)
    h_ref[...] = h
    logits = jnp.dot(h, w_ref[...], preferred_element_type=F32) + b_ref[...]
    tm = logits.shape[0]
    epg = n_experts // n_groups
    lane = lax.broadcasted_iota(jnp.int32, logits.shape, 1)
    neg = -jnp.inf
    big = jnp.int32(LANES)

    def first_max(mask):
        val = jnp.max(jnp.where(mask, logits, neg), axis=-1, keepdims=True)
        idx = jnp.min(jnp.where(mask & (logits == val), lane, big), axis=-1, keepdims=True)
        return val, idx

    gmask = (lane >= n_experts) & (lane < n_experts + n_groups)
    gmax, gidx = first_max(gmask)
    gsum = jnp.sum(jnp.where(gmask, jnp.exp(logits - gmax), 0.0), axis=-1, keepdims=True)
    g_w = 1.0 / gsum
    grp = gidx - n_experts
    assert epg & (epg - 1) == 0
    emask = (lane < n_experts) & (lax.shift_right_logical(lane, epg.bit_length() - 1) == grp)
    v1, i1 = first_max(emask)
    v2, i2 = first_max(emask & (lane != i1))
    e2 = jnp.exp(v2 - v1)
    den = 1.0 + e2
    col = lax.broadcasted_iota(jnp.int32, (tm, TOP_K), 1)
    wts_ref[...] = jnp.where(col == 0, 1.0 / den, e2 / den) * g_w

    hit1 = lane == i1
    hit2 = lane == i2
    onehot = jnp.where(hit1 | hit2, 1.0, 0.0)
    ri = lax.broadcasted_iota(jnp.int32, (tm, tm), 0)
    ci = lax.broadcasted_iota(jnp.int32, (tm, tm), 1)
    before = jnp.where(ci < ri, 1.0, 0.0).astype(BF16)
    prefix = jnp.dot(before, onehot.astype(BF16), preferred_element_type=F32) + carry_ref[0:1, :]
    r1 = jnp.sum(jnp.where(hit1, prefix, 0.0), axis=-1, keepdims=True)
    r2 = jnp.sum(jnp.where(hit2, prefix, 0.0), axis=-1, keepdims=True)
    packed = jnp.where(lane == 0, i1.astype(F32), jnp.where(lane == 1, i2.astype(F32),
                       jnp.where(lane == 2, r1, jnp.where(lane == 3, r2, 0.0))))
    route_ref[...] = packed.T[0:SUBLANES, :]
    carry_ref[...] = carry_ref[...] + jnp.sum(onehot, axis=0, keepdims=True)
    counts_ref[...] = carry_ref[...]


def _out_proj_router_kernel(*refs, n_acts, n_experts, n_groups):
    a_refs = refs[:n_acts]
    w_ref, x_ref, gate_ref, sh_ref, sc_ref, g_ref, wr_ref, br_ref = refs[n_acts:n_acts + 8]
    xo_ref, h_ref, wts_ref, route_ref, counts_ref, carry_ref = refs[n_acts + 8:]

    @pl.when((pl.program_id(0) == 0) & (pl.program_id(1) == 0))
    def _():
        carry_ref[...] = jnp.zeros_like(carry_ref)

    acc = None
    off = 0
    for a_ref in a_refs:
        kk = a_ref.shape[-1]
        part = jnp.dot(a_ref[...].astype(BF16), w_ref[off:off + kk, :], preferred_element_type=F32)
        acc = part if acc is None else acc + part
        off += kk
    x = x_ref[...] + gate_ref[...] * acc
    xo_ref[...] = x
    _route_tile(x, sh_ref, sc_ref, g_ref, wr_ref, br_ref, h_ref, wts_ref, route_ref, counts_ref, carry_ref,
                n_experts, n_groups)


def _out_proj_router(acts, w, x, gate, shift, scale, g, w_rt, b_rt, n_experts, n_groups, tm):
    b, s, d = x.shape
    row = lambda i, j: (i, 0, 0)
    tile = lambda i, j: (i, j, 0)
    const = lambda i, j: (0, 0)
    return pl.pallas_call(
        functools.partial(_out_proj_router_kernel, n_acts=len(acts), n_experts=n_experts, n_groups=n_groups),
        out_shape=[jax.ShapeDtypeStruct((b, s, d), F32),
                   jax.ShapeDtypeStruct((b, s, d), BF16),
                   jax.ShapeDtypeStruct((b, s, TOP_K), F32),
                   jax.ShapeDtypeStruct((SUBLANES, b * s), F32),
                   jax.ShapeDtypeStruct((SUBLANES, LANES), F32)],
        grid=(b, s // tm),
        in_specs=[pl.BlockSpec((None, tm, a.shape[-1]), tile) for a in acts] + [
            pl.BlockSpec(w.shape, const),
            pl.BlockSpec((None, tm, d), tile),
            pl.BlockSpec((None, 1, d), row),
            pl.BlockSpec((None, 1, d), row),
            pl.BlockSpec((None, 1, d), row),
            pl.BlockSpec((1, d), const),
            pl.BlockSpec((d, LANES), const),
            pl.BlockSpec((1, LANES), const),
        ],
        out_specs=[pl.BlockSpec((None, tm, d), tile),
                   pl.BlockSpec((None, tm, d), tile),
                   pl.BlockSpec((None, tm, TOP_K), tile),
                   pl.BlockSpec((SUBLANES, tm), lambda i, j: (0, i * (s // tm) + j)),
                   pl.BlockSpec((SUBLANES, LANES), const)],
        scratch_shapes=[pltpu.VMEM((SUBLANES, LANES), F32)],
        compiler_params=_params("arbitrary", "arbitrary"),
        name="out_proj_router",
    )(*acts, w, x, gate, shift, scale, g, w_rt, b_rt)


def _expert_kernel(ib_ref, ie_ref, lo_ref, hi_ref, xs_ref, w1_ref, w3_ref, w2_ref, ys_ref, w1b, w3b, w2b):
    j = pl.program_id(0)
    prev = jnp.maximum(j - 1, 0)
    e_changed = (j == 0) | (ie_ref[j] != ie_ref[prev])
    first_of_block = (j == 0) | (ib_ref[j] != ib_ref[prev])
    lo = lo_ref[j]
    hi = hi_ref[j]

    @pl.when(e_changed)
    def _():
        w1b[...] = w1_ref[...].astype(BF16)
        w3b[...] = w3_ref[...].astype(BF16)
        w2b[...] = w2_ref[...].astype(BF16)

    rows = ys_ref.shape[0]
    whole = (lo == 0) & (hi == rows)

    @pl.when(first_of_block & jnp.logical_not(whole))
    def _():
        ys_ref[...] = jnp.zeros_like(ys_ref)

    @pl.when(hi > lo)
    def _():
        x = xs_ref[...]
        a = jnp.dot(x, w1b[...], preferred_element_type=F32)
        g = jnp.dot(x, w3b[...], preferred_element_type=F32)
        y = jnp.dot((_silu(a) * g).astype(BF16), w2b[...], preferred_element_type=F32).astype(ys_ref.dtype)

        @pl.when(whole)
        def _():
            ys_ref[...] = y

        @pl.when(jnp.logical_not(whole))
        def _():
            row = lax.broadcasted_iota(jnp.int32, (rows, 1), 0)
            ys_ref[...] = jnp.where((row >= lo) & (row < hi), y, ys_ref[...])


def _experts(layer, items, xs, w1, w3, w2, rows):
    a, d = xs.shape
    hid = w1.shape[-1]
    blk = lambda j, ib, ie, lo, hi: (ib[j], 0)
    wsel = lambda j, ib, ie, lo, hi: (layer, ie[j], 0, 0)
    grid_spec = pltpu.PrefetchScalarGridSpec(
        num_scalar_prefetch=4,
        grid=(items[0].shape[0],),
        in_specs=[
            pl.BlockSpec((rows, d), blk),
            pl.BlockSpec((None, None, d, hid), wsel),
            pl.BlockSpec((None, None, d, hid), wsel),
            pl.BlockSpec((None, None, hid, d), wsel),
        ],
        out_specs=pl.BlockSpec((rows, d), blk),
        scratch_shapes=[pltpu.VMEM((d, hid), BF16), pltpu.VMEM((d, hid), BF16), pltpu.VMEM((hid, d), BF16)],
    )
    return pl.pallas_call(
        _expert_kernel,
        out_shape=jax.ShapeDtypeStruct((a, d), BF16),
        grid_spec=grid_spec,
        compiler_params=_params("arbitrary"),
        name="moe_experts",
    )(*items, xs, w1, w3, w2)


def _dispatch(ids, rank, counts, rows):
    n_experts = counts.shape[0]
    a = ids.size
    i32 = jnp.int32
    ends = jnp.cumsum(counts)
    starts = ends - counts
    dest = rank
    for e in range(n_experts):
        dest = dest + jnp.where(ids == e, starts[e], 0)
    t = ids.shape[1]
    tok = jnp.tile(jnp.arange(t, dtype=i32), TOP_K)
    row_tok = lax.sort_key_val(dest.reshape(-1), tok)[1]
    n_blk = a // rows
    bstart = jnp.arange(n_blk, dtype=i32) * rows
    count_le = lambda bounds, x: jnp.sum((bounds[None, :] <= x[:, None]).astype(i32), axis=1)
    e_lo = jnp.minimum(count_le(ends, bstart), n_experts - 1)
    e_hi = jnp.minimum(count_le(ends, bstart + rows - 1), n_experts - 1)
    n_items = e_hi - e_lo + 1
    item_end = jnp.cumsum(n_items)
    item_first = item_end - n_items
    jj = jnp.arange(n_blk + n_experts - 1, dtype=i32)
    valid = jj < item_end[-1]
    ib = jnp.minimum(count_le(item_end, jj), n_blk - 1)
    ie = jnp.where(valid, jnp.clip(e_lo[ib] + jj - item_first[ib], 0, n_experts - 1), e_hi[n_blk - 1]).astype(i32)
    lo = jnp.where(valid, jnp.clip(starts[ie] - ib * rows, 0, rows), 0).astype(i32)
    hi = jnp.where(valid, jnp.clip(ends[ie] - ib * rows, 0, rows), 0).astype(i32)
    return row_tok, dest, (ib, ie, lo, hi)


def _final_combine_kernel(x_ref, y0_ref, y1_ref, w_ref, gate_ref, ng_ref, o_ref):
    x = _apply_pending(x_ref, (y0_ref, y1_ref, w_ref, gate_ref), o_ref)[...]
    o_ref[...] = x * lax.rsqrt(jnp.mean(x * x, axis=-1, keepdims=True) + NORM_EPS) * ng_ref[...]


def _final_combine(x, pending, final_g, tm):
    b, s, d = x.shape
    tile = lambda i, j: (i, j, 0)
    return pl.pallas_call(
        _final_combine_kernel,
        out_shape=jax.ShapeDtypeStruct((b, s, d), F32),
        grid=(b, s // tm),
        in_specs=[pl.BlockSpec((None, tm, d), tile)] + _pending_specs(pending, tm, d) + [
            pl.BlockSpec((1, d), lambda i, j: (0, 0))],
        out_specs=pl.BlockSpec((None, tm, d), tile),
        compiler_params=_params("arbitrary", "arbitrary"),
        name="moe_final_combine",
    )(x, *pending, final_g)


def _hier_moe(layer, x, routed, gate, w1, w3, w2, rows):
    b, s, d = x.shape
    t = b * s
    n_experts = w1.shape[1]
    h, wts, route, counts = routed
    counts = counts[0, :n_experts].astype(jnp.int32)
    route = route.astype(jnp.int32)
    row_tok, dest, items = _dispatch(route[0:TOP_K], route[TOP_K:2 * TOP_K], counts, rows)
    xs = h.reshape(t, d)[row_tok]
    ys = _experts(layer, items, xs, w1, w3, w2, rows)
    y0 = ys[dest[0]].reshape(b, s, d)
    y1 = ys[dest[1]].reshape(b, s, d)
    return y0, y1, wts, gate


def _pick_tile(s, pref):
    tm = min(pref, s)
    assert s % tm == 0
    return tm


def kernel(x, c, positions, ada_w, ada_b, norm1_g, norm2_g, even_w_in, even_w_gate2, even_b_gate, even_gla_norm_g, even_conv_w, even_conv_b, even_conv_ln_g, even_conv_ln_b, even_w_out, odd_w_qkv, odd_w_out, moe_w_grp, moe_b_grp, moe_w_rt, moe_b_rt, moe_w1, moe_w3, moe_w2, final_norm_g):
    b, s, d = x.shape
    depth = ada_w.shape[0]
    n_experts = moe_w_rt.shape[-1]
    tm = _pick_tile(s, TOKEN_TILE)
    sb = _pick_tile(s, ATTN_SUPER)
    head_dim = d // ATTN_HEADS
    hk = GLA_HEADS * GLA_DK
    hv = GLA_HEADS * GLA_DV
    conv_ch = d // 2

    mods = _ada_mods(c, ada_w, ada_b)
    mod = lambda l, j: mods[l, j][:, None, :]
    cos_t = sin_t = None
    pending = None

    for layer in range(depth):
        i = layer // 2
        g1 = norm1_g[layer][None, :]
        if layer % 2 == 0:
            w_in = even_w_in[i]
            main = hk + hk + hv + hv
            w_cat = jnp.concatenate([
                w_in[:, :main], w_in[:, main + GLA_GATE_RANK:], w_in[:, main:main + GLA_GATE_RANK],
                jnp.zeros((d, LANES - GLA_GATE_RANK), w_in.dtype)], axis=1).astype(BF16)
            wg = jnp.concatenate([even_w_gate2[i], jnp.zeros((LANES - GLA_GATE_RANK, hk), F32)], axis=0).astype(BF16)
            x, o_gla, y_conv = _even_mixer(
                x, pending, mod(layer, 0), mod(layer, 1), g1, w_cat, wg, even_b_gate[i][None, :],
                even_gla_norm_g[i][None, :], even_conv_w[i], even_conv_b[i][None, :],
                even_conv_ln_g[i][None, :], even_conv_ln_b[i][None, :], tm)
            acts, w_out = [o_gla, y_conv], even_w_out[i]
        else:
            if cos_t is None:
                cos_t, sin_t = _rope_tables(positions, head_dim, tm)
            x, q, k, v = _qkv_rope(x, pending, mod(layer, 0), mod(layer, 1), g1, odd_w_qkv[i].astype(BF16),
                                   cos_t, sin_t, head_dim, tm)
            o = _dilated_attention(q, k, v, head_dim, DILATED_BRANCHES, sb)
            acts, w_out = [o], odd_w_out[i]

        w_rt_full = jnp.concatenate([moe_w_rt[layer], moe_w_grp[layer],
                                     jnp.zeros((d, LANES - n_experts - N_GROUPS), F32)], axis=1).astype(BF16)
        b_rt_full = jnp.concatenate([moe_b_rt[layer], moe_b_grp[layer],
                                     jnp.zeros((LANES - n_experts - N_GROUPS,), F32)])[None, :]
        x, *routed = _out_proj_router(acts, w_out.astype(BF16), x, mod(layer, 2), mod(layer, 3), mod(layer, 4),
                                      norm2_g[layer][None, :], w_rt_full, b_rt_full, n_experts, N_GROUPS, tm)
        pending = _hier_moe(layer, x, routed, mod(layer, 5), moe_w1, moe_w3, moe_w2, MOE_ROWS)
    return _final_combine(x, pending, final_norm_g[None, :], tm)
```

```python
import functools

import jax
import jax.numpy as jnp
import numpy as np
from jax import lax
from jax.experimental import pallas as pl
from jax.experimental.pallas import tpu as pltpu

F32 = jnp.float32
BF16 = jnp.bfloat16
HIGHEST = lax.Precision.HIGHEST

NORM_EPS = 1e-6
GLA_HEADS = 4
GLA_DK = 64
GLA_DV = 128
GLA_GATE_RANK = 16
GLA_TAU = 16.0
GLA_CHUNK = 64
CONV_WIDTH = 31
ATTN_HEADS = 16
DILATED_BRANCHES = ((128, 1), (512, 4), (2048, 16))
DIL_BLOCK = 128
ROPE_THETA = 500000.0
N_GROUPS = 4
EXPERTS_PER_GROUP = 8
TOP_K = 2
ADA_CHUNKS = 6

LANES = 128
SUBLANES = 8
VMEM_LIMIT = 56 * 1024 * 1024
TOKEN_TILE = 512
ROUTER_TILE = 1024
ATTN_SUPER = 2048
MOE_ROWS = 512
NEG_BIG = -1e30


def _params(*sem):
    return pltpu.CompilerParams(dimension_semantics=sem, vmem_limit_bytes=VMEM_LIMIT)


def _silu(x):
    return x * jax.nn.sigmoid(x)


def _modulate(x, g, shift, scale):
    y = x * lax.rsqrt(jnp.mean(x * x, axis=-1, keepdims=True) + NORM_EPS)
    return (y * g) * (1.0 + scale) + shift


def _ada_kernel(c_ref, w_ref, b_ref, o_ref):
    cond = _silu(c_ref[...])
    o_ref[...] = jnp.dot(cond, w_ref[...], preferred_element_type=F32, precision=HIGHEST) + b_ref[...]


def _ada_mods(c, ada_w, ada_b):
    depth, d, _ = ada_w.shape
    b = c.shape[0]
    return pl.pallas_call(
        _ada_kernel,
        out_shape=jax.ShapeDtypeStruct((depth, ADA_CHUNKS, b, d), F32),
        grid=(depth, ADA_CHUNKS),
        in_specs=[
            pl.BlockSpec((b, d), lambda l, j: (0, 0)),
            pl.BlockSpec((None, d, d), lambda l, j: (l, 0, j)),
            pl.BlockSpec((None, None, 1, d), lambda l, j: (l, j, 0, 0)),
        ],
        out_specs=pl.BlockSpec((None, None, b, d), lambda l, j: (l, j, 0, 0)),
        compiler_params=_params("arbitrary", "arbitrary"),
        name="ada_mods",
    )(c, ada_w, ada_b.reshape(depth, ADA_CHUNKS, 1, d))


def _norm_matmul_kernel(x_ref, sh_ref, sc_ref, g_ref, w_ref, *o_refs):
    h = _modulate(x_ref[...], g_ref[...], sh_ref[...], sc_ref[...]).astype(BF16)
    off = 0
    for o_ref in o_refs:
        n = o_ref.shape[-1]
        o_ref[...] = jnp.dot(h, w_ref[:, off:off + n], preferred_element_type=F32).astype(o_ref.dtype)
        off += n


def _log_sigmoid(z):
    return jnp.minimum(z, 0.0) - jnp.log1p(jnp.exp(-jnp.abs(z)))


def _gla_kernel(q_ref, k_ref, v_ref, g_ref, a_ref, wg_ref, bg_ref, ng_ref, o_ref, state_ref, la_ref, oacc_ref):
    tm = q_ref.shape[0]
    c = GLA_CHUNK

    z = jnp.dot(a_ref[...].astype(BF16), wg_ref[...], preferred_element_type=F32) + bg_ref[...]
    la_ref[...] = _log_sigmoid(z) * (1.0 / GLA_TAU)

    ri = lax.broadcasted_iota(jnp.int32, (c, c), 0)
    ci = lax.broadcasted_iota(jnp.int32, (c, c), 1)
    causal = ri >= ci
    tril = jnp.where(causal, 1.0, 0.0).astype(BF16)
    hk = GLA_HEADS * GLA_DK

    nh = GLA_HEADS
    lane_head = lax.shift_right_logical(lax.broadcasted_iota(jnp.int32, (c, hk), 1), GLA_DK.bit_length() - 1)
    r4 = lax.broadcasted_iota(jnp.int32, (nh * c, nh * c), 0)
    c4 = lax.broadcasted_iota(jnp.int32, (nh * c, nh * c), 1)
    shift_c = c.bit_length() - 1
    causal4 = (lax.shift_right_logical(r4, shift_c) == lax.shift_right_logical(c4, shift_c)) & (r4 >= c4)

    def stack_heads(t):
        return jnp.concatenate([jnp.where(lane_head == h, t, 0.0) for h in range(nh)], axis=0).astype(BF16)

    state = state_ref[...]
    for ic in range(tm // c):
        rows = slice(ic * c, (ic + 1) * c)
        la = la_ref[rows, :]
        p0 = la.astype(BF16)
        r1 = la - p0.astype(F32)
        p1 = r1.astype(BF16)
        p2 = (r1 - p1.astype(F32)).astype(BF16)
        parts = jnp.dot(tril, jnp.concatenate([p0, p1, p2], axis=1), preferred_element_type=F32)
        bcum = (parts[:, 2 * hk:] + parts[:, hk:2 * hk]) + parts[:, :hk]
        b_last = bcum[c - 1:c, :]
        q = q_ref[rows, :] * (GLA_DK ** -0.5)
        k = k_ref[rows, :]
        q4 = stack_heads(q * jnp.exp(bcum))
        k4 = stack_heads(k * jnp.exp(-bcum))
        kr4 = stack_heads(k * jnp.exp(b_last - bcum))
        dec = jnp.exp(jnp.broadcast_to(b_last, (GLA_DV, hk)).T)
        v4 = jnp.concatenate([v_ref[rows, h * GLA_DV:(h + 1) * GLA_DV] for h in range(nh)], axis=0)
        att = lax.dot_general(q4, k4, (((1,), (1,)), ((), ())), preferred_element_type=F32)
        att = jnp.where(causal4, att, 0.0).astype(BF16)
        o4 = jnp.dot(att, v4, preferred_element_type=F32)
        o4 = o4 + jnp.dot(q4, state.astype(BF16), preferred_element_type=F32)
        kv = lax.dot_general(kr4, v4, (((0,), (0,)), ((), ())), preferred_element_type=F32)
        state = dec * state + kv
        oacc_ref[rows, :] = jnp.concatenate([o4[h * c:(h + 1) * c, :] for h in range(nh)], axis=1)
    state_ref[...] = state

    for h in range(GLA_HEADS):
        vs = slice(h * GLA_DV, (h + 1) * GLA_DV)
        o = oacc_ref[:, vs]
        o = o * lax.rsqrt(jnp.mean(o * o, axis=-1, keepdims=True) + NORM_EPS) * ng_ref[...]
        o_ref[:, vs] = (o * _silu(g_ref[:, vs])).astype(o_ref.dtype)


CONV_HALO = 32


def _conv_kernel(u_ref, w_ref, cb_ref, lg_ref, lb_ref, o_ref, buf_ref):
    tm = u_ref.shape[0]
    ch = o_ref.shape[-1]

    buf_ref[CONV_HALO:, :] = u_ref[:, :ch] * jax.nn.sigmoid(u_ref[:, ch:])
    base = CONV_HALO - (CONV_WIDTH - 1)
    acc = None
    for b in range(SUBLANES):
        part = None
        span = tm + (SUBLANES if b else 0)
        for a in range((base + CONV_WIDTH - 1) // SUBLANES + 1):
            j = SUBLANES * a + b - base
            if 0 <= j < CONV_WIDTH:
                term = buf_ref[SUBLANES * a:SUBLANES * a + span, :] * w_ref[j:j + 1, :]
                part = term if part is None else part + term
        if part is not None:
            part = part[b:b + tm, :]
            acc = part if acc is None else acc + part
    buf_ref[0:CONV_HALO, :] = buf_ref[tm:tm + CONV_HALO, :]
    y = acc + cb_ref[...]
    mu = jnp.mean(y, axis=-1, keepdims=True)
    var = jnp.mean(jnp.square(y - mu), axis=-1, keepdims=True)
    y = (y - mu) * lax.rsqrt(var + NORM_EPS) * lg_ref[...] + lb_ref[...]
    o_ref[...] = _silu(y).astype(o_ref.dtype)


N_PENDING = 4


def _apply_pending(x_ref, pending, xo_ref):
    y0_ref, y1_ref, w_ref, gate_ref = pending
    y = y0_ref[...].astype(F32) * w_ref[:, 0:1] + y1_ref[...].astype(F32) * w_ref[:, 1:2]
    xo_ref[...] = x_ref[...] + gate_ref[...] * y
    return xo_ref


def _pending_specs(pending, tm, d):
    tile = lambda i, j: (i, j, 0)
    return [pl.BlockSpec((None, None, tm, d), lambda i, j: (0, i, j, 0)),
            pl.BlockSpec((None, None, tm, d), lambda i, j: (1, i, j, 0)),
            pl.BlockSpec((None, tm, TOP_K), tile), pl.BlockSpec((None, 1, d), lambda i, j: (i, 0, 0))]


def _even_mixer_kernel(*refs, has_pending):
    x_ref, refs = refs[0], refs[1:]
    if has_pending:
        pending, refs = refs[:N_PENDING], refs[N_PENDING:]
    sh_ref, sc_ref, g1_ref, w_ref, wg_ref, bg_ref, ng_ref, cw_ref, cb_ref, lg_ref, lb_ref = refs[:11]
    refs = refs[11:]
    if has_pending:
        xo_ref, refs = refs[0], refs[1:]
    o_gla_ref, y_conv_ref, q_s, k_s, v_s, g_s, u_s, a_s, state_ref, la_ref, oacc_ref, buf_ref = refs

    @pl.when(pl.program_id(1) == 0)
    def _():
        state_ref[...] = jnp.zeros_like(state_ref)
        buf_ref[0:CONV_HALO, :] = jnp.zeros((CONV_HALO, buf_ref.shape[1]), F32)

    if has_pending:
        x_ref = _apply_pending(x_ref, pending, xo_ref)
    _norm_matmul_kernel(x_ref, sh_ref, sc_ref, g1_ref, w_ref, q_s, k_s, v_s, g_s, u_s, a_s)
    _gla_kernel(q_s, k_s, v_s, g_s, a_s, wg_ref, bg_ref, ng_ref, o_gla_ref, state_ref, la_ref, oacc_ref)
    _conv_kernel(u_s, cw_ref, cb_ref, lg_ref, lb_ref, y_conv_ref, buf_ref)


def _even_mixer(x, pending, shift, scale, g1, w_cat, w_gate2, b_gate, norm_g, conv_w, conv_b, ln_g, ln_b, tm):
    b, s, d = x.shape
    hk = GLA_HEADS * GLA_DK
    hv = GLA_HEADS * GLA_DV
    ch = conv_w.shape[-1]
    row = lambda i, j: (i, 0, 0)
    tile = lambda i, j: (i, j, 0)
    const = lambda i, j: (0, 0)
    full = lambda arr: pl.BlockSpec(arr.shape, const)
    pending = list(pending or ())
    x_out = [jax.ShapeDtypeStruct((b, s, d), F32)] if pending else []
    outs = pl.pallas_call(
        functools.partial(_even_mixer_kernel, has_pending=bool(pending)),
        out_shape=x_out + [jax.ShapeDtypeStruct((b, s, hv), BF16), jax.ShapeDtypeStruct((b, s, ch), BF16)],
        grid=(b, s // tm),
        in_specs=[pl.BlockSpec((None, tm, d), tile)] + (_pending_specs(pending, tm, d) if pending else []) + [
            pl.BlockSpec((None, 1, d), row),
            pl.BlockSpec((None, 1, d), row),
            full(g1), full(w_cat), full(w_gate2), full(b_gate), full(norm_g),
            full(conv_w), full(conv_b), full(ln_g), full(ln_b),
        ],
        out_specs=[pl.BlockSpec((None, tm, d), tile)] * len(x_out) + [
            pl.BlockSpec((None, tm, hv), tile), pl.BlockSpec((None, tm, ch), tile)],
        scratch_shapes=[
            pltpu.VMEM((tm, hk), F32),
            pltpu.VMEM((tm, hk), F32),
            pltpu.VMEM((tm, hv), BF16),
            pltpu.VMEM((tm, hv), F32),
            pltpu.VMEM((tm, 2 * ch), F32),
            pltpu.VMEM((tm, w_gate2.shape[0]), F32),
            pltpu.VMEM((GLA_HEADS * GLA_DK, GLA_DV), F32),
            pltpu.VMEM((tm, hk), F32),
            pltpu.VMEM((tm, hv), F32),
            pltpu.VMEM((tm + CONV_HALO, ch), F32),
        ],
        compiler_params=_params("arbitrary", "arbitrary"),
        name="even_mixer",
    )(x, *pending, shift, scale, g1, w_cat, w_gate2, b_gate, norm_g, conv_w, conv_b, ln_g, ln_b)
    return outs if pending else [x] + list(outs)


def _rope_table_kernel(pos_ref, freq_ref, sign_ref, cos_ref, sin_ref):
    ang = pos_ref[...] * freq_ref[...]
    cos_ref[...] = jnp.cos(ang)
    sin_ref[...] = jnp.sin(ang) * sign_ref[...]


def _rope_tables(positions, head_dim, tm):
    b, s = positions.shape
    rope_dims = head_dim // 4
    half = rope_dims // 2
    inv_freq = ROPE_THETA ** (-jnp.arange(0, rope_dims, 2, dtype=F32) / rope_dims)
    jj = jnp.arange(LANES) % head_dim
    freq = jnp.where(jj < rope_dims, inv_freq[jj % half], 0.0).astype(F32)[None, :]
    sign = jnp.where(jj < half, -1.0, jnp.where(jj < rope_dims, 1.0, 0.0)).astype(F32)[None, :]
    pos = positions.astype(F32)[..., None]
    tile = lambda i, j: (i, j, 0)
    const = lambda i, j: (0, 0)
    return pl.pallas_call(
        _rope_table_kernel,
        out_shape=[jax.ShapeDtypeStruct((b, s, LANES), F32)] * 2,
        grid=(b, s // tm),
        in_specs=[pl.BlockSpec((None, tm, 1), tile), pl.BlockSpec((1, LANES), const),
                  pl.BlockSpec((1, LANES), const)],
        out_specs=[pl.BlockSpec((None, tm, LANES), tile)] * 2,
        compiler_params=_params("arbitrary", "arbitrary"),
        name="rope_tables",
    )(pos, freq, sign)


def _qkv_kernel(*refs, head_dim, has_pending):
    x_ref, refs = refs[0], refs[1:]
    if has_pending:
        pending, refs = refs[:N_PENDING], refs[N_PENDING:]
    sh_ref, sc_ref, g_ref, w_ref, cos_ref, sin_ref = refs[:6]
    refs = refs[6:]
    if has_pending:
        x_ref, refs = _apply_pending(x_ref, pending, refs[0]), refs[1:]
    q_ref, k_ref, v_ref = refs
    h = _modulate(x_ref[...], g_ref[...], sh_ref[...], sc_ref[...]).astype(BF16)
    d = q_ref.shape[-1]
    half = head_dim // 8
    cosf = jnp.tile(cos_ref[...], (1, d // LANES))
    sinf = jnp.tile(sin_ref[...], (1, d // LANES))
    lane = lax.broadcasted_iota(jnp.int32, (1, d), 1)
    first = (lane % head_dim) < half
    for idx, (o_ref, mult) in enumerate(((q_ref, head_dim ** -0.5), (k_ref, 1.0))):
        t = jnp.dot(h, w_ref[:, idx * d:(idx + 1) * d], preferred_element_type=F32)
        partner = jnp.where(first, pltpu.roll(t, d - half, 1), pltpu.roll(t, half, 1))
        o_ref[...] = ((t * cosf + partner * sinf) * mult).astype(o_ref.dtype)
    v_ref[...] = jnp.dot(h, w_ref[:, 2 * d:], preferred_element_type=F32).astype(v_ref.dtype)


def _qkv_rope(x, pending, shift, scale, g, w, cos_t, sin_t, head_dim, tm):
    b, s, d = x.shape
    row = lambda i, j: (i, 0, 0)
    tile = lambda i, j: (i, j, 0)
    pending = list(pending or ())
    x_out = [jax.ShapeDtypeStruct((b, s, d), F32)] if pending else []
    outs = pl.pallas_call(
        functools.partial(_qkv_kernel, head_dim=head_dim, has_pending=bool(pending)),
        out_shape=x_out + [jax.ShapeDtypeStruct((b, s, d), BF16)] * 3,
        grid=(b, s // tm),
        in_specs=[pl.BlockSpec((None, tm, d), tile)] + (_pending_specs(pending, tm, d) if pending else []) + [
            pl.BlockSpec((None, 1, d), row),
            pl.BlockSpec((None, 1, d), row),
            pl.BlockSpec((1, d), lambda i, j: (0, 0)),
            pl.BlockSpec(w.shape, lambda i, j: (0, 0)),
            pl.BlockSpec((None, tm, LANES), tile),
            pl.BlockSpec((None, tm, LANES), tile),
        ],
        out_specs=[pl.BlockSpec((None, tm, d), tile)] * (len(x_out) + 3),
        compiler_params=_params("arbitrary", "arbitrary"),
        name="qkv_rope",
    )(x, *pending, shift, scale, g, w, cos_t, sin_t)
    return outs if pending else [x] + list(outs)


ONES_ROWS = 16
ATTN_BLOCKS = ATTN_SUPER // DIL_BLOCK


def _attn_bias(branches, heads):
    blk = DIL_BLOCK
    kj = np.arange(2 * blk)[:, None]
    qi = np.arange(blk)[None, :]
    dist = qi + blk - kj
    out = []
    for window, dil in branches:
        band = (dist >= 0) & (dist <= window // dil)
        both = np.stack([band, band & (kj >= blk)])
        out.append(np.tile(np.where(both, 0.0, NEG_BIG), (1, 1, heads)))
    return jnp.asarray(np.stack(out), F32)


def _attn_kernel(bias_ref, q_ref, kc_ref, kp_ref, vc_ref, vp_ref, o_ref, qf, kf, vf, ob, lb, st_s, vt_s,
                 *, head_dim, branches):
    sb = q_ref.shape[0]
    blk = DIL_BLOCK
    heads = q_ref.shape[1] // head_dim
    first_super = pl.program_id(2) == 0
    slot0 = jnp.maximum(pl.program_id(2) - pl.num_programs(2), 0)

    qf[...] = q_ref[...].astype(F32)
    kf[0:sb, :] = kp_ref[...].astype(F32)
    kf[sb:, :] = kc_ref[...].astype(F32)
    vf[0:sb, :] = vp_ref[...].astype(F32)
    vf[sb:, :] = vc_ref[...].astype(F32)

    vt_s[:, LANES:, :] = jnp.ones((ATTN_BLOCKS, ONES_ROWS, 2 * blk), BF16)

    lane = lax.broadcasted_iota(jnp.int32, (blk, LANES), 1)
    head_masks = [(lane >= h * head_dim) & (lane < (h + 1) * head_dim) for h in range(heads)]

    for bi, (window, dil) in enumerate(branches):
        assert window // dil <= blk and sb == ATTN_SUPER and sb % (dil * blk) == 0
        unit = dil * blk

        def scores(j, q0, first_unit, dil=dil, unit=unit, bi=bi):
            k0 = sb + q0 - unit
            no_prev = jnp.where(first_super, 1, 0) if first_unit else 0
            qb = qf[pl.ds(q0, blk, stride=dil), :]
            q2 = jnp.concatenate([jnp.where(mk, qb, 0.0) for mk in head_masks], axis=0).astype(BF16)
            kb = kf[pl.ds(k0, 2 * blk, stride=dil), :].astype(BF16)
            vt_s[j, 0:LANES, :] = vf[pl.ds(k0, 2 * blk, stride=dil), :].T.astype(BF16)
            st = lax.dot_general(kb, q2, (((1,), (1,)), ((), ())), preferred_element_type=F32)
            st_s[slot0 + j] = st + bias_ref[bi, no_prev]

        def softmax_pv(j):
            m = jnp.max(st_s[slot0 + j], axis=0, keepdims=True)
            p = jnp.exp(st_s[slot0 + j] - m).astype(BF16)
            of = jnp.dot(vt_s[j], p, preferred_element_type=F32)
            l = of[LANES:LANES + 1, :]
            lse = m + jnp.log(l)
            o_rows, lse_rows = [], []
            for h in range(heads):
                cols = slice(h * blk, (h + 1) * blk)
                o_rows.append(of[h * head_dim:(h + 1) * head_dim, cols] / l[:, cols])
                lse_rows.append(jnp.broadcast_to(lse[:, cols], (head_dim, blk)))
            return jnp.concatenate(o_rows, axis=0).T, jnp.concatenate(lse_rows, axis=0).T

        starts = [(idx // dil) * unit + idx % dil for idx in range(ATTN_BLOCKS)]
        for j, q0 in enumerate(starts):
            scores(j, q0, q0 < unit)
        for j, q0 in enumerate(starts):
            o_tok, lse_tok = softmax_pv(j)
            ob[bi, pl.ds(q0, blk, stride=dil), :] = o_tok
            lb[bi, pl.ds(q0, blk, stride=dil), :] = lse_tok

    nb = len(branches)
    m = lb[0]
    for bi in range(1, nb):
        m = jnp.maximum(m, lb[bi])
    num = jnp.zeros_like(m)
    den = jnp.zeros_like(m)
    for bi in range(nb):
        e = jnp.exp(lb[bi] - m)
        num = num + e * ob[bi]
        den = den + e
    o_ref[...] = (num / den).astype(o_ref.dtype)


def _dilated_attention(q, k, v, head_dim, branches, sb):
    b, s, d = q.shape
    groups = d // LANES
    cur = lambda i, g, n: (i, n, g)
    prev = lambda i, g, n: (i, jnp.maximum(n - 1, 0), g)
    nb = len(branches)
    bias = _attn_bias(branches, LANES // head_dim)
    return pl.pallas_call(
        functools.partial(_attn_kernel, head_dim=head_dim, branches=branches),
        out_shape=jax.ShapeDtypeStruct((b, s, d), BF16),
        grid=(b, groups, s // sb),
        in_specs=[
            pl.BlockSpec(bias.shape, lambda i, g, n: (0, 0, 0, 0)),
            pl.BlockSpec((None, sb, LANES), cur),
            pl.BlockSpec((None, sb, LANES), cur),
            pl.BlockSpec((None, sb, LANES), prev),
            pl.BlockSpec((None, sb, LANES), cur),
            pl.BlockSpec((None, sb, LANES), prev),
        ],
        out_specs=pl.BlockSpec((None, sb, LANES), cur),
        scratch_shapes=[
            pltpu.VMEM((sb, LANES), F32),
            pltpu.VMEM((2 * sb, LANES), F32),
            pltpu.VMEM((2 * sb, LANES), F32),
            pltpu.VMEM((nb, sb, LANES), F32),
            pltpu.VMEM((nb, sb, LANES), F32),
            pltpu.VMEM((ATTN_BLOCKS, 2 * DIL_BLOCK, LANES // head_dim * DIL_BLOCK), F32),
            pltpu.VMEM((ATTN_BLOCKS, LANES + ONES_ROWS, 2 * DIL_BLOCK), BF16),
        ],
        compiler_params=_params("arbitrary", "arbitrary", "arbitrary"),
        name="dilated_attention",
    )(bias, q, k, k, v, v)


def _route_tile(x, sh_ref, sc_ref, g_ref, w_ref, b_ref, h_ref, wts_ref, route_ref, counts_ref, carry_ref,
                n_experts, n_groups):
    h = _modulate(x, g_ref[...], sh_ref[...], sc_ref[...]).astype(BF16)
    h_ref[...] = h
    logits = jnp.dot(h, w_ref[...], preferred_element_type=F32) + b_ref[...]
    tm = logits.shape[0]
    epg = n_experts // n_groups
    lane = lax.broadcasted_iota(jnp.int32, logits.shape, 1)
    neg = -jnp.inf
    big = jnp.int32(LANES)

    def first_max(mask):
        val = jnp.max(jnp.where(mask, logits, neg), axis=-1, keepdims=True)
        idx = jnp.min(jnp.where(mask & (logits == val), lane, big), axis=-1, keepdims=True)
        return val, idx

    gmask = (lane >= n_experts) & (lane < n_experts + n_groups)
    gmax, gidx = first_max(gmask)
    gsum = jnp.sum(jnp.where(gmask, jnp.exp(logits - gmax), 0.0), axis=-1, keepdims=True)
    g_w = 1.0 / gsum
    grp = gidx - n_experts
    assert epg & (epg - 1) == 0
    emask = (lane < n_experts) & (lax.shift_right_logical(lane, epg.bit_length() - 1) == grp)
    v1, i1 = first_max(emask)
    v2, i2 = first_max(emask & (lane != i1))
    e2 = jnp.exp(v2 - v1)
    den = 1.0 + e2
    col = lax.broadcasted_iota(jnp.int32, (tm, TOP_K), 1)
    wts_ref[...] = jnp.where(col == 0, 1.0 / den, e2 / den) * g_w

    hit1 = lane == i1
    hit2 = lane == i2
    onehot = jnp.where(hit1 | hit2, 1.0, 0.0)
    ri = lax.broadcasted_iota(jnp.int32, (tm, tm), 0)
    ci = lax.broadcasted_iota(jnp.int32, (tm, tm), 1)
    before = jnp.where(ci < ri, 1.0, 0.0).astype(BF16)
    prefix = jnp.dot(before, onehot.astype(BF16), preferred_element_type=F32) + carry_ref[0:1, :]
    r1 = jnp.sum(jnp.where(hit1, prefix, 0.0), axis=-1, keepdims=True)
    r2 = jnp.sum(jnp.where(hit2, prefix, 0.0), axis=-1, keepdims=True)
    packed = jnp.where(lane == 0, i1.astype(F32), jnp.where(lane == 1, i2.astype(F32),
                       jnp.where(lane == 2, r1, jnp.where(lane == 3, r2, 0.0))))
    route_ref[...] = packed.T[0:SUBLANES, :]
    carry_ref[...] = carry_ref[...] + jnp.sum(onehot, axis=0, keepdims=True)
    counts_ref[...] = carry_ref[...]


def _out_proj_router_kernel(*refs, n_acts, n_experts, n_groups):
    a_refs = refs[:n_acts]
    w_ref, x_ref, gate_ref, sh_ref, sc_ref, g_ref, wr_ref, br_ref = refs[n_acts:n_acts + 8]
    xo_ref, h_ref, wts_ref, route_ref, counts_ref, carry_ref = refs[n_acts + 8:]

    @pl.when((pl.program_id(0) == 0) & (pl.program_id(1) == 0))
    def _():
        carry_ref[...] = jnp.zeros_like(carry_ref)

    acc = None
    off = 0
    for a_ref in a_refs:
        kk = a_ref.shape[-1]
        part = jnp.dot(a_ref[...].astype(BF16), w_ref[off:off + kk, :], preferred_element_type=F32)
        acc = part if acc is None else acc + part
        off += kk
    x = x_ref[...] + gate_ref[...] * acc
    xo_ref[...] = x
    _route_tile(x, sh_ref, sc_ref, g_ref, wr_ref, br_ref, h_ref, wts_ref, route_ref, counts_ref, carry_ref,
                n_experts, n_groups)


def _out_proj_router(acts, w, x, gate, shift, scale, g, w_rt, b_rt, n_experts, n_groups, tm):
    b, s, d = x.shape
    row = lambda i, j: (i, 0, 0)
    tile = lambda i, j: (i, j, 0)
    const = lambda i, j: (0, 0)
    return pl.pallas_call(
        functools.partial(_out_proj_router_kernel, n_acts=len(acts), n_experts=n_experts, n_groups=n_groups),
        out_shape=[jax.ShapeDtypeStruct((b, s, d), F32),
                   jax.ShapeDtypeStruct((b, s, d), BF16),
                   jax.ShapeDtypeStruct((b, s, TOP_K), F32),
                   jax.ShapeDtypeStruct((SUBLANES, b * s), F32),
                   jax.ShapeDtypeStruct((SUBLANES, LANES), F32)],
        grid=(b, s // tm),
        in_specs=[pl.BlockSpec((None, tm, a.shape[-1]), tile) for a in acts] + [
            pl.BlockSpec(w.shape, const),
            pl.BlockSpec((None, tm, d), tile),
            pl.BlockSpec((None, 1, d), row),
            pl.BlockSpec((None, 1, d), row),
            pl.BlockSpec((None, 1, d), row),
            pl.BlockSpec((1, d), const),
            pl.BlockSpec((d, LANES), const),
            pl.BlockSpec((1, LANES), const),
        ],
        out_specs=[pl.BlockSpec((None, tm, d), tile),
                   pl.BlockSpec((None, tm, d), tile),
                   pl.BlockSpec((None, tm, TOP_K), tile),
                   pl.BlockSpec((SUBLANES, tm), lambda i, j: (0, i * (s // tm) + j)),
                   pl.BlockSpec((SUBLANES, LANES), const)],
        scratch_shapes=[pltpu.VMEM((SUBLANES, LANES), F32)],
        compiler_params=_params("arbitrary", "arbitrary"),
        name="out_proj_router",
    )(*acts, w, x, gate, shift, scale, g, w_rt, b_rt)


def _expert_kernel(ib_ref, ie_ref, lo_ref, hi_ref, xs_ref, w1_ref, w3_ref, w2_ref, ys_ref, w1b, w3b, w2b):
    j = pl.program_id(0)
    prev = jnp.maximum(j - 1, 0)
    e_changed = (j == 0) | (ie_ref[j] != ie_ref[prev])
    first_of_block = (j == 0) | (ib_ref[j] != ib_ref[prev])
    lo = lo_ref[j]
    hi = hi_ref[j]

    @pl.when(e_changed)
    def _():
        w1b[...] = w1_ref[...].astype(BF16)
        w3b[...] = w3_ref[...].astype(BF16)
        w2b[...] = w2_ref[...].astype(BF16)

    rows = ys_ref.shape[0]
    whole = (lo == 0) & (hi == rows)

    @pl.when(first_of_block & jnp.logical_not(whole))
    def _():
        ys_ref[...] = jnp.zeros_like(ys_ref)

    @pl.when(hi > lo)
    def _():
        x = xs_ref[...]
        a = jnp.dot(x, w1b[...], preferred_element_type=F32)
        g = jnp.dot(x, w3b[...], preferred_element_type=F32)
        y = jnp.dot((_silu(a) * g).astype(BF16), w2b[...], preferred_element_type=F32).astype(ys_ref.dtype)

        @pl.when(whole)
        def _():
            ys_ref[...] = y

        @pl.when(jnp.logical_not(whole))
        def _():
            row = lax.broadcasted_iota(jnp.int32, (rows, 1), 0)
            ys_ref[...] = jnp.where((row >= lo) & (row < hi), y, ys_ref[...])


def _experts(layer, items, xs, w1, w3, w2, rows):
    a, d = xs.shape
    hid = w1.shape[-1]
    blk = lambda j, ib, ie, lo, hi: (ib[j], 0)
    wsel = lambda j, ib, ie, lo, hi: (layer, ie[j], 0, 0)
    grid_spec = pltpu.PrefetchScalarGridSpec(
        num_scalar_prefetch=4,
        grid=(items[0].shape[0],),
        in_specs=[
            pl.BlockSpec((rows, d), blk),
            pl.BlockSpec((None, None, d, hid), wsel),
            pl.BlockSpec((None, None, d, hid), wsel),
            pl.BlockSpec((None, None, hid, d), wsel),
        ],
        out_specs=pl.BlockSpec((rows, d), blk),
        scratch_shapes=[pltpu.VMEM((d, hid), BF16), pltpu.VMEM((d, hid), BF16), pltpu.VMEM((hid, d), BF16)],
    )
    return pl.pallas_call(
        _expert_kernel,
        out_shape=jax.ShapeDtypeStruct((a, d), BF16),
        grid_spec=grid_spec,
        compiler_params=_params("arbitrary"),
        name="moe_experts",
    )(*items, xs, w1, w3, w2)


def _dispatch(ids, rank, counts, rows):
    n_experts = counts.shape[0]
    a = ids.size
    i32 = jnp.int32
    ends = jnp.cumsum(counts)
    starts = ends - counts
    dest = rank
    for e in range(n_experts):
        dest = dest + jnp.where(ids == e, starts[e], 0)
    t = ids.shape[1]
    tok = jnp.tile(jnp.arange(t, dtype=i32), TOP_K)
    row_tok = lax.sort_key_val(dest.reshape(-1), tok)[1]
    n_blk = a // rows
    bstart = jnp.arange(n_blk, dtype=i32) * rows
    count_le = lambda bounds, x: jnp.sum((bounds[None, :] <= x[:, None]).astype(i32), axis=1)
    e_lo = jnp.minimum(count_le(ends, bstart), n_experts - 1)
    e_hi = jnp.minimum(count_le(ends, bstart + rows - 1), n_experts - 1)
    n_items = e_hi - e_lo + 1
    item_end = jnp.cumsum(n_items)
    item_first = item_end - n_items
    jj = jnp.arange(n_blk + n_experts - 1, dtype=i32)
    valid = jj < item_end[-1]
    ib = jnp.minimum(count_le(item_end, jj), n_blk - 1)
    ie = jnp.where(valid, jnp.clip(e_lo[ib] + jj - item_first[ib], 0, n_experts - 1), e_hi[n_blk - 1]).astype(i32)
    lo = jnp.where(valid, jnp.clip(starts[ie] - ib * rows, 0, rows), 0).astype(i32)
    hi = jnp.where(valid, jnp.clip(ends[ie] - ib * rows, 0, rows), 0).astype(i32)
    return row_tok, dest, (ib, ie, lo, hi)


def _final_combine_kernel(x_ref, y0_ref, y1_ref, w_ref, gate_ref, ng_ref, o_ref):
    x = _apply_pending(x_ref, (y0_ref, y1_ref, w_ref, gate_ref), o_ref)[...]
    o_ref[...] = x * lax.rsqrt(jnp.mean(x * x, axis=-1, keepdims=True) + NORM_EPS) * ng_ref[...]


def _final_combine(x, pending, final_g, tm):
    b, s, d = x.shape
    tile = lambda i, j: (i, j, 0)
    return pl.pallas_call(
        _final_combine_kernel,
        out_shape=jax.ShapeDtypeStruct((b, s, d), F32),
        grid=(b, s // tm),
        in_specs=[pl.BlockSpec((None, tm, d), tile)] + _pending_specs(pending, tm, d) + [
            pl.BlockSpec((1, d), lambda i, j: (0, 0))],
        out_specs=pl.BlockSpec((None, tm, d), tile),
        compiler_params=_params("arbitrary", "arbitrary"),
        name="moe_final_combine",
    )(x, *pending, final_g)


def _hier_moe(layer, x, routed, gate, w1, w3, w2, rows):
    b, s, d = x.shape
    t = b * s
    n_experts = w1.shape[1]
    h, wts, route, counts = routed
    counts = counts[0, :n_experts].astype(jnp.int32)
    route = route.astype(jnp.int32)
    row_tok, dest, items = _dispatch(route[0:TOP_K], route[TOP_K:2 * TOP_K], counts, rows)
    xs = h.reshape(t, d)[row_tok]
    ys = _experts(layer, items, xs, w1, w3, w2, rows)
    yy = ys[dest.reshape(-1)].reshape(TOP_K, b, s, d)
    return yy, yy, wts, gate


def _pick_tile(s, pref):
    tm = min(pref, s)
    assert s % tm == 0
    return tm


def kernel(x, c, positions, ada_w, ada_b, norm1_g, norm2_g, even_w_in, even_w_gate2, even_b_gate, even_gla_norm_g, even_conv_w, even_conv_b, even_conv_ln_g, even_conv_ln_b, even_w_out, odd_w_qkv, odd_w_out, moe_w_grp, moe_b_grp, moe_w_rt, moe_b_rt, moe_w1, moe_w3, moe_w2, final_norm_g):
    b, s, d = x.shape
    depth = ada_w.shape[0]
    n_experts = moe_w_rt.shape[-1]
    tm = _pick_tile(s, TOKEN_TILE)
    sb = _pick_tile(s, ATTN_SUPER)
    head_dim = d // ATTN_HEADS
    hk = GLA_HEADS * GLA_DK
    hv = GLA_HEADS * GLA_DV
    conv_ch = d // 2

    mods = _ada_mods(c, ada_w, ada_b)
    mod = lambda l, j: mods[l, j][:, None, :]
    cos_t = sin_t = None
    pending = None

    for layer in range(depth):
        i = layer // 2
        g1 = norm1_g[layer][None, :]
        if layer % 2 == 0:
            w_in = even_w_in[i]
            main = hk + hk + hv + hv
            w_cat = jnp.concatenate([
                w_in[:, :main], w_in[:, main + GLA_GATE_RANK:], w_in[:, main:main + GLA_GATE_RANK],
                jnp.zeros((d, LANES - GLA_GATE_RANK), w_in.dtype)], axis=1).astype(BF16)
            wg = jnp.concatenate([even_w_gate2[i], jnp.zeros((LANES - GLA_GATE_RANK, hk), F32)], axis=0).astype(BF16)
            x, o_gla, y_conv = _even_mixer(
                x, pending, mod(layer, 0), mod(layer, 1), g1, w_cat, wg, even_b_gate[i][None, :],
                even_gla_norm_g[i][None, :], even_conv_w[i], even_conv_b[i][None, :],
                even_conv_ln_g[i][None, :], even_conv_ln_b[i][None, :], tm)
            acts, w_out = [o_gla, y_conv], even_w_out[i]
        else:
            if cos_t is None:
                cos_t, sin_t = _rope_tables(positions, head_dim, tm)
            x, q, k, v = _qkv_rope(x, pending, mod(layer, 0), mod(layer, 1), g1, odd_w_qkv[i].astype(BF16),
                                   cos_t, sin_t, head_dim, tm)
            o = _dilated_attention(q, k, v, head_dim, DILATED_BRANCHES, sb)
            acts, w_out = [o], odd_w_out[i]

        w_rt_full = jnp.concatenate([moe_w_rt[layer], moe_w_grp[layer],
                                     jnp.zeros((d, LANES - n_experts - N_GROUPS), F32)], axis=1).astype(BF16)
        b_rt_full = jnp.concatenate([moe_b_rt[layer], moe_b_grp[layer],
                                     jnp.zeros((LANES - n_experts - N_GROUPS,), F32)])[None, :]
        x, *routed = _out_proj_router(acts, w_out.astype(BF16), x, mod(layer, 2), mod(layer, 3), mod(layer, 4),
                                      norm2_g[layer][None, :], w_rt_full, b_rt_full, n_experts, N_GROUPS,
                                      _pick_tile(s, ROUTER_TILE))
        pending = _hier_moe(layer, x, routed, mod(layer, 5), moe_w1, moe_w3, moe_w2, MOE_ROWS)
    return _final_combine(x, pending, final_norm_g[None, :], tm)
```

```python
import functools

import jax
import jax.numpy as jnp
import numpy as np
from jax import lax
from jax.experimental import pallas as pl
from jax.experimental.pallas import tpu as pltpu

F32 = jnp.float32
BF16 = jnp.bfloat16
HIGHEST = lax.Precision.HIGHEST

NORM_EPS = 1e-6
GLA_HEADS = 4
GLA_DK = 64
GLA_DV = 128
GLA_GATE_RANK = 16
GLA_TAU = 16.0
GLA_CHUNK = 64
CONV_WIDTH = 31
ATTN_HEADS = 16
DILATED_BRANCHES = ((128, 1), (512, 4), (2048, 16))
DIL_BLOCK = 128
ROPE_THETA = 500000.0
N_GROUPS = 4
EXPERTS_PER_GROUP = 8
TOP_K = 2
ADA_CHUNKS = 6

LANES = 128
SUBLANES = 8
VMEM_LIMIT = 56 * 1024 * 1024
TOKEN_TILE = 512
ROUTER_TILE = 1024
BATCH_CHAINS = 2
ATTN_SUPER = 2048
MOE_ROWS = 512
NEG_BIG = -1e30


def _params(*sem):
    return pltpu.CompilerParams(dimension_semantics=sem, vmem_limit_bytes=VMEM_LIMIT)


def _silu(x):
    return x * jax.nn.sigmoid(x)


def _modulate(x, g, shift, scale):
    y = x * lax.rsqrt(jnp.mean(x * x, axis=-1, keepdims=True) + NORM_EPS)
    return (y * g) * (1.0 + scale) + shift


def _ada_kernel(c_ref, w_ref, b_ref, o_ref):
    cond = _silu(c_ref[...])
    o_ref[...] = jnp.dot(cond, w_ref[...], preferred_element_type=F32, precision=HIGHEST) + b_ref[...]


def _ada_mods(c, ada_w, ada_b):
    depth, d, _ = ada_w.shape
    b = c.shape[0]
    return pl.pallas_call(
        _ada_kernel,
        out_shape=jax.ShapeDtypeStruct((depth, ADA_CHUNKS, b, d), F32),
        grid=(depth, ADA_CHUNKS),
        in_specs=[
            pl.BlockSpec((b, d), lambda l, j: (0, 0)),
            pl.BlockSpec((None, d, d), lambda l, j: (l, 0, j)),
            pl.BlockSpec((None, None, 1, d), lambda l, j: (l, j, 0, 0)),
        ],
        out_specs=pl.BlockSpec((None, None, b, d), lambda l, j: (l, j, 0, 0)),
        compiler_params=_params("arbitrary", "arbitrary"),
        name="ada_mods",
    )(c, ada_w, ada_b.reshape(depth, ADA_CHUNKS, 1, d))


def _norm_matmul_kernel(x_ref, sh_ref, sc_ref, g_ref, w_ref, *o_refs):
    h = _modulate(x_ref[...], g_ref[...], sh_ref[...], sc_ref[...]).astype(BF16)
    off = 0
    for o_ref in o_refs:
        n = o_ref.shape[-1]
        o_ref[...] = jnp.dot(h, w_ref[:, off:off + n], preferred_element_type=F32).astype(o_ref.dtype)
        off += n


def _log_sigmoid(z):
    return jnp.minimum(z, 0.0) - jnp.log1p(jnp.exp(-jnp.abs(z)))


def _gla_kernel(q_ref, k_ref, v_ref, g_ref, a_ref, wg_ref, bg_ref, ng_ref, o_ref, state_ref, la_ref, oacc_ref):
    tm = q_ref.shape[0]
    c = GLA_CHUNK

    z = jnp.dot(a_ref[...].astype(BF16), wg_ref[...], preferred_element_type=F32) + bg_ref[...]
    la_ref[...] = _log_sigmoid(z) * (1.0 / GLA_TAU)

    ri = lax.broadcasted_iota(jnp.int32, (c, c), 0)
    ci = lax.broadcasted_iota(jnp.int32, (c, c), 1)
    causal = ri >= ci
    tril = jnp.where(causal, 1.0, 0.0).astype(BF16)
    hk = GLA_HEADS * GLA_DK

    nh = GLA_HEADS
    lane_head = lax.shift_right_logical(lax.broadcasted_iota(jnp.int32, (c, hk), 1), GLA_DK.bit_length() - 1)
    r4 = lax.broadcasted_iota(jnp.int32, (nh * c, nh * c), 0)
    c4 = lax.broadcasted_iota(jnp.int32, (nh * c, nh * c), 1)
    shift_c = c.bit_length() - 1
    causal4 = (lax.shift_right_logical(r4, shift_c) == lax.shift_right_logical(c4, shift_c)) & (r4 >= c4)

    def stack_heads(t):
        return jnp.concatenate([jnp.where(lane_head == h, t, 0.0) for h in range(nh)], axis=0).astype(BF16)

    state = state_ref[...]
    for ic in range(tm // c):
        rows = slice(ic * c, (ic + 1) * c)
        la = la_ref[rows, :]
        p0 = la.astype(BF16)
        r1 = la - p0.astype(F32)
        p1 = r1.astype(BF16)
        p2 = (r1 - p1.astype(F32)).astype(BF16)
        parts = jnp.dot(tril, jnp.concatenate([p0, p1, p2], axis=1), preferred_element_type=F32)
        bcum = (parts[:, 2 * hk:] + parts[:, hk:2 * hk]) + parts[:, :hk]
        b_last = bcum[c - 1:c, :]
        q = q_ref[rows, :] * (GLA_DK ** -0.5)
        k = k_ref[rows, :]
        q4 = stack_heads(q * jnp.exp(bcum))
        k4 = stack_heads(k * jnp.exp(-bcum))
        kr4 = stack_heads(k * jnp.exp(b_last - bcum))
        dec = jnp.exp(jnp.broadcast_to(b_last, (GLA_DV, hk)).T)
        v4 = jnp.concatenate([v_ref[rows, h * GLA_DV:(h + 1) * GLA_DV] for h in range(nh)], axis=0)
        att = lax.dot_general(q4, k4, (((1,), (1,)), ((), ())), preferred_element_type=F32)
        att = jnp.where(causal4, att, 0.0).astype(BF16)
        o4 = jnp.dot(att, v4, preferred_element_type=F32)
        o4 = o4 + jnp.dot(q4, state.astype(BF16), preferred_element_type=F32)
        kv = lax.dot_general(kr4, v4, (((0,), (0,)), ((), ())), preferred_element_type=F32)
        state = dec * state + kv
        oacc_ref[rows, :] = jnp.concatenate([o4[h * c:(h + 1) * c, :] for h in range(nh)], axis=1)
    state_ref[...] = state

    for h in range(GLA_HEADS):
        vs = slice(h * GLA_DV, (h + 1) * GLA_DV)
        o = oacc_ref[:, vs]
        o = o * lax.rsqrt(jnp.mean(o * o, axis=-1, keepdims=True) + NORM_EPS) * ng_ref[...]
        o_ref[:, vs] = (o * _silu(g_ref[:, vs])).astype(o_ref.dtype)


CONV_HALO = 32


def _conv_kernel(u_ref, w_ref, cb_ref, lg_ref, lb_ref, o_ref, buf_ref):
    tm = u_ref.shape[0]
    ch = o_ref.shape[-1]

    buf_ref[CONV_HALO:, :] = u_ref[:, :ch] * jax.nn.sigmoid(u_ref[:, ch:])
    base = CONV_HALO - (CONV_WIDTH - 1)
    acc = None
    for b in range(SUBLANES):
        part = None
        span = tm + (SUBLANES if b else 0)
        for a in range((base + CONV_WIDTH - 1) // SUBLANES + 1):
            j = SUBLANES * a + b - base
            if 0 <= j < CONV_WIDTH:
                term = buf_ref[SUBLANES * a:SUBLANES * a + span, :] * w_ref[j:j + 1, :]
                part = term if part is None else part + term
        if part is not None:
            part = part[b:b + tm, :]
            acc = part if acc is None else acc + part
    buf_ref[0:CONV_HALO, :] = buf_ref[tm:tm + CONV_HALO, :]
    y = acc + cb_ref[...]
    mu = jnp.mean(y, axis=-1, keepdims=True)
    var = jnp.mean(jnp.square(y - mu), axis=-1, keepdims=True)
    y = (y - mu) * lax.rsqrt(var + NORM_EPS) * lg_ref[...] + lb_ref[...]
    o_ref[...] = _silu(y).astype(o_ref.dtype)


N_PENDING = 4


def _apply_pending(x_ref, pending, xo_ref):
    y0_ref, y1_ref, w_ref, gate_ref = pending
    y = y0_ref[...].astype(F32) * w_ref[:, 0:1] + y1_ref[...].astype(F32) * w_ref[:, 1:2]
    xo_ref[...] = x_ref[...] + gate_ref[...] * y
    return xo_ref


def _pending_specs(pending, tm, d):
    tile = lambda i, j: (i, j, 0)
    return [pl.BlockSpec((None, None, tm, d), lambda i, j: (0, i, j, 0)),
            pl.BlockSpec((None, None, tm, d), lambda i, j: (1, i, j, 0)),
            pl.BlockSpec((None, tm, TOP_K), tile), pl.BlockSpec((None, 1, d), lambda i, j: (i, 0, 0))]


def _even_mixer_kernel(*refs, has_pending):
    x_ref, refs = refs[0], refs[1:]
    if has_pending:
        pending, refs = refs[:N_PENDING], refs[N_PENDING:]
    sh_ref, sc_ref, g1_ref, w_ref, wg_ref, bg_ref, ng_ref, cw_ref, cb_ref, lg_ref, lb_ref = refs[:11]
    refs = refs[11:]
    if has_pending:
        xo_ref, refs = refs[0], refs[1:]
    o_gla_ref, y_conv_ref, q_s, k_s, v_s, g_s, u_s, a_s, state_ref, la_ref, oacc_ref, buf_ref = refs

    @pl.when(pl.program_id(1) == 0)
    def _():
        state_ref[...] = jnp.zeros_like(state_ref)
        buf_ref[0:CONV_HALO, :] = jnp.zeros((CONV_HALO, buf_ref.shape[1]), F32)

    if has_pending:
        x_ref = _apply_pending(x_ref, pending, xo_ref)
    _norm_matmul_kernel(x_ref, sh_ref, sc_ref, g1_ref, w_ref, q_s, k_s, v_s, g_s, u_s, a_s)
    _gla_kernel(q_s, k_s, v_s, g_s, a_s, wg_ref, bg_ref, ng_ref, o_gla_ref, state_ref, la_ref, oacc_ref)
    _conv_kernel(u_s, cw_ref, cb_ref, lg_ref, lb_ref, y_conv_ref, buf_ref)


def _even_mixer(x, pending, shift, scale, g1, w_cat, w_gate2, b_gate, norm_g, conv_w, conv_b, ln_g, ln_b, tm,
                batch0=0):
    _, s, d = x.shape
    b = shift.shape[0]
    hk = GLA_HEADS * GLA_DK
    hv = GLA_HEADS * GLA_DV
    ch = conv_w.shape[-1]
    row = lambda i, j: (i, 0, 0)
    tile = lambda i, j: (i, j, 0)
    const = lambda i, j: (0, 0)
    full = lambda arr: pl.BlockSpec(arr.shape, const)
    pending = list(pending or ())
    x_out = [jax.ShapeDtypeStruct((b, s, d), F32)] if pending else []
    outs = pl.pallas_call(
        functools.partial(_even_mixer_kernel, has_pending=bool(pending)),
        out_shape=x_out + [jax.ShapeDtypeStruct((b, s, hv), BF16), jax.ShapeDtypeStruct((b, s, ch), BF16)],
        grid=(b, s // tm),
        in_specs=[pl.BlockSpec((None, tm, d), lambda i, j: (i + batch0, j, 0))] + (
            _pending_specs(pending, tm, d) if pending else []) + [
            pl.BlockSpec((None, 1, d), row),
            pl.BlockSpec((None, 1, d), row),
            full(g1), full(w_cat), full(w_gate2), full(b_gate), full(norm_g),
            full(conv_w), full(conv_b), full(ln_g), full(ln_b),
        ],
        out_specs=[pl.BlockSpec((None, tm, d), tile)] * len(x_out) + [
            pl.BlockSpec((None, tm, hv), tile), pl.BlockSpec((None, tm, ch), tile)],
        scratch_shapes=[
            pltpu.VMEM((tm, hk), F32),
            pltpu.VMEM((tm, hk), F32),
            pltpu.VMEM((tm, hv), BF16),
            pltpu.VMEM((tm, hv), F32),
            pltpu.VMEM((tm, 2 * ch), F32),
            pltpu.VMEM((tm, w_gate2.shape[0]), F32),
            pltpu.VMEM((GLA_HEADS * GLA_DK, GLA_DV), F32),
            pltpu.VMEM((tm, hk), F32),
            pltpu.VMEM((tm, hv), F32),
            pltpu.VMEM((tm + CONV_HALO, ch), F32),
        ],
        compiler_params=_params("arbitrary", "arbitrary"),
        name="even_mixer",
    )(x, *pending, shift, scale, g1, w_cat, w_gate2, b_gate, norm_g, conv_w, conv_b, ln_g, ln_b)
    return outs if pending else [x] + list(outs)


def _rope_table_kernel(pos_ref, freq_ref, sign_ref, cos_ref, sin_ref):
    ang = pos_ref[...] * freq_ref[...]
    cos_ref[...] = jnp.cos(ang)
    sin_ref[...] = jnp.sin(ang) * sign_ref[...]


def _rope_tables(positions, head_dim, tm):
    b, s = positions.shape
    rope_dims = head_dim // 4
    half = rope_dims // 2
    inv_freq = ROPE_THETA ** (-jnp.arange(0, rope_dims, 2, dtype=F32) / rope_dims)
    jj = jnp.arange(LANES) % head_dim
    freq = jnp.where(jj < rope_dims, inv_freq[jj % half], 0.0).astype(F32)[None, :]
    sign = jnp.where(jj < half, -1.0, jnp.where(jj < rope_dims, 1.0, 0.0)).astype(F32)[None, :]
    pos = positions.astype(F32)[..., None]
    tile = lambda i, j: (i, j, 0)
    const = lambda i, j: (0, 0)
    return pl.pallas_call(
        _rope_table_kernel,
        out_shape=[jax.ShapeDtypeStruct((b, s, LANES), F32)] * 2,
        grid=(b, s // tm),
        in_specs=[pl.BlockSpec((None, tm, 1), tile), pl.BlockSpec((1, LANES), const),
                  pl.BlockSpec((1, LANES), const)],
        out_specs=[pl.BlockSpec((None, tm, LANES), tile)] * 2,
        compiler_params=_params("arbitrary", "arbitrary"),
        name="rope_tables",
    )(pos, freq, sign)


def _qkv_kernel(*refs, head_dim, has_pending):
    x_ref, refs = refs[0], refs[1:]
    if has_pending:
        pending, refs = refs[:N_PENDING], refs[N_PENDING:]
    sh_ref, sc_ref, g_ref, w_ref, cos_ref, sin_ref = refs[:6]
    refs = refs[6:]
    if has_pending:
        x_ref, refs = _apply_pending(x_ref, pending, refs[0]), refs[1:]
    q_ref, k_ref, v_ref = refs
    h = _modulate(x_ref[...], g_ref[...], sh_ref[...], sc_ref[...]).astype(BF16)
    d = q_ref.shape[-1]
    half = head_dim // 8
    cosf = jnp.tile(cos_ref[...], (1, d // LANES))
    sinf = jnp.tile(sin_ref[...], (1, d // LANES))
    lane = lax.broadcasted_iota(jnp.int32, (1, d), 1)
    first = (lane % head_dim) < half
    for idx, (o_ref, mult) in enumerate(((q_ref, head_dim ** -0.5), (k_ref, 1.0))):
        t = jnp.dot(h, w_ref[:, idx * d:(idx + 1) * d], preferred_element_type=F32)
        partner = jnp.where(first, pltpu.roll(t, d - half, 1), pltpu.roll(t, half, 1))
        o_ref[...] = ((t * cosf + partner * sinf) * mult).astype(o_ref.dtype)
    v_ref[...] = jnp.dot(h, w_ref[:, 2 * d:], preferred_element_type=F32).astype(v_ref.dtype)


def _qkv_rope(x, pending, shift, scale, g, w, cos_t, sin_t, head_dim, tm):
    b, s, d = x.shape
    row = lambda i, j: (i, 0, 0)
    tile = lambda i, j: (i, j, 0)
    pending = list(pending or ())
    x_out = [jax.ShapeDtypeStruct((b, s, d), F32)] if pending else []
    outs = pl.pallas_call(
        functools.partial(_qkv_kernel, head_dim=head_dim, has_pending=bool(pending)),
        out_shape=x_out + [jax.ShapeDtypeStruct((b, s, d), BF16)] * 3,
        grid=(b, s // tm),
        in_specs=[pl.BlockSpec((None, tm, d), tile)] + (_pending_specs(pending, tm, d) if pending else []) + [
            pl.BlockSpec((None, 1, d), row),
            pl.BlockSpec((None, 1, d), row),
            pl.BlockSpec((1, d), lambda i, j: (0, 0)),
            pl.BlockSpec(w.shape, lambda i, j: (0, 0)),
            pl.BlockSpec((None, tm, LANES), tile),
            pl.BlockSpec((None, tm, LANES), tile),
        ],
        out_specs=[pl.BlockSpec((None, tm, d), tile)] * (len(x_out) + 3),
        compiler_params=_params("arbitrary", "arbitrary"),
        name="qkv_rope",
    )(x, *pending, shift, scale, g, w, cos_t, sin_t)
    return outs if pending else [x] + list(outs)


ONES_ROWS = 16
ATTN_BLOCKS = ATTN_SUPER // DIL_BLOCK


def _attn_bias(branches, heads):
    blk = DIL_BLOCK
    kj = np.arange(2 * blk)[:, None]
    qi = np.arange(blk)[None, :]
    dist = qi + blk - kj
    out = []
    for window, dil in branches:
        band = (dist >= 0) & (dist <= window // dil)
        both = np.stack([band, band & (kj >= blk)])
        out.append(np.tile(np.where(both, 0.0, NEG_BIG), (1, 1, heads)))
    return jnp.asarray(np.stack(out), F32)


def _attn_kernel(bias_ref, q_ref, kc_ref, kp_ref, vc_ref, vp_ref, o_ref, qf, kf, vf, ob, lb, st_s, vt_s,
                 *, head_dim, branches):
    sb = q_ref.shape[0]
    blk = DIL_BLOCK
    heads = q_ref.shape[1] // head_dim
    first_super = pl.program_id(2) == 0
    slot0 = jnp.maximum(pl.program_id(2) - pl.num_programs(2), 0)

    qf[...] = q_ref[...].astype(F32)
    kf[0:sb, :] = kp_ref[...].astype(F32)
    kf[sb:, :] = kc_ref[...].astype(F32)
    vf[0:sb, :] = vp_ref[...].astype(F32)
    vf[sb:, :] = vc_ref[...].astype(F32)

    vt_s[:, LANES:, :] = jnp.ones((ATTN_BLOCKS, ONES_ROWS, 2 * blk), BF16)

    lane = lax.broadcasted_iota(jnp.int32, (blk, LANES), 1)
    head_masks = [(lane >= h * head_dim) & (lane < (h + 1) * head_dim) for h in range(heads)]

    for bi, (window, dil) in enumerate(branches):
        assert window // dil <= blk and sb == ATTN_SUPER and sb % (dil * blk) == 0
        unit = dil * blk

        def scores(j, q0, first_unit, dil=dil, unit=unit, bi=bi):
            k0 = sb + q0 - unit
            no_prev = jnp.where(first_super, 1, 0) if first_unit else 0
            qb = qf[pl.ds(q0, blk, stride=dil), :]
            q2 = jnp.concatenate([jnp.where(mk, qb, 0.0) for mk in head_masks], axis=0).astype(BF16)
            kb = kf[pl.ds(k0, 2 * blk, stride=dil), :].astype(BF16)
            vt_s[j, 0:LANES, :] = vf[pl.ds(k0, 2 * blk, stride=dil), :].T.astype(BF16)
            st = lax.dot_general(kb, q2, (((1,), (1,)), ((), ())), preferred_element_type=F32)
            st_s[slot0 + j] = st + bias_ref[bi, no_prev]

        def softmax_pv(j):
            m = jnp.max(st_s[slot0 + j], axis=0, keepdims=True)
            p = jnp.exp(st_s[slot0 + j] - m).astype(BF16)
            of = jnp.dot(vt_s[j], p, preferred_element_type=F32)
            l = of[LANES:LANES + 1, :]
            lse = m + jnp.log(l)
            o_rows, lse_rows = [], []
            for h in range(heads):
                cols = slice(h * blk, (h + 1) * blk)
                o_rows.append(of[h * head_dim:(h + 1) * head_dim, cols] / l[:, cols])
                lse_rows.append(jnp.broadcast_to(lse[:, cols], (head_dim, blk)))
            return jnp.concatenate(o_rows, axis=0).T, jnp.concatenate(lse_rows, axis=0).T

        starts = [(idx // dil) * unit + idx % dil for idx in range(ATTN_BLOCKS)]
        for j, q0 in enumerate(starts):
            scores(j, q0, q0 < unit)
        for j, q0 in enumerate(starts):
            o_tok, lse_tok = softmax_pv(j)
            ob[bi, pl.ds(q0, blk, stride=dil), :] = o_tok
            lb[bi, pl.ds(q0, blk, stride=dil), :] = lse_tok

    nb = len(branches)
    m = lb[0]
    for bi in range(1, nb):
        m = jnp.maximum(m, lb[bi])
    num = jnp.zeros_like(m)
    den = jnp.zeros_like(m)
    for bi in range(nb):
        e = jnp.exp(lb[bi] - m)
        num = num + e * ob[bi]
        den = den + e
    o_ref[...] = (num / den).astype(o_ref.dtype)


def _dilated_attention(q, k, v, head_dim, branches, sb):
    b, s, d = q.shape
    groups = d // LANES
    cur = lambda i, g, n: (i, n, g)
    prev = lambda i, g, n: (i, jnp.maximum(n - 1, 0), g)
    nb = len(branches)
    bias = _attn_bias(branches, LANES // head_dim)
    return pl.pallas_call(
        functools.partial(_attn_kernel, head_dim=head_dim, branches=branches),
        out_shape=jax.ShapeDtypeStruct((b, s, d), BF16),
        grid=(b, groups, s // sb),
        in_specs=[
            pl.BlockSpec(bias.shape, lambda i, g, n: (0, 0, 0, 0)),
            pl.BlockSpec((None, sb, LANES), cur),
            pl.BlockSpec((None, sb, LANES), cur),
            pl.BlockSpec((None, sb, LANES), prev),
            pl.BlockSpec((None, sb, LANES), cur),
            pl.BlockSpec((None, sb, LANES), prev),
        ],
        out_specs=pl.BlockSpec((None, sb, LANES), cur),
        scratch_shapes=[
            pltpu.VMEM((sb, LANES), F32),
            pltpu.VMEM((2 * sb, LANES), F32),
            pltpu.VMEM((2 * sb, LANES), F32),
            pltpu.VMEM((nb, sb, LANES), F32),
            pltpu.VMEM((nb, sb, LANES), F32),
            pltpu.VMEM((ATTN_BLOCKS, 2 * DIL_BLOCK, LANES // head_dim * DIL_BLOCK), F32),
            pltpu.VMEM((ATTN_BLOCKS, LANES + ONES_ROWS, 2 * DIL_BLOCK), BF16),
        ],
        compiler_params=_params("arbitrary", "arbitrary", "arbitrary"),
        name="dilated_attention",
    )(bias, q, k, k, v, v)


def _route_tile(x, sh_ref, sc_ref, g_ref, w_ref, b_ref, h_ref, wts_ref, route_ref, counts_ref, carry_ref,
                n_experts, n_groups):
    h = _modulate(x, g_ref[...], sh_ref[...], sc_ref[...]).astype(BF16)
    h_ref[...] = h
    logits = jnp.dot(h, w_ref[...], preferred_element_type=F32) + b_ref[...]
    tm = logits.shape[0]
    epg = n_experts // n_groups
    lane = lax.broadcasted_iota(jnp.int32, logits.shape, 1)
    neg = -jnp.inf
    big = jnp.int32(LANES)

    def first_max(mask):
        val = jnp.max(jnp.where(mask, logits, neg), axis=-1, keepdims=True)
        idx = jnp.min(jnp.where(mask & (logits == val), lane, big), axis=-1, keepdims=True)
        return val, idx

    gmask = (lane >= n_experts) & (lane < n_experts + n_groups)
    gmax, gidx = first_max(gmask)
    gsum = jnp.sum(jnp.where(gmask, jnp.exp(logits - gmax), 0.0), axis=-1, keepdims=True)
    g_w = 1.0 / gsum
    grp = gidx - n_experts
    assert epg & (epg - 1) == 0
    emask = (lane < n_experts) & (lax.shift_right_logical(lane, epg.bit_length() - 1) == grp)
    v1, i1 = first_max(emask)
    v2, i2 = first_max(emask & (lane != i1))
    e2 = jnp.exp(v2 - v1)
    den = 1.0 + e2
    col = lax.broadcasted_iota(jnp.int32, (tm, TOP_K), 1)
    wts_ref[...] = jnp.where(col == 0, 1.0 / den, e2 / den) * g_w

    hit1 = lane == i1
    hit2 = lane == i2
    onehot = jnp.where(hit1 | hit2, 1.0, 0.0)
    ri = lax.broadcasted_iota(jnp.int32, (tm, tm), 0)
    ci = lax.broadcasted_iota(jnp.int32, (tm, tm), 1)
    before = jnp.where(ci < ri, 1.0, 0.0).astype(BF16)
    prefix = jnp.dot(before, onehot.astype(BF16), preferred_element_type=F32) + carry_ref[0:1, :]
    r1 = jnp.sum(jnp.where(hit1, prefix, 0.0), axis=-1, keepdims=True)
    r2 = jnp.sum(jnp.where(hit2, prefix, 0.0), axis=-1, keepdims=True)
    packed = jnp.where(lane == 0, i1.astype(F32), jnp.where(lane == 1, i2.astype(F32),
                       jnp.where(lane == 2, r1, jnp.where(lane == 3, r2, 0.0))))
    route_ref[...] = packed.T[0:SUBLANES, :]
    carry_ref[...] = carry_ref[...] + jnp.sum(onehot, axis=0, keepdims=True)
    counts_ref[...] = carry_ref[...]


def _out_proj_router_kernel(*refs, n_acts, n_experts, n_groups):
    a_refs = refs[:n_acts]
    w_ref, x_ref, gate_ref, sh_ref, sc_ref, g_ref, wr_ref, br_ref = refs[n_acts:n_acts + 8]
    xo_ref, h_ref, wts_ref, route_ref, counts_ref, carry_ref = refs[n_acts + 8:]

    @pl.when((pl.program_id(0) == 0) & (pl.program_id(1) == 0))
    def _():
        carry_ref[...] = jnp.zeros_like(carry_ref)

    acc = None
    off = 0
    for a_ref in a_refs:
        kk = a_ref.shape[-1]
        part = jnp.dot(a_ref[...].astype(BF16), w_ref[off:off + kk, :], preferred_element_type=F32)
        acc = part if acc is None else acc + part
        off += kk
    x = x_ref[...] + gate_ref[...] * acc
    xo_ref[...] = x
    _route_tile(x, sh_ref, sc_ref, g_ref, wr_ref, br_ref, h_ref, wts_ref, route_ref, counts_ref, carry_ref,
                n_experts, n_groups)


def _out_proj_router(acts, w, x, gate, shift, scale, g, w_rt, b_rt, n_experts, n_groups, tm, batch0=0):
    _, s, d = x.shape
    b = acts[0].shape[0]
    row = lambda i, j: (i, 0, 0)
    tile = lambda i, j: (i, j, 0)
    const = lambda i, j: (0, 0)
    return pl.pallas_call(
        functools.partial(_out_proj_router_kernel, n_acts=len(acts), n_experts=n_experts, n_groups=n_groups),
        out_shape=[jax.ShapeDtypeStruct((b, s, d), F32),
                   jax.ShapeDtypeStruct((b, s, d), BF16),
                   jax.ShapeDtypeStruct((b, s, TOP_K), F32),
                   jax.ShapeDtypeStruct((SUBLANES, b * s), F32),
                   jax.ShapeDtypeStruct((SUBLANES, LANES), F32)],
        grid=(b, s // tm),
        in_specs=[pl.BlockSpec((None, tm, a.shape[-1]), tile) for a in acts] + [
            pl.BlockSpec(w.shape, const),
            pl.BlockSpec((None, tm, d), lambda i, j: (i + batch0, j, 0)),
            pl.BlockSpec((None, 1, d), row),
            pl.BlockSpec((None, 1, d), row),
            pl.BlockSpec((None, 1, d), row),
            pl.BlockSpec((1, d), const),
            pl.BlockSpec((d, LANES), const),
            pl.BlockSpec((1, LANES), const),
        ],
        out_specs=[pl.BlockSpec((None, tm, d), tile),
                   pl.BlockSpec((None, tm, d), tile),
                   pl.BlockSpec((None, tm, TOP_K), tile),
                   pl.BlockSpec((SUBLANES, tm), lambda i, j: (0, i * (s // tm) + j)),
                   pl.BlockSpec((SUBLANES, LANES), const)],
        scratch_shapes=[pltpu.VMEM((SUBLANES, LANES), F32)],
        compiler_params=_params("arbitrary", "arbitrary"),
        name="out_proj_router",
    )(*acts, w, x, gate, shift, scale, g, w_rt, b_rt)


def _expert_kernel(ib_ref, ie_ref, lo_ref, hi_ref, xs_ref, w1_ref, w3_ref, w2_ref, ys_ref, w1b, w3b, w2b):
    j = pl.program_id(0)
    prev = jnp.maximum(j - 1, 0)
    e_changed = (j == 0) | (ie_ref[j] != ie_ref[prev])
    first_of_block = (j == 0) | (ib_ref[j] != ib_ref[prev])
    lo = lo_ref[j]
    hi = hi_ref[j]

    @pl.when(e_changed)
    def _():
        w1b[...] = w1_ref[...].astype(BF16)
        w3b[...] = w3_ref[...].astype(BF16)
        w2b[...] = w2_ref[...].astype(BF16)

    rows = ys_ref.shape[0]
    whole = (lo == 0) & (hi == rows)

    @pl.when(first_of_block & jnp.logical_not(whole))
    def _():
        ys_ref[...] = jnp.zeros_like(ys_ref)

    @pl.when(hi > lo)
    def _():
        x = xs_ref[...]
        a = jnp.dot(x, w1b[...], preferred_element_type=F32)
        g = jnp.dot(x, w3b[...], preferred_element_type=F32)
        y = jnp.dot((_silu(a) * g).astype(BF16), w2b[...], preferred_element_type=F32).astype(ys_ref.dtype)

        @pl.when(whole)
        def _():
            ys_ref[...] = y

        @pl.when(jnp.logical_not(whole))
        def _():
            row = lax.broadcasted_iota(jnp.int32, (rows, 1), 0)
            ys_ref[...] = jnp.where((row >= lo) & (row < hi), y, ys_ref[...])


def _experts(layer, items, xs, w1, w3, w2, rows):
    a, d = xs.shape
    hid = w1.shape[-1]
    blk = lambda j, ib, ie, lo, hi: (ib[j], 0)
    wsel = lambda j, ib, ie, lo, hi: (layer, ie[j], 0, 0)
    grid_spec = pltpu.PrefetchScalarGridSpec(
        num_scalar_prefetch=4,
        grid=(items[0].shape[0],),
        in_specs=[
            pl.BlockSpec((rows, d), blk),
            pl.BlockSpec((None, None, d, hid), wsel),
            pl.BlockSpec((None, None, d, hid), wsel),
            pl.BlockSpec((None, None, hid, d), wsel),
        ],
        out_specs=pl.BlockSpec((rows, d), blk),
        scratch_shapes=[pltpu.VMEM((d, hid), BF16), pltpu.VMEM((d, hid), BF16), pltpu.VMEM((hid, d), BF16)],
    )
    return pl.pallas_call(
        _expert_kernel,
        out_shape=jax.ShapeDtypeStruct((a, d), BF16),
        grid_spec=grid_spec,
        compiler_params=_params("arbitrary"),
        name="moe_experts",
    )(*items, xs, w1, w3, w2)


def _dispatch(ids, rank, counts, rows):
    n_experts = counts.shape[0]
    a = ids.size
    i32 = jnp.int32
    ends = jnp.cumsum(counts)
    starts = ends - counts
    dest = rank
    for e in range(n_experts):
        dest = dest + jnp.where(ids == e, starts[e], 0)
    t = ids.shape[1]
    tok = jnp.tile(jnp.arange(t, dtype=i32), TOP_K)
    row_tok = lax.sort_key_val(dest.reshape(-1), tok)[1]
    n_blk = a // rows
    bstart = jnp.arange(n_blk, dtype=i32) * rows
    count_le = lambda bounds, x: jnp.sum((bounds[None, :] <= x[:, None]).astype(i32), axis=1)
    e_lo = jnp.minimum(count_le(ends, bstart), n_experts - 1)
    e_hi = jnp.minimum(count_le(ends, bstart + rows - 1), n_experts - 1)
    n_items = e_hi - e_lo + 1
    item_end = jnp.cumsum(n_items)
    item_first = item_end - n_items
    jj = jnp.arange(n_blk + n_experts - 1, dtype=i32)
    valid = jj < item_end[-1]
    ib = jnp.minimum(count_le(item_end, jj), n_blk - 1)
    ie = jnp.where(valid, jnp.clip(e_lo[ib] + jj - item_first[ib], 0, n_experts - 1), e_hi[n_blk - 1]).astype(i32)
    lo = jnp.where(valid, jnp.clip(starts[ie] - ib * rows, 0, rows), 0).astype(i32)
    hi = jnp.where(valid, jnp.clip(ends[ie] - ib * rows, 0, rows), 0).astype(i32)
    return row_tok, dest, (ib, ie, lo, hi)


def _final_combine_kernel(x_ref, y0_ref, y1_ref, w_ref, gate_ref, ng_ref, o_ref):
    x = _apply_pending(x_ref, (y0_ref, y1_ref, w_ref, gate_ref), o_ref)[...]
    o_ref[...] = x * lax.rsqrt(jnp.mean(x * x, axis=-1, keepdims=True) + NORM_EPS) * ng_ref[...]


def _final_combine_kernel_into(x_ref, y0_ref, y1_ref, w_ref, gate_ref, ng_ref, prev_ref, o_ref):
    del prev_ref
    _final_combine_kernel(x_ref, y0_ref, y1_ref, w_ref, gate_ref, ng_ref, o_ref)


def _final_combine(x, pending, final_g, tm, out, batch0, b_total):
    b, s, d = x.shape
    tile = lambda i, j: (i, j, 0)
    in_specs = [pl.BlockSpec((None, tm, d), tile)] + _pending_specs(pending, tm, d) + [
        pl.BlockSpec((1, d), lambda i, j: (0, 0))]
    args = [x, *pending, final_g]
    if out is not None:
        in_specs.append(pl.BlockSpec(memory_space=pl.ANY))
        args.append(out)
    return pl.pallas_call(
        _final_combine_kernel if out is None else _final_combine_kernel_into,
        out_shape=jax.ShapeDtypeStruct((b_total, s, d), F32),
        grid=(b, s // tm),
        in_specs=in_specs,
        out_specs=pl.BlockSpec((None, tm, d), lambda i, j: (i + batch0, j, 0)),
        input_output_aliases={} if out is None else {len(args) - 1: 0},
        compiler_params=_params("arbitrary", "arbitrary"),
        name="moe_final_combine",
    )(*args)


def _hier_moe(layer, x, routed, gate, w1, w3, w2, rows):
    b, s, d = x.shape
    t = b * s
    n_experts = w1.shape[1]
    h, wts, route, counts = routed
    counts = counts[0, :n_experts].astype(jnp.int32)
    route = route.astype(jnp.int32)
    row_tok, dest, items = _dispatch(route[0:TOP_K], route[TOP_K:2 * TOP_K], counts, rows)
    xs = h.reshape(t, d)[row_tok]
    ys = _experts(layer, items, xs, w1, w3, w2, rows)
    yy = ys[dest.reshape(-1)].reshape(TOP_K, b, s, d)
    return yy, yy, wts, gate


def _pick_tile(s, pref):
    tm = min(pref, s)
    assert s % tm == 0
    return tm


def kernel(x, c, positions, ada_w, ada_b, norm1_g, norm2_g, even_w_in, even_w_gate2, even_b_gate, even_gla_norm_g, even_conv_w, even_conv_b, even_conv_ln_g, even_conv_ln_b, even_w_out, odd_w_qkv, odd_w_out, moe_w_grp, moe_b_grp, moe_w_rt, moe_b_rt, moe_w1, moe_w3, moe_w2, final_norm_g):
    b, s, d = x.shape
    depth = ada_w.shape[0]
    n_experts = moe_w_rt.shape[-1]
    tm = _pick_tile(s, TOKEN_TILE)
    sb = _pick_tile(s, ATTN_SUPER)
    head_dim = d // ATTN_HEADS
    hk = GLA_HEADS * GLA_DK
    hv = GLA_HEADS * GLA_DV
    conv_ch = d // 2

    mods = _ada_mods(c, ada_w, ada_b)
    n_chains = BATCH_CHAINS if b % BATCH_CHAINS == 0 else 1
    bc = b // n_chains
    chains = [dict(x=x, batch0=ch * bc, pending=None, tables=None) for ch in range(n_chains)]

    for layer in range(depth):
        i = layer // 2
        g1 = norm1_g[layer][None, :]
        w_rt_full = jnp.concatenate([moe_w_rt[layer], moe_w_grp[layer],
                                     jnp.zeros((d, LANES - n_experts - N_GROUPS), F32)], axis=1).astype(BF16)
        b_rt_full = jnp.concatenate([moe_b_rt[layer], moe_b_grp[layer],
                                     jnp.zeros((LANES - n_experts - N_GROUPS,), F32)])[None, :]
        if layer % 2 == 0:
            w_in = even_w_in[i]
            main = hk + hk + hv + hv
            w_cat = jnp.concatenate([
                w_in[:, :main], w_in[:, main + GLA_GATE_RANK:], w_in[:, main:main + GLA_GATE_RANK],
                jnp.zeros((d, LANES - GLA_GATE_RANK), w_in.dtype)], axis=1).astype(BF16)
            wg = jnp.concatenate([even_w_gate2[i], jnp.zeros((LANES - GLA_GATE_RANK, hk), F32)], axis=0).astype(BF16)
            w_out = even_w_out[i].astype(BF16)
        else:
            w_qkv = odd_w_qkv[i].astype(BF16)
            w_out = odd_w_out[i].astype(BF16)

        for ch, st in enumerate(chains):
            seqs = slice(ch * bc, (ch + 1) * bc)
            mod = lambda j: mods[layer, j][seqs][:, None, :]
            xin, batch0 = st['x'], st['batch0']
            if layer % 2 == 0:
                x_new, o_gla, y_conv = _even_mixer(
                    xin, st['pending'], mod(0), mod(1), g1, w_cat, wg, even_b_gate[i][None, :],
                    even_gla_norm_g[i][None, :], even_conv_w[i], even_conv_b[i][None, :],
                    even_conv_ln_g[i][None, :], even_conv_ln_b[i][None, :], tm, batch0)
                acts = [o_gla, y_conv]
            else:
                assert batch0 == 0
                if st['tables'] is None:
                    st['tables'] = _rope_tables(positions[seqs], head_dim, tm)
                x_new, q, k, v = _qkv_rope(xin, st['pending'], mod(0), mod(1), g1, w_qkv,
                                           *st['tables'], head_dim, tm)
                acts = [_dilated_attention(q, k, v, head_dim, DILATED_BRANCHES, sb)]
            if st['pending'] is None:
                x_new = xin
            x_new, *routed = _out_proj_router(acts, w_out, x_new, mod(2), mod(3), mod(4), norm2_g[layer][None, :],
                                              w_rt_full, b_rt_full, n_experts, N_GROUPS,
                                              _pick_tile(s, ROUTER_TILE), batch0)
            st['x'], st['batch0'] = x_new, 0
            st['pending'] = _hier_moe(layer, x_new, routed, mod(5), moe_w1, moe_w3, moe_w2, MOE_ROWS)

    out = None
    for ch, st in enumerate(chains):
        out = _final_combine(st['x'], st['pending'], final_norm_g[None, :], tm, out, ch * bc, b)
    return out
```

```python
import functools

import jax
import jax.numpy as jnp
import numpy as np
from jax import lax
from jax.experimental import pallas as pl
from jax.experimental.pallas import tpu as pltpu

F32 = jnp.float32
BF16 = jnp.bfloat16
HIGHEST = lax.Precision.HIGHEST

NORM_EPS = 1e-6
GLA_HEADS = 4
GLA_DK = 64
GLA_DV = 128
GLA_GATE_RANK = 16
GLA_TAU = 16.0
GLA_CHUNK = 64
CONV_WIDTH = 31
ATTN_HEADS = 16
DILATED_BRANCHES = ((128, 1), (512, 4), (2048, 16))
DIL_BLOCK = 128
ROPE_THETA = 500000.0
N_GROUPS = 4
TOP_K = 2
ADA_CHUNKS = 6

LANES = 128
SUBLANES = 8
VMEM_LIMIT = 56 * 1024 * 1024
TOKEN_TILE = 512
ROUTER_TILE = 1024
BATCH_CHAINS = 2
ATTN_SUPER = 2048
MOE_ROWS = 512
NEG_BIG = -1e30


def _params(*sem):
    return pltpu.CompilerParams(dimension_semantics=sem, vmem_limit_bytes=VMEM_LIMIT)


def _silu(x):
    return x * jax.nn.sigmoid(x)


def _modulate(x, g, shift, scale):
    y = x * lax.rsqrt(jnp.mean(x * x, axis=-1, keepdims=True) + NORM_EPS)
    return (y * g) * (1.0 + scale) + shift


def _ada_kernel(c_ref, w_ref, b_ref, o_ref):
    cond = _silu(c_ref[...])
    o_ref[...] = jnp.dot(cond, w_ref[...], preferred_element_type=F32, precision=HIGHEST) + b_ref[...]


def _ada_mods(c, ada_w, ada_b):
    depth, d, _ = ada_w.shape
    b = c.shape[0]
    return pl.pallas_call(
        _ada_kernel,
        out_shape=jax.ShapeDtypeStruct((depth, ADA_CHUNKS, b, d), F32),
        grid=(depth, ADA_CHUNKS),
        in_specs=[
            pl.BlockSpec((b, d), lambda l, j: (0, 0)),
            pl.BlockSpec((None, d, d), lambda l, j: (l, 0, j)),
            pl.BlockSpec((None, None, 1, d), lambda l, j: (l, j, 0, 0)),
        ],
        out_specs=pl.BlockSpec((None, None, b, d), lambda l, j: (l, j, 0, 0)),
        compiler_params=_params("arbitrary", "arbitrary"),
        name="ada_mods",
    )(c, ada_w, ada_b.reshape(depth, ADA_CHUNKS, 1, d))


def _norm_matmul_kernel(x_ref, sh_ref, sc_ref, g_ref, w_ref, *o_refs):
    h = _modulate(x_ref[...], g_ref[...], sh_ref[...], sc_ref[...]).astype(BF16)
    off = 0
    for o_ref in o_refs:
        n = o_ref.shape[-1]
        o_ref[...] = jnp.dot(h, w_ref[:, off:off + n], preferred_element_type=F32).astype(o_ref.dtype)
        off += n


def _log_sigmoid(z):
    return jnp.minimum(z, 0.0) - jnp.log1p(jnp.exp(-jnp.abs(z)))


def _gla_kernel(q_ref, k_ref, v_ref, g_ref, a_ref, wg_ref, bg_ref, ng_ref, o_ref, state_ref, la_ref, oacc_ref):
    tm = q_ref.shape[0]
    c = GLA_CHUNK

    z = jnp.dot(a_ref[...].astype(BF16), wg_ref[...], preferred_element_type=F32) + bg_ref[...]
    la_ref[...] = _log_sigmoid(z) * (1.0 / GLA_TAU)

    ri = lax.broadcasted_iota(jnp.int32, (c, c), 0)
    ci = lax.broadcasted_iota(jnp.int32, (c, c), 1)
    causal = ri >= ci
    tril = jnp.where(causal, 1.0, 0.0).astype(BF16)
    hk = GLA_HEADS * GLA_DK

    nh = GLA_HEADS
    lane_head = lax.shift_right_logical(lax.broadcasted_iota(jnp.int32, (c, hk), 1), GLA_DK.bit_length() - 1)
    r4 = lax.broadcasted_iota(jnp.int32, (nh * c, nh * c), 0)
    c4 = lax.broadcasted_iota(jnp.int32, (nh * c, nh * c), 1)
    shift_c = c.bit_length() - 1
    causal4 = (lax.shift_right_logical(r4, shift_c) == lax.shift_right_logical(c4, shift_c)) & (r4 >= c4)

    def stack_heads(t):
        return jnp.concatenate([jnp.where(lane_head == h, t, 0.0) for h in range(nh)], axis=0).astype(BF16)

    state = state_ref[...]
    for ic in range(tm // c):
        rows = slice(ic * c, (ic + 1) * c)
        la = la_ref[rows, :]
        p0 = la.astype(BF16)
        r1 = la - p0.astype(F32)
        p1 = r1.astype(BF16)
        p2 = (r1 - p1.astype(F32)).astype(BF16)
        parts = jnp.dot(tril, jnp.concatenate([p0, p1, p2], axis=1), preferred_element_type=F32)
        bcum = (parts[:, 2 * hk:] + parts[:, hk:2 * hk]) + parts[:, :hk]
        b_last = bcum[c - 1:c, :]
        q = q_ref[rows, :] * (GLA_DK ** -0.5)
        k = k_ref[rows, :]
        q4 = stack_heads(q * jnp.exp(bcum))
        k4 = stack_heads(k * jnp.exp(-bcum))
        kr4 = stack_heads(k * jnp.exp(b_last - bcum))
        dec = jnp.exp(jnp.broadcast_to(b_last, (GLA_DV, hk)).T)
        v4 = jnp.concatenate([v_ref[rows, h * GLA_DV:(h + 1) * GLA_DV] for h in range(nh)], axis=0)
        att = lax.dot_general(q4, k4, (((1,), (1,)), ((), ())), preferred_element_type=F32)
        att = jnp.where(causal4, att, 0.0).astype(BF16)
        o4 = jnp.dot(att, v4, preferred_element_type=F32)
        o4 = o4 + jnp.dot(q4, state.astype(BF16), preferred_element_type=F32)
        kv = lax.dot_general(kr4, v4, (((0,), (0,)), ((), ())), preferred_element_type=F32)
        state = dec * state + kv
        oacc_ref[rows, :] = jnp.concatenate([o4[h * c:(h + 1) * c, :] for h in range(nh)], axis=1)
    state_ref[...] = state

    for h in range(GLA_HEADS):
        vs = slice(h * GLA_DV, (h + 1) * GLA_DV)
        o = oacc_ref[:, vs]
        o = o * lax.rsqrt(jnp.mean(o * o, axis=-1, keepdims=True) + NORM_EPS) * ng_ref[...]
        o_ref[:, vs] = (o * _silu(g_ref[:, vs])).astype(o_ref.dtype)


CONV_HALO = 32


def _conv_kernel(u_ref, w_ref, cb_ref, lg_ref, lb_ref, o_ref, buf_ref):
    tm = u_ref.shape[0]
    ch = o_ref.shape[-1]

    buf_ref[CONV_HALO:, :] = u_ref[:, :ch] * jax.nn.sigmoid(u_ref[:, ch:])
    base = CONV_HALO - (CONV_WIDTH - 1)
    acc = None
    for b in range(SUBLANES):
        part = None
        span = tm + (SUBLANES if b else 0)
        for a in range((base + CONV_WIDTH - 1) // SUBLANES + 1):
            j = SUBLANES * a + b - base
            if 0 <= j < CONV_WIDTH:
                term = buf_ref[SUBLANES * a:SUBLANES * a + span, :] * w_ref[j:j + 1, :]
                part = term if part is None else part + term
        if part is not None:
            part = part[b:b + tm, :]
            acc = part if acc is None else acc + part
    buf_ref[0:CONV_HALO, :] = buf_ref[tm:tm + CONV_HALO, :]
    y = acc + cb_ref[...]
    mu = jnp.mean(y, axis=-1, keepdims=True)
    var = jnp.mean(jnp.square(y - mu), axis=-1, keepdims=True)
    y = (y - mu) * lax.rsqrt(var + NORM_EPS) * lg_ref[...] + lb_ref[...]
    o_ref[...] = _silu(y).astype(o_ref.dtype)


N_PENDING = 4


def _apply_pending(x_ref, pending, xo_ref):
    y0_ref, y1_ref, w_ref, gate_ref = pending
    y = y0_ref[...].astype(F32) * w_ref[:, 0:1] + y1_ref[...].astype(F32) * w_ref[:, 1:2]
    xo_ref[...] = x_ref[...] + gate_ref[...] * y
    return xo_ref


def _pending_specs(pending, tm, d):
    tile = lambda i, j: (i, j, 0)
    return [pl.BlockSpec((None, None, tm, d), lambda i, j: (0, i, j, 0)),
            pl.BlockSpec((None, None, tm, d), lambda i, j: (1, i, j, 0)),
            pl.BlockSpec((None, tm, TOP_K), tile), pl.BlockSpec((None, 1, d), lambda i, j: (i, 0, 0))]


def _even_mixer_kernel(*refs, has_pending):
    x_ref, refs = refs[0], refs[1:]
    if has_pending:
        pending, refs = refs[:N_PENDING], refs[N_PENDING:]
    sh_ref, sc_ref, g1_ref, w_ref, wg_ref, bg_ref, ng_ref, cw_ref, cb_ref, lg_ref, lb_ref = refs[:11]
    refs = refs[11:]
    if has_pending:
        xo_ref, refs = refs[0], refs[1:]
    o_gla_ref, y_conv_ref, q_s, k_s, v_s, g_s, u_s, a_s, state_ref, la_ref, oacc_ref, buf_ref = refs

    @pl.when(pl.program_id(1) == 0)
    def _():
        state_ref[...] = jnp.zeros_like(state_ref)
        buf_ref[0:CONV_HALO, :] = jnp.zeros((CONV_HALO, buf_ref.shape[1]), F32)

    if has_pending:
        x_ref = _apply_pending(x_ref, pending, xo_ref)
    _norm_matmul_kernel(x_ref, sh_ref, sc_ref, g1_ref, w_ref, q_s, k_s, v_s, g_s, u_s, a_s)
    _gla_kernel(q_s, k_s, v_s, g_s, a_s, wg_ref, bg_ref, ng_ref, o_gla_ref, state_ref, la_ref, oacc_ref)
    _conv_kernel(u_s, cw_ref, cb_ref, lg_ref, lb_ref, y_conv_ref, buf_ref)


def _even_mixer(x, pending, shift, scale, g1, w_cat, w_gate2, b_gate, norm_g, conv_w, conv_b, ln_g, ln_b, tm,
                batch0=0):
    _, s, d = x.shape
    b = shift.shape[0]
    hk = GLA_HEADS * GLA_DK
    hv = GLA_HEADS * GLA_DV
    ch = conv_w.shape[-1]
    row = lambda i, j: (i, 0, 0)
    tile = lambda i, j: (i, j, 0)
    const = lambda i, j: (0, 0)
    full = lambda arr: pl.BlockSpec(arr.shape, const)
    pending = list(pending or ())
    x_out = [jax.ShapeDtypeStruct((b, s, d), F32)] if pending else []
    outs = pl.pallas_call(
        functools.partial(_even_mixer_kernel, has_pending=bool(pending)),
        out_shape=x_out + [jax.ShapeDtypeStruct((b, s, hv), BF16), jax.ShapeDtypeStruct((b, s, ch), BF16)],
        grid=(b, s // tm),
        in_specs=[pl.BlockSpec((None, tm, d), lambda i, j: (i + batch0, j, 0))] + (
            _pending_specs(pending, tm, d) if pending else []) + [
            pl.BlockSpec((None, 1, d), row),
            pl.BlockSpec((None, 1, d), row),
            full(g1), full(w_cat), full(w_gate2), full(b_gate), full(norm_g),
            full(conv_w), full(conv_b), full(ln_g), full(ln_b),
        ],
        out_specs=[pl.BlockSpec((None, tm, d), tile)] * len(x_out) + [
            pl.BlockSpec((None, tm, hv), tile), pl.BlockSpec((None, tm, ch), tile)],
        scratch_shapes=[
            pltpu.VMEM((tm, hk), F32),
            pltpu.VMEM((tm, hk), F32),
            pltpu.VMEM((tm, hv), BF16),
            pltpu.VMEM((tm, hv), F32),
            pltpu.VMEM((tm, 2 * ch), F32),
            pltpu.VMEM((tm, w_gate2.shape[0]), F32),
            pltpu.VMEM((GLA_HEADS * GLA_DK, GLA_DV), F32),
            pltpu.VMEM((tm, hk), F32),
            pltpu.VMEM((tm, hv), F32),
            pltpu.VMEM((tm + CONV_HALO, ch), F32),
        ],
        compiler_params=_params("arbitrary", "arbitrary"),
        name="even_mixer",
    )(x, *pending, shift, scale, g1, w_cat, w_gate2, b_gate, norm_g, conv_w, conv_b, ln_g, ln_b)
    return outs if pending else [x] + list(outs)


def _rope_table_kernel(pos_ref, freq_ref, sign_ref, cos_ref, sin_ref):
    ang = pos_ref[...] * freq_ref[...]
    cos_ref[...] = jnp.cos(ang)
    sin_ref[...] = jnp.sin(ang) * sign_ref[...]


def _rope_tables(positions, head_dim, tm):
    b, s = positions.shape
    rope_dims = head_dim // 4
    half = rope_dims // 2
    inv_freq = ROPE_THETA ** (-jnp.arange(0, rope_dims, 2, dtype=F32) / rope_dims)
    jj = jnp.arange(LANES) % head_dim
    freq = jnp.where(jj < rope_dims, inv_freq[jj % half], 0.0).astype(F32)[None, :]
    sign = jnp.where(jj < half, -1.0, jnp.where(jj < rope_dims, 1.0, 0.0)).astype(F32)[None, :]
    pos = positions.astype(F32)[..., None]
    tile = lambda i, j: (i, j, 0)
    const = lambda i, j: (0, 0)
    return pl.pallas_call(
        _rope_table_kernel,
        out_shape=[jax.ShapeDtypeStruct((b, s, LANES), F32)] * 2,
        grid=(b, s // tm),
        in_specs=[pl.BlockSpec((None, tm, 1), tile), pl.BlockSpec((1, LANES), const),
                  pl.BlockSpec((1, LANES), const)],
        out_specs=[pl.BlockSpec((None, tm, LANES), tile)] * 2,
        compiler_params=_params("arbitrary", "arbitrary"),
        name="rope_tables",
    )(pos, freq, sign)


def _qkv_kernel(*refs, head_dim, has_pending):
    x_ref, refs = refs[0], refs[1:]
    if has_pending:
        pending, refs = refs[:N_PENDING], refs[N_PENDING:]
    sh_ref, sc_ref, g_ref, w_ref, cos_ref, sin_ref = refs[:6]
    refs = refs[6:]
    if has_pending:
        x_ref, refs = _apply_pending(x_ref, pending, refs[0]), refs[1:]
    q_ref, k_ref, v_ref = refs
    h = _modulate(x_ref[...], g_ref[...], sh_ref[...], sc_ref[...]).astype(BF16)
    d = q_ref.shape[-1]
    half = (head_dim // 4) // 2
    cosf = jnp.tile(cos_ref[...], (1, d // LANES))
    sinf = jnp.tile(sin_ref[...], (1, d // LANES))
    lane = lax.broadcasted_iota(jnp.int32, (1, d), 1)
    first = (lane % head_dim) < half
    for idx, (o_ref, mult) in enumerate(((q_ref, head_dim ** -0.5), (k_ref, 1.0))):
        t = jnp.dot(h, w_ref[:, idx * d:(idx + 1) * d], preferred_element_type=F32)
        partner = jnp.where(first, pltpu.roll(t, d - half, 1), pltpu.roll(t, half, 1))
        o_ref[...] = ((t * cosf + partner * sinf) * mult).astype(o_ref.dtype)
    v_ref[...] = jnp.dot(h, w_ref[:, 2 * d:], preferred_element_type=F32).astype(v_ref.dtype)


def _qkv_rope(x, pending, shift, scale, g, w, cos_t, sin_t, head_dim, tm):
    b, s, d = x.shape
    row = lambda i, j: (i, 0, 0)
    tile = lambda i, j: (i, j, 0)
    pending = list(pending or ())
    x_out = [jax.ShapeDtypeStruct((b, s, d), F32)] if pending else []
    outs = pl.pallas_call(
        functools.partial(_qkv_kernel, head_dim=head_dim, has_pending=bool(pending)),
        out_shape=x_out + [jax.ShapeDtypeStruct((b, s, d), BF16)] * 3,
        grid=(b, s // tm),
        in_specs=[pl.BlockSpec((None, tm, d), tile)] + (_pending_specs(pending, tm, d) if pending else []) + [
            pl.BlockSpec((None, 1, d), row),
            pl.BlockSpec((None, 1, d), row),
            pl.BlockSpec((1, d), lambda i, j: (0, 0)),
            pl.BlockSpec(w.shape, lambda i, j: (0, 0)),
            pl.BlockSpec((None, tm, LANES), tile),
            pl.BlockSpec((None, tm, LANES), tile),
        ],
        out_specs=[pl.BlockSpec((None, tm, d), tile)] * (len(x_out) + 3),
        compiler_params=_params("arbitrary", "arbitrary"),
        name="qkv_rope",
    )(x, *pending, shift, scale, g, w, cos_t, sin_t)
    return outs if pending else [x] + list(outs)


ONES_ROWS = 16
ATTN_BLOCKS = ATTN_SUPER // DIL_BLOCK


def _attn_bias(branches, heads):
    blk = DIL_BLOCK
    kj = np.arange(2 * blk)[:, None]
    qi = np.arange(blk)[None, :]
    dist = qi + blk - kj
    out = []
    for window, dil in branches:
        band = (dist >= 0) & (dist <= window // dil)
        both = np.stack([band, band & (kj >= blk)])
        out.append(np.tile(np.where(both, 0.0, NEG_BIG), (1, 1, heads)))
    return jnp.asarray(np.stack(out), F32)


def _attn_kernel(bias_ref, q_ref, kc_ref, kp_ref, vc_ref, vp_ref, o_ref, qf, kf, vf, ob, lb, st_s, vt_s,
                 *, head_dim, branches):
    sb = q_ref.shape[0]
    blk = DIL_BLOCK
    heads = q_ref.shape[1] // head_dim
    first_super = pl.program_id(2) == 0
    slot0 = jnp.maximum(pl.program_id(2) - pl.num_programs(2), 0)

    qf[...] = q_ref[...].astype(F32)
    kf[0:sb, :] = kp_ref[...].astype(F32)
    kf[sb:, :] = kc_ref[...].astype(F32)
    vf[0:sb, :] = vp_ref[...].astype(F32)
    vf[sb:, :] = vc_ref[...].astype(F32)

    vt_s[:, LANES:, :] = jnp.ones((ATTN_BLOCKS, ONES_ROWS, 2 * blk), BF16)

    lane = lax.broadcasted_iota(jnp.int32, (blk, LANES), 1)
    head_masks = [(lane >= h * head_dim) & (lane < (h + 1) * head_dim) for h in range(heads)]

    for bi, (window, dil) in enumerate(branches):
        assert window // dil <= blk and sb == ATTN_SUPER and sb % (dil * blk) == 0
        unit = dil * blk

        def scores(j, q0, first_unit, dil=dil, unit=unit, bi=bi):
            k0 = sb + q0 - unit
            no_prev = jnp.where(first_super, 1, 0) if first_unit else 0
            qb = qf[pl.ds(q0, blk, stride=dil), :]
            q2 = jnp.concatenate([jnp.where(mk, qb, 0.0) for mk in head_masks], axis=0).astype(BF16)
            kb = kf[pl.ds(k0, 2 * blk, stride=dil), :].astype(BF16)
            vt_s[j, 0:LANES, :] = vf[pl.ds(k0, 2 * blk, stride=dil), :].T.astype(BF16)
            st = lax.dot_general(kb, q2, (((1,), (1,)), ((), ())), preferred_element_type=F32)
            st_s[slot0 + j] = st + bias_ref[bi, no_prev]

        def softmax_pv(j):
            m = jnp.max(st_s[slot0 + j], axis=0, keepdims=True)
            p = jnp.exp(st_s[slot0 + j] - m).astype(BF16)
            of = jnp.dot(vt_s[j], p, preferred_element_type=F32)
            l = of[LANES:LANES + 1, :]
            lse = m + jnp.log(l)
            o_rows, lse_rows = [], []
            for h in range(heads):
                cols = slice(h * blk, (h + 1) * blk)
                o_rows.append(of[h * head_dim:(h + 1) * head_dim, cols] / l[:, cols])
                lse_rows.append(jnp.broadcast_to(lse[:, cols], (head_dim, blk)))
            return jnp.concatenate(o_rows, axis=0).T, jnp.concatenate(lse_rows, axis=0).T

        starts = [(idx // dil) * unit + idx % dil for idx in range(ATTN_BLOCKS)]
        for j, q0 in enumerate(starts):
            scores(j, q0, q0 < unit)
        for j, q0 in enumerate(starts):
            o_tok, lse_tok = softmax_pv(j)
            ob[bi, pl.ds(q0, blk, stride=dil), :] = o_tok
            lb[bi, pl.ds(q0, blk, stride=dil), :] = lse_tok

    nb = len(branches)
    m = lb[0]
    for bi in range(1, nb):
        m = jnp.maximum(m, lb[bi])
    num = jnp.zeros_like(m)
    den = jnp.zeros_like(m)
    for bi in range(nb):
        e = jnp.exp(lb[bi] - m)
        num = num + e * ob[bi]
        den = den + e
    o_ref[...] = (num / den).astype(o_ref.dtype)


def _dilated_attention(q, k, v, head_dim, branches, sb):
    b, s, d = q.shape
    groups = d // LANES
    cur = lambda i, g, n: (i, n, g)
    prev = lambda i, g, n: (i, jnp.maximum(n - 1, 0), g)
    nb = len(branches)
    bias = _attn_bias(branches, LANES // head_dim)
    return pl.pallas_call(
        functools.partial(_attn_kernel, head_dim=head_dim, branches=branches),
        out_shape=jax.ShapeDtypeStruct((b, s, d), BF16),
        grid=(b, groups, s // sb),
        in_specs=[
            pl.BlockSpec(bias.shape, lambda i, g, n: (0, 0, 0, 0)),
            pl.BlockSpec((None, sb, LANES), cur),
            pl.BlockSpec((None, sb, LANES), cur),
            pl.BlockSpec((None, sb, LANES), prev),
            pl.BlockSpec((None, sb, LANES), cur),
            pl.BlockSpec((None, sb, LANES), prev),
        ],
        out_specs=pl.BlockSpec((None, sb, LANES), cur),
        scratch_shapes=[
            pltpu.VMEM((sb, LANES), F32),
            pltpu.VMEM((2 * sb, LANES), F32),
            pltpu.VMEM((2 * sb, LANES), F32),
            pltpu.VMEM((nb, sb, LANES), F32),
            pltpu.VMEM((nb, sb, LANES), F32),
            pltpu.VMEM((ATTN_BLOCKS, 2 * DIL_BLOCK, LANES // head_dim * DIL_BLOCK), F32),
            pltpu.VMEM((ATTN_BLOCKS, LANES + ONES_ROWS, 2 * DIL_BLOCK), BF16),
        ],
        compiler_params=_params("arbitrary", "arbitrary", "arbitrary"),
        name="dilated_attention",
    )(bias, q, k, k, v, v)


def _route_tile(x, sh_ref, sc_ref, g_ref, w_ref, b_ref, h_ref, wts_ref, route_ref, counts_ref, carry_ref,
                n_experts, n_groups):
    h = _modulate(x, g_ref[...], sh_ref[...], sc_ref[...]).astype(BF16)
    h_ref[...] = h
    logits = jnp.dot(h, w_ref[...], preferred_element_type=F32) + b_ref[...]
    tm = logits.shape[0]
    epg = n_experts // n_groups
    lane = lax.broadcasted_iota(jnp.int32, logits.shape, 1)
    neg = -jnp.inf
    big = jnp.int32(LANES)

    def first_max(mask):
        val = jnp.max(jnp.where(mask, logits, neg), axis=-1, keepdims=True)
        idx = jnp.min(jnp.where(mask & (logits == val), lane, big), axis=-1, keepdims=True)
        return val, idx

    gmask = (lane >= n_experts) & (lane < n_experts + n_groups)
    gmax, gidx = first_max(gmask)
    gsum = jnp.sum(jnp.where(gmask, jnp.exp(logits - gmax), 0.0), axis=-1, keepdims=True)
    g_w = 1.0 / gsum
    grp = gidx - n_experts
    assert epg & (epg - 1) == 0
    emask = (lane < n_experts) & (lax.shift_right_logical(lane, epg.bit_length() - 1) == grp)
    v1, i1 = first_max(emask)
    v2, i2 = first_max(emask & (lane != i1))
    e2 = jnp.exp(v2 - v1)
    den = 1.0 + e2
    col = lax.broadcasted_iota(jnp.int32, (tm, TOP_K), 1)
    wts_ref[...] = jnp.where(col == 0, 1.0 / den, e2 / den) * g_w

    hit1 = lane == i1
    hit2 = lane == i2
    onehot = jnp.where(hit1 | hit2, 1.0, 0.0)
    ri = lax.broadcasted_iota(jnp.int32, (tm, tm), 0)
    ci = lax.broadcasted_iota(jnp.int32, (tm, tm), 1)
    before = jnp.where(ci < ri, 1.0, 0.0).astype(BF16)
    prefix = jnp.dot(before, onehot.astype(BF16), preferred_element_type=F32) + carry_ref[0:1, :]
    r1 = jnp.sum(jnp.where(hit1, prefix, 0.0), axis=-1, keepdims=True)
    r2 = jnp.sum(jnp.where(hit2, prefix, 0.0), axis=-1, keepdims=True)
    packed = jnp.where(lane == 0, i1.astype(F32), jnp.where(lane == 1, i2.astype(F32),
                       jnp.where(lane == 2, r1, jnp.where(lane == 3, r2, 0.0))))
    route_ref[...] = packed.T[0:SUBLANES, :]
    carry_ref[...] = carry_ref[...] + jnp.sum(onehot, axis=0, keepdims=True)
    counts_ref[...] = carry_ref[...]


def _out_proj_router_kernel(*refs, n_acts, n_experts, n_groups):
    a_refs = refs[:n_acts]
    w_ref, x_ref, gate_ref, sh_ref, sc_ref, g_ref, wr_ref, br_ref = refs[n_acts:n_acts + 8]
    xo_ref, h_ref, wts_ref, route_ref, counts_ref, carry_ref = refs[n_acts + 8:]

    @pl.when((pl.program_id(0) == 0) & (pl.program_id(1) == 0))
    def _():
        carry_ref[...] = jnp.zeros_like(carry_ref)

    acc = None
    off = 0
    for a_ref in a_refs:
        kk = a_ref.shape[-1]
        part = jnp.dot(a_ref[...].astype(BF16), w_ref[off:off + kk, :], preferred_element_type=F32)
        acc = part if acc is None else acc + part
        off += kk
    x = x_ref[...] + gate_ref[...] * acc
    xo_ref[...] = x
    _route_tile(x, sh_ref, sc_ref, g_ref, wr_ref, br_ref, h_ref, wts_ref, route_ref, counts_ref, carry_ref,
                n_experts, n_groups)


def _out_proj_router(acts, w, x, gate, shift, scale, g, w_rt, b_rt, n_experts, n_groups, tm, batch0=0):
    _, s, d = x.shape
    b = acts[0].shape[0]
    row = lambda i, j: (i, 0, 0)
    tile = lambda i, j: (i, j, 0)
    const = lambda i, j: (0, 0)
    return pl.pallas_call(
        functools.partial(_out_proj_router_kernel, n_acts=len(acts), n_experts=n_experts, n_groups=n_groups),
        out_shape=[jax.ShapeDtypeStruct((b, s, d), F32),
                   jax.ShapeDtypeStruct((b, s, d), BF16),
                   jax.ShapeDtypeStruct((b, s, TOP_K), F32),
                   jax.ShapeDtypeStruct((SUBLANES, b * s), F32),
                   jax.ShapeDtypeStruct((SUBLANES, LANES), F32)],
        grid=(b, s // tm),
        in_specs=[pl.BlockSpec((None, tm, a.shape[-1]), tile) for a in acts] + [
            pl.BlockSpec(w.shape, const),
            pl.BlockSpec((None, tm, d), lambda i, j: (i + batch0, j, 0)),
            pl.BlockSpec((None, 1, d), row),
            pl.BlockSpec((None, 1, d), row),
            pl.BlockSpec((None, 1, d), row),
            pl.BlockSpec((1, d), const),
            pl.BlockSpec((d, LANES), const),
            pl.BlockSpec((1, LANES), const),
        ],
        out_specs=[pl.BlockSpec((None, tm, d), tile),
                   pl.BlockSpec((None, tm, d), tile),
                   pl.BlockSpec((None, tm, TOP_K), tile),
                   pl.BlockSpec((SUBLANES, tm), lambda i, j: (0, i * (s // tm) + j)),
                   pl.BlockSpec((SUBLANES, LANES), const)],
        scratch_shapes=[pltpu.VMEM((SUBLANES, LANES), F32)],
        compiler_params=_params("arbitrary", "arbitrary"),
        name="out_proj_router",
    )(*acts, w, x, gate, shift, scale, g, w_rt, b_rt)


def _expert_kernel(ib_ref, ie_ref, lo_ref, hi_ref, xs_ref, w1_ref, w3_ref, w2_ref, ys_ref, w1b, w3b, w2b):
    j = pl.program_id(0)
    prev = jnp.maximum(j - 1, 0)
    e_changed = (j == 0) | (ie_ref[j] != ie_ref[prev])
    first_of_block = (j == 0) | (ib_ref[j] != ib_ref[prev])
    lo = lo_ref[j]
    hi = hi_ref[j]

    @pl.when(e_changed)
    def _():
        w1b[...] = w1_ref[...].astype(BF16)
        w3b[...] = w3_ref[...].astype(BF16)
        w2b[...] = w2_ref[...].astype(BF16)

    rows = ys_ref.shape[0]
    whole = (lo == 0) & (hi == rows)

    @pl.when(first_of_block & jnp.logical_not(whole))
    def _():
        ys_ref[...] = jnp.zeros_like(ys_ref)

    @pl.when(hi > lo)
    def _():
        x = xs_ref[...]
        a = jnp.dot(x, w1b[...], preferred_element_type=F32)
        g = jnp.dot(x, w3b[...], preferred_element_type=F32)
        y = jnp.dot((_silu(a) * g).astype(BF16), w2b[...], preferred_element_type=F32).astype(ys_ref.dtype)

        @pl.when(whole)
        def _():
            ys_ref[...] = y

        @pl.when(jnp.logical_not(whole))
        def _():
            row = lax.broadcasted_iota(jnp.int32, (rows, 1), 0)
            ys_ref[...] = jnp.where((row >= lo) & (row < hi), y, ys_ref[...])


def _experts(layer, items, xs, w1, w3, w2, rows):
    a, d = xs.shape
    hid = w1.shape[-1]
    blk = lambda j, ib, ie, lo, hi: (ib[j], 0)
    wsel = lambda j, ib, ie, lo, hi: (layer, ie[j], 0, 0)
    grid_spec = pltpu.PrefetchScalarGridSpec(
        num_scalar_prefetch=4,
        grid=(items[0].shape[0],),
        in_specs=[
            pl.BlockSpec((rows, d), blk),
            pl.BlockSpec((None, None, d, hid), wsel),
            pl.BlockSpec((None, None, d, hid), wsel),
            pl.BlockSpec((None, None, hid, d), wsel),
        ],
        out_specs=pl.BlockSpec((rows, d), blk),
        scratch_shapes=[pltpu.VMEM((d, hid), BF16), pltpu.VMEM((d, hid), BF16), pltpu.VMEM((hid, d), BF16)],
    )
    return pl.pallas_call(
        _expert_kernel,
        out_shape=jax.ShapeDtypeStruct((a, d), BF16),
        grid_spec=grid_spec,
        compiler_params=_params("arbitrary"),
        name="moe_experts",
    )(*items, xs, w1, w3, w2)


def _dispatch(ids, rank, counts, rows):
    n_experts = counts.shape[0]
    a = ids.size
    i32 = jnp.int32
    ends = jnp.cumsum(counts)
    starts = ends - counts
    dest = rank
    for e in range(n_experts):
        dest = dest + jnp.where(ids == e, starts[e], 0)
    t = ids.shape[1]
    tok = jnp.tile(jnp.arange(t, dtype=i32), TOP_K)
    row_tok = lax.sort_key_val(dest.reshape(-1), tok)[1]
    n_blk = a // rows
    bstart = jnp.arange(n_blk, dtype=i32) * rows
    count_le = lambda bounds, x: jnp.sum((bounds[None, :] <= x[:, None]).astype(i32), axis=1)
    e_lo = jnp.minimum(count_le(ends, bstart), n_experts - 1)
    e_hi = jnp.minimum(count_le(ends, bstart + rows - 1), n_experts - 1)
    n_items = e_hi - e_lo + 1
    item_end = jnp.cumsum(n_items)
    item_first = item_end - n_items
    jj = jnp.arange(n_blk + n_experts - 1, dtype=i32)
    valid = jj < item_end[-1]
    ib = jnp.minimum(count_le(item_end, jj), n_blk - 1)
    ie = jnp.where(valid, jnp.clip(e_lo[ib] + jj - item_first[ib], 0, n_experts - 1), e_hi[n_blk - 1]).astype(i32)
    lo = jnp.where(valid, jnp.clip(starts[ie] - ib * rows, 0, rows), 0).astype(i32)
    hi = jnp.where(valid, jnp.clip(ends[ie] - ib * rows, 0, rows), 0).astype(i32)
    return row_tok, dest, (ib, ie, lo, hi)


def _final_combine_kernel(x_ref, y0_ref, y1_ref, w_ref, gate_ref, ng_ref, o_ref):
    x = _apply_pending(x_ref, (y0_ref, y1_ref, w_ref, gate_ref), o_ref)[...]
    o_ref[...] = x * lax.rsqrt(jnp.mean(x * x, axis=-1, keepdims=True) + NORM_EPS) * ng_ref[...]


def _final_combine_kernel_into(x_ref, y0_ref, y1_ref, w_ref, gate_ref, ng_ref, prev_ref, o_ref):
    del prev_ref
    _final_combine_kernel(x_ref, y0_ref, y1_ref, w_ref, gate_ref, ng_ref, o_ref)


def _final_combine(x, pending, final_g, tm, out, batch0, b_total):
    b, s, d = x.shape
    tile = lambda i, j: (i, j, 0)
    in_specs = [pl.BlockSpec((None, tm, d), tile)] + _pending_specs(pending, tm, d) + [
        pl.BlockSpec((1, d), lambda i, j: (0, 0))]
    args = [x, *pending, final_g]
    if out is not None:
        in_specs.append(pl.BlockSpec(memory_space=pl.ANY))
        args.append(out)
    return pl.pallas_call(
        _final_combine_kernel if out is None else _final_combine_kernel_into,
        out_shape=jax.ShapeDtypeStruct((b_total, s, d), F32),
        grid=(b, s // tm),
        in_specs=in_specs,
        out_specs=pl.BlockSpec((None, tm, d), lambda i, j: (i + batch0, j, 0)),
        input_output_aliases={} if out is None else {len(args) - 1: 0},
        compiler_params=_params("arbitrary", "arbitrary"),
        name="moe_final_combine",
    )(*args)


def _hier_moe(layer, x, routed, gate, w1, w3, w2, rows):
    b, s, d = x.shape
    t = b * s
    n_experts = w1.shape[1]
    h, wts, route, counts = routed
    counts = counts[0, :n_experts].astype(jnp.int32)
    route = route.astype(jnp.int32)
    row_tok, dest, items = _dispatch(route[0:TOP_K], route[TOP_K:2 * TOP_K], counts, rows)
    xs = h.reshape(t, d)[row_tok]
    ys = _experts(layer, items, xs, w1, w3, w2, rows)
    yy = ys[dest.reshape(-1)].reshape(TOP_K, b, s, d)
    return yy, yy, wts, gate


def _pick_tile(s, pref):
    tm = min(pref, s)
    assert s % tm == 0
    return tm


def kernel(x, c, positions, ada_w, ada_b, norm1_g, norm2_g, even_w_in, even_w_gate2, even_b_gate, even_gla_norm_g, even_conv_w, even_conv_b, even_conv_ln_g, even_conv_ln_b, even_w_out, odd_w_qkv, odd_w_out, moe_w_grp, moe_b_grp, moe_w_rt, moe_b_rt, moe_w1, moe_w3, moe_w2, final_norm_g):
    b, s, d = x.shape
    depth = ada_w.shape[0]
    n_experts = moe_w_rt.shape[-1]
    tm = _pick_tile(s, TOKEN_TILE)
    sb = _pick_tile(s, ATTN_SUPER)
    head_dim = d // ATTN_HEADS
    hk = GLA_HEADS * GLA_DK
    hv = GLA_HEADS * GLA_DV

    mods = _ada_mods(c, ada_w, ada_b)
    n_chains = BATCH_CHAINS if b % BATCH_CHAINS == 0 else 1
    bc = b // n_chains
    chains = [dict(x=x, batch0=ch * bc, pending=None, tables=None) for ch in range(n_chains)]

    for layer in range(depth):
        i = layer // 2
        g1 = norm1_g[layer][None, :]
        w_rt_full = jnp.concatenate([moe_w_rt[layer], moe_w_grp[layer],
                                     jnp.zeros((d, LANES - n_experts - N_GROUPS), F32)], axis=1).astype(BF16)
        b_rt_full = jnp.concatenate([moe_b_rt[layer], moe_b_grp[layer],
                                     jnp.zeros((LANES - n_experts - N_GROUPS,), F32)])[None, :]
        if layer % 2 == 0:
            w_in = even_w_in[i]
            main = hk + hk + hv + hv
            w_cat = jnp.concatenate([
                w_in[:, :main], w_in[:, main + GLA_GATE_RANK:], w_in[:, main:main + GLA_GATE_RANK],
                jnp.zeros((d, LANES - GLA_GATE_RANK), w_in.dtype)], axis=1).astype(BF16)
            wg = jnp.concatenate([even_w_gate2[i], jnp.zeros((LANES - GLA_GATE_RANK, hk), F32)], axis=0).astype(BF16)
            w_out = even_w_out[i].astype(BF16)
        else:
            w_qkv = odd_w_qkv[i].astype(BF16)
            w_out = odd_w_out[i].astype(BF16)

        for ch, st in enumerate(chains):
            seqs = slice(ch * bc, (ch + 1) * bc)
            mod = lambda j: mods[layer, j][seqs][:, None, :]
            xin, batch0 = st['x'], st['batch0']
            if layer % 2 == 0:
                x_new, o_gla, y_conv = _even_mixer(
                    xin, st['pending'], mod(0), mod(1), g1, w_cat, wg, even_b_gate[i][None, :],
                    even_gla_norm_g[i][None, :], even_conv_w[i], even_conv_b[i][None, :],
                    even_conv_ln_g[i][None, :], even_conv_ln_b[i][None, :], tm, batch0)
                acts = [o_gla, y_conv]
            else:
                assert batch0 == 0
                if st['tables'] is None:
                    st['tables'] = _rope_tables(positions[seqs], head_dim, tm)
                x_new, q, k, v = _qkv_rope(xin, st['pending'], mod(0), mod(1), g1, w_qkv,
                                           *st['tables'], head_dim, tm)
                acts = [_dilated_attention(q, k, v, head_dim, DILATED_BRANCHES, sb)]
            if st['pending'] is None:
                x_new = xin
            x_new, *routed = _out_proj_router(acts, w_out, x_new, mod(2), mod(3), mod(4), norm2_g[layer][None, :],
                                              w_rt_full, b_rt_full, n_experts, N_GROUPS,
                                              _pick_tile(s, ROUTER_TILE), batch0)
            st['x'], st['batch0'] = x_new, 0
            st['pending'] = _hier_moe(layer, x_new, routed, mod(5), moe_w1, moe_w3, moe_w2, MOE_ROWS)

    out = None
    for ch, st in enumerate(chains):
        out = _final_combine(st['x'], st['pending'], final_norm_g[None, :], tm, out, ch * bc, b)
    return out
```

```python
import functools

import jax
import jax.numpy as jnp
import numpy as np
from jax import lax
from jax.experimental import pallas as pl
from jax.experimental.pallas import tpu as pltpu

F32 = jnp.float32
BF16 = jnp.bfloat16
HIGHEST = lax.Precision.HIGHEST

NORM_EPS = 1e-6
GLA_HEADS = 4
GLA_DK = 64
GLA_DV = 128
GLA_GATE_RANK = 16
GLA_TAU = 16.0
GLA_CHUNK = 64
CONV_WIDTH = 31
ATTN_HEADS = 16
DILATED_BRANCHES = ((128, 1), (512, 4), (2048, 16))
DIL_BLOCK = 128
ROPE_THETA = 500000.0
N_GROUPS = 4
TOP_K = 2
ADA_CHUNKS = 6

LANES = 128
SUBLANES = 8
VMEM_LIMIT = 56 * 1024 * 1024
TOKEN_TILE = 512
ROUTER_TILE = 1024
BATCH_CHAINS = 2
ATTN_SUPER = 2048
MOE_ROWS = 512
NEG_BIG = -1e30


def _params(*sem):
    return pltpu.CompilerParams(dimension_semantics=sem, vmem_limit_bytes=VMEM_LIMIT)


def _silu(x):
    return x * jax.nn.sigmoid(x)


def _modulate(x, g, shift, scale):
    y = x * lax.rsqrt(jnp.mean(x * x, axis=-1, keepdims=True) + NORM_EPS)
    return (y * g) * (1.0 + scale) + shift


def _ada_kernel(c_ref, w_ref, b_ref, o_ref):
    cond = _silu(c_ref[...])
    o_ref[...] = jnp.dot(cond, w_ref[...], preferred_element_type=F32, precision=HIGHEST) + b_ref[...]


def _ada_mods(c, ada_w, ada_b):
    depth, d, _ = ada_w.shape
    b = c.shape[0]
    return pl.pallas_call(
        _ada_kernel,
        out_shape=jax.ShapeDtypeStruct((depth, ADA_CHUNKS, b, d), F32),
        grid=(depth, ADA_CHUNKS),
        in_specs=[
            pl.BlockSpec((b, d), lambda l, j: (0, 0)),
            pl.BlockSpec((None, d, d), lambda l, j: (l, 0, j)),
            pl.BlockSpec((None, None, 1, d), lambda l, j: (l, j, 0, 0)),
        ],
        out_specs=pl.BlockSpec((None, None, b, d), lambda l, j: (l, j, 0, 0)),
        compiler_params=_params("arbitrary", "arbitrary"),
        name="ada_mods",
    )(c, ada_w, ada_b.reshape(depth, ADA_CHUNKS, 1, d))


def _norm_matmul_kernel(x_ref, sh_ref, sc_ref, g_ref, w_ref, *o_refs):
    h = _modulate(x_ref[...], g_ref[...], sh_ref[...], sc_ref[...]).astype(BF16)
    off = 0
    for o_ref in o_refs:
        n = o_ref.shape[-1]
        o_ref[...] = jnp.dot(h, w_ref[:, off:off + n], preferred_element_type=F32).astype(o_ref.dtype)
        off += n


def _log_sigmoid(z):
    return jnp.minimum(z, 0.0) - jnp.log1p(jnp.exp(-jnp.abs(z)))


def _gla_kernel(q_ref, k_ref, v_ref, g_ref, a_ref, wg_ref, bg_ref, ng_ref, o_ref, state_ref, la_ref, oacc_ref):
    tm = q_ref.shape[0]
    c = GLA_CHUNK

    z = jnp.dot(a_ref[...].astype(BF16), wg_ref[...], preferred_element_type=F32) + bg_ref[...]
    la_ref[...] = _log_sigmoid(z) * (1.0 / GLA_TAU)

    ri = lax.broadcasted_iota(jnp.int32, (c, c), 0)
    ci = lax.broadcasted_iota(jnp.int32, (c, c), 1)
    causal = ri >= ci
    tril = jnp.where(causal, 1.0, 0.0).astype(BF16)
    hk = GLA_HEADS * GLA_DK

    nh = GLA_HEADS
    lane_head = lax.shift_right_logical(lax.broadcasted_iota(jnp.int32, (c, hk), 1), GLA_DK.bit_length() - 1)
    r4 = lax.broadcasted_iota(jnp.int32, (nh * c, nh * c), 0)
    c4 = lax.broadcasted_iota(jnp.int32, (nh * c, nh * c), 1)
    shift_c = c.bit_length() - 1
    causal4 = (lax.shift_right_logical(r4, shift_c) == lax.shift_right_logical(c4, shift_c)) & (r4 >= c4)

    def stack_heads(t):
        return jnp.concatenate([jnp.where(lane_head == h, t, 0.0) for h in range(nh)], axis=0).astype(BF16)

    state = state_ref[...]
    for ic in range(tm // c):
        rows = slice(ic * c, (ic + 1) * c)
        la = la_ref[rows, :]
        p0 = la.astype(BF16)
        r1 = la - p0.astype(F32)
        p1 = r1.astype(BF16)
        p2 = (r1 - p1.astype(F32)).astype(BF16)
        parts = jnp.dot(tril, jnp.concatenate([p0, p1, p2], axis=1), preferred_element_type=F32)
        bcum = (parts[:, 2 * hk:] + parts[:, hk:2 * hk]) + parts[:, :hk]
        b_last = bcum[c - 1:c, :]
        q = q_ref[rows, :] * (GLA_DK ** -0.5)
        k = k_ref[rows, :]
        q4 = stack_heads(q * jnp.exp(bcum))
        k4 = stack_heads(k * jnp.exp(-bcum))
        kr4 = stack_heads(k * jnp.exp(b_last - bcum))
        dec = jnp.exp(jnp.broadcast_to(b_last, (GLA_DV, hk)).T)
        v4 = jnp.concatenate([v_ref[rows, h * GLA_DV:(h + 1) * GLA_DV] for h in range(nh)], axis=0)
        att = lax.dot_general(q4, k4, (((1,), (1,)), ((), ())), preferred_element_type=F32)
        att = jnp.where(causal4, att, 0.0).astype(BF16)
        o4 = jnp.dot(att, v4, preferred_element_type=F32)
        o4 = o4 + jnp.dot(q4, state.astype(BF16), preferred_element_type=F32)
        kv = lax.dot_general(kr4, v4, (((0,), (0,)), ((), ())), preferred_element_type=F32)
        state = dec * state + kv
        oacc_ref[rows, :] = jnp.concatenate([o4[h * c:(h + 1) * c, :] for h in range(nh)], axis=1)
    state_ref[...] = state

    for h in range(GLA_HEADS):
        vs = slice(h * GLA_DV, (h + 1) * GLA_DV)
        o = oacc_ref[:, vs]
        o = o * lax.rsqrt(jnp.mean(o * o, axis=-1, keepdims=True) + NORM_EPS) * ng_ref[...]
        o_ref[:, vs] = (o * _silu(g_ref[:, vs])).astype(o_ref.dtype)


CONV_HALO = 32


def _conv_kernel(u_ref, w_ref, cb_ref, lg_ref, lb_ref, o_ref, buf_ref):
    tm = u_ref.shape[0]
    ch = o_ref.shape[-1]

    buf_ref[CONV_HALO:, :] = u_ref[:, :ch] * jax.nn.sigmoid(u_ref[:, ch:])
    base = CONV_HALO - (CONV_WIDTH - 1)
    acc = None
    for b in range(SUBLANES):
        part = None
        span = tm + (SUBLANES if b else 0)
        for a in range((base + CONV_WIDTH - 1) // SUBLANES + 1):
            j = SUBLANES * a + b - base
            if 0 <= j < CONV_WIDTH:
                term = buf_ref[SUBLANES * a:SUBLANES * a + span, :] * w_ref[j:j + 1, :]
                part = term if part is None else part + term
        if part is not None:
            part = part[b:b + tm, :]
            acc = part if acc is None else acc + part
    buf_ref[0:CONV_HALO, :] = buf_ref[tm:tm + CONV_HALO, :]
    y = acc + cb_ref[...]
    mu = jnp.mean(y, axis=-1, keepdims=True)
    var = jnp.mean(jnp.square(y - mu), axis=-1, keepdims=True)
    y = (y - mu) * lax.rsqrt(var + NORM_EPS) * lg_ref[...] + lb_ref[...]
    o_ref[...] = _silu(y).astype(o_ref.dtype)


N_PENDING = 4


def _apply_pending(x_ref, pending, xo_ref):
    y0_ref, y1_ref, w_ref, gate_ref = pending
    y = y0_ref[...].astype(F32) * w_ref[:, 0:1] + y1_ref[...].astype(F32) * w_ref[:, 1:2]
    xo_ref[...] = x_ref[...] + gate_ref[...] * y
    return xo_ref


def _pending_specs(pending, tm, d):
    tile = lambda i, j: (i, j, 0)
    return [pl.BlockSpec((None, None, tm, d), lambda i, j: (0, i, j, 0)),
            pl.BlockSpec((None, None, tm, d), lambda i, j: (1, i, j, 0)),
            pl.BlockSpec((None, tm, TOP_K), tile), pl.BlockSpec((None, 1, d), lambda i, j: (i, 0, 0))]


def _even_mixer_kernel(*refs, has_pending):
    x_ref, refs = refs[0], refs[1:]
    if has_pending:
        pending, refs = refs[:N_PENDING], refs[N_PENDING:]
    sh_ref, sc_ref, g1_ref, w_ref, wg_ref, bg_ref, ng_ref, cw_ref, cb_ref, lg_ref, lb_ref = refs[:11]
    refs = refs[11:]
    if has_pending:
        xo_ref, refs = refs[0], refs[1:]
    o_gla_ref, y_conv_ref, q_s, k_s, v_s, g_s, u_s, a_s, state_ref, la_ref, oacc_ref, buf_ref = refs

    @pl.when(pl.program_id(1) == 0)
    def _():
        state_ref[...] = jnp.zeros_like(state_ref)
        buf_ref[0:CONV_HALO, :] = jnp.zeros((CONV_HALO, buf_ref.shape[1]), F32)

    if has_pending:
        x_ref = _apply_pending(x_ref, pending, xo_ref)
    _norm_matmul_kernel(x_ref, sh_ref, sc_ref, g1_ref, w_ref, q_s, k_s, v_s, g_s, u_s, a_s)
    _gla_kernel(q_s, k_s, v_s, g_s, a_s, wg_ref, bg_ref, ng_ref, o_gla_ref, state_ref, la_ref, oacc_ref)
    _conv_kernel(u_s, cw_ref, cb_ref, lg_ref, lb_ref, y_conv_ref, buf_ref)


def _even_mixer(x, pending, shift, scale, g1, w_cat, w_gate2, b_gate, norm_g, conv_w, conv_b, ln_g, ln_b, tm,
                batch0=0):
    _, s, d = x.shape
    b = shift.shape[0]
    hk = GLA_HEADS * GLA_DK
    hv = GLA_HEADS * GLA_DV
    ch = conv_w.shape[-1]
    row = lambda i, j: (i, 0, 0)
    tile = lambda i, j: (i, j, 0)
    const = lambda i, j: (0, 0)
    full = lambda arr: pl.BlockSpec(arr.shape, const)
    pending = list(pending or ())
    x_out = [jax.ShapeDtypeStruct((b, s, d), F32)] if pending else []
    outs = pl.pallas_call(
        functools.partial(_even_mixer_kernel, has_pending=bool(pending)),
        out_shape=x_out + [jax.ShapeDtypeStruct((b, s, hv), BF16), jax.ShapeDtypeStruct((b, s, ch), BF16)],
        grid=(b, s // tm),
        in_specs=[pl.BlockSpec((None, tm, d), lambda i, j: (i + batch0, j, 0))] + (
            _pending_specs(pending, tm, d) if pending else []) + [
            pl.BlockSpec((None, 1, d), row),
            pl.BlockSpec((None, 1, d), row),
            full(g1), full(w_cat), full(w_gate2), full(b_gate), full(norm_g),
            full(conv_w), full(conv_b), full(ln_g), full(ln_b),
        ],
        out_specs=[pl.BlockSpec((None, tm, d), tile)] * len(x_out) + [
            pl.BlockSpec((None, tm, hv), tile), pl.BlockSpec((None, tm, ch), tile)],
        scratch_shapes=[
            pltpu.VMEM((tm, hk), F32),
            pltpu.VMEM((tm, hk), F32),
            pltpu.VMEM((tm, hv), BF16),
            pltpu.VMEM((tm, hv), F32),
            pltpu.VMEM((tm, 2 * ch), F32),
            pltpu.VMEM((tm, w_gate2.shape[0]), F32),
            pltpu.VMEM((GLA_HEADS * GLA_DK, GLA_DV), F32),
            pltpu.VMEM((tm, hk), F32),
            pltpu.VMEM((tm, hv), F32),
            pltpu.VMEM((tm + CONV_HALO, ch), F32),
        ],
        compiler_params=_params("arbitrary", "arbitrary"),
        name="even_mixer",
    )(x, *pending, shift, scale, g1, w_cat, w_gate2, b_gate, norm_g, conv_w, conv_b, ln_g, ln_b)
    return outs if pending else [x] + list(outs)


def _rope_table_kernel(pos_ref, freq_ref, sign_ref, cos_ref, sin_ref):
    ang = pos_ref[...] * freq_ref[...]
    cos_ref[...] = jnp.cos(ang)
    sin_ref[...] = jnp.sin(ang) * sign_ref[...]


def _rope_tables(positions, head_dim, tm):
    b, s = positions.shape
    rope_dims = head_dim // 4
    half = rope_dims // 2
    inv_freq = ROPE_THETA ** (-jnp.arange(0, rope_dims, 2, dtype=F32) / rope_dims)
    jj = jnp.arange(LANES) % head_dim
    freq = jnp.where(jj < rope_dims, inv_freq[jj % half], 0.0).astype(F32)[None, :]
    sign = jnp.where(jj < half, -1.0, jnp.where(jj < rope_dims, 1.0, 0.0)).astype(F32)[None, :]
    pos = positions.astype(F32)[..., None]
    tile = lambda i, j: (i, j, 0)
    const = lambda i, j: (0, 0)
    return pl.pallas_call(
        _rope_table_kernel,
        out_shape=[jax.ShapeDtypeStruct((b, s, LANES), F32)] * 2,
        grid=(b, s // tm),
        in_specs=[pl.BlockSpec((None, tm, 1), tile), pl.BlockSpec((1, LANES), const),
                  pl.BlockSpec((1, LANES), const)],
        out_specs=[pl.BlockSpec((None, tm, LANES), tile)] * 2,
        compiler_params=_params("arbitrary", "arbitrary"),
        name="rope_tables",
    )(pos, freq, sign)


def _qkv_kernel(*refs, head_dim, has_pending):
    x_ref, refs = refs[0], refs[1:]
    if has_pending:
        pending, refs = refs[:N_PENDING], refs[N_PENDING:]
    sh_ref, sc_ref, g_ref, w_ref, cos_ref, sin_ref = refs[:6]
    refs = refs[6:]
    if has_pending:
        x_ref, refs = _apply_pending(x_ref, pending, refs[0]), refs[1:]
    q_ref, k_ref, v_ref = refs
    h = _modulate(x_ref[...], g_ref[...], sh_ref[...], sc_ref[...]).astype(BF16)
    d = q_ref.shape[-1]
    half = (head_dim // 4) // 2
    cosf = jnp.tile(cos_ref[...], (1, d // LANES))
    sinf = jnp.tile(sin_ref[...], (1, d // LANES))
    lane = lax.broadcasted_iota(jnp.int32, (1, d), 1)
    first = (lane % head_dim) < half
    for idx, (o_ref, mult) in enumerate(((q_ref, head_dim ** -0.5), (k_ref, 1.0))):
        t = jnp.dot(h, w_ref[:, idx * d:(idx + 1) * d], preferred_element_type=F32)
        partner = jnp.where(first, pltpu.roll(t, d - half, 1), pltpu.roll(t, half, 1))
        o_ref[...] = ((t * cosf + partner * sinf) * mult).astype(o_ref.dtype)
    v_ref[...] = jnp.dot(h, w_ref[:, 2 * d:], preferred_element_type=F32).astype(v_ref.dtype)


def _qkv_rope(x, pending, shift, scale, g, w, cos_t, sin_t, head_dim, tm):
    b, s, d = x.shape
    row = lambda i, j: (i, 0, 0)
    tile = lambda i, j: (i, j, 0)
    pending = list(pending or ())
    x_out = [jax.ShapeDtypeStruct((b, s, d), F32)] if pending else []
    outs = pl.pallas_call(
        functools.partial(_qkv_kernel, head_dim=head_dim, has_pending=bool(pending)),
        out_shape=x_out + [jax.ShapeDtypeStruct((b, s, d), BF16)] * 3,
        grid=(b, s // tm),
        in_specs=[pl.BlockSpec((None, tm, d), tile)] + (_pending_specs(pending, tm, d) if pending else []) + [
            pl.BlockSpec((None, 1, d), row),
            pl.BlockSpec((None, 1, d), row),
            pl.BlockSpec((1, d), lambda i, j: (0, 0)),
            pl.BlockSpec(w.shape, lambda i, j: (0, 0)),
            pl.BlockSpec((None, tm, LANES), tile),
            pl.BlockSpec((None, tm, LANES), tile),
        ],
        out_specs=[pl.BlockSpec((None, tm, d), tile)] * (len(x_out) + 3),
        compiler_params=_params("arbitrary", "arbitrary"),
        name="qkv_rope",
    )(x, *pending, shift, scale, g, w, cos_t, sin_t)
    return outs if pending else [x] + list(outs)


ONES_ROWS = 16
ATTN_BLOCKS = ATTN_SUPER // DIL_BLOCK


def _attn_bias(branches, heads):
    blk = DIL_BLOCK
    kj = np.arange(2 * blk)[:, None]
    qi = np.arange(blk)[None, :]
    dist = qi + blk - kj
    out = []
    for window, dil in branches:
        band = (dist >= 0) & (dist <= window // dil)
        both = np.stack([band, band & (kj >= blk)])
        out.append(np.tile(np.where(both, 0.0, NEG_BIG), (1, 1, heads)))
    return jnp.asarray(np.stack(out), F32)


def _attn_kernel(bias_ref, q_ref, kc_ref, kp_ref, vc_ref, vp_ref, o_ref, qf, kf, vf, ob, lb, st_s, vt_s,
                 *, head_dim, branches):
    sb = q_ref.shape[0]
    blk = DIL_BLOCK
    heads = q_ref.shape[1] // head_dim
    first_super = pl.program_id(2) == 0
    slot0 = jnp.maximum(pl.program_id(2) - pl.num_programs(2), 0)

    qf[...] = q_ref[...].astype(F32)
    kf[0:sb, :] = kp_ref[...].astype(F32)
    kf[sb:, :] = kc_ref[...].astype(F32)
    vf[0:sb, :] = vp_ref[...].astype(F32)
    vf[sb:, :] = vc_ref[...].astype(F32)

    vt_s[:, LANES:, :] = jnp.ones((ATTN_BLOCKS, ONES_ROWS, 2 * blk), BF16)

    lane = lax.broadcasted_iota(jnp.int32, (blk, LANES), 1)
    head_masks = [(lane >= h * head_dim) & (lane < (h + 1) * head_dim) for h in range(heads)]

    for bi, (window, dil) in enumerate(branches):
        assert window // dil <= blk and sb == ATTN_SUPER and sb % (dil * blk) == 0
        unit = dil * blk

        def scores(j, q0, first_unit, dil=dil, unit=unit, bi=bi):
            k0 = sb + q0 - unit
            no_prev = jnp.where(first_super, 1, 0) if first_unit else 0
            qb = qf[pl.ds(q0, blk, stride=dil), :]
            q2 = jnp.concatenate([jnp.where(mk, qb, 0.0) for mk in head_masks], axis=0).astype(BF16)
            kb = kf[pl.ds(k0, 2 * blk, stride=dil), :].astype(BF16)
            vt_s[j, 0:LANES, :] = vf[pl.ds(k0, 2 * blk, stride=dil), :].T.astype(BF16)
            st = lax.dot_general(kb, q2, (((1,), (1,)), ((), ())), preferred_element_type=F32)
            st_s[slot0 + j] = st + bias_ref[bi, no_prev]

        def softmax_pv(j):
            m = jnp.max(st_s[slot0 + j], axis=0, keepdims=True)
            p = jnp.exp(st_s[slot0 + j] - m).astype(BF16)
            of = jnp.dot(vt_s[j], p, preferred_element_type=F32)
            l = of[LANES:LANES + 1, :]
            lse = m + jnp.log(l)
            o_rows, lse_rows = [], []
            for h in range(heads):
                cols = slice(h * blk, (h + 1) * blk)
                o_rows.append(of[h * head_dim:(h + 1) * head_dim, cols] / l[:, cols])
                lse_rows.append(jnp.broadcast_to(lse[:, cols], (head_dim, blk)))
            return jnp.concatenate(o_rows, axis=0).T, jnp.concatenate(lse_rows, axis=0).T

        starts = [(idx // dil) * unit + idx % dil for idx in range(ATTN_BLOCKS)]
        for j, q0 in enumerate(starts):
            scores(j, q0, q0 < unit)
        for j, q0 in enumerate(starts):
            o_tok, lse_tok = softmax_pv(j)
            ob[bi, pl.ds(q0, blk, stride=dil), :] = o_tok
            lb[bi, pl.ds(q0, blk, stride=dil), :] = lse_tok

    nb = len(branches)
    m = lb[0]
    for bi in range(1, nb):
        m = jnp.maximum(m, lb[bi])
    num = jnp.zeros_like(m)
    den = jnp.zeros_like(m)
    for bi in range(nb):
        e = jnp.exp(lb[bi] - m)
        num = num + e * ob[bi]
        den = den + e
    o_ref[...] = (num / den).astype(o_ref.dtype)


def _dilated_attention(q, k, v, head_dim, branches, sb):
    b, s, d = q.shape
    groups = d // LANES
    cur = lambda i, g, n: (i, n, g)
    prev = lambda i, g, n: (i, jnp.maximum(n - 1, 0), g)
    nb = len(branches)
    bias = _attn_bias(branches, LANES // head_dim)
    return pl.pallas_call(
        functools.partial(_attn_kernel, head_dim=head_dim, branches=branches),
        out_shape=jax.ShapeDtypeStruct((b, s, d), BF16),
        grid=(b, groups, s // sb),
        in_specs=[
            pl.BlockSpec(bias.shape, lambda i, g, n: (0, 0, 0, 0)),
            pl.BlockSpec((None, sb, LANES), cur),
            pl.BlockSpec((None, sb, LANES), cur),
            pl.BlockSpec((None, sb, LANES), prev),
            pl.BlockSpec((None, sb, LANES), cur),
            pl.BlockSpec((None, sb, LANES), prev),
        ],
        out_specs=pl.BlockSpec((None, sb, LANES), cur),
        scratch_shapes=[
            pltpu.VMEM((sb, LANES), F32),
            pltpu.VMEM((2 * sb, LANES), F32),
            pltpu.VMEM((2 * sb, LANES), F32),
            pltpu.VMEM((nb, sb, LANES), F32),
            pltpu.VMEM((nb, sb, LANES), F32),
            pltpu.VMEM((ATTN_BLOCKS, 2 * DIL_BLOCK, LANES // head_dim * DIL_BLOCK), F32),
            pltpu.VMEM((ATTN_BLOCKS, LANES + ONES_ROWS, 2 * DIL_BLOCK), BF16),
        ],
        compiler_params=_params("arbitrary", "arbitrary", "arbitrary"),
        name="dilated_attention",
    )(bias, q, k, k, v, v)


def _route_tile(x, sh_ref, sc_ref, g_ref, w_ref, b_ref, h_ref, wts_ref, route_ref, counts_ref, carry_ref,
                n_experts, n_groups):
    h = _modulate(x, g_ref[...], sh_ref[...], sc_ref[...]).astype(BF16)
    h_ref[...] = h
    logits = jnp.dot(h, w_ref[...], preferred_element_type=F32) + b_ref[...]
    tm = logits.shape[0]
    epg = n_experts // n_groups
    lane = lax.broadcasted_iota(jnp.int32, logits.shape, 1)
    neg = -jnp.inf
    big = jnp.int32(LANES)

    def first_max(mask):
        val = jnp.max(jnp.where(mask, logits, neg), axis=-1, keepdims=True)
        idx = jnp.min(jnp.where(mask & (logits == val), lane, big), axis=-1, keepdims=True)
        return val, idx

    gmask = (lane >= n_experts) & (lane < n_experts + n_groups)
    gmax, gidx = first_max(gmask)
    gsum = jnp.sum(jnp.where(gmask, jnp.exp(logits - gmax), 0.0), axis=-1, keepdims=True)
    g_w = 1.0 / gsum
    grp = gidx - n_experts
    assert epg & (epg - 1) == 0
    emask = (lane < n_experts) & (lax.shift_right_logical(lane, epg.bit_length() - 1) == grp)
    v1, i1 = first_max(emask)
    v2, i2 = first_max(emask & (lane != i1))
    e2 = jnp.exp(v2 - v1)
    den = 1.0 + e2
    col = lax.broadcasted_iota(jnp.int32, (tm, TOP_K), 1)
    wts_ref[...] = jnp.where(col == 0, 1.0 / den, e2 / den) * g_w

    hit1 = lane == i1
    hit2 = lane == i2
    onehot = jnp.where(hit1 | hit2, 1.0, 0.0)
    ri = lax.broadcasted_iota(jnp.int32, (tm, tm), 0)
    ci = lax.broadcasted_iota(jnp.int32, (tm, tm), 1)
    before = jnp.where(ci < ri, 1.0, 0.0).astype(BF16)
    prefix = jnp.dot(before, onehot.astype(BF16), preferred_element_type=F32) + carry_ref[0:1, :]
    r1 = jnp.sum(jnp.where(hit1, prefix, 0.0), axis=-1, keepdims=True)
    r2 = jnp.sum(jnp.where(hit2, prefix, 0.0), axis=-1, keepdims=True)
    packed = jnp.where(lane == 0, i1.astype(F32), jnp.where(lane == 1, i2.astype(F32),
                       jnp.where(lane == 2, r1, jnp.where(lane == 3, r2, 0.0))))
    route_ref[...] = packed.T[0:SUBLANES, :]
    carry_ref[...] = carry_ref[...] + jnp.sum(onehot, axis=0, keepdims=True)
    counts_ref[...] = carry_ref[...]


def _out_proj_router_kernel(*refs, n_acts, n_experts, n_groups):
    a_refs = refs[:n_acts]
    w_ref, x_ref, gate_ref, sh_ref, sc_ref, g_ref, wr_ref, br_ref = refs[n_acts:n_acts + 8]
    xo_ref, h_ref, wts_ref, route_ref, counts_ref, carry_ref = refs[n_acts + 8:]

    @pl.when((pl.program_id(0) == 0) & (pl.program_id(1) == 0))
    def _():
        carry_ref[...] = jnp.zeros_like(carry_ref)

    acc = None
    off = 0
    for a_ref in a_refs:
        kk = a_ref.shape[-1]
        part = jnp.dot(a_ref[...].astype(BF16), w_ref[off:off + kk, :], preferred_element_type=F32)
        acc = part if acc is None else acc + part
        off += kk
    x = x_ref[...] + gate_ref[...] * acc
    xo_ref[...] = x
    _route_tile(x, sh_ref, sc_ref, g_ref, wr_ref, br_ref, h_ref, wts_ref, route_ref, counts_ref, carry_ref,
                n_experts, n_groups)


def _out_proj_router(acts, w, x, gate, shift, scale, g, w_rt, b_rt, n_experts, n_groups, tm, batch0=0):
    _, s, d = x.shape
    b = acts[0].shape[0]
    row = lambda i, j: (i, 0, 0)
    tile = lambda i, j: (i, j, 0)
    const = lambda i, j: (0, 0)
    return pl.pallas_call(
        functools.partial(_out_proj_router_kernel, n_acts=len(acts), n_experts=n_experts, n_groups=n_groups),
        out_shape=[jax.ShapeDtypeStruct((b, s, d), F32),
                   jax.ShapeDtypeStruct((b, s, d), BF16),
                   jax.ShapeDtypeStruct((b, s, TOP_K), F32),
                   jax.ShapeDtypeStruct((SUBLANES, b * s), F32),
                   jax.ShapeDtypeStruct((SUBLANES, LANES), F32)],
        grid=(b, s // tm),
        in_specs=[pl.BlockSpec((None, tm, a.shape[-1]), tile) for a in acts] + [
            pl.BlockSpec(w.shape, const),
            pl.BlockSpec((None, tm, d), lambda i, j: (i + batch0, j, 0)),
            pl.BlockSpec((None, 1, d), row),
            pl.BlockSpec((None, 1, d), row),
            pl.BlockSpec((None, 1, d), row),
            pl.BlockSpec((1, d), const),
            pl.BlockSpec((d, LANES), const),
            pl.BlockSpec((1, LANES), const),
        ],
        out_specs=[pl.BlockSpec((None, tm, d), tile),
                   pl.BlockSpec((None, tm, d), tile),
                   pl.BlockSpec((None, tm, TOP_K), tile),
                   pl.BlockSpec((SUBLANES, tm), lambda i, j: (0, i * (s // tm) + j)),
                   pl.BlockSpec((SUBLANES, LANES), const)],
        scratch_shapes=[pltpu.VMEM((SUBLANES, LANES), F32)],
        compiler_params=_params("arbitrary", "arbitrary"),
        name="out_proj_router",
    )(*acts, w, x, gate, shift, scale, g, w_rt, b_rt)


def _expert_kernel(ib_ref, ie_ref, lo_ref, hi_ref, xs_ref, w1_ref, w3_ref, w2_ref, ys_ref, w1b, w3b, w2b):
    j = pl.program_id(0)
    prev = jnp.maximum(j - 1, 0)
    e_changed = (j == 0) | (ie_ref[j] != ie_ref[prev])
    first_of_block = (j == 0) | (ib_ref[j] != ib_ref[prev])
    lo = lo_ref[j]
    hi = hi_ref[j]

    @pl.when(e_changed)
    def _():
        w1b[...] = w1_ref[...].astype(BF16)
        w3b[...] = w3_ref[...].astype(BF16)
        w2b[...] = w2_ref[...].astype(BF16)

    rows = ys_ref.shape[0]
    whole = (lo == 0) & (hi == rows)

    @pl.when(first_of_block & jnp.logical_not(whole))
    def _():
        ys_ref[...] = jnp.zeros_like(ys_ref)

    @pl.when(hi > lo)
    def _():
        x = xs_ref[...]
        a = jnp.dot(x, w1b[...], preferred_element_type=F32)
        g = jnp.dot(x, w3b[...], preferred_element_type=F32)
        y = jnp.dot((_silu(a) * g).astype(BF16), w2b[...], preferred_element_type=F32).astype(ys_ref.dtype)

        @pl.when(whole)
        def _():
            ys_ref[...] = y

        @pl.when(jnp.logical_not(whole))
        def _():
            row = lax.broadcasted_iota(jnp.int32, (rows, 1), 0)
            ys_ref[...] = jnp.where((row >= lo) & (row < hi), y, ys_ref[...])


def _experts(layer, items, xs, w1, w3, w2, rows):
    a, d = xs.shape
    hid = w1.shape[-1]
    blk = lambda j, ib, ie, lo, hi: (ib[j], 0)
    wsel = lambda j, ib, ie, lo, hi: (layer, ie[j], 0, 0)
    grid_spec = pltpu.PrefetchScalarGridSpec(
        num_scalar_prefetch=4,
        grid=(items[0].shape[0],),
        in_specs=[
            pl.BlockSpec((rows, d), blk),
            pl.BlockSpec((None, None, d, hid), wsel),
            pl.BlockSpec((None, None, d, hid), wsel),
            pl.BlockSpec((None, None, hid, d), wsel),
        ],
        out_specs=pl.BlockSpec((rows, d), blk),
        scratch_shapes=[pltpu.VMEM((d, hid), BF16), pltpu.VMEM((d, hid), BF16), pltpu.VMEM((hid, d), BF16)],
    )
    return pl.pallas_call(
        _expert_kernel,
        out_shape=jax.ShapeDtypeStruct((a, d), BF16),
        grid_spec=grid_spec,
        compiler_params=_params("arbitrary"),
        name="moe_experts",
    )(*items, xs, w1, w3, w2)


DEST_TILE = 8192


def _dest_kernel(route_ref, starts_ref, dest_ref, *, n_experts):
    ids = route_ref[0:TOP_K, :]
    dest = route_ref[TOP_K:2 * TOP_K, :]
    for e in range(n_experts):
        dest = dest + jnp.where(ids == float(e), starts_ref[0:1, e:e + 1], 0.0)
    dest_ref[...] = dest.astype(jnp.int32)


def _dest_rows(route, starts):
    t = route.shape[1]
    tt = _pick_tile(t, DEST_TILE)
    n_experts = starts.shape[0]
    assert 2 * TOP_K <= SUBLANES and TOP_K * t < 2 ** 24 and n_experts <= LANES
    starts_row = jnp.pad(starts.astype(F32), (0, LANES - n_experts))[None, :]
    return pl.pallas_call(
        functools.partial(_dest_kernel, n_experts=n_experts),
        out_shape=jax.ShapeDtypeStruct((TOP_K, t), jnp.int32),
        grid=(t // tt,),
        in_specs=[pl.BlockSpec((SUBLANES, tt), lambda i: (0, i)), pl.BlockSpec((1, LANES), lambda i: (0, 0))],
        out_specs=pl.BlockSpec((TOP_K, tt), lambda i: (0, i)),
        compiler_params=_params("arbitrary"),
        name="moe_dest",
    )(route, starts_row)


def _dispatch(route, counts, rows):
    n_experts = counts.shape[0]
    t = route.shape[1]
    a = TOP_K * t
    i32 = jnp.int32
    ends = jnp.cumsum(counts)
    starts = ends - counts
    dest = _dest_rows(route, starts)
    tok = jnp.tile(jnp.arange(t, dtype=i32), TOP_K)
    row_tok = lax.sort_key_val(dest.reshape(-1), tok)[1]
    n_blk = a // rows
    bstart = jnp.arange(n_blk, dtype=i32) * rows
    count_le = lambda bounds, x: jnp.sum((bounds[None, :] <= x[:, None]).astype(i32), axis=1)
    e_lo = jnp.minimum(count_le(ends, bstart), n_experts - 1)
    e_hi = jnp.minimum(count_le(ends, bstart + rows - 1), n_experts - 1)
    n_items = e_hi - e_lo + 1
    item_end = jnp.cumsum(n_items)
    item_first = item_end - n_items
    jj = jnp.arange(n_blk + n_experts - 1, dtype=i32)
    valid = jj < item_end[-1]
    ib = jnp.minimum(count_le(item_end, jj), n_blk - 1)
    ie = jnp.where(valid, jnp.clip(e_lo[ib] + jj - item_first[ib], 0, n_experts - 1), e_hi[n_blk - 1]).astype(i32)
    lo = jnp.where(valid, jnp.clip(starts[ie] - ib * rows, 0, rows), 0).astype(i32)
    hi = jnp.where(valid, jnp.clip(ends[ie] - ib * rows, 0, rows), 0).astype(i32)
    return row_tok, dest, (ib, ie, lo, hi)


def _final_combine_kernel(x_ref, y0_ref, y1_ref, w_ref, gate_ref, ng_ref, o_ref):
    x = _apply_pending(x_ref, (y0_ref, y1_ref, w_ref, gate_ref), o_ref)[...]
    o_ref[...] = x * lax.rsqrt(jnp.mean(x * x, axis=-1, keepdims=True) + NORM_EPS) * ng_ref[...]


def _final_combine_kernel_into(x_ref, y0_ref, y1_ref, w_ref, gate_ref, ng_ref, prev_ref, o_ref):
    del prev_ref
    _final_combine_kernel(x_ref, y0_ref, y1_ref, w_ref, gate_ref, ng_ref, o_ref)


def _final_combine(x, pending, final_g, tm, out, batch0, b_total):
    b, s, d = x.shape
    tile = lambda i, j: (i, j, 0)
    in_specs = [pl.BlockSpec((None, tm, d), tile)] + _pending_specs(pending, tm, d) + [
        pl.BlockSpec((1, d), lambda i, j: (0, 0))]
    args = [x, *pending, final_g]
    if out is not None:
        in_specs.append(pl.BlockSpec(memory_space=pl.ANY))
        args.append(out)
    return pl.pallas_call(
        _final_combine_kernel if out is None else _final_combine_kernel_into,
        out_shape=jax.ShapeDtypeStruct((b_total, s, d), F32),
        grid=(b, s // tm),
        in_specs=in_specs,
        out_specs=pl.BlockSpec((None, tm, d), lambda i, j: (i + batch0, j, 0)),
        input_output_aliases={} if out is None else {len(args) - 1: 0},
        compiler_params=_params("arbitrary", "arbitrary"),
        name="moe_final_combine",
    )(*args)


def _hier_moe(layer, x, routed, gate, w1, w3, w2, rows):
    b, s, d = x.shape
    t = b * s
    n_experts = w1.shape[1]
    h, wts, route, counts = routed
    counts = counts[0, :n_experts].astype(jnp.int32)
    row_tok, dest, items = _dispatch(route, counts, rows)
    xs = h.reshape(t, d)[row_tok]
    ys = _experts(layer, items, xs, w1, w3, w2, rows)
    yy = ys[dest.reshape(-1)].reshape(TOP_K, b, s, d)
    return yy, yy, wts, gate


def _pick_tile(s, pref):
    tm = min(pref, s)
    assert s % tm == 0
    return tm


def kernel(x, c, positions, ada_w, ada_b, norm1_g, norm2_g, even_w_in, even_w_gate2, even_b_gate, even_gla_norm_g, even_conv_w, even_conv_b, even_conv_ln_g, even_conv_ln_b, even_w_out, odd_w_qkv, odd_w_out, moe_w_grp, moe_b_grp, moe_w_rt, moe_b_rt, moe_w1, moe_w3, moe_w2, final_norm_g):
    b, s, d = x.shape
    depth = ada_w.shape[0]
    n_experts = moe_w_rt.shape[-1]
    tm = _pick_tile(s, TOKEN_TILE)
    sb = _pick_tile(s, ATTN_SUPER)
    head_dim = d // ATTN_HEADS
    hk = GLA_HEADS * GLA_DK
    hv = GLA_HEADS * GLA_DV

    mods = _ada_mods(c, ada_w, ada_b)
    n_chains = BATCH_CHAINS if b % BATCH_CHAINS == 0 else 1
    bc = b // n_chains
    chains = [dict(x=x, batch0=ch * bc, pending=None, tables=None) for ch in range(n_chains)]

    for layer in range(depth):
        i = layer // 2
        g1 = norm1_g[layer][None, :]
        w_rt_full = jnp.concatenate([moe_w_rt[layer], moe_w_grp[layer],
                                     jnp.zeros((d, LANES - n_experts - N_GROUPS), F32)], axis=1).astype(BF16)
        b_rt_full = jnp.concatenate([moe_b_rt[layer], moe_b_grp[layer],
                                     jnp.zeros((LANES - n_experts - N_GROUPS,), F32)])[None, :]
        if layer % 2 == 0:
            w_in = even_w_in[i]
            main = hk + hk + hv + hv
            w_cat = jnp.concatenate([
                w_in[:, :main], w_in[:, main + GLA_GATE_RANK:], w_in[:, main:main + GLA_GATE_RANK],
                jnp.zeros((d, LANES - GLA_GATE_RANK), w_in.dtype)], axis=1).astype(BF16)
            wg = jnp.concatenate([even_w_gate2[i], jnp.zeros((LANES - GLA_GATE_RANK, hk), F32)], axis=0).astype(BF16)
            w_out = even_w_out[i].astype(BF16)
        else:
            w_qkv = odd_w_qkv[i].astype(BF16)
            w_out = odd_w_out[i].astype(BF16)

        for ch, st in enumerate(chains):
            seqs = slice(ch * bc, (ch + 1) * bc)
            mod = lambda j: mods[layer, j][seqs][:, None, :]
            xin, batch0 = st['x'], st['batch0']
            if layer % 2 == 0:
                x_new, o_gla, y_conv = _even_mixer(
                    xin, st['pending'], mod(0), mod(1), g1, w_cat, wg, even_b_gate[i][None, :],
                    even_gla_norm_g[i][None, :], even_conv_w[i], even_conv_b[i][None, :],
                    even_conv_ln_g[i][None, :], even_conv_ln_b[i][None, :], tm, batch0)
                acts = [o_gla, y_conv]
            else:
                assert batch0 == 0
                if st['tables'] is None:
                    st['tables'] = _rope_tables(positions[seqs], head_dim, tm)
                x_new, q, k, v = _qkv_rope(xin, st['pending'], mod(0), mod(1), g1, w_qkv,
                                           *st['tables'], head_dim, tm)
                acts = [_dilated_attention(q, k, v, head_dim, DILATED_BRANCHES, sb)]
            if st['pending'] is None:
                x_new = xin
            x_new, *routed = _out_proj_router(acts, w_out, x_new, mod(2), mod(3), mod(4), norm2_g[layer][None, :],
                                              w_rt_full, b_rt_full, n_experts, N_GROUPS,
                                              _pick_tile(s, ROUTER_TILE), batch0)
            st['x'], st['batch0'] = x_new, 0
            st['pending'] = _hier_moe(layer, x_new, routed, mod(5), moe_w1, moe_w3, moe_w2, MOE_ROWS)

    out = None
    for ch, st in enumerate(chains):
        out = _final_combine(st['x'], st['pending'], final_norm_g[None, :], tm, out, ch * bc, b)
    return out
```

```python
import functools

import jax
import jax.numpy as jnp
import numpy as np
from jax import lax
from jax.experimental import pallas as pl
from jax.experimental.pallas import tpu as pltpu

F32 = jnp.float32
BF16 = jnp.bfloat16
HIGHEST = lax.Precision.HIGHEST

NORM_EPS = 1e-6
GLA_HEADS = 4
GLA_DK = 64
GLA_DV = 128
GLA_GATE_RANK = 16
GLA_TAU = 16.0
GLA_CHUNK = 64
CONV_WIDTH = 31
ATTN_HEADS = 16
DILATED_BRANCHES = ((128, 1), (512, 4), (2048, 16))
DIL_BLOCK = 128
ROPE_THETA = 500000.0
N_GROUPS = 4
TOP_K = 2
ADA_CHUNKS = 6

LANES = 128
SUBLANES = 8
VMEM_LIMIT = 56 * 1024 * 1024
TOKEN_TILE = 512
ROUTER_TILE = 1024
BATCH_CHAINS = 2
ATTN_SUPER = 2048
MOE_ROWS = 512
NEG_BIG = -1e30


def _params(*sem):
    return pltpu.CompilerParams(dimension_semantics=sem, vmem_limit_bytes=VMEM_LIMIT)


def _silu(x):
    return x * jax.nn.sigmoid(x)


def _modulate(x, g, shift, scale):
    y = x * lax.rsqrt(jnp.mean(x * x, axis=-1, keepdims=True) + NORM_EPS)
    return (y * g) * (1.0 + scale) + shift


def _ada_kernel(c_ref, w_ref, b_ref, o_ref):
    cond = _silu(c_ref[...])
    o_ref[...] = jnp.dot(cond, w_ref[...], preferred_element_type=F32, precision=HIGHEST) + b_ref[...]


def _ada_mods(c, ada_w, ada_b):
    depth, d, _ = ada_w.shape
    b = c.shape[0]
    return pl.pallas_call(
        _ada_kernel,
        out_shape=jax.ShapeDtypeStruct((depth, ADA_CHUNKS, b, d), F32),
        grid=(depth, ADA_CHUNKS),
        in_specs=[
            pl.BlockSpec((b, d), lambda l, j: (0, 0)),
            pl.BlockSpec((None, d, d), lambda l, j: (l, 0, j)),
            pl.BlockSpec((None, None, 1, d), lambda l, j: (l, j, 0, 0)),
        ],
        out_specs=pl.BlockSpec((None, None, b, d), lambda l, j: (l, j, 0, 0)),
        compiler_params=_params("arbitrary", "arbitrary"),
        name="ada_mods",
    )(c, ada_w, ada_b.reshape(depth, ADA_CHUNKS, 1, d))


def _norm_matmul_kernel(x_ref, sh_ref, sc_ref, g_ref, w_ref, *o_refs):
    h = _modulate(x_ref[...], g_ref[...], sh_ref[...], sc_ref[...]).astype(BF16)
    off = 0
    for o_ref in o_refs:
        n = o_ref.shape[-1]
        o_ref[...] = jnp.dot(h, w_ref[:, off:off + n], preferred_element_type=F32).astype(o_ref.dtype)
        off += n


def _log_sigmoid(z):
    return jnp.minimum(z, 0.0) - jnp.log1p(jnp.exp(-jnp.abs(z)))


def _gla_kernel(q_ref, k_ref, v_ref, g_ref, a_ref, wg_ref, bg_ref, ng_ref, o_ref, state_ref, la_ref, oacc_ref):
    tm = q_ref.shape[0]
    c = GLA_CHUNK

    z = jnp.dot(a_ref[...].astype(BF16), wg_ref[...], preferred_element_type=F32) + bg_ref[...]
    la_ref[...] = _log_sigmoid(z) * (1.0 / GLA_TAU)

    ri = lax.broadcasted_iota(jnp.int32, (c, c), 0)
    ci = lax.broadcasted_iota(jnp.int32, (c, c), 1)
    causal = ri >= ci
    tril = jnp.where(causal, 1.0, 0.0).astype(BF16)
    hk = GLA_HEADS * GLA_DK

    nh = GLA_HEADS
    lane_head = lax.shift_right_logical(lax.broadcasted_iota(jnp.int32, (c, hk), 1), GLA_DK.bit_length() - 1)
    r4 = lax.broadcasted_iota(jnp.int32, (nh * c, nh * c), 0)
    c4 = lax.broadcasted_iota(jnp.int32, (nh * c, nh * c), 1)
    shift_c = c.bit_length() - 1
    causal4 = (lax.shift_right_logical(r4, shift_c) == lax.shift_right_logical(c4, shift_c)) & (r4 >= c4)

    def stack_heads(t):
        return jnp.concatenate([jnp.where(lane_head == h, t, 0.0) for h in range(nh)], axis=0).astype(BF16)

    state = state_ref[...]
    for ic in range(tm // c):
        rows = slice(ic * c, (ic + 1) * c)
        la = la_ref[rows, :]
        p0 = la.astype(BF16)
        r1 = la - p0.astype(F32)
        p1 = r1.astype(BF16)
        p2 = (r1 - p1.astype(F32)).astype(BF16)
        parts = jnp.dot(tril, jnp.concatenate([p0, p1, p2], axis=1), preferred_element_type=F32)
        bcum = (parts[:, 2 * hk:] + parts[:, hk:2 * hk]) + parts[:, :hk]
        b_last = bcum[c - 1:c, :]
        q = q_ref[rows, :] * (GLA_DK ** -0.5)
        k = k_ref[rows, :]
        q4 = stack_heads(q * jnp.exp(bcum))
        k4 = stack_heads(k * jnp.exp(-bcum))
        kr4 = stack_heads(k * jnp.exp(b_last - bcum))
        dec = jnp.exp(jnp.broadcast_to(b_last, (GLA_DV, hk)).T)
        v4 = jnp.concatenate([v_ref[rows, h * GLA_DV:(h + 1) * GLA_DV] for h in range(nh)], axis=0)
        att = lax.dot_general(q4, k4, (((1,), (1,)), ((), ())), preferred_element_type=F32)
        att = jnp.where(causal4, att, 0.0).astype(BF16)
        o4 = jnp.dot(att, v4, preferred_element_type=F32)
        o4 = o4 + jnp.dot(q4, state.astype(BF16), preferred_element_type=F32)
        kv = lax.dot_general(kr4, v4, (((0,), (0,)), ((), ())), preferred_element_type=F32)
        state = dec * state + kv
        oacc_ref[rows, :] = jnp.concatenate([o4[h * c:(h + 1) * c, :] for h in range(nh)], axis=1)
    state_ref[...] = state

    for h in range(GLA_HEADS):
        vs = slice(h * GLA_DV, (h + 1) * GLA_DV)
        o = oacc_ref[:, vs]
        o = o * lax.rsqrt(jnp.mean(o * o, axis=-1, keepdims=True) + NORM_EPS) * ng_ref[...]
        o_ref[:, vs] = (o * _silu(g_ref[:, vs])).astype(o_ref.dtype)


CONV_HALO = 32


def _conv_kernel(u_ref, w_ref, cb_ref, lg_ref, lb_ref, o_ref, buf_ref):
    tm = u_ref.shape[0]
    ch = o_ref.shape[-1]

    buf_ref[CONV_HALO:, :] = u_ref[:, :ch] * jax.nn.sigmoid(u_ref[:, ch:])
    base = CONV_HALO - (CONV_WIDTH - 1)
    acc = None
    for b in range(SUBLANES):
        part = None
        span = tm + (SUBLANES if b else 0)
        for a in range((base + CONV_WIDTH - 1) // SUBLANES + 1):
            j = SUBLANES * a + b - base
            if 0 <= j < CONV_WIDTH:
                term = buf_ref[SUBLANES * a:SUBLANES * a + span, :] * w_ref[j:j + 1, :]
                part = term if part is None else part + term
        if part is not None:
            part = part[b:b + tm, :]
            acc = part if acc is None else acc + part
    buf_ref[0:CONV_HALO, :] = buf_ref[tm:tm + CONV_HALO, :]
    y = acc + cb_ref[...]
    mu = jnp.mean(y, axis=-1, keepdims=True)
    var = jnp.mean(jnp.square(y - mu), axis=-1, keepdims=True)
    y = (y - mu) * lax.rsqrt(var + NORM_EPS) * lg_ref[...] + lb_ref[...]
    o_ref[...] = _silu(y).astype(o_ref.dtype)


N_PENDING = 4


def _apply_pending(x_ref, pending, xo_ref):
    y0_ref, y1_ref, w_ref, gate_ref = pending
    y = y0_ref[...].astype(F32) * w_ref[:, 0:1] + y1_ref[...].astype(F32) * w_ref[:, 1:2]
    xo_ref[...] = x_ref[...] + gate_ref[...] * y
    return xo_ref


def _pending_specs(pending, tm, d):
    tile = lambda i, j: (i, j, 0)
    return [pl.BlockSpec((None, None, tm, d), lambda i, j: (0, i, j, 0)),
            pl.BlockSpec((None, None, tm, d), lambda i, j: (1, i, j, 0)),
            pl.BlockSpec((None, tm, TOP_K), tile), pl.BlockSpec((None, 1, d), lambda i, j: (i, 0, 0))]


def _even_mixer_kernel(*refs, has_pending):
    x_ref, refs = refs[0], refs[1:]
    if has_pending:
        pending, refs = refs[:N_PENDING], refs[N_PENDING:]
    sh_ref, sc_ref, g1_ref, w_ref, wg_ref, bg_ref, ng_ref, cw_ref, cb_ref, lg_ref, lb_ref = refs[:11]
    refs = refs[11:]
    if has_pending:
        xo_ref, refs = refs[0], refs[1:]
    o_gla_ref, y_conv_ref, q_s, k_s, v_s, g_s, u_s, a_s, state_ref, la_ref, oacc_ref, buf_ref = refs

    @pl.when(pl.program_id(1) == 0)
    def _():
        state_ref[...] = jnp.zeros_like(state_ref)
        buf_ref[0:CONV_HALO, :] = jnp.zeros((CONV_HALO, buf_ref.shape[1]), F32)

    if has_pending:
        x_ref = _apply_pending(x_ref, pending, xo_ref)
    _norm_matmul_kernel(x_ref, sh_ref, sc_ref, g1_ref, w_ref, q_s, k_s, v_s, g_s, u_s, a_s)
    _gla_kernel(q_s, k_s, v_s, g_s, a_s, wg_ref, bg_ref, ng_ref, o_gla_ref, state_ref, la_ref, oacc_ref)
    _conv_kernel(u_s, cw_ref, cb_ref, lg_ref, lb_ref, y_conv_ref, buf_ref)


def _even_mixer(x, pending, shift, scale, g1, w_cat, w_gate2, b_gate, norm_g, conv_w, conv_b, ln_g, ln_b, tm,
                batch0=0):
    _, s, d = x.shape
    b = shift.shape[0]
    hk = GLA_HEADS * GLA_DK
    hv = GLA_HEADS * GLA_DV
    ch = conv_w.shape[-1]
    row = lambda i, j: (i, 0, 0)
    tile = lambda i, j: (i, j, 0)
    const = lambda i, j: (0, 0)
    full = lambda arr: pl.BlockSpec(arr.shape, const)
    pending = list(pending or ())
    x_out = [jax.ShapeDtypeStruct((b, s, d), F32)] if pending else []
    outs = pl.pallas_call(
        functools.partial(_even_mixer_kernel, has_pending=bool(pending)),
        out_shape=x_out + [jax.ShapeDtypeStruct((b, s, hv), BF16), jax.ShapeDtypeStruct((b, s, ch), BF16)],
        grid=(b, s // tm),
        in_specs=[pl.BlockSpec((None, tm, d), lambda i, j: (i + batch0, j, 0))] + (
            _pending_specs(pending, tm, d) if pending else []) + [
            pl.BlockSpec((None, 1, d), row),
            pl.BlockSpec((None, 1, d), row),
            full(g1), full(w_cat), full(w_gate2), full(b_gate), full(norm_g),
            full(conv_w), full(conv_b), full(ln_g), full(ln_b),
        ],
        out_specs=[pl.BlockSpec((None, tm, d), tile)] * len(x_out) + [
            pl.BlockSpec((None, tm, hv), tile), pl.BlockSpec((None, tm, ch), tile)],
        scratch_shapes=[
            pltpu.VMEM((tm, hk), F32),
            pltpu.VMEM((tm, hk), F32),
            pltpu.VMEM((tm, hv), BF16),
            pltpu.VMEM((tm, hv), F32),
            pltpu.VMEM((tm, 2 * ch), F32),
            pltpu.VMEM((tm, w_gate2.shape[0]), F32),
            pltpu.VMEM((GLA_HEADS * GLA_DK, GLA_DV), F32),
            pltpu.VMEM((tm, hk), F32),
            pltpu.VMEM((tm, hv), F32),
            pltpu.VMEM((tm + CONV_HALO, ch), F32),
        ],
        compiler_params=_params("arbitrary", "arbitrary"),
        name="even_mixer",
    )(x, *pending, shift, scale, g1, w_cat, w_gate2, b_gate, norm_g, conv_w, conv_b, ln_g, ln_b)
    return outs if pending else [x] + list(outs)


def _rope_table_kernel(pos_ref, freq_ref, sign_ref, cos_ref, sin_ref):
    ang = pos_ref[...] * freq_ref[...]
    cos_ref[...] = jnp.cos(ang)
    sin_ref[...] = jnp.sin(ang) * sign_ref[...]


def _rope_tables(positions, head_dim, tm):
    b, s = positions.shape
    rope_dims = head_dim // 4
    half = rope_dims // 2
    inv_freq = ROPE_THETA ** (-jnp.arange(0, rope_dims, 2, dtype=F32) / rope_dims)
    jj = jnp.arange(LANES) % head_dim
    freq = jnp.where(jj < rope_dims, inv_freq[jj % half], 0.0).astype(F32)[None, :]
    sign = jnp.where(jj < half, -1.0, jnp.where(jj < rope_dims, 1.0, 0.0)).astype(F32)[None, :]
    pos = positions.astype(F32)[..., None]
    tile = lambda i, j: (i, j, 0)
    const = lambda i, j: (0, 0)
    return pl.pallas_call(
        _rope_table_kernel,
        out_shape=[jax.ShapeDtypeStruct((b, s, LANES), F32)] * 2,
        grid=(b, s // tm),
        in_specs=[pl.BlockSpec((None, tm, 1), tile), pl.BlockSpec((1, LANES), const),
                  pl.BlockSpec((1, LANES), const)],
        out_specs=[pl.BlockSpec((None, tm, LANES), tile)] * 2,
        compiler_params=_params("arbitrary", "arbitrary"),
        name="rope_tables",
    )(pos, freq, sign)


def _qkv_kernel(*refs, head_dim, has_pending):
    x_ref, refs = refs[0], refs[1:]
    if has_pending:
        pending, refs = refs[:N_PENDING], refs[N_PENDING:]
    sh_ref, sc_ref, g_ref, w_ref, cos_ref, sin_ref = refs[:6]
    refs = refs[6:]
    if has_pending:
        x_ref, refs = _apply_pending(x_ref, pending, refs[0]), refs[1:]
    q_ref, k_ref, v_ref = refs
    h = _modulate(x_ref[...], g_ref[...], sh_ref[...], sc_ref[...]).astype(BF16)
    d = q_ref.shape[-1]
    half = (head_dim // 4) // 2
    cosf = jnp.tile(cos_ref[...], (1, d // LANES))
    sinf = jnp.tile(sin_ref[...], (1, d // LANES))
    lane = lax.broadcasted_iota(jnp.int32, (1, d), 1)
    first = (lane % head_dim) < half
    for idx, (o_ref, mult) in enumerate(((q_ref, head_dim ** -0.5), (k_ref, 1.0))):
        t = jnp.dot(h, w_ref[:, idx * d:(idx + 1) * d], preferred_element_type=F32)
        partner = jnp.where(first, pltpu.roll(t, d - half, 1), pltpu.roll(t, half, 1))
        o_ref[...] = ((t * cosf + partner * sinf) * mult).astype(o_ref.dtype)
    v_ref[...] = jnp.dot(h, w_ref[:, 2 * d:], preferred_element_type=F32).astype(v_ref.dtype)


def _qkv_rope(x, pending, shift, scale, g, w, cos_t, sin_t, head_dim, tm):
    b, s, d = x.shape
    row = lambda i, j: (i, 0, 0)
    tile = lambda i, j: (i, j, 0)
    pending = list(pending or ())
    x_out = [jax.ShapeDtypeStruct((b, s, d), F32)] if pending else []
    outs = pl.pallas_call(
        functools.partial(_qkv_kernel, head_dim=head_dim, has_pending=bool(pending)),
        out_shape=x_out + [jax.ShapeDtypeStruct((b, s, d), BF16)] * 3,
        grid=(b, s // tm),
        in_specs=[pl.BlockSpec((None, tm, d), tile)] + (_pending_specs(pending, tm, d) if pending else []) + [
            pl.BlockSpec((None, 1, d), row),
            pl.BlockSpec((None, 1, d), row),
            pl.BlockSpec((1, d), lambda i, j: (0, 0)),
            pl.BlockSpec(w.shape, lambda i, j: (0, 0)),
            pl.BlockSpec((None, tm, LANES), tile),
            pl.BlockSpec((None, tm, LANES), tile),
        ],
        out_specs=[pl.BlockSpec((None, tm, d), tile)] * (len(x_out) + 3),
        compiler_params=_params("arbitrary", "arbitrary"),
        name="qkv_rope",
    )(x, *pending, shift, scale, g, w, cos_t, sin_t)
    return outs if pending else [x] + list(outs)


ONES_ROWS = 16
ATTN_BLOCKS = ATTN_SUPER // DIL_BLOCK


def _attn_bias(branches, heads):
    blk = DIL_BLOCK
    kj = np.arange(2 * blk)[:, None]
    qi = np.arange(blk)[None, :]
    dist = qi + blk - kj
    out = []
    for window, dil in branches:
        band = (dist >= 0) & (dist <= window // dil)
        both = np.stack([band, band & (kj >= blk)])
        out.append(np.tile(np.where(both, 0.0, NEG_BIG), (1, 1, heads)))
    return jnp.asarray(np.stack(out), F32)


def _attn_kernel(bias_ref, q_ref, kc_ref, kp_ref, vc_ref, vp_ref, o_ref, qf, kf, vf, ob, lb, st_s, vt_s, kd, vd,
                 *, head_dim, branches):
    sb = q_ref.shape[0]
    blk = DIL_BLOCK
    heads = q_ref.shape[1] // head_dim
    first_super = pl.program_id(2) == 0
    slot0 = jnp.maximum(pl.program_id(2) - pl.num_programs(2), 0)

    @pl.when(first_super)
    def _():
        kd[...] = jnp.zeros_like(kd)
        vd[...] = jnp.zeros_like(vd)

    qf[...] = q_ref[...].astype(F32)
    kf[0:sb, :] = kp_ref[...].astype(F32)
    kf[sb:, :] = kc_ref[...].astype(F32)
    vf[0:sb, :] = vp_ref[...].astype(F32)
    vf[sb:, :] = vc_ref[...].astype(F32)

    vt_s[:, LANES:, :] = jnp.ones((ATTN_BLOCKS, ONES_ROWS, 2 * blk), BF16)

    lane = lax.broadcasted_iota(jnp.int32, (blk, LANES), 1)
    head_masks = [(lane >= h * head_dim) & (lane < (h + 1) * head_dim) for h in range(heads)]

    for bi, (window, dil) in enumerate(branches):
        assert window // dil <= blk and sb == ATTN_SUPER and sb % (dil * blk) == 0
        unit = dil * blk

        def scores(j, q0, first_unit, dil=dil, unit=unit, bi=bi):
            k0 = sb + q0 - unit
            no_prev = jnp.where(first_super, 1, 0) if first_unit else 0
            qb = qf[pl.ds(q0, blk, stride=dil), :]
            q2 = jnp.concatenate([jnp.where(mk, qb, 0.0) for mk in head_masks], axis=0).astype(BF16)
            if unit == sb:
                keep = pl.ds(q0 * blk, blk)
                k_cur = kf[pl.ds(sb + q0, blk, stride=dil), :]
                v_cur = vf[pl.ds(sb + q0, blk, stride=dil), :]
                k_all = jnp.concatenate([kd[keep, :], k_cur], axis=0)
                v_all = jnp.concatenate([vd[keep, :], v_cur], axis=0)
                kd[keep, :] = k_cur
                vd[keep, :] = v_cur
            else:
                k_all = kf[pl.ds(k0, 2 * blk, stride=dil), :]
                v_all = vf[pl.ds(k0, 2 * blk, stride=dil), :]
            kb = k_all.astype(BF16)
            vt_s[j, 0:LANES, :] = v_all.T.astype(BF16)
            st = lax.dot_general(kb, q2, (((1,), (1,)), ((), ())), preferred_element_type=F32)
            st_s[slot0 + j] = st + bias_ref[bi, no_prev]

        def softmax_pv(j):
            m = jnp.max(st_s[slot0 + j], axis=0, keepdims=True)
            p = jnp.exp(st_s[slot0 + j] - m).astype(BF16)
            of = jnp.dot(vt_s[j], p, preferred_element_type=F32)
            l = of[LANES:LANES + 1, :]
            lse = m + jnp.log(l)
            o_rows, lse_rows = [], []
            for h in range(heads):
                cols = slice(h * blk, (h + 1) * blk)
                o_rows.append(of[h * head_dim:(h + 1) * head_dim, cols] / l[:, cols])
                lse_rows.append(jnp.broadcast_to(lse[:, cols], (head_dim, blk)))
            return jnp.concatenate(o_rows, axis=0).T, jnp.concatenate(lse_rows, axis=0).T

        starts = [(idx // dil) * unit + idx % dil for idx in range(ATTN_BLOCKS)]
        for j, q0 in enumerate(starts):
            scores(j, q0, q0 < unit)
        for j, q0 in enumerate(starts):
            o_tok, lse_tok = softmax_pv(j)
            ob[bi, pl.ds(q0, blk, stride=dil), :] = o_tok
            lb[bi, pl.ds(q0, blk, stride=dil), :] = lse_tok

    nb = len(branches)
    m = lb[0]
    for bi in range(1, nb):
        m = jnp.maximum(m, lb[bi])
    num = jnp.zeros_like(m)
    den = jnp.zeros_like(m)
    for bi in range(nb):
        e = jnp.exp(lb[bi] - m)
        num = num + e * ob[bi]
        den = den + e
    o_ref[...] = (num / den).astype(o_ref.dtype)


def _dilated_attention(q, k, v, head_dim, branches, sb):
    b, s, d = q.shape
    groups = d // LANES
    cur = lambda i, g, n: (i, n, g)
    prev = lambda i, g, n: (i, jnp.maximum(n - 1, 0), g)
    nb = len(branches)
    bias = _attn_bias(branches, LANES // head_dim)
    return pl.pallas_call(
        functools.partial(_attn_kernel, head_dim=head_dim, branches=branches),
        out_shape=jax.ShapeDtypeStruct((b, s, d), BF16),
        grid=(b, groups, s // sb),
        in_specs=[
            pl.BlockSpec(bias.shape, lambda i, g, n: (0, 0, 0, 0)),
            pl.BlockSpec((None, sb, LANES), cur),
            pl.BlockSpec((None, sb, LANES), cur),
            pl.BlockSpec((None, sb, LANES), prev),
            pl.BlockSpec((None, sb, LANES), cur),
            pl.BlockSpec((None, sb, LANES), prev),
        ],
        out_specs=pl.BlockSpec((None, sb, LANES), cur),
        scratch_shapes=[
            pltpu.VMEM((sb, LANES), F32),
            pltpu.VMEM((2 * sb, LANES), F32),
            pltpu.VMEM((2 * sb, LANES), F32),
            pltpu.VMEM((nb, sb, LANES), F32),
            pltpu.VMEM((nb, sb, LANES), F32),
            pltpu.VMEM((ATTN_BLOCKS, 2 * DIL_BLOCK, LANES // head_dim * DIL_BLOCK), F32),
            pltpu.VMEM((ATTN_BLOCKS, LANES + ONES_ROWS, 2 * DIL_BLOCK), BF16),
            pltpu.VMEM((sb, LANES), F32),
            pltpu.VMEM((sb, LANES), F32),
        ],
        compiler_params=_params("arbitrary", "arbitrary", "arbitrary"),
        name="dilated_attention",
    )(bias, q, k, k, v, v)


def _route_tile(x, sh_ref, sc_ref, g_ref, w_ref, b_ref, h_ref, wts_ref, route_ref, counts_ref, carry_ref,
                n_experts, n_groups):
    h = _modulate(x, g_ref[...], sh_ref[...], sc_ref[...]).astype(BF16)
    h_ref[...] = h
    logits = jnp.dot(h, w_ref[...], preferred_element_type=F32) + b_ref[...]
    tm = logits.shape[0]
    epg = n_experts // n_groups
    lane = lax.broadcasted_iota(jnp.int32, logits.shape, 1)
    neg = -jnp.inf
    big = jnp.int32(LANES)

    def first_max(mask):
        val = jnp.max(jnp.where(mask, logits, neg), axis=-1, keepdims=True)
        idx = jnp.min(jnp.where(mask & (logits == val), lane, big), axis=-1, keepdims=True)
        return val, idx

    gmask = (lane >= n_experts) & (lane < n_experts + n_groups)
    gmax, gidx = first_max(gmask)
    gsum = jnp.sum(jnp.where(gmask, jnp.exp(logits - gmax), 0.0), axis=-1, keepdims=True)
    g_w = 1.0 / gsum
    grp = gidx - n_experts
    assert epg & (epg - 1) == 0
    emask = (lane < n_experts) & (lax.shift_right_logical(lane, epg.bit_length() - 1) == grp)
    v1, i1 = first_max(emask)
    v2, i2 = first_max(emask & (lane != i1))
    e2 = jnp.exp(v2 - v1)
    den = 1.0 + e2
    col = lax.broadcasted_iota(jnp.int32, (tm, TOP_K), 1)
    wts_ref[...] = jnp.where(col == 0, 1.0 / den, e2 / den) * g_w

    hit1 = lane == i1
    hit2 = lane == i2
    onehot = jnp.where(hit1 | hit2, 1.0, 0.0)
    ri = lax.broadcasted_iota(jnp.int32, (tm, tm), 0)
    ci = lax.broadcasted_iota(jnp.int32, (tm, tm), 1)
    before = jnp.where(ci < ri, 1.0, 0.0).astype(BF16)
    prefix = jnp.dot(before, onehot.astype(BF16), preferred_element_type=F32) + carry_ref[0:1, :]
    r1 = jnp.sum(jnp.where(hit1, prefix, 0.0), axis=-1, keepdims=True)
    r2 = jnp.sum(jnp.where(hit2, prefix, 0.0), axis=-1, keepdims=True)
    packed = jnp.where(lane == 0, i1.astype(F32), jnp.where(lane == 1, i2.astype(F32),
                       jnp.where(lane == 2, r1, jnp.where(lane == 3, r2, 0.0))))
    route_ref[...] = packed.T[0:SUBLANES, :]
    carry_ref[...] = carry_ref[...] + jnp.sum(onehot, axis=0, keepdims=True)
    counts_ref[...] = carry_ref[...]


def _out_proj_router_kernel(*refs, n_acts, n_experts, n_groups):
    a_refs = refs[:n_acts]
    w_ref, x_ref, gate_ref, sh_ref, sc_ref, g_ref, wr_ref, br_ref = refs[n_acts:n_acts + 8]
    xo_ref, h_ref, wts_ref, route_ref, counts_ref, carry_ref = refs[n_acts + 8:]

    @pl.when((pl.program_id(0) == 0) & (pl.program_id(1) == 0))
    def _():
        carry_ref[...] = jnp.zeros_like(carry_ref)

    acc = None
    off = 0
    for a_ref in a_refs:
        kk = a_ref.shape[-1]
        part = jnp.dot(a_ref[...].astype(BF16), w_ref[off:off + kk, :], preferred_element_type=F32)
        acc = part if acc is None else acc + part
        off += kk
    x = x_ref[...] + gate_ref[...] * acc
    xo_ref[...] = x
    _route_tile(x, sh_ref, sc_ref, g_ref, wr_ref, br_ref, h_ref, wts_ref, route_ref, counts_ref, carry_ref,
                n_experts, n_groups)


def _out_proj_router(acts, w, x, gate, shift, scale, g, w_rt, b_rt, n_experts, n_groups, tm, batch0=0):
    _, s, d = x.shape
    b = acts[0].shape[0]
    row = lambda i, j: (i, 0, 0)
    tile = lambda i, j: (i, j, 0)
    const = lambda i, j: (0, 0)
    return pl.pallas_call(
        functools.partial(_out_proj_router_kernel, n_acts=len(acts), n_experts=n_experts, n_groups=n_groups),
        out_shape=[jax.ShapeDtypeStruct((b, s, d), F32),
                   jax.ShapeDtypeStruct((b, s, d), BF16),
                   jax.ShapeDtypeStruct((b, s, TOP_K), F32),
                   jax.ShapeDtypeStruct((SUBLANES, b * s), F32),
                   jax.ShapeDtypeStruct((SUBLANES, LANES), F32)],
        grid=(b, s // tm),
        in_specs=[pl.BlockSpec((None, tm, a.shape[-1]), tile) for a in acts] + [
            pl.BlockSpec(w.shape, const),
            pl.BlockSpec((None, tm, d), lambda i, j: (i + batch0, j, 0)),
            pl.BlockSpec((None, 1, d), row),
            pl.BlockSpec((None, 1, d), row),
            pl.BlockSpec((None, 1, d), row),
            pl.BlockSpec((1, d), const),
            pl.BlockSpec((d, LANES), const),
            pl.BlockSpec((1, LANES), const),
        ],
        out_specs=[pl.BlockSpec((None, tm, d), tile),
                   pl.BlockSpec((None, tm, d), tile),
                   pl.BlockSpec((None, tm, TOP_K), tile),
                   pl.BlockSpec((SUBLANES, tm), lambda i, j: (0, i * (s // tm) + j)),
                   pl.BlockSpec((SUBLANES, LANES), const)],
        scratch_shapes=[pltpu.VMEM((SUBLANES, LANES), F32)],
        compiler_params=_params("arbitrary", "arbitrary"),
        name="out_proj_router",
    )(*acts, w, x, gate, shift, scale, g, w_rt, b_rt)


def _expert_kernel(ib_ref, ie_ref, lo_ref, hi_ref, xs_ref, w1_ref, w3_ref, w2_ref, ys_ref, w1b, w3b, w2b):
    j = pl.program_id(0)
    prev = jnp.maximum(j - 1, 0)
    e_changed = (j == 0) | (ie_ref[j] != ie_ref[prev])
    first_of_block = (j == 0) | (ib_ref[j] != ib_ref[prev])
    lo = lo_ref[j]
    hi = hi_ref[j]

    @pl.when(e_changed)
    def _():
        w1b[...] = w1_ref[...].astype(BF16)
        w3b[...] = w3_ref[...].astype(BF16)
        w2b[...] = w2_ref[...].astype(BF16)

    rows = ys_ref.shape[0]
    whole = (lo == 0) & (hi == rows)

    @pl.when(first_of_block & jnp.logical_not(whole))
    def _():
        ys_ref[...] = jnp.zeros_like(ys_ref)

    @pl.when(hi > lo)
    def _():
        x = xs_ref[...]
        a = jnp.dot(x, w1b[...], preferred_element_type=F32)
        g = jnp.dot(x, w3b[...], preferred_element_type=F32)
        y = jnp.dot((_silu(a) * g).astype(BF16), w2b[...], preferred_element_type=F32).astype(ys_ref.dtype)

        @pl.when(whole)
        def _():
            ys_ref[...] = y

        @pl.when(jnp.logical_not(whole))
        def _():
            row = lax.broadcasted_iota(jnp.int32, (rows, 1), 0)
            ys_ref[...] = jnp.where((row >= lo) & (row < hi), y, ys_ref[...])


def _experts(layer, items, xs, w1, w3, w2, rows):
    a, d = xs.shape
    hid = w1.shape[-1]
    blk = lambda j, ib, ie, lo, hi: (ib[j], 0)
    wsel = lambda j, ib, ie, lo, hi: (layer, ie[j], 0, 0)
    grid_spec = pltpu.PrefetchScalarGridSpec(
        num_scalar_prefetch=4,
        grid=(items[0].shape[0],),
        in_specs=[
            pl.BlockSpec((rows, d), blk),
            pl.BlockSpec((None, None, d, hid), wsel),
            pl.BlockSpec((None, None, d, hid), wsel),
            pl.BlockSpec((None, None, hid, d), wsel),
        ],
        out_specs=pl.BlockSpec((rows, d), blk),
        scratch_shapes=[pltpu.VMEM((d, hid), BF16), pltpu.VMEM((d, hid), BF16), pltpu.VMEM((hid, d), BF16)],
    )
    return pl.pallas_call(
        _expert_kernel,
        out_shape=jax.ShapeDtypeStruct((a, d), BF16),
        grid_spec=grid_spec,
        compiler_params=_params("arbitrary"),
        name="moe_experts",
    )(*items, xs, w1, w3, w2)


DEST_TILE = 8192


def _dest_kernel(route_ref, starts_ref, dest_ref, *, n_experts):
    ids = route_ref[0:TOP_K, :]
    dest = route_ref[TOP_K:2 * TOP_K, :]
    for e in range(n_experts):
        dest = dest + jnp.where(ids == float(e), starts_ref[0:1, e:e + 1], 0.0)
    dest_ref[...] = dest.astype(jnp.int32)


def _dest_rows(route, starts):
    t = route.shape[1]
    tt = _pick_tile(t, DEST_TILE)
    n_experts = starts.shape[0]
    assert 2 * TOP_K <= SUBLANES and TOP_K * t < 2 ** 24 and n_experts <= LANES
    starts_row = jnp.pad(starts.astype(F32), (0, LANES - n_experts))[None, :]
    return pl.pallas_call(
        functools.partial(_dest_kernel, n_experts=n_experts),
        out_shape=jax.ShapeDtypeStruct((TOP_K, t), jnp.int32),
        grid=(t // tt,),
        in_specs=[pl.BlockSpec((SUBLANES, tt), lambda i: (0, i)), pl.BlockSpec((1, LANES), lambda i: (0, 0))],
        out_specs=pl.BlockSpec((TOP_K, tt), lambda i: (0, i)),
        compiler_params=_params("arbitrary"),
        name="moe_dest",
    )(route, starts_row)


def _dispatch(route, counts, rows):
    n_experts = counts.shape[0]
    t = route.shape[1]
    a = TOP_K * t
    i32 = jnp.int32
    ends = jnp.cumsum(counts)
    starts = ends - counts
    dest = _dest_rows(route, starts)
    tok = jnp.tile(jnp.arange(t, dtype=i32), TOP_K)
    row_tok = lax.sort_key_val(dest.reshape(-1), tok)[1]
    n_blk = a // rows
    bstart = jnp.arange(n_blk, dtype=i32) * rows
    count_le = lambda bounds, x: jnp.sum((bounds[None, :] <= x[:, None]).astype(i32), axis=1)
    e_lo = jnp.minimum(count_le(ends, bstart), n_experts - 1)
    e_hi = jnp.minimum(count_le(ends, bstart + rows - 1), n_experts - 1)
    n_items = e_hi - e_lo + 1
    item_end = jnp.cumsum(n_items)
    item_first = item_end - n_items
    jj = jnp.arange(n_blk + n_experts - 1, dtype=i32)
    valid = jj < item_end[-1]
    ib = jnp.minimum(count_le(item_end, jj), n_blk - 1)
    ie = jnp.where(valid, jnp.clip(e_lo[ib] + jj - item_first[ib], 0, n_experts - 1), e_hi[n_blk - 1]).astype(i32)
    lo = jnp.where(valid, jnp.clip(starts[ie] - ib * rows, 0, rows), 0).astype(i32)
    hi = jnp.where(valid, jnp.clip(ends[ie] - ib * rows, 0, rows), 0).astype(i32)
    return row_tok, dest, (ib, ie, lo, hi)


def _final_combine_kernel(x_ref, y0_ref, y1_ref, w_ref, gate_ref, ng_ref, o_ref):
    x = _apply_pending(x_ref, (y0_ref, y1_ref, w_ref, gate_ref), o_ref)[...]
    o_ref[...] = x * lax.rsqrt(jnp.mean(x * x, axis=-1, keepdims=True) + NORM_EPS) * ng_ref[...]


def _final_combine_kernel_into(x_ref, y0_ref, y1_ref, w_ref, gate_ref, ng_ref, prev_ref, o_ref):
    del prev_ref
    _final_combine_kernel(x_ref, y0_ref, y1_ref, w_ref, gate_ref, ng_ref, o_ref)


def _final_combine(x, pending, final_g, tm, out, batch0, b_total):
    b, s, d = x.shape
    tile = lambda i, j: (i, j, 0)
    in_specs = [pl.BlockSpec((None, tm, d), tile)] + _pending_specs(pending, tm, d) + [
        pl.BlockSpec((1, d), lambda i, j: (0, 0))]
    args = [x, *pending, final_g]
    if out is not None:
        in_specs.append(pl.BlockSpec(memory_space=pl.ANY))
        args.append(out)
    return pl.pallas_call(
        _final_combine_kernel if out is None else _final_combine_kernel_into,
        out_shape=jax.ShapeDtypeStruct((b_total, s, d), F32),
        grid=(b, s // tm),
        in_specs=in_specs,
        out_specs=pl.BlockSpec((None, tm, d), lambda i, j: (i + batch0, j, 0)),
        input_output_aliases={} if out is None else {len(args) - 1: 0},
        compiler_params=_params("arbitrary", "arbitrary"),
        name="moe_final_combine",
    )(*args)


def _hier_moe(layer, x, routed, gate, w1, w3, w2, rows):
    b, s, d = x.shape
    t = b * s
    n_experts = w1.shape[1]
    h, wts, route, counts = routed
    counts = counts[0, :n_experts].astype(jnp.int32)
    row_tok, dest, items = _dispatch(route, counts, rows)
    xs = h.reshape(t, d)[row_tok]
    ys = _experts(layer, items, xs, w1, w3, w2, rows)
    yy = ys[dest.reshape(-1)].reshape(TOP_K, b, s, d)
    return yy, yy, wts, gate


def _pick_tile(s, pref):
    tm = min(pref, s)
    assert s % tm == 0
    return tm


def kernel(x, c, positions, ada_w, ada_b, norm1_g, norm2_g, even_w_in, even_w_gate2, even_b_gate, even_gla_norm_g, even_conv_w, even_conv_b, even_conv_ln_g, even_conv_ln_b, even_w_out, odd_w_qkv, odd_w_out, moe_w_grp, moe_b_grp, moe_w_rt, moe_b_rt, moe_w1, moe_w3, moe_w2, final_norm_g):
    b, s, d = x.shape
    depth = ada_w.shape[0]
    n_experts = moe_w_rt.shape[-1]
    tm = _pick_tile(s, TOKEN_TILE)
    sb = _pick_tile(s, ATTN_SUPER)
    head_dim = d // ATTN_HEADS
    hk = GLA_HEADS * GLA_DK
    hv = GLA_HEADS * GLA_DV

    mods = _ada_mods(c, ada_w, ada_b)
    n_chains = BATCH_CHAINS if b % BATCH_CHAINS == 0 else 1
    bc = b // n_chains
    chains = [dict(x=x, batch0=ch * bc, pending=None, tables=None) for ch in range(n_chains)]

    for layer in range(depth):
        i = layer // 2
        g1 = norm1_g[layer][None, :]
        w_rt_full = jnp.concatenate([moe_w_rt[layer], moe_w_grp[layer],
                                     jnp.zeros((d, LANES - n_experts - N_GROUPS), F32)], axis=1).astype(BF16)
        b_rt_full = jnp.concatenate([moe_b_rt[layer], moe_b_grp[layer],
                                     jnp.zeros((LANES - n_experts - N_GROUPS,), F32)])[None, :]
        if layer % 2 == 0:
            w_in = even_w_in[i]
            main = hk + hk + hv + hv
            w_cat = jnp.concatenate([
                w_in[:, :main], w_in[:, main + GLA_GATE_RANK:], w_in[:, main:main + GLA_GATE_RANK],
                jnp.zeros((d, LANES - GLA_GATE_RANK), w_in.dtype)], axis=1).astype(BF16)
            wg = jnp.concatenate([even_w_gate2[i], jnp.zeros((LANES - GLA_GATE_RANK, hk), F32)], axis=0).astype(BF16)
            w_out = even_w_out[i].astype(BF16)
        else:
            w_qkv = odd_w_qkv[i].astype(BF16)
            w_out = odd_w_out[i].astype(BF16)

        for ch, st in enumerate(chains):
            seqs = slice(ch * bc, (ch + 1) * bc)
            mod = lambda j: mods[layer, j][seqs][:, None, :]
            xin, batch0 = st['x'], st['batch0']
            if layer % 2 == 0:
                x_new, o_gla, y_conv = _even_mixer(
                    xin, st['pending'], mod(0), mod(1), g1, w_cat, wg, even_b_gate[i][None, :],
                    even_gla_norm_g[i][None, :], even_conv_w[i], even_conv_b[i][None, :],
                    even_conv_ln_g[i][None, :], even_conv_ln_b[i][None, :], tm, batch0)
                acts = [o_gla, y_conv]
            else:
                assert batch0 == 0
                if st['tables'] is None:
                    st['tables'] = _rope_tables(positions[seqs], head_dim, tm)
                x_new, q, k, v = _qkv_rope(xin, st['pending'], mod(0), mod(1), g1, w_qkv,
                                           *st['tables'], head_dim, tm)
                acts = [_dilated_attention(q, k, v, head_dim, DILATED_BRANCHES, sb)]
            if st['pending'] is None:
                x_new = xin
            x_new, *routed = _out_proj_router(acts, w_out, x_new, mod(2), mod(3), mod(4), norm2_g[layer][None, :],
                                              w_rt_full, b_rt_full, n_experts, N_GROUPS,
                                              _pick_tile(s, ROUTER_TILE), batch0)
            st['x'], st['batch0'] = x_new, 0
            st['pending'] = _hier_moe(layer, x_new, routed, mod(5), moe_w1, moe_w3, moe_w2, MOE_ROWS)

    out = None
    for ch, st in enumerate(chains):
        out = _final_combine(st['x'], st['pending'], final_norm_g[None, :], tm, out, ch * bc, b)
    return out
```

```python
import functools

import jax
import jax.numpy as jnp
import numpy as np
from jax import lax
from jax.experimental import pallas as pl
from jax.experimental.pallas import tpu as pltpu

F32 = jnp.float32
BF16 = jnp.bfloat16
HIGHEST = lax.Precision.HIGHEST

NORM_EPS = 1e-6
GLA_HEADS = 4
GLA_DK = 64
GLA_DV = 128
GLA_GATE_RANK = 16
GLA_TAU = 16.0
GLA_CHUNK = 64
CONV_WIDTH = 31
ATTN_HEADS = 16
DILATED_BRANCHES = ((128, 1), (512, 4), (2048, 16))
DIL_BLOCK = 128
ROPE_THETA = 500000.0
N_GROUPS = 4
TOP_K = 2
ADA_CHUNKS = 6

LANES = 128
SUBLANES = 8
VMEM_LIMIT = 56 * 1024 * 1024
TOKEN_TILE = 512
ROUTER_TILE = 1024
BATCH_CHAINS = 2
ATTN_SUPER = 2048
MOE_ROWS = 512
NEG_BIG = -1e30


def _params(*sem):
    return pltpu.CompilerParams(dimension_semantics=sem, vmem_limit_bytes=VMEM_LIMIT)


def _silu(x):
    return x * jax.nn.sigmoid(x)


def _modulate(x, g, shift, scale):
    y = x * lax.rsqrt(jnp.mean(x * x, axis=-1, keepdims=True) + NORM_EPS)
    return (y * g) * (1.0 + scale) + shift


def _ada_kernel(c_ref, w_ref, b_ref, o_ref):
    cond = _silu(c_ref[...])
    o_ref[...] = jnp.dot(cond, w_ref[...], preferred_element_type=F32, precision=HIGHEST) + b_ref[...]


def _ada_mods(c, ada_w, ada_b):
    depth, d, _ = ada_w.shape
    b = c.shape[0]
    return pl.pallas_call(
        _ada_kernel,
        out_shape=jax.ShapeDtypeStruct((depth, ADA_CHUNKS, b, d), F32),
        grid=(depth, ADA_CHUNKS),
        in_specs=[
            pl.BlockSpec((b, d), lambda l, j: (0, 0)),
            pl.BlockSpec((None, d, d), lambda l, j: (l, 0, j)),
            pl.BlockSpec((None, None, 1, d), lambda l, j: (l, j, 0, 0)),
        ],
        out_specs=pl.BlockSpec((None, None, b, d), lambda l, j: (l, j, 0, 0)),
        compiler_params=_params("arbitrary", "arbitrary"),
        name="ada_mods",
    )(c, ada_w, ada_b.reshape(depth, ADA_CHUNKS, 1, d))


def _norm_matmul_kernel(x_ref, sh_ref, sc_ref, g_ref, w_ref, *o_refs):
    h = _modulate(x_ref[...], g_ref[...], sh_ref[...], sc_ref[...]).astype(BF16)
    off = 0
    for o_ref in o_refs:
        n = o_ref.shape[-1]
        o_ref[...] = jnp.dot(h, w_ref[:, off:off + n], preferred_element_type=F32).astype(o_ref.dtype)
        off += n


def _log_sigmoid(z):
    return jnp.minimum(z, 0.0) - jnp.log1p(jnp.exp(-jnp.abs(z)))


def _gla_kernel(q_ref, k_ref, v_ref, g_ref, a_ref, wg_ref, bg_ref, ng_ref, o_ref, state_ref, la_ref, oacc_ref):
    tm = q_ref.shape[0]
    c = GLA_CHUNK

    z = jnp.dot(a_ref[...].astype(BF16), wg_ref[...], preferred_element_type=F32) + bg_ref[...]
    la_ref[...] = _log_sigmoid(z) * (1.0 / GLA_TAU)

    ri = lax.broadcasted_iota(jnp.int32, (c, c), 0)
    ci = lax.broadcasted_iota(jnp.int32, (c, c), 1)
    causal = ri >= ci
    tril = jnp.where(causal, 1.0, 0.0).astype(BF16)
    hk = GLA_HEADS * GLA_DK

    nh = GLA_HEADS
    lane_head = lax.shift_right_logical(lax.broadcasted_iota(jnp.int32, (c, hk), 1), GLA_DK.bit_length() - 1)
    r4 = lax.broadcasted_iota(jnp.int32, (nh * c, nh * c), 0)
    c4 = lax.broadcasted_iota(jnp.int32, (nh * c, nh * c), 1)
    shift_c = c.bit_length() - 1
    causal4 = (lax.shift_right_logical(r4, shift_c) == lax.shift_right_logical(c4, shift_c)) & (r4 >= c4)

    def stack_heads(t):
        return jnp.concatenate([jnp.where(lane_head == h, t, 0.0) for h in range(nh)], axis=0).astype(BF16)

    state = state_ref[...]
    for ic in range(tm // c):
        rows = slice(ic * c, (ic + 1) * c)
        la = la_ref[rows, :]
        p0 = la.astype(BF16)
        r1 = la - p0.astype(F32)
        p1 = r1.astype(BF16)
        p2 = (r1 - p1.astype(F32)).astype(BF16)
        parts = jnp.dot(tril, jnp.concatenate([p0, p1, p2], axis=1), preferred_element_type=F32)
        bcum = (parts[:, 2 * hk:] + parts[:, hk:2 * hk]) + parts[:, :hk]
        b_last = bcum[c - 1:c, :]
        q = q_ref[rows, :] * (GLA_DK ** -0.5)
        k = k_ref[rows, :]
        q4 = stack_heads(q * jnp.exp(bcum))
        k4 = stack_heads(k * jnp.exp(-bcum))
        kr4 = stack_heads(k * jnp.exp(b_last - bcum))
        dec = jnp.exp(jnp.broadcast_to(b_last, (GLA_DV, hk)).T)
        v4 = jnp.concatenate([v_ref[rows, h * GLA_DV:(h + 1) * GLA_DV] for h in range(nh)], axis=0)
        att = lax.dot_general(q4, k4, (((1,), (1,)), ((), ())), preferred_element_type=F32)
        att = jnp.where(causal4, att, 0.0).astype(BF16)
        o4 = jnp.dot(att, v4, preferred_element_type=F32)
        o4 = o4 + jnp.dot(q4, state.astype(BF16), preferred_element_type=F32)
        kv = lax.dot_general(kr4, v4, (((0,), (0,)), ((), ())), preferred_element_type=F32)
        state = dec * state + kv
        oacc_ref[rows, :] = jnp.concatenate([o4[h * c:(h + 1) * c, :] for h in range(nh)], axis=1)
    state_ref[...] = state

    for h in range(GLA_HEADS):
        vs = slice(h * GLA_DV, (h + 1) * GLA_DV)
        o = oacc_ref[:, vs]
        o = o * lax.rsqrt(jnp.mean(o * o, axis=-1, keepdims=True) + NORM_EPS) * ng_ref[...]
        o_ref[:, vs] = (o * _silu(g_ref[:, vs])).astype(o_ref.dtype)


CONV_HALO = 32


def _conv_kernel(u_ref, w_ref, cb_ref, lg_ref, lb_ref, o_ref, buf_ref):
    tm = u_ref.shape[0]
    ch = o_ref.shape[-1]

    buf_ref[CONV_HALO:, :] = u_ref[:, :ch] * jax.nn.sigmoid(u_ref[:, ch:])
    base = CONV_HALO - (CONV_WIDTH - 1)
    acc = None
    for b in range(SUBLANES):
        part = None
        span = tm + (SUBLANES if b else 0)
        for a in range((base + CONV_WIDTH - 1) // SUBLANES + 1):
            j = SUBLANES * a + b - base
            if 0 <= j < CONV_WIDTH:
                term = buf_ref[SUBLANES * a:SUBLANES * a + span, :] * w_ref[j:j + 1, :]
                part = term if part is None else part + term
        if part is not None:
            part = part[b:b + tm, :]
            acc = part if acc is None else acc + part
    buf_ref[0:CONV_HALO, :] = buf_ref[tm:tm + CONV_HALO, :]
    y = acc + cb_ref[...]
    mu = jnp.mean(y, axis=-1, keepdims=True)
    var = jnp.mean(jnp.square(y - mu), axis=-1, keepdims=True)
    y = (y - mu) * lax.rsqrt(var + NORM_EPS) * lg_ref[...] + lb_ref[...]
    o_ref[...] = _silu(y).astype(o_ref.dtype)


N_PENDING = 4


def _apply_pending(x_ref, pending, xo_ref):
    y0_ref, y1_ref, w_ref, gate_ref = pending
    y = y0_ref[...].astype(F32) * w_ref[:, 0:1] + y1_ref[...].astype(F32) * w_ref[:, 1:2]
    xo_ref[...] = x_ref[...] + gate_ref[...] * y
    return xo_ref


def _pending_specs(pending, tm, d):
    tile = lambda i, j: (i, j, 0)
    return [pl.BlockSpec((None, None, tm, d), lambda i, j: (0, i, j, 0)),
            pl.BlockSpec((None, None, tm, d), lambda i, j: (1, i, j, 0)),
            pl.BlockSpec((None, tm, TOP_K), tile), pl.BlockSpec((None, 1, d), lambda i, j: (i, 0, 0))]


def _even_mixer_kernel(*refs, has_pending):
    x_ref, refs = refs[0], refs[1:]
    if has_pending:
        pending, refs = refs[:N_PENDING], refs[N_PENDING:]
    sh_ref, sc_ref, g1_ref, w_ref, wg_ref, bg_ref, ng_ref, cw_ref, cb_ref, lg_ref, lb_ref = refs[:11]
    refs = refs[11:]
    if has_pending:
        xo_ref, refs = refs[0], refs[1:]
    o_gla_ref, y_conv_ref, q_s, k_s, v_s, g_s, u_s, a_s, state_ref, la_ref, oacc_ref, buf_ref = refs

    @pl.when(pl.program_id(1) == 0)
    def _():
        state_ref[...] = jnp.zeros_like(state_ref)
        buf_ref[0:CONV_HALO, :] = jnp.zeros((CONV_HALO, buf_ref.shape[1]), F32)

    if has_pending:
        x_ref = _apply_pending(x_ref, pending, xo_ref)
    _norm_matmul_kernel(x_ref, sh_ref, sc_ref, g1_ref, w_ref, q_s, k_s, v_s, g_s, u_s, a_s)
    _gla_kernel(q_s, k_s, v_s, g_s, a_s, wg_ref, bg_ref, ng_ref, o_gla_ref, state_ref, la_ref, oacc_ref)
    _conv_kernel(u_s, cw_ref, cb_ref, lg_ref, lb_ref, y_conv_ref, buf_ref)


def _even_mixer(x, pending, shift, scale, g1, w_cat, w_gate2, b_gate, norm_g, conv_w, conv_b, ln_g, ln_b, tm,
                batch0=0):
    _, s, d = x.shape
    b = shift.shape[0]
    hk = GLA_HEADS * GLA_DK
    hv = GLA_HEADS * GLA_DV
    ch = conv_w.shape[-1]
    row = lambda i, j: (i, 0, 0)
    tile = lambda i, j: (i, j, 0)
    const = lambda i, j: (0, 0)
    full = lambda arr: pl.BlockSpec(arr.shape, const)
    pending = list(pending or ())
    x_out = [jax.ShapeDtypeStruct((b, s, d), F32)] if pending else []
    outs = pl.pallas_call(
        functools.partial(_even_mixer_kernel, has_pending=bool(pending)),
        out_shape=x_out + [jax.ShapeDtypeStruct((b, s, hv), BF16), jax.ShapeDtypeStruct((b, s, ch), BF16)],
        grid=(b, s // tm),
        in_specs=[pl.BlockSpec((None, tm, d), lambda i, j: (i + batch0, j, 0))] + (
            _pending_specs(pending, tm, d) if pending else []) + [
            pl.BlockSpec((None, 1, d), row),
            pl.BlockSpec((None, 1, d), row),
            full(g1), full(w_cat), full(w_gate2), full(b_gate), full(norm_g),
            full(conv_w), full(conv_b), full(ln_g), full(ln_b),
        ],
        out_specs=[pl.BlockSpec((None, tm, d), tile)] * len(x_out) + [
            pl.BlockSpec((None, tm, hv), tile), pl.BlockSpec((None, tm, ch), tile)],
        scratch_shapes=[
            pltpu.VMEM((tm, hk), F32),
            pltpu.VMEM((tm, hk), F32),
            pltpu.VMEM((tm, hv), BF16),
            pltpu.VMEM((tm, hv), F32),
            pltpu.VMEM((tm, 2 * ch), F32),
            pltpu.VMEM((tm, w_gate2.shape[0]), F32),
            pltpu.VMEM((GLA_HEADS * GLA_DK, GLA_DV), F32),
            pltpu.VMEM((tm, hk), F32),
            pltpu.VMEM((tm, hv), F32),
            pltpu.VMEM((tm + CONV_HALO, ch), F32),
        ],
        compiler_params=_params("arbitrary", "arbitrary"),
        name="even_mixer",
    )(x, *pending, shift, scale, g1, w_cat, w_gate2, b_gate, norm_g, conv_w, conv_b, ln_g, ln_b)
    return outs if pending else [x] + list(outs)


def _rope_table_kernel(pos_ref, freq_ref, sign_ref, cos_ref, sin_ref):
    ang = pos_ref[...] * freq_ref[...]
    cos_ref[...] = jnp.cos(ang)
    sin_ref[...] = jnp.sin(ang) * sign_ref[...]


def _rope_tables(positions, head_dim, tm):
    b, s = positions.shape
    rope_dims = head_dim // 4
    half = rope_dims // 2
    inv_freq = ROPE_THETA ** (-jnp.arange(0, rope_dims, 2, dtype=F32) / rope_dims)
    jj = jnp.arange(LANES) % head_dim
    freq = jnp.where(jj < rope_dims, inv_freq[jj % half], 0.0).astype(F32)[None, :]
    sign = jnp.where(jj < half, -1.0, jnp.where(jj < rope_dims, 1.0, 0.0)).astype(F32)[None, :]
    pos = positions.astype(F32)[..., None]
    tile = lambda i, j: (i, j, 0)
    const = lambda i, j: (0, 0)
    return pl.pallas_call(
        _rope_table_kernel,
        out_shape=[jax.ShapeDtypeStruct((b, s, LANES), F32)] * 2,
        grid=(b, s // tm),
        in_specs=[pl.BlockSpec((None, tm, 1), tile), pl.BlockSpec((1, LANES), const),
                  pl.BlockSpec((1, LANES), const)],
        out_specs=[pl.BlockSpec((None, tm, LANES), tile)] * 2,
        compiler_params=_params("arbitrary", "arbitrary"),
        name="rope_tables",
    )(pos, freq, sign)


def _qkv_kernel(*refs, head_dim, has_pending):
    x_ref, refs = refs[0], refs[1:]
    if has_pending:
        pending, refs = refs[:N_PENDING], refs[N_PENDING:]
    sh_ref, sc_ref, g_ref, w_ref, cos_ref, sin_ref = refs[:6]
    refs = refs[6:]
    if has_pending:
        x_ref, refs = _apply_pending(x_ref, pending, refs[0]), refs[1:]
    q_ref, k_ref, v_ref = refs
    h = _modulate(x_ref[...], g_ref[...], sh_ref[...], sc_ref[...]).astype(BF16)
    d = q_ref.shape[-1]
    half = (head_dim // 4) // 2
    cosf = jnp.tile(cos_ref[...], (1, d // LANES))
    sinf = jnp.tile(sin_ref[...], (1, d // LANES))
    lane = lax.broadcasted_iota(jnp.int32, (1, d), 1)
    first = (lane % head_dim) < half
    for idx, (o_ref, mult) in enumerate(((q_ref, head_dim ** -0.5), (k_ref, 1.0))):
        t = jnp.dot(h, w_ref[:, idx * d:(idx + 1) * d], preferred_element_type=F32)
        partner = jnp.where(first, pltpu.roll(t, d - half, 1), pltpu.roll(t, half, 1))
        o_ref[...] = ((t * cosf + partner * sinf) * mult).astype(o_ref.dtype)
    v_ref[...] = jnp.dot(h, w_ref[:, 2 * d:], preferred_element_type=F32).astype(v_ref.dtype)


def _qkv_rope(x, pending, shift, scale, g, w, cos_t, sin_t, head_dim, tm):
    b, s, d = x.shape
    row = lambda i, j: (i, 0, 0)
    tile = lambda i, j: (i, j, 0)
    pending = list(pending or ())
    x_out = [jax.ShapeDtypeStruct((b, s, d), F32)] if pending else []
    outs = pl.pallas_call(
        functools.partial(_qkv_kernel, head_dim=head_dim, has_pending=bool(pending)),
        out_shape=x_out + [jax.ShapeDtypeStruct((b, s, d), BF16)] * 3,
        grid=(b, s // tm),
        in_specs=[pl.BlockSpec((None, tm, d), tile)] + (_pending_specs(pending, tm, d) if pending else []) + [
            pl.BlockSpec((None, 1, d), row),
            pl.BlockSpec((None, 1, d), row),
            pl.BlockSpec((1, d), lambda i, j: (0, 0)),
            pl.BlockSpec(w.shape, lambda i, j: (0, 0)),
            pl.BlockSpec((None, tm, LANES), tile),
            pl.BlockSpec((None, tm, LANES), tile),
        ],
        out_specs=[pl.BlockSpec((None, tm, d), tile)] * (len(x_out) + 3),
        compiler_params=_params("arbitrary", "arbitrary"),
        name="qkv_rope",
    )(x, *pending, shift, scale, g, w, cos_t, sin_t)
    return outs if pending else [x] + list(outs)


ONES_ROWS = 16
ATTN_BLOCKS = ATTN_SUPER // DIL_BLOCK


def _attn_bias(branches, heads):
    blk = DIL_BLOCK
    kj = np.arange(2 * blk)[:, None]
    qi = np.arange(blk)[None, :]
    dist = qi + blk - kj
    out = []
    for window, dil in branches:
        band = (dist >= 0) & (dist <= window // dil)
        both = np.stack([band, band & (kj >= blk)])
        out.append(np.tile(np.where(both, 0.0, NEG_BIG), (1, 1, heads)))
    return jnp.asarray(np.stack(out), F32)


def _attn_kernel(bias_ref, q_ref, kc_ref, kp_ref, vc_ref, vp_ref, o_ref, qf, kf, vf, ob, lb, st_s, vt_s, kd, vd,
                 *, head_dim, branches):
    sb = q_ref.shape[0]
    blk = DIL_BLOCK
    heads = q_ref.shape[1] // head_dim
    first_super = pl.program_id(2) == 0
    slot0 = jnp.maximum(pl.program_id(2) - pl.num_programs(2), 0)

    @pl.when(first_super)
    def _():
        kd[...] = jnp.zeros_like(kd)
        vd[...] = jnp.zeros_like(vd)

    qf[...] = q_ref[...].astype(F32)
    kf[0:sb, :] = kp_ref[...].astype(F32)
    kf[sb:, :] = kc_ref[...].astype(F32)
    vf[0:sb, :] = vp_ref[...].astype(F32)
    vf[sb:, :] = vc_ref[...].astype(F32)

    vt_s[:, LANES:, :] = jnp.ones((ATTN_BLOCKS, ONES_ROWS, 2 * blk), BF16)

    lane = lax.broadcasted_iota(jnp.int32, (blk, LANES), 1)
    head_masks = [(lane >= h * head_dim) & (lane < (h + 1) * head_dim) for h in range(heads)]

    for bi, (window, dil) in enumerate(branches):
        assert window // dil <= blk and sb == ATTN_SUPER and sb % (dil * blk) == 0
        unit = dil * blk

        def scores(j, q0, first_unit, dil=dil, unit=unit, bi=bi):
            k0 = sb + q0 - unit
            no_prev = jnp.where(first_super, 1, 0) if first_unit else 0
            qb = qf[pl.ds(q0, blk, stride=dil), :]
            q2 = jnp.concatenate([jnp.where(mk, qb, 0.0) for mk in head_masks], axis=0).astype(BF16)
            if unit == sb:
                keep = pl.ds(q0 * blk, blk)
                k_cur = kf[pl.ds(sb + q0, blk, stride=dil), :]
                v_cur = vf[pl.ds(sb + q0, blk, stride=dil), :]
                k_all = jnp.concatenate([kd[keep, :], k_cur], axis=0)
                v_all = jnp.concatenate([vd[keep, :], v_cur], axis=0)
                kd[keep, :] = k_cur
                vd[keep, :] = v_cur
            else:
                k_all = kf[pl.ds(k0, 2 * blk, stride=dil), :]
                v_all = vf[pl.ds(k0, 2 * blk, stride=dil), :]
            kb = k_all.astype(BF16)
            vt_s[j, 0:LANES, :] = v_all.T.astype(BF16)
            st = lax.dot_general(kb, q2, (((1,), (1,)), ((), ())), preferred_element_type=F32)
            st_s[slot0 + j] = st + bias_ref[bi, no_prev]

        def softmax_pv(j):
            m = jnp.max(st_s[slot0 + j], axis=0, keepdims=True)
            p = jnp.exp(st_s[slot0 + j] - m).astype(BF16)
            of = jnp.dot(vt_s[j], p, preferred_element_type=F32)
            l = of[LANES:LANES + 1, :]
            lse = m + jnp.log(l)
            o_rows, lse_rows = [], []
            for h in range(heads):
                cols = slice(h * blk, (h + 1) * blk)
                o_rows.append(of[h * head_dim:(h + 1) * head_dim, cols] / l[:, cols])
                lse_rows.append(jnp.broadcast_to(lse[:, cols], (head_dim, blk)))
            return jnp.concatenate(o_rows, axis=0).T, jnp.concatenate(lse_rows, axis=0).T

        starts = [(idx // dil) * unit + idx % dil for idx in range(ATTN_BLOCKS)]
        for j, q0 in enumerate(starts):
            scores(j, q0, q0 < unit)
        for j, q0 in enumerate(starts):
            o_tok, lse_tok = softmax_pv(j)
            ob[bi, pl.ds(q0, blk, stride=dil), :] = o_tok
            lb[bi, pl.ds(q0, blk, stride=dil), :] = lse_tok

    nb = len(branches)
    m = lb[0]
    for bi in range(1, nb):
        m = jnp.maximum(m, lb[bi])
    num = jnp.zeros_like(m)
    den = jnp.zeros_like(m)
    for bi in range(nb):
        e = jnp.exp(lb[bi] - m)
        num = num + e * ob[bi]
        den = den + e
    o_ref[...] = (num / den).astype(o_ref.dtype)


def _dilated_attention(q, k, v, head_dim, branches, sb):
    b, s, d = q.shape
    groups = d // LANES
    cur = lambda i, g, n: (i, n, g)
    prev = lambda i, g, n: (i, jnp.maximum(n - 1, 0), g)
    nb = len(branches)
    bias = _attn_bias(branches, LANES // head_dim)
    return pl.pallas_call(
        functools.partial(_attn_kernel, head_dim=head_dim, branches=branches),
        out_shape=jax.ShapeDtypeStruct((b, s, d), BF16),
        grid=(b, groups, s // sb),
        in_specs=[
            pl.BlockSpec(bias.shape, lambda i, g, n: (0, 0, 0, 0)),
            pl.BlockSpec((None, sb, LANES), cur),
            pl.BlockSpec((None, sb, LANES), cur),
            pl.BlockSpec((None, sb, LANES), prev),
            pl.BlockSpec((None, sb, LANES), cur),
            pl.BlockSpec((None, sb, LANES), prev),
        ],
        out_specs=pl.BlockSpec((None, sb, LANES), cur),
        scratch_shapes=[
            pltpu.VMEM((sb, LANES), F32),
            pltpu.VMEM((2 * sb, LANES), F32),
            pltpu.VMEM((2 * sb, LANES), F32),
            pltpu.VMEM((nb, sb, LANES), F32),
            pltpu.VMEM((nb, sb, LANES), F32),
            pltpu.VMEM((ATTN_BLOCKS, 2 * DIL_BLOCK, LANES // head_dim * DIL_BLOCK), F32),
            pltpu.VMEM((ATTN_BLOCKS, LANES + ONES_ROWS, 2 * DIL_BLOCK), BF16),
            pltpu.VMEM((sb, LANES), F32),
            pltpu.VMEM((sb, LANES), F32),
        ],
        compiler_params=_params("arbitrary", "arbitrary", "arbitrary"),
        name="dilated_attention",
    )(bias, q, k, k, v, v)


def _route_tile(x, sh_ref, sc_ref, g_ref, w_ref, b_ref, h_ref, wts_ref, route_ref, counts_ref, carry_ref,
                n_experts, n_groups):
    h = _modulate(x, g_ref[...], sh_ref[...], sc_ref[...]).astype(BF16)
    h_ref[...] = h
    logits = jnp.dot(h, w_ref[...], preferred_element_type=F32) + b_ref[...]
    tm = logits.shape[0]
    epg = n_experts // n_groups
    lane = lax.broadcasted_iota(jnp.int32, logits.shape, 1)
    neg = -jnp.inf
    big = jnp.int32(LANES)

    def first_max(mask):
        val = jnp.max(jnp.where(mask, logits, neg), axis=-1, keepdims=True)
        idx = jnp.min(jnp.where(mask & (logits == val), lane, big), axis=-1, keepdims=True)
        return val, idx

    gmask = (lane >= n_experts) & (lane < n_experts + n_groups)
    gmax, gidx = first_max(gmask)
    gsum = jnp.sum(jnp.where(gmask, jnp.exp(logits - gmax), 0.0), axis=-1, keepdims=True)
    g_w = 1.0 / gsum
    grp = gidx - n_experts
    assert epg & (epg - 1) == 0
    emask = (lane < n_experts) & (lax.shift_right_logical(lane, epg.bit_length() - 1) == grp)
    v1, i1 = first_max(emask)
    v2, i2 = first_max(emask & (lane != i1))
    e2 = jnp.exp(v2 - v1)
    den = 1.0 + e2
    col = lax.broadcasted_iota(jnp.int32, (tm, TOP_K), 1)
    wts_ref[...] = jnp.where(col == 0, 1.0 / den, e2 / den) * g_w

    hit1 = lane == i1
    hit2 = lane == i2
    onehot = jnp.where(hit1 | hit2, 1.0, 0.0)
    ri = lax.broadcasted_iota(jnp.int32, (tm, tm), 0)
    ci = lax.broadcasted_iota(jnp.int32, (tm, tm), 1)
    before = jnp.where(ci < ri, 1.0, 0.0).astype(BF16)
    prefix = jnp.dot(before, onehot.astype(BF16), preferred_element_type=F32) + carry_ref[0:1, :]
    r1 = jnp.sum(jnp.where(hit1, prefix, 0.0), axis=-1, keepdims=True)
    r2 = jnp.sum(jnp.where(hit2, prefix, 0.0), axis=-1, keepdims=True)
    packed = jnp.where(lane == 0, i1.astype(F32), jnp.where(lane == 1, i2.astype(F32),
                       jnp.where(lane == 2, r1, jnp.where(lane == 3, r2, 0.0))))
    route_ref[...] = packed.T[0:SUBLANES, :]
    carry_ref[...] = carry_ref[...] + jnp.sum(onehot, axis=0, keepdims=True)
    counts_ref[...] = carry_ref[...]


def _out_proj_router_kernel(*refs, n_acts, n_experts, n_groups):
    a_refs = refs[:n_acts]
    w_ref, x_ref, gate_ref, sh_ref, sc_ref, g_ref, wr_ref, br_ref = refs[n_acts:n_acts + 8]
    xo_ref, h_ref, wts_ref, route_ref, counts_ref, carry_ref = refs[n_acts + 8:]

    @pl.when((pl.program_id(0) == 0) & (pl.program_id(1) == 0))
    def _():
        carry_ref[...] = jnp.zeros_like(carry_ref)

    acc = None
    off = 0
    for a_ref in a_refs:
        kk = a_ref.shape[-1]
        part = jnp.dot(a_ref[...].astype(BF16), w_ref[off:off + kk, :], preferred_element_type=F32)
        acc = part if acc is None else acc + part
        off += kk
    x = x_ref[...] + gate_ref[...] * acc
    xo_ref[...] = x
    _route_tile(x, sh_ref, sc_ref, g_ref, wr_ref, br_ref, h_ref, wts_ref, route_ref, counts_ref, carry_ref,
                n_experts, n_groups)


def _out_proj_router(acts, w, x, gate, shift, scale, g, w_rt, b_rt, n_experts, n_groups, tm, batch0=0):
    _, s, d = x.shape
    b = acts[0].shape[0]
    row = lambda i, j: (i, 0, 0)
    tile = lambda i, j: (i, j, 0)
    const = lambda i, j: (0, 0)
    return pl.pallas_call(
        functools.partial(_out_proj_router_kernel, n_acts=len(acts), n_experts=n_experts, n_groups=n_groups),
        out_shape=[jax.ShapeDtypeStruct((b, s, d), F32),
                   jax.ShapeDtypeStruct((b, s, d), BF16),
                   jax.ShapeDtypeStruct((b, s, TOP_K), F32),
                   jax.ShapeDtypeStruct((SUBLANES, b * s), F32),
                   jax.ShapeDtypeStruct((SUBLANES, LANES), F32)],
        grid=(b, s // tm),
        in_specs=[pl.BlockSpec((None, tm, a.shape[-1]), tile) for a in acts] + [
            pl.BlockSpec(w.shape, const),
            pl.BlockSpec((None, tm, d), lambda i, j: (i + batch0, j, 0)),
            pl.BlockSpec((None, 1, d), row),
            pl.BlockSpec((None, 1, d), row),
            pl.BlockSpec((None, 1, d), row),
            pl.BlockSpec((1, d), const),
            pl.BlockSpec((d, LANES), const),
            pl.BlockSpec((1, LANES), const),
        ],
        out_specs=[pl.BlockSpec((None, tm, d), tile),
                   pl.BlockSpec((None, tm, d), tile),
                   pl.BlockSpec((None, tm, TOP_K), tile),
                   pl.BlockSpec((SUBLANES, tm), lambda i, j: (0, i * (s // tm) + j)),
                   pl.BlockSpec((SUBLANES, LANES), const)],
        scratch_shapes=[pltpu.VMEM((SUBLANES, LANES), F32)],
        compiler_params=_params("arbitrary", "arbitrary"),
        name="out_proj_router",
    )(*acts, w, x, gate, shift, scale, g, w_rt, b_rt)


def _expert_kernel(ib_ref, ie_ref, lo_ref, hi_ref, xs_ref, w1_ref, w3_ref, w2_ref, ys_ref, w1b, w3b, w2b):
    j = pl.program_id(0)
    prev = jnp.maximum(j - 1, 0)
    e_changed = (j == 0) | (ie_ref[j] != ie_ref[prev])
    first_of_block = (j == 0) | (ib_ref[j] != ib_ref[prev])
    lo = lo_ref[j]
    hi = hi_ref[j]

    @pl.when(e_changed)
    def _():
        w1b[...] = w1_ref[...].astype(BF16)
        w3b[...] = w3_ref[...].astype(BF16)
        w2b[...] = w2_ref[...].astype(BF16)

    rows = ys_ref.shape[0]
    whole = (lo == 0) & (hi == rows)

    @pl.when(first_of_block & jnp.logical_not(whole))
    def _():
        ys_ref[...] = jnp.zeros_like(ys_ref)

    def expert_mlp(x):
        a = jnp.dot(x, w1b[...], preferred_element_type=F32)
        g = jnp.dot(x, w3b[...], preferred_element_type=F32)
        return jnp.dot((_silu(a) * g).astype(BF16), w2b[...], preferred_element_type=F32).astype(ys_ref.dtype)

    @pl.when(whole)
    def _():
        ys_ref[...] = expert_mlp(xs_ref[...])

    half = rows // 2
    for r0 in (0, half):
        @pl.when(jnp.logical_not(whole) & (hi > r0) & (lo < r0 + half) & (hi > lo))
        def _(r0=r0):
            part = pl.ds(r0, half)
            y = expert_mlp(xs_ref[part, :])
            row = lax.broadcasted_iota(jnp.int32, (half, 1), 0) + r0
            ys_ref[part, :] = jnp.where((row >= lo) & (row < hi), y, ys_ref[part, :])


def _experts(layer, items, xs, w1, w3, w2, rows):
    a, d = xs.shape
    hid = w1.shape[-1]
    blk = lambda j, ib, ie, lo, hi: (ib[j], 0)
    wsel = lambda j, ib, ie, lo, hi: (layer, ie[j], 0, 0)
    grid_spec = pltpu.PrefetchScalarGridSpec(
        num_scalar_prefetch=4,
        grid=(items[0].shape[0],),
        in_specs=[
            pl.BlockSpec((rows, d), blk),
            pl.BlockSpec((None, None, d, hid), wsel),
            pl.BlockSpec((None, None, d, hid), wsel),
            pl.BlockSpec((None, None, hid, d), wsel),
        ],
        out_specs=pl.BlockSpec((rows, d), blk),
        scratch_shapes=[pltpu.VMEM((d, hid), BF16), pltpu.VMEM((d, hid), BF16), pltpu.VMEM((hid, d), BF16)],
    )
    return pl.pallas_call(
        _expert_kernel,
        out_shape=jax.ShapeDtypeStruct((a, d), BF16),
        grid_spec=grid_spec,
        compiler_params=_params("arbitrary"),
        name="moe_experts",
    )(*items, xs, w1, w3, w2)


DEST_TILE = 8192


def _dest_kernel(route_ref, starts_ref, dest_ref, *, n_experts):
    ids = route_ref[0:TOP_K, :]
    dest = route_ref[TOP_K:2 * TOP_K, :]
    for e in range(n_experts):
        dest = dest + jnp.where(ids == float(e), starts_ref[0:1, e:e + 1], 0.0)
    dest_ref[...] = dest.astype(jnp.int32)


def _dest_rows(route, starts):
    t = route.shape[1]
    tt = _pick_tile(t, DEST_TILE)
    n_experts = starts.shape[0]
    assert 2 * TOP_K <= SUBLANES and TOP_K * t < 2 ** 24 and n_experts <= LANES
    starts_row = jnp.pad(starts.astype(F32), (0, LANES - n_experts))[None, :]
    return pl.pallas_call(
        functools.partial(_dest_kernel, n_experts=n_experts),
        out_shape=jax.ShapeDtypeStruct((TOP_K, t), jnp.int32),
        grid=(t // tt,),
        in_specs=[pl.BlockSpec((SUBLANES, tt), lambda i: (0, i)), pl.BlockSpec((1, LANES), lambda i: (0, 0))],
        out_specs=pl.BlockSpec((TOP_K, tt), lambda i: (0, i)),
        compiler_params=_params("arbitrary"),
        name="moe_dest",
    )(route, starts_row)


def _dispatch(route, counts, rows):
    n_experts = counts.shape[0]
    t = route.shape[1]
    a = TOP_K * t
    i32 = jnp.int32
    ends = jnp.cumsum(counts)
    starts = ends - counts
    dest = _dest_rows(route, starts)
    tok = jnp.tile(jnp.arange(t, dtype=i32), TOP_K)
    row_tok = lax.sort_key_val(dest.reshape(-1), tok)[1]
    n_blk = a // rows
    bstart = jnp.arange(n_blk, dtype=i32) * rows
    count_le = lambda bounds, x: jnp.sum((bounds[None, :] <= x[:, None]).astype(i32), axis=1)
    e_lo = jnp.minimum(count_le(ends, bstart), n_experts - 1)
    e_hi = jnp.minimum(count_le(ends, bstart + rows - 1), n_experts - 1)
    n_items = e_hi - e_lo + 1
    item_end = jnp.cumsum(n_items)
    item_first = item_end - n_items
    jj = jnp.arange(n_blk + n_experts - 1, dtype=i32)
    valid = jj < item_end[-1]
    ib = jnp.minimum(count_le(item_end, jj), n_blk - 1)
    ie = jnp.where(valid, jnp.clip(e_lo[ib] + jj - item_first[ib], 0, n_experts - 1), e_hi[n_blk - 1]).astype(i32)
    lo = jnp.where(valid, jnp.clip(starts[ie] - ib * rows, 0, rows), 0).astype(i32)
    hi = jnp.where(valid, jnp.clip(ends[ie] - ib * rows, 0, rows), 0).astype(i32)
    return row_tok, dest, (ib, ie, lo, hi)


def _final_combine_kernel(x_ref, y0_ref, y1_ref, w_ref, gate_ref, ng_ref, o_ref):
    x = _apply_pending(x_ref, (y0_ref, y1_ref, w_ref, gate_ref), o_ref)[...]
    o_ref[...] = x * lax.rsqrt(jnp.mean(x * x, axis=-1, keepdims=True) + NORM_EPS) * ng_ref[...]


def _final_combine_kernel_into(x_ref, y0_ref, y1_ref, w_ref, gate_ref, ng_ref, prev_ref, o_ref):
    del prev_ref
    _final_combine_kernel(x_ref, y0_ref, y1_ref, w_ref, gate_ref, ng_ref, o_ref)


def _final_combine(x, pending, final_g, tm, out, batch0, b_total):
    b, s, d = x.shape
    tile = lambda i, j: (i, j, 0)
    in_specs = [pl.BlockSpec((None, tm, d), tile)] + _pending_specs(pending, tm, d) + [
        pl.BlockSpec((1, d), lambda i, j: (0, 0))]
    args = [x, *pending, final_g]
    if out is not None:
        in_specs.append(pl.BlockSpec(memory_space=pl.ANY))
        args.append(out)
    return pl.pallas_call(
        _final_combine_kernel if out is None else _final_combine_kernel_into,
        out_shape=jax.ShapeDtypeStruct((b_total, s, d), F32),
        grid=(b, s // tm),
        in_specs=in_specs,
        out_specs=pl.BlockSpec((None, tm, d), lambda i, j: (i + batch0, j, 0)),
        input_output_aliases={} if out is None else {len(args) - 1: 0},
        compiler_params=_params("arbitrary", "arbitrary"),
        name="moe_final_combine",
    )(*args)


def _hier_moe(layer, x, routed, gate, w1, w3, w2, rows):
    b, s, d = x.shape
    t = b * s
    n_experts = w1.shape[1]
    h, wts, route, counts = routed
    counts = counts[0, :n_experts].astype(jnp.int32)
    row_tok, dest, items = _dispatch(route, counts, rows)
    xs = h.reshape(t, d)[row_tok]
    ys = _experts(layer, items, xs, w1, w3, w2, rows)
    yy = ys[dest.reshape(-1)].reshape(TOP_K, b, s, d)
    return yy, yy, wts, gate


def _pick_tile(s, pref):
    tm = min(pref, s)
    assert s % tm == 0
    return tm


def kernel(x, c, positions, ada_w, ada_b, norm1_g, norm2_g, even_w_in, even_w_gate2, even_b_gate, even_gla_norm_g, even_conv_w, even_conv_b, even_conv_ln_g, even_conv_ln_b, even_w_out, odd_w_qkv, odd_w_out, moe_w_grp, moe_b_grp, moe_w_rt, moe_b_rt, moe_w1, moe_w3, moe_w2, final_norm_g):
    b, s, d = x.shape
    depth = ada_w.shape[0]
    n_experts = moe_w_rt.shape[-1]
    tm = _pick_tile(s, TOKEN_TILE)
    sb = _pick_tile(s, ATTN_SUPER)
    head_dim = d // ATTN_HEADS
    hk = GLA_HEADS * GLA_DK
    hv = GLA_HEADS * GLA_DV

    mods = _ada_mods(c, ada_w, ada_b)
    n_chains = BATCH_CHAINS if b % BATCH_CHAINS == 0 else 1
    bc = b // n_chains
    chains = [dict(x=x, batch0=ch * bc, pending=None, tables=None) for ch in range(n_chains)]

    for layer in range(depth):
        i = layer // 2
        g1 = norm1_g[layer][None, :]
        w_rt_full = jnp.concatenate([moe_w_rt[layer], moe_w_grp[layer],
                                     jnp.zeros((d, LANES - n_experts - N_GROUPS), F32)], axis=1).astype(BF16)
        b_rt_full = jnp.concatenate([moe_b_rt[layer], moe_b_grp[layer],
                                     jnp.zeros((LANES - n_experts - N_GROUPS,), F32)])[None, :]
        if layer % 2 == 0:
            w_in = even_w_in[i]
            main = hk + hk + hv + hv
            w_cat = jnp.concatenate([
                w_in[:, :main], w_in[:, main + GLA_GATE_RANK:], w_in[:, main:main + GLA_GATE_RANK],
                jnp.zeros((d, LANES - GLA_GATE_RANK), w_in.dtype)], axis=1).astype(BF16)
            wg = jnp.concatenate([even_w_gate2[i], jnp.zeros((LANES - GLA_GATE_RANK, hk), F32)], axis=0).astype(BF16)
            w_out = even_w_out[i].astype(BF16)
        else:
            w_qkv = odd_w_qkv[i].astype(BF16)
            w_out = odd_w_out[i].astype(BF16)

        for ch, st in enumerate(chains):
            seqs = slice(ch * bc, (ch + 1) * bc)
            mod = lambda j: mods[layer, j][seqs][:, None, :]
            xin, batch0 = st['x'], st['batch0']
            if layer % 2 == 0:
                x_new, o_gla, y_conv = _even_mixer(
                    xin, st['pending'], mod(0), mod(1), g1, w_cat, wg, even_b_gate[i][None, :],
                    even_gla_norm_g[i][None, :], even_conv_w[i], even_conv_b[i][None, :],
                    even_conv_ln_g[i][None, :], even_conv_ln_b[i][None, :], tm, batch0)
                acts = [o_gla, y_conv]
            else:
                assert batch0 == 0
                if st['tables'] is None:
                    st['tables'] = _rope_tables(positions[seqs], head_dim, tm)
                x_new, q, k, v = _qkv_rope(xin, st['pending'], mod(0), mod(1), g1, w_qkv,
                                           *st['tables'], head_dim, tm)
                acts = [_dilated_attention(q, k, v, head_dim, DILATED_BRANCHES, sb)]
            if st['pending'] is None:
                x_new = xin
            x_new, *routed = _out_proj_router(acts, w_out, x_new, mod(2), mod(3), mod(4), norm2_g[layer][None, :],
                                              w_rt_full, b_rt_full, n_experts, N_GROUPS,
                                              _pick_tile(s, ROUTER_TILE), batch0)
            st['x'], st['batch0'] = x_new, 0
            st['pending'] = _hier_moe(layer, x_new, routed, mod(5), moe_w1, moe_w3, moe_w2, MOE_ROWS)

    out = None
    for ch, st in enumerate(chains):
        out = _final_combine(st['x'], st['pending'], final_norm_g[None, :], tm, out, ch * bc, b)
    return out
```
